```python
import math
import jax, jax.numpy as jnp
from jax import lax
import numpy as np

D_MODEL = 2048
BATCH = 2
SEQ = 8192
DEPTH = 1

HEAD_DIM = 128
N_HEADS_SB = 8
N_HEADS_DSA = 8
N_IDX_HEADS = 16
IDX_DIM = 64
TOPK_MAX = 256
D_FF = 5632
CONV_WIDTH = 3
N_BUCKETS = 32
MAX_DISTANCE = 128
Q_BLOCK = 128
EPS = 1e-6
W_SB = N_HEADS_SB * HEAD_DIM
W_DSA = N_HEADS_DSA * HEAD_DIM
IN_SIZES = (W_SB, W_SB, W_SB, W_DSA, W_DSA, W_DSA,
            N_IDX_HEADS * IDX_DIM, IDX_DIM, N_IDX_HEADS, D_MODEL, D_MODEL)
P_IN = 3 * W_SB + 3 * W_DSA + N_IDX_HEADS * IDX_DIM + IDX_DIM + N_IDX_HEADS + 2 * D_MODEL

kernel_name = 'hybrid_stickbreak_dsa_convffn_block'


def rms_norm(x, g):
    xf = x.astype(jnp.float32)
    y = xf * lax.rsqrt(jnp.mean(xf * xf, axis=-1, keepdims=True) + EPS)
    return (y * g.astype(jnp.float32)).astype(x.dtype)


def modulate(h, shift, scale):
    return h * (1.0 + scale[:, None, :]) + shift[:, None, :]


def split_columns(z):
    cuts, acc = [], 0
    for s in IN_SIZES[:-1]:
        acc += s
        cuts.append(acc)
    return jnp.split(z, cuts, axis=-1)


def rel_bucket(dist):
    n = jnp.maximum(dist, 0)
    max_exact = N_BUCKETS // 2
    nf = jnp.maximum(n, 1).astype(jnp.float32)
    large = max_exact + (jnp.log(nf / max_exact) / math.log(MAX_DISTANCE / max_exact)
                         * (N_BUCKETS - max_exact)).astype(jnp.int32)
    large = jnp.minimum(large, N_BUCKETS - 1)
    return jnp.where(n < max_exact, n, large)


def stick_breaking_attention(q, k, v):
    B, S, H, Dh = q.shape
    n_blk = S // Q_BLOCK
    kf = k.astype(jnp.float32)
    vf = v.astype(jnp.float32)
    key_pos = jnp.arange(S)
    scale = HEAD_DIM ** -0.5

    def block(i):
        start = i * Q_BLOCK
        qb = lax.dynamic_slice_in_dim(q, start, Q_BLOCK, axis=1).astype(jnp.float32)
        z = jnp.einsum('bqhd,bshd->bhqs', qb, kf) * scale
        q_pos = start + jnp.arange(Q_BLOCK)
        causal = key_pos[None, :] < q_pos[:, None]
        log_beta = jax.nn.log_sigmoid(z)
        log_keep = jnp.where(causal, jax.nn.log_sigmoid(-z), 0.0)
        suffix = lax.cumsum(log_keep, axis=3, reverse=True) - log_keep
        a = jnp.where(causal, jnp.exp(log_beta + suffix), 0.0)
        return jnp.einsum('bhqs,bshd->bqhd', a, vf)

    out = lax.map(block, jnp.arange(n_blk))
    return out.transpose(1, 0, 2, 3, 4).reshape(B, S, H * Dh).astype(q.dtype)


def indexer_sparse_attention(q, k, v, q_idx, k_idx, w_idx, rel_bias):
    B, S, H, Dh = q.shape
    n_blk = S // Q_BLOCK
    n_sel = min(TOPK_MAX, S // 4)
    kf = k.astype(jnp.float32)
    vf = v.astype(jnp.float32)
    kif = k_idx.astype(jnp.float32)
    key_pos = jnp.arange(S)
    scale = HEAD_DIM ** -0.5
    idx_scale = IDX_DIM ** -0.5
    gather = jax.vmap(lambda arr, ix: arr[ix])

    def block(i):
        start = i * Q_BLOCK
        q_pos = start + jnp.arange(Q_BLOCK)
        qi = lax.dynamic_slice_in_dim(q_idx, start, Q_BLOCK, axis=1).astype(jnp.float32)
        wi = lax.dynamic_slice_in_dim(w_idx, start, Q_BLOCK, axis=1).astype(jnp.float32)
        per_head = jax.nn.relu(jnp.einsum('bqhd,bsd->bqhs', qi, kif) * idx_scale)
        score = jnp.einsum('bqh,bqhs->bqs', wi, per_head)
        causal = key_pos[None, :] <= q_pos[:, None]
        score = jnp.where(causal[None], score, -jnp.inf)
        _, sel = lax.top_k(score, n_sel)
        valid = sel <= q_pos[None, :, None]
        k_sel = gather(kf, sel)
        v_sel = gather(vf, sel)
        qb = lax.dynamic_slice_in_dim(q, start, Q_BLOCK, axis=1).astype(jnp.float32)
        logits = jnp.einsum('bqhd,bqkhd->bhqk', qb, k_sel) * scale
        bias = rel_bias.astype(jnp.float32)[rel_bucket(q_pos[None, :, None] - sel)]
        logits = logits + bias.transpose(0, 3, 1, 2)
        logits = jnp.where(valid[:, None], logits, -jnp.inf)
        p = jax.nn.softmax(logits, axis=-1)
        return jnp.einsum('bhqk,bqkhd->bqhd', p, v_sel)

    out = lax.map(block, jnp.arange(n_blk))
    return out.transpose(1, 0, 2, 3, 4).reshape(B, S, H * Dh).astype(q.dtype)


def causal_dwconv(a, w, b):
    S = a.shape[1]
    ap = jnp.pad(a, ((0, 0), (CONV_WIDTH - 1, 0), (0, 0)))
    y = b
    for j in range(CONV_WIDTH):
        y = y + ap[:, j:j + S, :] * w[j]
    return y


def setup_inputs(seed: int = 0) -> dict:
    key = jax.random.key(seed)
    ks = jax.random.split(key, 20)
    f32 = jnp.float32

    def nrm(k, shape, fan_in):
        return jax.random.normal(k, shape, f32) * fan_in ** -0.5

    return {
        'x': jax.random.normal(ks[0], (BATCH, SEQ, D_MODEL), f32),
        'c': jax.random.normal(ks[1], (BATCH, D_MODEL), f32),
        'w_ada': 0.5 * nrm(ks[2], (DEPTH, D_MODEL, 6 * D_MODEL), D_MODEL),
        'b_ada': 0.01 * jax.random.normal(ks[3], (DEPTH, 6 * D_MODEL), f32),
        'g_mix': 1.0 + 0.05 * jax.random.normal(ks[4], (DEPTH, D_MODEL), f32),
        'w_in': nrm(ks[5], (DEPTH, D_MODEL, P_IN), D_MODEL),
        'w_o_sb': nrm(ks[6], (DEPTH, W_SB, D_MODEL), W_SB),
        'w_o_dsa': nrm(ks[7], (DEPTH, W_DSA, D_MODEL), W_DSA),
        'w_out': nrm(ks[8], (DEPTH, D_MODEL, D_MODEL), D_MODEL),
        'rel_bias': 0.5 * jax.random.normal(ks[9], (N_BUCKETS, N_HEADS_DSA), f32),
        'g_ffn': 1.0 + 0.05 * jax.random.normal(ks[10], (DEPTH, D_MODEL), f32),
        'w_gate': nrm(ks[11], (DEPTH, D_MODEL, D_FF), D_MODEL),
        'w_up': nrm(ks[12], (DEPTH, D_MODEL, D_FF), D_MODEL),
        'conv_w': nrm(ks[13], (DEPTH, CONV_WIDTH, D_FF), CONV_WIDTH),
        'conv_b': 0.01 * jax.random.normal(ks[14], (DEPTH, D_FF), f32),
        'w_down': nrm(ks[15], (DEPTH, D_FF, D_MODEL), D_FF),
        'g_final': 1.0 + 0.05 * jax.random.normal(ks[16], (D_MODEL,), f32),
    }


def reference(x, c, w_ada, b_ada, g_mix, w_in, w_o_sb, w_o_dsa, w_out, rel_bias,
              g_ffn, w_gate, w_up, conv_w, conv_b, w_down, g_final):
    B, S, _ = x.shape
    c_act = jax.nn.silu(c)
    for l in range(DEPTH):
        mod = c_act @ w_ada[l] + b_ada[l]
        sh1, sc1, gt1, sh2, sc2, gt2 = jnp.split(mod, 6, axis=-1)

        h = modulate(rms_norm(x, g_mix[l]), sh1, sc1)
        z = h @ w_in[l]
        (q_sb, k_sb, v_sb, q_ds, k_ds, v_ds, q_ix, k_ix, w_ix,
         gate_sb, gate_ds) = split_columns(z)
        hs = (B, S, N_HEADS_SB, HEAD_DIM)
        hd = (B, S, N_HEADS_DSA, HEAD_DIM)
        o_sb = stick_breaking_attention(q_sb.reshape(hs), k_sb.reshape(hs), v_sb.reshape(hs))
        o_ds = indexer_sparse_attention(
            q_ds.reshape(hd), k_ds.reshape(hd), v_ds.reshape(hd),
            q_ix.reshape(B, S, N_IDX_HEADS, IDX_DIM), k_ix,
            w_ix * (N_IDX_HEADS ** -0.5), rel_bias)
        merged = (jax.nn.sigmoid(gate_sb) * (o_sb @ w_o_sb[l])
                  + jax.nn.sigmoid(gate_ds) * (o_ds @ w_o_dsa[l]))
        x = x + gt1[:, None, :] * (merged @ w_out[l])

        h = modulate(rms_norm(x, g_ffn[l]), sh2, sc2)
        a = causal_dwconv(h @ w_gate[l], conv_w[l], conv_b[l])
        y = (jax.nn.silu(a) * (h @ w_up[l])) @ w_down[l]
        x = x + gt2[:, None, :] * y
    return rms_norm(x, g_final)
```

```python
import functools
import math

import jax
import jax.numpy as jnp
from jax import lax
from jax.experimental import pallas as pl
from jax.experimental.pallas import tpu as pltpu

HEAD_DIM = 128
N_IDX_HEADS = 16
IDX_DIM = 64
TOPK_MAX = 256
N_BUCKETS = 32
MAX_DISTANCE = 128
CONV_WIDTH = 3
EPS = 1e-6

F32 = jnp.float32
BF16 = jnp.bfloat16
NEG = -1e30
INT_MIN = -2 ** 31
EXP_ZERO_BELOW = 104.0
V7X_VMEM_LIMIT = 56 * 1024 * 1024
NT_DIMS = (((1,), (1,)), ((), ()))


def _params(*sem):
    return pltpu.CompilerParams(dimension_semantics=sem, vmem_limit_bytes=V7X_VMEM_LIMIT)


def _sigmoid(x):
    return 1.0 / (1.0 + jnp.exp(-x))


def _adaln_kernel(ct_ref, w_ref, b_ref, o_ref):
    ct = ct_ref[...]
    act = ct * _sigmoid(ct)
    w = w_ref[...]
    for b in range(ct.shape[1]):
        o_ref[b:b + 1, :] = jnp.sum(act[:, b:b + 1] * w, axis=0, keepdims=True) + b_ref[...]


def _adaln(c, w, bias):
    B, D = c.shape
    N = w.shape[1]
    tn = min(N, 1024)
    return pl.pallas_call(
        _adaln_kernel,
        grid=(N // tn,),
        in_specs=[pl.BlockSpec((D, B), lambda j: (0, 0)),
                  pl.BlockSpec((D, tn), lambda j: (0, j)),
                  pl.BlockSpec((1, tn), lambda j: (0, j))],
        out_specs=pl.BlockSpec((B, tn), lambda j: (0, j)),
        out_shape=jax.ShapeDtypeStruct((B, N), F32),
        compiler_params=_params("arbitrary"),
        name="adaln",
    )(c.T, w, bias.reshape(1, N))


def _rms_mod(x, g, sc, sh):
    ms = jnp.mean(x * x, axis=-1, keepdims=True)
    y = x * lax.rsqrt(ms + EPS) * g
    return y * (1.0 + sc) + sh


def _norm_mod_kernel(x_ref, g_ref, sc_ref, sh_ref, o_ref):
    o_ref[...] = _rms_mod(x_ref[...], g_ref[...], sc_ref[0], sh_ref[0]).astype(o_ref.dtype)


def _norm_mod(x2, g, sc, sh, S):
    M, D = x2.shape
    B = sc.shape[0]
    tm = min(S, 512)
    per_b = S // tm
    return pl.pallas_call(
        _norm_mod_kernel,
        grid=(M // tm,),
        in_specs=[pl.BlockSpec((tm, D), lambda i: (i, 0)),
                  pl.BlockSpec((1, D), lambda i: (0, 0)),
                  pl.BlockSpec((1, 1, D), lambda i: (i // per_b, 0, 0)),
                  pl.BlockSpec((1, 1, D), lambda i: (i // per_b, 0, 0))],
        out_specs=pl.BlockSpec((tm, D), lambda i: (i, 0)),
        out_shape=jax.ShapeDtypeStruct((M, D), BF16),
        compiler_params=_params("arbitrary"),
        name="norm_mod",
    )(x2, g.reshape(1, D), sc.reshape(B, 1, D), sh.reshape(B, 1, D))


def _in_proj_kernel(a_ref, b_ref, o_ref, *, scaled_tiles, scale):
    j = pl.program_id(0)
    acc = jnp.dot(a_ref[...], b_ref[...], preferred_element_type=F32)
    is_scaled = functools.reduce(jnp.logical_or, [j == t for t in scaled_tiles], False)
    o_ref[...] = (acc * jnp.where(is_scaled, scale, 1.0)).astype(o_ref.dtype)


def _in_proj(h, w, tn, scaled_tiles, scale, name):
    M, K = h.shape
    N = w.shape[1]
    tm = min(M, 1024)
    return pl.pallas_call(
        functools.partial(_in_proj_kernel, scaled_tiles=scaled_tiles, scale=scale),
        grid=(N // tn, M // tm),
        in_specs=[pl.BlockSpec((tm, K), lambda j, i: (i, 0)),
                  pl.BlockSpec((K, tn), lambda j, i: (0, j))],
        out_specs=pl.BlockSpec((tm, tn), lambda j, i: (i, j)),
        out_shape=jax.ShapeDtypeStruct((M, N), BF16),
        compiler_params=_params("arbitrary", "arbitrary"),
        name=name,
    )(h, w)


def _sb_kernel(q_ref, k_ref, v_ref, o_ref, run_ref, acc_ref, *, tq, tk):
    qi = pl.program_id(2)
    q0 = qi * tq
    q = q_ref[...]
    run_ref[...] = jnp.zeros_like(run_ref)
    acc_ref[...] = jnp.zeros_like(acc_ref)
    tri = (lax.broadcasted_iota(jnp.int32, (tk, tk), 0)
           > lax.broadcasted_iota(jnp.int32, (tk, tk), 1)).astype(BF16)
    row = q0 + lax.broadcasted_iota(jnp.int32, (tq, tk), 0)
    col_in_blk = lax.broadcasted_iota(jnp.int32, (tq, tk), 1)

    def body(carry):
        kb, _ = carry
        k0 = pl.multiple_of(kb * tk, tk)
        k = k_ref[pl.ds(k0, tk), :]
        v = v_ref[pl.ds(k0, tk), :]
        z = lax.dot_general(q, k, NT_DIMS, preferred_element_type=F32)
        sp = jnp.maximum(z, 0.0) + jnp.log(1.0 + jnp.exp(-jnp.abs(z)))
        causal = (k0 + col_in_blk) < row
        spm = jnp.where(causal, sp, 0.0)
        hi = spm.astype(BF16)
        lo = (spm - hi.astype(F32)).astype(BF16)
        suffix = (jnp.dot(hi, tri, preferred_element_type=F32)
                  + jnp.dot(lo, tri, preferred_element_type=F32))
        run = run_ref[...]
        a = jnp.where(causal, jnp.exp(z - sp - suffix - run), 0.0)
        acc_ref[...] += jnp.dot(a.astype(BF16), v, preferred_element_type=F32)
        run_new = run + jnp.sum(spm, axis=1, keepdims=True)
        run_ref[...] = run_new
        return kb - 1, jnp.min(run_new) > EXP_ZERO_BELOW

    def cond(carry):
        kb, dead = carry
        return jnp.logical_and(kb >= 0, jnp.logical_not(dead))

    lax.while_loop(cond, body, ((q0 + tq - 1) // tk, jnp.bool_(False)))
    o_ref[...] = acc_ref[...].astype(o_ref.dtype)


def _sb_attn(z, B, S, H, q_blk, k_blk, v_blk):
    M = z.shape[0]
    tq = tk = min(S, 256)
    nq = S // tq
    return pl.pallas_call(
        functools.partial(_sb_kernel, tq=tq, tk=tk),
        grid=(B, H, nq),
        in_specs=[pl.BlockSpec((tq, HEAD_DIM), lambda b, h, i: (b * nq + i, q_blk + h)),
                  pl.BlockSpec((S, HEAD_DIM), lambda b, h, i: (b, k_blk + h)),
                  pl.BlockSpec((S, HEAD_DIM), lambda b, h, i: (b, v_blk + h))],
        out_specs=pl.BlockSpec((tq, HEAD_DIM), lambda b, h, i: (b * nq + i, h)),
        out_shape=jax.ShapeDtypeStruct((M, H * HEAD_DIM), BF16),
        scratch_shapes=[pltpu.VMEM((tq, 1), F32), pltpu.VMEM((tq, HEAD_DIM), F32)],
        compiler_params=_params("arbitrary", "arbitrary", "arbitrary"),
        name="sb_attn",
    )(z, z, z)


def _dsa_select_kernel(q_ref, k_ref, w_ref, mask_ref, key_ref, *, tq, tk, n_sel, n_kblocks):
    qi = pl.program_id(1)
    q0 = qi * tq
    nkb = (q0 + tq + tk - 1) // tk
    lane = lax.broadcasted_iota(jnp.int32, (tk, 2 * IDX_DIM), 1)
    row = q0 + lax.broadcasted_iota(jnp.int32, (tq, tk), 0)
    col_in_blk = lax.broadcasted_iota(jnp.int32, (tq, tk), 1)
    w_scale = (N_IDX_HEADS ** -0.5) * (IDX_DIM ** -0.5)
    w = w_ref[...].astype(F32) * w_scale
    w_cols = [w[:, h:h + 1] for h in range(N_IDX_HEADS)]

    def score_block(kb, carry):
        k0 = pl.multiple_of(kb * tk, tk)
        kk = k_ref[pl.ds(k0, tk), :]
        k_halves = (jnp.where(lane < IDX_DIM, kk, jnp.zeros_like(kk)),
                    jnp.where(lane >= IDX_DIM, kk, jnp.zeros_like(kk)))
        acc = jnp.zeros((tq, tk), F32)
        for p in range(N_IDX_HEADS // 2):
            q2 = q_ref[:, p * 2 * IDX_DIM:(p + 1) * 2 * IDX_DIM]
            for half in range(2):
                ph = lax.dot_general(q2, k_halves[half], NT_DIMS, preferred_element_type=F32)
                acc = acc + jnp.maximum(ph, 0.0) * w_cols[2 * p + half]
        bits = lax.bitcast_convert_type(acc, jnp.int32)
        key = bits ^ ((bits >> 31) & 0x7FFFFFFF)
        key = jnp.where((k0 + col_in_blk) <= row, key, INT_MIN)
        key_ref[:, pl.ds(k0, tk)] = key
        return carry

    lax.fori_loop(0, nkb, score_block, 0)

    def count(pred_fn, thr):
        def blk(kb, cnt):
            k0 = pl.multiple_of(kb * tk, tk)
            ones = jnp.where(pred_fn(key_ref[:, pl.ds(k0, tk)], thr), 1, 0)
            part = ones[:, 0:128]
            for c in range(1, tk // 128):
                part = part + ones[:, c * 128:(c + 1) * 128]
            return cnt + part
        cnt = lax.fori_loop(0, nkb, blk, jnp.zeros((tq, 128), jnp.int32))
        return jnp.sum(cnt, axis=1, keepdims=True)

    def bisect(it, thr):
        cand = thr + jnp.left_shift(jnp.int32(1), 31 - it)
        cnt = count(lambda key, c: key >= c, cand)
        return jnp.where(cnt >= n_sel, cand, thr)

    thr = lax.fori_loop(0, 32, bisect, jnp.full((tq, 1), INT_MIN, jnp.int32))
    need = (n_sel - count(lambda key, c: key > c, thr)).astype(F32)

    before = (lax.broadcasted_iota(jnp.int32, (tk, tk), 0)
              < lax.broadcasted_iota(jnp.int32, (tk, tk), 1)).astype(BF16)

    def emit(kb, ties_seen):
        k0 = pl.multiple_of(kb * tk, tk)
        key = key_ref[:, pl.ds(k0, tk)]
        eq = jnp.where(key == thr, 1.0, 0.0)
        rank = ties_seen + jnp.dot(eq.astype(BF16), before, preferred_element_type=F32)
        tie_ok = jnp.where(rank < need, eq, 0.0)
        sel = jnp.where(key > thr, 1.0, tie_ok)
        sel = jnp.where((k0 + col_in_blk) <= row, sel, 0.0)
        mask_ref[:, pl.ds(k0, tk)] = jnp.where(sel > 0.5, 0.0, NEG).astype(mask_ref.dtype)
        return ties_seen + jnp.sum(eq, axis=1, keepdims=True)

    lax.fori_loop(0, nkb, emit, jnp.zeros((tq, 1), F32))

    def fill(kb, carry):
        k0 = pl.multiple_of(kb * tk, tk)
        mask_ref[:, pl.ds(k0, tk)] = jnp.full((tq, tk), NEG, mask_ref.dtype)
        return carry

    lax.fori_loop(nkb, n_kblocks, fill, 0)


def _dsa_select(z, zs, B, S, qix_blk, n_sel):
    M = z.shape[0]
    tq = tk = min(S, 256)
    nq = S // tq
    qw = N_IDX_HEADS * IDX_DIM
    return pl.pallas_call(
        functools.partial(_dsa_select_kernel, tq=tq, tk=tk, n_sel=n_sel, n_kblocks=S // tk),
        grid=(B, nq),
        in_specs=[pl.BlockSpec((tq, qw), lambda b, i: (b * nq + i, qix_blk)),
                  pl.BlockSpec((S, 2 * IDX_DIM), lambda b, i: (b, 0)),
                  pl.BlockSpec((tq, 2 * IDX_DIM), lambda b, i: (b * nq + i, 1))],
        out_specs=pl.BlockSpec((tq, S), lambda b, i: (b * nq + i, 0)),
        out_shape=jax.ShapeDtypeStruct((M, S), BF16),
        scratch_shapes=[pltpu.VMEM((tq, S), jnp.int32)],
        compiler_params=_params("arbitrary", "arbitrary"),
        name="dsa_select",
    )(z, zs, zs)


def _rel_bucket(dist):
    n = jnp.maximum(dist, 0)
    max_exact = N_BUCKETS // 2
    nf = jnp.maximum(n, 1).astype(F32)
    large = max_exact + (jnp.log(nf / max_exact) / math.log(MAX_DISTANCE / max_exact)
                         * (N_BUCKETS - max_exact)).astype(jnp.int32)
    large = jnp.minimum(large, N_BUCKETS - 1)
    return jnp.where(n < max_exact, n, large)


def _bias_tiles(rel_bias, tq):
    assert tq >= MAX_DISTANCE
    r = jnp.arange(tq)[:, None]
    c = jnp.arange(tq)[None, :]
    dist = jnp.stack([r - c, tq + r - c])
    rb = rel_bias.astype(F32)
    tiles = rb[_rel_bucket(dist)] - rb[N_BUCKETS - 1]
    return tiles.transpose(3, 0, 1, 2)


def _dsa_attn_kernel(q_ref, k_ref, v_ref, mask_ref, bias_ref, o_ref, m_ref, l_ref, acc_ref, *, tq):
    qi = pl.program_id(2)
    q = q_ref[...]
    m_ref[...] = jnp.full_like(m_ref, NEG)
    l_ref[...] = jnp.zeros_like(l_ref)
    acc_ref[...] = jnp.zeros_like(acc_ref)

    def step(kb, bias):
        k0 = pl.multiple_of(kb * tq, tq)
        k = k_ref[pl.ds(k0, tq), :]
        v = v_ref[pl.ds(k0, tq), :]
        s = lax.dot_general(q, k, NT_DIMS, preferred_element_type=F32)
        s = s + mask_ref[:, pl.ds(k0, tq)].astype(F32)
        if bias is not None:
            s = s + bias
        m_old = m_ref[...]
        m_new = jnp.maximum(m_old, jnp.max(s, axis=1, keepdims=True))
        alpha = jnp.exp(m_old - m_new)
        p = jnp.exp(s - m_new)
        l_ref[...] = alpha * l_ref[...] + jnp.sum(p, axis=1, keepdims=True)
        acc_ref[...] = alpha * acc_ref[...] + jnp.dot(p.astype(BF16), v, preferred_element_type=F32)
        m_ref[...] = m_new

    def far(kb, carry):
        step(kb, None)
        return carry

    lax.fori_loop(0, qi - 1, far, 0)

    @pl.when(qi >= 1)
    def _():
        step(qi - 1, bias_ref[0, 1])

    step(qi, bias_ref[0, 0])
    o_ref[...] = (acc_ref[...] / l_ref[...]).astype(o_ref.dtype)


def _dsa_attn(z, mask, bias_tiles, B, S, H, q_blk, k_blk, v_blk):
    M = z.shape[0]
    tq = bias_tiles.shape[-1]
    nq = S // tq
    return pl.pallas_call(
        functools.partial(_dsa_attn_kernel, tq=tq),
        grid=(B, H, nq),
        in_specs=[pl.BlockSpec((tq, HEAD_DIM), lambda b, h, i: (b * nq + i, q_blk + h)),
                  pl.BlockSpec((S, HEAD_DIM), lambda b, h, i: (b, k_blk + h)),
                  pl.BlockSpec((S, HEAD_DIM), lambda b, h, i: (b, v_blk + h)),
                  pl.BlockSpec((tq, S), lambda b, h, i: (b * nq + i, 0)),
                  pl.BlockSpec((1, 2, tq, tq), lambda b, h, i: (h, 0, 0, 0))],
        out_specs=pl.BlockSpec((tq, HEAD_DIM), lambda b, h, i: (b * nq + i, h)),
        out_shape=jax.ShapeDtypeStruct((M, H * HEAD_DIM), BF16),
        scratch_shapes=[pltpu.VMEM((tq, 1), F32), pltpu.VMEM((tq, 1), F32),
                        pltpu.VMEM((tq, HEAD_DIM), F32)],
        compiler_params=_params("arbitrary", "arbitrary", "arbitrary"),
        name="dsa_attn",
    )(z, z, z, mask, bias_tiles)


def _mix_out_kernel(osb_ref, ods_ref, gsb_ref, gds_ref, x_ref, wsb_ref, wds_ref, wout_ref,
                    gt_ref, g_ref, sc_ref, sh_ref, x1_ref, h2_ref):
    t_sb = jnp.dot(osb_ref[...], wsb_ref[...], preferred_element_type=F32)
    t_ds = jnp.dot(ods_ref[...], wds_ref[...], preferred_element_type=F32)
    merged = (_sigmoid(gsb_ref[...].astype(F32)) * t_sb
              + _sigmoid(gds_ref[...].astype(F32)) * t_ds)
    y = jnp.dot(merged.astype(BF16), wout_ref[...], preferred_element_type=F32)
    x1 = x_ref[...] + gt_ref[0] * y
    x1_ref[...] = x1
    h2_ref[...] = _rms_mod(x1, g_ref[...], sc_ref[0], sh_ref[0]).astype(h2_ref.dtype)


def _mix_out(o_sb, o_ds, z, gate_blk, x2, w_sb, w_ds, w_out, gt, g, sc, sh, S):
    M, D = x2.shape
    B = gt.shape[0]
    W = o_sb.shape[1]
    tm = min(S, 512)
    per_b = S // tm
    row = lambda i: (i, 0)
    const = lambda i: (0, 0)
    per_batch = lambda i: (i // per_b, 0, 0)
    return pl.pallas_call(
        _mix_out_kernel,
        grid=(M // tm,),
        in_specs=[pl.BlockSpec((tm, W), row),
                  pl.BlockSpec((tm, W), row),
                  pl.BlockSpec((tm, D), lambda i: (i, gate_blk)),
                  pl.BlockSpec((tm, D), lambda i: (i, gate_blk + 1)),
                  pl.BlockSpec((tm, D), row),
                  pl.BlockSpec((W, D), const),
                  pl.BlockSpec((W, D), const),
                  pl.BlockSpec((D, D), const),
                  pl.BlockSpec((1, 1, D), per_batch),
                  pl.BlockSpec((1, D), const),
                  pl.BlockSpec((1, 1, D), per_batch),
                  pl.BlockSpec((1, 1, D), per_batch)],
        out_specs=[pl.BlockSpec((tm, D), row), pl.BlockSpec((tm, D), row)],
        out_shape=[jax.ShapeDtypeStruct((M, D), F32), jax.ShapeDtypeStruct((M, D), BF16)],
        compiler_params=_params("arbitrary"),
        name="mix_out",
    )(o_sb, o_ds, z, z, x2, w_sb, w_ds, w_out, gt.reshape(B, 1, D), g.reshape(1, D),
      sc.reshape(B, 1, D), sh.reshape(B, 1, D))


HALO = 16


def _conv_ffn_kernel(h_ref, halo_ref, x1_ref, wg_ref, wu_ref, wd_ref, cw_ref, cb_ref, gt_ref,
                     gf_ref, o_ref, acc_ref, *, tiles_per_seq):
    i = pl.program_id(0)
    f = pl.program_id(1)

    @pl.when(f == 0)
    def _():
        acc_ref[...] = jnp.zeros_like(acc_ref)

    h = h_ref[...]
    wg = wg_ref[...]
    g0 = jnp.dot(h, wg, preferred_element_type=F32)
    g_prev = jnp.dot(halo_ref[...], wg, preferred_element_type=F32)
    g_prev = jnp.where(i % tiles_per_seq == 0, 0.0, g_prev)
    ridx = lax.broadcasted_iota(jnp.int32, g0.shape, 0)
    g1 = jnp.where(ridx == 0, g_prev[HALO - 1:HALO, :], pltpu.roll(g0, 1, 0))
    g2 = jnp.where(ridx == 0, g_prev[HALO - 2:HALO - 1, :],
                   jnp.where(ridx == 1, g_prev[HALO - 1:HALO, :], pltpu.roll(g0, 2, 0)))
    cw = cw_ref[...]
    a = cb_ref[...] + g2 * cw[0:1, :] + g1 * cw[1:2, :] + g0 * cw[2:3, :]
    u = jnp.dot(h, wu_ref[...], preferred_element_type=F32)
    act = (a * _sigmoid(a) * u).astype(BF16)
    acc_ref[...] += jnp.dot(act, wd_ref[...], preferred_element_type=F32)

    @pl.when(f == pl.num_programs(1) - 1)
    def _():
        x = x1_ref[...] + gt_ref[0] * acc_ref[...]
        ms = jnp.mean(x * x, axis=-1, keepdims=True)
        o_ref[...] = x * lax.rsqrt(ms + EPS) * gf_ref[...]


def _conv_ffn(h2, x1, w_gate, w_up, w_down, conv_w, conv_b, gt, g_final, S):
    M, D = x1.shape
    B = gt.shape[0]
    F = w_gate.shape[1]
    tm = min(S, 512)
    tf = min(F, 512)
    per_b = S // tm
    halo_per_tile = tm // HALO
    return pl.pallas_call(
        functools.partial(_conv_ffn_kernel, tiles_per_seq=per_b),
        grid=(M // tm, F // tf),
        in_specs=[pl.BlockSpec((tm, D), lambda i, f: (i, 0)),
                  pl.BlockSpec((HALO, D), lambda i, f: (jnp.maximum(i * halo_per_tile - 1, 0), 0)),
                  pl.BlockSpec((tm, D), lambda i, f: (i, 0)),
                  pl.BlockSpec((D, tf), lambda i, f: (0, f)),
                  pl.BlockSpec((D, tf), lambda i, f: (0, f)),
                  pl.BlockSpec((tf, D), lambda i, f: (f, 0)),
                  pl.BlockSpec((CONV_WIDTH, tf), lambda i, f: (0, f)),
                  pl.BlockSpec((1, tf), lambda i, f: (0, f)),
                  pl.BlockSpec((1, 1, D), lambda i, f: (i // per_b, 0, 0)),
                  pl.BlockSpec((1, D), lambda i, f: (0, 0))],
        out_specs=pl.BlockSpec((tm, D), lambda i, f: (i, 0)),
        out_shape=jax.ShapeDtypeStruct((M, D), F32),
        scratch_shapes=[pltpu.VMEM((tm, D), F32)],
        compiler_params=_params("arbitrary", "arbitrary"),
        name="conv_ffn",
    )(h2, h2, x1, w_gate, w_up, w_down, conv_w, conv_b.reshape(1, F), gt.reshape(B, 1, D),
      g_final.reshape(1, D))


def kernel(x, c, w_ada, b_ada, g_mix, w_in, w_o_sb, w_o_dsa, w_out, rel_bias, g_ffn, w_gate,
           w_up, conv_w, conv_b, w_down, g_final):
    B, S, D = x.shape
    depth = w_ada.shape[0]
    W = w_o_sb.shape[1]
    H = W // HEAD_DIM
    assert w_o_dsa.shape[1] == W and D % W == 0 and S % 256 == 0
    n_sel = min(TOPK_MAX, S // 4)
    qw = N_IDX_HEADS * IDX_DIM
    scale = HEAD_DIM ** -0.5
    x2 = x.reshape(B * S, D)

    for l in range(depth):
        mod = _adaln(c, w_ada[l], b_ada[l])
        sh1, sc1, gt1, sh2, sc2, gt2 = jnp.split(mod, 6, axis=-1)

        wl = w_in[l]
        o_qix = 6 * W
        o_kix = o_qix + qw
        o_wix = o_kix + IDX_DIM
        o_gate = o_wix + N_IDX_HEADS
        w_main = jnp.concatenate([wl[:, :o_qix], wl[:, o_gate:], wl[:, o_qix:o_kix]], axis=1).astype(BF16)
        k_cols = wl[:, o_kix:o_wix]
        w_tail = jnp.concatenate(
            [k_cols, k_cols, wl[:, o_wix:o_gate],
             jnp.zeros((D, 2 * IDX_DIM - N_IDX_HEADS), wl.dtype)], axis=1).astype(BF16)

        h1 = _norm_mod(x2, g_mix[l], sc1, sh1, S)
        z = _in_proj(h1, w_main, W, (0, 3), scale, "in_proj")
        zs = _in_proj(h1, w_tail, 4 * IDX_DIM, (), 1.0, "in_proj_idx")

        hb = W // HEAD_DIM
        o_sb = _sb_attn(z, B, S, H, 0, hb, 2 * hb)
        mask = _dsa_select(z, zs, B, S, (6 * W + 2 * D) // qw, n_sel)
        o_ds = _dsa_attn(z, mask, _bias_tiles(rel_bias, min(S, 256)), B, S, H, 3 * hb, 4 * hb, 5 * hb)

        x2, h2 = _mix_out(o_sb, o_ds, z, 6 * W // D, x2, w_o_sb[l].astype(BF16),
                          w_o_dsa[l].astype(BF16), w_out[l].astype(BF16), gt1, g_ffn[l], sc2, sh2, S)
        last = l == depth - 1
        assert last, "the final rms_norm is fused into the last layer's FFN"
        x2 = _conv_ffn(h2, x2, w_gate[l].astype(BF16), w_up[l].astype(BF16), w_down[l].astype(BF16),
                       conv_w[l], conv_b[l], gt2, g_final, S)
    return x2.reshape(B, S, D)
```

```python
import functools
import math

import jax
import jax.numpy as jnp
from jax import lax
from jax.experimental import pallas as pl
from jax.experimental.pallas import tpu as pltpu

HEAD_DIM = 128
N_IDX_HEADS = 16
IDX_DIM = 64
TOPK_MAX = 256
N_BUCKETS = 32
MAX_DISTANCE = 128
CONV_WIDTH = 3
EPS = 1e-6

F32 = jnp.float32
BF16 = jnp.bfloat16
NEG = -1e30
INT_MIN = -2 ** 31
EXP_ZERO_BELOW = 104.0
V7X_VMEM_LIMIT = 56 * 1024 * 1024
SUBLANES = 8
BF16_ROWS = 16
ATTN_BLOCK = 256
HEAD_GROUP = 4
NT_DIMS = (((1,), (1,)), ((), ()))


def _params(*sem):
    return pltpu.CompilerParams(dimension_semantics=sem, vmem_limit_bytes=V7X_VMEM_LIMIT)


def _sigmoid(x):
    return 1.0 / (1.0 + jnp.exp(-x))


def _adaln_kernel(ct_ref, w_ref, b_ref, o_ref):
    ct = ct_ref[...]
    act = ct * _sigmoid(ct)
    w = w_ref[...]
    for b in range(ct.shape[1]):
        o_ref[b:b + 1, :] = jnp.sum(act[:, b:b + 1] * w, axis=0, keepdims=True) + b_ref[...]


def _adaln(c, w, bias):
    B, D = c.shape
    N = w.shape[1]
    tn = min(N, 1024)
    return pl.pallas_call(
        _adaln_kernel,
        grid=(N // tn,),
        in_specs=[pl.BlockSpec((D, B), lambda j: (0, 0)),
                  pl.BlockSpec((D, tn), lambda j: (0, j)),
                  pl.BlockSpec((1, tn), lambda j: (0, j))],
        out_specs=pl.BlockSpec((B, tn), lambda j: (0, j)),
        out_shape=jax.ShapeDtypeStruct((B, N), F32),
        compiler_params=_params("arbitrary"),
        name="adaln",
    )(c.T, w, bias.reshape(1, N))


def _rms_mod(x, g, sc, sh):
    ms = jnp.mean(x * x, axis=-1, keepdims=True)
    y = x * lax.rsqrt(ms + EPS) * g
    return y * (1.0 + sc) + sh


def _norm_mod_kernel(x_ref, g_ref, sc_ref, sh_ref, o_ref):
    o_ref[...] = _rms_mod(x_ref[...], g_ref[...], sc_ref[0], sh_ref[0]).astype(o_ref.dtype)


def _norm_mod(x2, g, sc, sh, S):
    M, D = x2.shape
    B = sc.shape[0]
    tm = min(S, 512)
    per_b = S // tm
    return pl.pallas_call(
        _norm_mod_kernel,
        grid=(M // tm,),
        in_specs=[pl.BlockSpec((tm, D), lambda i: (i, 0)),
                  pl.BlockSpec((1, D), lambda i: (0, 0)),
                  pl.BlockSpec((1, 1, D), lambda i: (i // per_b, 0, 0)),
                  pl.BlockSpec((1, 1, D), lambda i: (i // per_b, 0, 0))],
        out_specs=pl.BlockSpec((tm, D), lambda i: (i, 0)),
        out_shape=jax.ShapeDtypeStruct((M, D), BF16),
        compiler_params=_params("arbitrary"),
        name="norm_mod",
    )(x2, g.reshape(1, D), sc.reshape(B, 1, D), sh.reshape(B, 1, D))


def _in_proj_kernel(a_ref, b_ref, o_ref, *, scaled_tiles, scale):
    j = pl.program_id(0)
    acc = jnp.dot(a_ref[...], b_ref[...], preferred_element_type=F32)
    is_scaled = functools.reduce(jnp.logical_or, [j == t for t in scaled_tiles], False)
    o_ref[...] = (acc * jnp.where(is_scaled, scale, 1.0)).astype(o_ref.dtype)


def _in_proj(h, w, tn, scaled_tiles, scale, name):
    M, K = h.shape
    N = w.shape[1]
    tm = min(M, 1024)
    return pl.pallas_call(
        functools.partial(_in_proj_kernel, scaled_tiles=scaled_tiles, scale=scale),
        grid=(N // tn, M // tm),
        in_specs=[pl.BlockSpec((tm, K), lambda j, i: (i, 0)),
                  pl.BlockSpec((K, tn), lambda j, i: (0, j))],
        out_specs=pl.BlockSpec((tm, tn), lambda j, i: (i, j)),
        out_shape=jax.ShapeDtypeStruct((M, N), BF16),
        compiler_params=_params("arbitrary", "arbitrary"),
        name=name,
    )(h, w)


def _key_query_iotas(k0, q0, tk, tq):
    key_pos = k0 + lax.broadcasted_iota(jnp.int32, (tk, tq), 0)
    query_pos = q0 + lax.broadcasted_iota(jnp.int32, (tk, tq), 1)
    return key_pos, query_pos


def _head_cols(g):
    return slice(g * HEAD_DIM, (g + 1) * HEAD_DIM)


def _sb_kernel(q_ref, k_ref, vt_ref, o_ref, run_ref, acc_ref, *, tq, tk, n_heads):
    qi = pl.program_id(2)
    q0 = qi * tq
    run_ref[...] = jnp.zeros_like(run_ref)
    acc_ref[...] = jnp.zeros_like(acc_ref)
    later = (lax.broadcasted_iota(jnp.int32, (tk, tk), 1)
             > lax.broadcasted_iota(jnp.int32, (tk, tk), 0)).astype(BF16)

    def body(carry):
        kb, _ = carry
        k0 = pl.multiple_of(kb * tk, tk)
        key_pos, query_pos = _key_query_iotas(k0, q0, tk, tq)
        causal = key_pos < query_pos
        heads = range(n_heads)
        zs = [lax.dot_general(k_ref[pl.ds(k0, tk), _head_cols(g)], q_ref[:, _head_cols(g)],
                              NT_DIMS, preferred_element_type=F32) for g in heads]
        sps = [jnp.maximum(z, 0.0) + jnp.log(1.0 + jnp.exp(-jnp.abs(z))) for z in zs]
        spms = [jnp.where(causal, sp, 0.0) for sp in sps]
        his = [spm.astype(BF16) for spm in spms]
        los = [(spm - hi.astype(F32)).astype(BF16) for spm, hi in zip(spms, his)]
        suffixes = [jnp.dot(later, hi, preferred_element_type=F32)
                    + jnp.dot(later, lo, preferred_element_type=F32) for hi, lo in zip(his, los)]
        runs = [run_ref[g] for g in heads]
        weights = [jnp.where(causal, jnp.exp(zs[g] - sps[g] - suffixes[g] - runs[g]), 0.0).astype(BF16)
                   for g in heads]
        min_run = None
        for g in heads:
            vt = vt_ref[_head_cols(g), pl.ds(k0, tk)]
            acc_ref[g] += jnp.dot(vt, weights[g], preferred_element_type=F32)
            run_new = runs[g] + jnp.sum(spms[g], axis=0, keepdims=True)
            run_ref[g] = run_new
            head_min = jnp.min(run_new)
            min_run = head_min if min_run is None else jnp.minimum(min_run, head_min)
        return kb - 1, min_run > EXP_ZERO_BELOW

    def cond(carry):
        kb, dead = carry
        return jnp.logical_and(kb >= 0, jnp.logical_not(dead))

    lax.while_loop(cond, body, ((q0 + tq - 1) // tk, jnp.bool_(False)))
    for g in range(n_heads):
        o_ref[:, _head_cols(g)] = acc_ref[g].T.astype(o_ref.dtype)


def _attn_group_specs(S, tq, nq, G, q_blk, k_blk):
    assert q_blk % G == 0 and k_blk % G == 0
    gw = G * HEAD_DIM
    return [pl.BlockSpec((tq, gw), lambda b, hg, i: (b * nq + i, q_blk // G + hg)),
            pl.BlockSpec((S, gw), lambda b, hg, i: (b, k_blk // G + hg)),
            pl.BlockSpec((gw, S), lambda b, hg, i: (hg, b))]


def _sb_attn(z, vt, B, S, H, q_blk, k_blk):
    M = z.shape[0]
    tq = tk = min(S, ATTN_BLOCK)
    nq = S // tq
    G = min(H, HEAD_GROUP)
    gw = G * HEAD_DIM
    return pl.pallas_call(
        functools.partial(_sb_kernel, tq=tq, tk=tk, n_heads=G),
        grid=(B, H // G, nq),
        in_specs=_attn_group_specs(S, tq, nq, G, q_blk, k_blk),
        out_specs=pl.BlockSpec((tq, gw), lambda b, hg, i: (b * nq + i, hg)),
        out_shape=jax.ShapeDtypeStruct((M, H * HEAD_DIM), BF16),
        scratch_shapes=[pltpu.VMEM((G, 1, tq), F32), pltpu.VMEM((G, HEAD_DIM, tq), F32)],
        compiler_params=_params("arbitrary", "arbitrary", "arbitrary"),
        name="sb_attn",
    )(z, z, vt)


def _dsa_select_kernel(q_ref, k_ref, w_ref, mask_ref, key_ref, *, tq, tk, n_sel, n_kblocks):
    qi = pl.program_id(1)
    q0 = qi * tq
    nkb = (q0 + tq + tk - 1) // tk
    lane = lax.broadcasted_iota(jnp.int32, (tk, 2 * IDX_DIM), 1)
    w_scale = (N_IDX_HEADS ** -0.5) * (IDX_DIM ** -0.5)
    wt = (w_ref[...].astype(F32) * w_scale).T

    def score_block(kb, carry):
        k0 = pl.multiple_of(kb * tk, tk)
        kk = k_ref[pl.ds(k0, tk), :]
        k_halves = (jnp.where(lane < IDX_DIM, kk, jnp.zeros_like(kk)),
                    jnp.where(lane >= IDX_DIM, kk, jnp.zeros_like(kk)))
        acc = jnp.zeros((tk, tq), F32)
        for p in range(N_IDX_HEADS // 2):
            q2 = q_ref[:, p * 2 * IDX_DIM:(p + 1) * 2 * IDX_DIM]
            for half in range(2):
                h = 2 * p + half
                ph = lax.dot_general(k_halves[half], q2, NT_DIMS, preferred_element_type=F32)
                acc = acc + jnp.maximum(ph, 0.0) * wt[h:h + 1, :]
        bits = lax.bitcast_convert_type(acc, jnp.int32)
        key = bits ^ ((bits >> 31) & 0x7FFFFFFF)
        key_pos, query_pos = _key_query_iotas(k0, q0, tk, tq)
        key_ref[pl.ds(k0, tk), :] = jnp.where(key_pos <= query_pos, key, INT_MIN)
        return carry

    lax.fori_loop(0, nkb, score_block, 0)

    def count(pred_fn, thr):
        def blk(kb, cnt):
            k0 = pl.multiple_of(kb * tk, tk)
            ones = jnp.where(pred_fn(key_ref[pl.ds(k0, tk), :], thr), 1, 0)
            return cnt + jnp.sum(ones.reshape(tk // SUBLANES, SUBLANES, tq), axis=0)
        cnt = lax.fori_loop(0, nkb, blk, jnp.zeros((SUBLANES, tq), jnp.int32))
        return jnp.sum(cnt, axis=0, keepdims=True)

    def bisect(it, thr):
        cand = thr + jnp.left_shift(jnp.int32(1), 31 - it)
        cnt = count(lambda key, c: key >= c, cand)
        return jnp.where(cnt >= n_sel, cand, thr)

    thr = lax.fori_loop(0, 32, bisect, jnp.full((1, tq), INT_MIN, jnp.int32))
    need = (n_sel - count(lambda key, c: key > c, thr)).astype(F32)

    earlier = (lax.broadcasted_iota(jnp.int32, (tk, tk), 1)
               < lax.broadcasted_iota(jnp.int32, (tk, tk), 0)).astype(BF16)

    def emit(kb, ties_seen):
        k0 = pl.multiple_of(kb * tk, tk)
        key = key_ref[pl.ds(k0, tk), :]
        eq = jnp.where(key == thr, 1.0, 0.0)
        rank = ties_seen + jnp.dot(earlier, eq.astype(BF16), preferred_element_type=F32)
        tie_ok = jnp.where(rank < need, eq, 0.0)
        sel = jnp.where(key > thr, 1.0, tie_ok)
        key_pos, query_pos = _key_query_iotas(k0, q0, tk, tq)
        sel = jnp.where(key_pos <= query_pos, sel, 0.0)
        mask_ref[0, pl.ds(k0, tk), :] = jnp.where(sel > 0.5, 0.0, NEG).astype(mask_ref.dtype)
        return ties_seen + jnp.sum(eq, axis=0, keepdims=True)

    lax.fori_loop(0, nkb, emit, jnp.zeros((1, tq), F32))

    def fill(kb, carry):
        k0 = pl.multiple_of(kb * tk, tk)
        mask_ref[0, pl.ds(k0, tk), :] = jnp.full((tk, tq), NEG, mask_ref.dtype)
        return carry

    lax.fori_loop(nkb, n_kblocks, fill, 0)


def _dsa_select(z, zs, B, S, qix_blk, n_sel):
    tq = tk = min(S, ATTN_BLOCK)
    nq = S // tq
    qw = N_IDX_HEADS * IDX_DIM
    return pl.pallas_call(
        functools.partial(_dsa_select_kernel, tq=tq, tk=tk, n_sel=n_sel, n_kblocks=S // tk),
        grid=(B, nq),
        in_specs=[pl.BlockSpec((tq, qw), lambda b, i: (b * nq + i, qix_blk)),
                  pl.BlockSpec((S, 2 * IDX_DIM), lambda b, i: (b, 0)),
                  pl.BlockSpec((tq, 2 * IDX_DIM), lambda b, i: (b * nq + i, 1))],
        out_specs=pl.BlockSpec((1, S, tq), lambda b, i: (b * nq + i, 0, 0)),
        out_shape=jax.ShapeDtypeStruct((B * nq, S, tq), BF16),
        scratch_shapes=[pltpu.VMEM((S, tq), jnp.int32)],
        compiler_params=_params("arbitrary", "arbitrary"),
        name="dsa_select",
    )(z, zs, zs)


def _rel_bucket(dist):
    n = jnp.maximum(dist, 0)
    max_exact = N_BUCKETS // 2
    nf = jnp.maximum(n, 1).astype(F32)
    large = max_exact + (jnp.log(nf / max_exact) / math.log(MAX_DISTANCE / max_exact)
                         * (N_BUCKETS - max_exact)).astype(jnp.int32)
    large = jnp.minimum(large, N_BUCKETS - 1)
    return jnp.where(n < max_exact, n, large)


def _bias_tiles(rel_bias, tq):
    assert tq >= MAX_DISTANCE
    H = rel_bias.shape[1]
    period = 2 * tq
    rb = rel_bias.astype(F32)
    j = jnp.arange(period)
    dist = jnp.stack([jnp.where(j < tq, j, 0), jnp.where(j < tq, tq + j, j - tq)])
    seq = (rb[_rel_bucket(dist)] - rb[N_BUCKETS - 1]).transpose(2, 0, 1)
    rows = jnp.tile(seq, (1, 1, tq))[:, :, :tq * (period - 1)].reshape(H, 2, tq, period - 1)
    return rows[:, :, :, :tq]


def _dsa_attn_kernel(q_ref, k_ref, vt_ref, mask_ref, bias_ref, o_ref, m_ref, acc_ref, *, tq, n_heads):
    qi = pl.program_id(2)
    m_ref[...] = jnp.full_like(m_ref, NEG)
    acc_ref[...] = jnp.zeros_like(acc_ref)
    ones_rows = jnp.ones((BF16_ROWS, tq), BF16)

    def step(kb, bias_idx):
        k0 = pl.multiple_of(kb * tq, tq)
        maskf = mask_ref[0, pl.ds(k0, tq), :].astype(F32)
        heads = range(n_heads)
        scores = [lax.dot_general(k_ref[pl.ds(k0, tq), _head_cols(g)], q_ref[:, _head_cols(g)],
                                  NT_DIMS, preferred_element_type=F32) for g in heads]
        probs, alphas = [], []
        for g in heads:
            s = scores[g] + maskf
            if bias_idx is not None:
                s = s + bias_ref[g, bias_idx]
            m_old = m_ref[g]
            m_new = jnp.maximum(m_old, jnp.max(s, axis=0, keepdims=True))
            alphas.append(jnp.exp(m_old - m_new))
            probs.append(jnp.exp(s - m_new).astype(BF16))
            m_ref[g] = m_new
        for g in heads:
            vt = jnp.concatenate([vt_ref[_head_cols(g), pl.ds(k0, tq)], ones_rows], axis=0)
            acc_ref[g] = alphas[g] * acc_ref[g] + jnp.dot(vt, probs[g], preferred_element_type=F32)

    def far(kb, carry):
        step(kb, None)
        return carry

    lax.fori_loop(0, qi - 1, far, 0)

    @pl.when(qi >= 1)
    def _():
        step(qi - 1, 1)

    step(qi, 0)
    for g in range(n_heads):
        acc = acc_ref[g]
        out_t = acc[:HEAD_DIM] / acc[HEAD_DIM:HEAD_DIM + 1]
        o_ref[:, _head_cols(g)] = out_t.T.astype(o_ref.dtype)


def _dsa_attn(z, vt, mask, bias_tiles, B, S, H, q_blk, k_blk):
    M = z.shape[0]
    tq = bias_tiles.shape[-1]
    nq = S // tq
    G = min(H, HEAD_GROUP)
    gw = G * HEAD_DIM
    return pl.pallas_call(
        functools.partial(_dsa_attn_kernel, tq=tq, n_heads=G),
        grid=(B, H // G, nq),
        in_specs=_attn_group_specs(S, tq, nq, G, q_blk, k_blk) + [
            pl.BlockSpec((1, S, tq), lambda b, hg, i: (b * nq + i, 0, 0)),
            pl.BlockSpec((G, 2, tq, tq), lambda b, hg, i: (hg, 0, 0, 0))],
        out_specs=pl.BlockSpec((tq, gw), lambda b, hg, i: (b * nq + i, hg)),
        out_shape=jax.ShapeDtypeStruct((M, H * HEAD_DIM), BF16),
        scratch_shapes=[pltpu.VMEM((G, 1, tq), F32),
                        pltpu.VMEM((G, HEAD_DIM + BF16_ROWS, tq), F32)],
        compiler_params=_params("arbitrary", "arbitrary", "arbitrary"),
        name="dsa_attn",
    )(z, z, vt, mask, bias_tiles)


def _mix_out_kernel(osb_ref, ods_ref, gsb_ref, gds_ref, x_ref, wsb_ref, wds_ref, wout_ref,
                    gt_ref, g_ref, sc_ref, sh_ref, x1_ref, h2_ref):
    t_sb = jnp.dot(osb_ref[...], wsb_ref[...], preferred_element_type=F32)
    t_ds = jnp.dot(ods_ref[...], wds_ref[...], preferred_element_type=F32)
    merged = (_sigmoid(gsb_ref[...].astype(F32)) * t_sb
              + _sigmoid(gds_ref[...].astype(F32)) * t_ds)
    y = jnp.dot(merged.astype(BF16), wout_ref[...], preferred_element_type=F32)
    x1 = x_ref[...] + gt_ref[0] * y
    x1_ref[...] = x1
    h2_ref[...] = _rms_mod(x1, g_ref[...], sc_ref[0], sh_ref[0]).astype(h2_ref.dtype)


def _mix_out(o_sb, o_ds, z, gate_blk, x2, w_sb, w_ds, w_out, gt, g, sc, sh, S):
    M, D = x2.shape
    B = gt.shape[0]
    W = o_sb.shape[1]
    tm = min(S, 512)
    per_b = S // tm
    row = lambda i: (i, 0)
    const = lambda i: (0, 0)
    per_batch = lambda i: (i // per_b, 0, 0)
    return pl.pallas_call(
        _mix_out_kernel,
        grid=(M // tm,),
        in_specs=[pl.BlockSpec((tm, W), row),
                  pl.BlockSpec((tm, W), row),
                  pl.BlockSpec((tm, D), lambda i: (i, gate_blk)),
                  pl.BlockSpec((tm, D), lambda i: (i, gate_blk + 1)),
                  pl.BlockSpec((tm, D), row),
                  pl.BlockSpec((W, D), const),
                  pl.BlockSpec((W, D), const),
                  pl.BlockSpec((D, D), const),
                  pl.BlockSpec((1, 1, D), per_batch),
                  pl.BlockSpec((1, D), const),
                  pl.BlockSpec((1, 1, D), per_batch),
                  pl.BlockSpec((1, 1, D), per_batch)],
        out_specs=[pl.BlockSpec((tm, D), row), pl.BlockSpec((tm, D), row)],
        out_shape=[jax.ShapeDtypeStruct((M, D), F32), jax.ShapeDtypeStruct((M, D), BF16)],
        compiler_params=_params("arbitrary"),
        name="mix_out",
    )(o_sb, o_ds, z, z, x2, w_sb, w_ds, w_out, gt.reshape(B, 1, D), g.reshape(1, D),
      sc.reshape(B, 1, D), sh.reshape(B, 1, D))


HALO = BF16_ROWS


def _conv_ffn_kernel(h_ref, halo_ref, x1_ref, wg_ref, wu_ref, wd_ref, cw_ref, cb_ref, gt_ref,
                     gf_ref, o_ref, acc_ref, *, tiles_per_seq):
    i = pl.program_id(0)
    f = pl.program_id(1)

    @pl.when(f == 0)
    def _():
        acc_ref[...] = jnp.zeros_like(acc_ref)

    h = h_ref[...]
    wg = wg_ref[...]
    g0 = jnp.dot(h, wg, preferred_element_type=F32)
    g_prev = jnp.dot(halo_ref[...], wg, preferred_element_type=F32)
    g_prev = jnp.where(i % tiles_per_seq == 0, 0.0, g_prev)
    ridx = lax.broadcasted_iota(jnp.int32, g0.shape, 0)
    g1 = jnp.where(ridx == 0, g_prev[HALO - 1:HALO, :], pltpu.roll(g0, 1, 0))
    g2 = jnp.where(ridx == 0, g_prev[HALO - 2:HALO - 1, :],
                   jnp.where(ridx == 1, g_prev[HALO - 1:HALO, :], pltpu.roll(g0, 2, 0)))
    cw = cw_ref[...]
    a = cb_ref[...] + g2 * cw[0:1, :] + g1 * cw[1:2, :] + g0 * cw[2:3, :]
    u = jnp.dot(h, wu_ref[...], preferred_element_type=F32)
    act = (a * _sigmoid(a) * u).astype(BF16)
    acc_ref[...] += jnp.dot(act, wd_ref[...], preferred_element_type=F32)

    @pl.when(f == pl.num_programs(1) - 1)
    def _():
        x = x1_ref[...] + gt_ref[0] * acc_ref[...]
        ms = jnp.mean(x * x, axis=-1, keepdims=True)
        o_ref[...] = x * lax.rsqrt(ms + EPS) * gf_ref[...]


def _conv_ffn(h2, x1, w_gate, w_up, w_down, conv_w, conv_b, gt, g_final, S):
    M, D = x1.shape
    B = gt.shape[0]
    F = w_gate.shape[1]
    tm = min(S, 512)
    tf = min(F, 512)
    per_b = S // tm
    halo_per_tile = tm // HALO
    return pl.pallas_call(
        functools.partial(_conv_ffn_kernel, tiles_per_seq=per_b),
        grid=(M // tm, F // tf),
        in_specs=[pl.BlockSpec((tm, D), lambda i, f: (i, 0)),
                  pl.BlockSpec((HALO, D), lambda i, f: (jnp.maximum(i * halo_per_tile - 1, 0), 0)),
                  pl.BlockSpec((tm, D), lambda i, f: (i, 0)),
                  pl.BlockSpec((D, tf), lambda i, f: (0, f)),
                  pl.BlockSpec((D, tf), lambda i, f: (0, f)),
                  pl.BlockSpec((tf, D), lambda i, f: (f, 0)),
                  pl.BlockSpec((CONV_WIDTH, tf), lambda i, f: (0, f)),
                  pl.BlockSpec((1, tf), lambda i, f: (0, f)),
                  pl.BlockSpec((1, 1, D), lambda i, f: (i // per_b, 0, 0)),
                  pl.BlockSpec((1, D), lambda i, f: (0, 0))],
        out_specs=pl.BlockSpec((tm, D), lambda i, f: (i, 0)),
        out_shape=jax.ShapeDtypeStruct((M, D), F32),
        scratch_shapes=[pltpu.VMEM((tm, D), F32)],
        compiler_params=_params("arbitrary", "arbitrary"),
        name="conv_ffn",
    )(h2, h2, x1, w_gate, w_up, w_down, conv_w, conv_b.reshape(1, F), gt.reshape(B, 1, D),
      g_final.reshape(1, D))


def kernel(x, c, w_ada, b_ada, g_mix, w_in, w_o_sb, w_o_dsa, w_out, rel_bias, g_ffn, w_gate,
           w_up, conv_w, conv_b, w_down, g_final):
    B, S, D = x.shape
    depth = w_ada.shape[0]
    W = w_o_sb.shape[1]
    H = W // HEAD_DIM
    assert w_o_dsa.shape[1] == W and D % W == 0 and S % ATTN_BLOCK == 0
    n_sel = min(TOPK_MAX, S // 4)
    qw = N_IDX_HEADS * IDX_DIM
    scale = HEAD_DIM ** -0.5
    x2 = x.reshape(B * S, D)

    for l in range(depth):
        mod = _adaln(c, w_ada[l], b_ada[l])
        sh1, sc1, gt1, sh2, sc2, gt2 = jnp.split(mod, 6, axis=-1)

        wl = w_in[l]
        o_qix = 6 * W
        o_kix = o_qix + qw
        o_wix = o_kix + IDX_DIM
        o_gate = o_wix + N_IDX_HEADS
        w_main = jnp.concatenate([wl[:, :o_qix], wl[:, o_gate:], wl[:, o_qix:o_kix]], axis=1).astype(BF16)
        k_cols = wl[:, o_kix:o_wix]
        w_tail = jnp.concatenate(
            [k_cols, k_cols, wl[:, o_wix:o_gate],
             jnp.zeros((D, 2 * IDX_DIM - N_IDX_HEADS), wl.dtype)], axis=1).astype(BF16)

        h1 = _norm_mod(x2, g_mix[l], sc1, sh1, S)
        z = _in_proj(h1, w_main, W, (0, 3), scale, "in_proj")
        zs = _in_proj(h1, w_tail, 4 * IDX_DIM, (), 1.0, "in_proj_idx")

        o_sb = _sb_attn(z, z[:, 2 * W:3 * W].T, B, S, H, 0, H)
        mask = _dsa_select(z, zs, B, S, (6 * W + 2 * D) // qw, n_sel)
        o_ds = _dsa_attn(z, z[:, 5 * W:6 * W].T, mask, _bias_tiles(rel_bias, min(S, ATTN_BLOCK)),
                         B, S, H, 3 * H, 4 * H)

        x2, h2 = _mix_out(o_sb, o_ds, z, 6 * W // D, x2, w_o_sb[l].astype(BF16),
                          w_o_dsa[l].astype(BF16), w_out[l].astype(BF16), gt1, g_ffn[l], sc2, sh2, S)
        last = l == depth - 1
        assert last, "the final rms_norm is fused into the last layer's FFN"
        x2 = _conv_ffn(h2, x2, w_gate[l].astype(BF16), w_up[l].astype(BF16), w_down[l].astype(BF16),
                       conv_w[l], conv_b[l], gt2, g_final, S)
    return x2.reshape(B, S, D)
```

```python
import functools
import math

import jax
import jax.numpy as jnp
from jax import lax
from jax.experimental import pallas as pl
from jax.experimental.pallas import tpu as pltpu

HEAD_DIM = 128
N_IDX_HEADS = 16
IDX_DIM = 64
TOPK_MAX = 256
N_BUCKETS = 32
MAX_DISTANCE = 128
CONV_WIDTH = 3
EPS = 1e-6

F32 = jnp.float32
BF16 = jnp.bfloat16
NEG = -1e30
I16 = jnp.int16
INT_MIN = -2 ** 31
I16_MIN = -2 ** 15
EXP_ZERO_BELOW = 104.0
LOG2_E = math.log2(math.e)
V7X_VMEM_LIMIT = 56 * 1024 * 1024
SUBLANES = 8
BF16_ROWS = 16
ATTN_BLOCK = 256
HEAD_GROUP = 4
COUNT_BLOCKS = 4
COUNT_CHUNK = 64
NT_DIMS = (((1,), (1,)), ((), ()))


def _params(*sem):
    return pltpu.CompilerParams(dimension_semantics=sem, vmem_limit_bytes=V7X_VMEM_LIMIT)


def _sigmoid(x):
    return 1.0 / (1.0 + jnp.exp(-x))


def _adaln_kernel(ct_ref, w_ref, b_ref, o_ref):
    ct = ct_ref[...]
    act = ct * _sigmoid(ct)
    w = w_ref[...]
    for b in range(ct.shape[1]):
        o_ref[b:b + 1, :] = jnp.sum(act[:, b:b + 1] * w, axis=0, keepdims=True) + b_ref[...]


def _adaln(c, w, bias):
    B, D = c.shape
    N = w.shape[1]
    tn = min(N, 1024)
    return pl.pallas_call(
        _adaln_kernel,
        grid=(N // tn,),
        in_specs=[pl.BlockSpec((D, B), lambda j: (0, 0)),
                  pl.BlockSpec((D, tn), lambda j: (0, j)),
                  pl.BlockSpec((1, tn), lambda j: (0, j))],
        out_specs=pl.BlockSpec((B, tn), lambda j: (0, j)),
        out_shape=jax.ShapeDtypeStruct((B, N), F32),
        compiler_params=_params("arbitrary"),
        name="adaln",
    )(c.T, w, bias.reshape(1, N))


def _rms_mod(x, g, sc, sh):
    ms = jnp.mean(x * x, axis=-1, keepdims=True)
    y = x * lax.rsqrt(ms + EPS) * g
    return y * (1.0 + sc) + sh


def _norm_mod_kernel(x_ref, g_ref, sc_ref, sh_ref, o_ref):
    o_ref[...] = _rms_mod(x_ref[...], g_ref[...], sc_ref[0], sh_ref[0]).astype(o_ref.dtype)


def _norm_mod(x2, g, sc, sh, S):
    M, D = x2.shape
    B = sc.shape[0]
    tm = min(S, 512)
    per_b = S // tm
    return pl.pallas_call(
        _norm_mod_kernel,
        grid=(M // tm,),
        in_specs=[pl.BlockSpec((tm, D), lambda i: (i, 0)),
                  pl.BlockSpec((1, D), lambda i: (0, 0)),
                  pl.BlockSpec((1, 1, D), lambda i: (i // per_b, 0, 0)),
                  pl.BlockSpec((1, 1, D), lambda i: (i // per_b, 0, 0))],
        out_specs=pl.BlockSpec((tm, D), lambda i: (i, 0)),
        out_shape=jax.ShapeDtypeStruct((M, D), BF16),
        compiler_params=_params("arbitrary"),
        name="norm_mod",
    )(x2, g.reshape(1, D), sc.reshape(B, 1, D), sh.reshape(B, 1, D))


def _in_proj_kernel(a_ref, b_ref, o_ref, *, tile_scales):
    j = pl.program_id(0)
    acc = jnp.dot(a_ref[...], b_ref[...], preferred_element_type=F32)
    scale = jnp.float32(1.0)
    for tile, tile_scale in tile_scales:
        scale = jnp.where(j == tile, tile_scale, scale)
    o_ref[...] = (acc * scale).astype(o_ref.dtype)


def _in_proj(h, w, tn, tile_scales, name):
    M, K = h.shape
    N = w.shape[1]
    tm = min(M, 1024)
    return pl.pallas_call(
        functools.partial(_in_proj_kernel, tile_scales=tile_scales),
        grid=(N // tn, M // tm),
        in_specs=[pl.BlockSpec((tm, K), lambda j, i: (i, 0)),
                  pl.BlockSpec((K, tn), lambda j, i: (0, j))],
        out_specs=pl.BlockSpec((tm, tn), lambda j, i: (i, j)),
        out_shape=jax.ShapeDtypeStruct((M, N), BF16),
        compiler_params=_params("arbitrary", "arbitrary"),
        name=name,
    )(h, w)


def _key_query_iotas(k0, q0, tk, tq):
    key_pos = k0 + lax.broadcasted_iota(jnp.int32, (tk, tq), 0)
    query_pos = q0 + lax.broadcasted_iota(jnp.int32, (tk, tq), 1)
    return key_pos, query_pos


def _head_cols(g):
    return slice(g * HEAD_DIM, (g + 1) * HEAD_DIM)


def _sb_kernel(q_ref, k_ref, vt_ref, o_ref, run_ref, acc_ref, *, tq, tk, n_heads):
    qi = pl.program_id(2)
    q0 = qi * tq
    run_ref[...] = jnp.zeros_like(run_ref)
    acc_ref[...] = jnp.zeros_like(acc_ref)
    later = (lax.broadcasted_iota(jnp.int32, (tk, tk), 1)
             > lax.broadcasted_iota(jnp.int32, (tk, tk), 0)).astype(BF16)

    def body(carry):
        kb, _ = carry
        k0 = pl.multiple_of(kb * tk, tk)
        key_pos, query_pos = _key_query_iotas(k0, q0, tk, tq)
        causal = key_pos < query_pos
        heads = range(n_heads)
        zs = [lax.dot_general(k_ref[pl.ds(k0, tk), _head_cols(g)], q_ref[:, _head_cols(g)],
                              NT_DIMS, preferred_element_type=F32) for g in heads]
        sps = [jnp.maximum(z, 0.0) + jnp.log(1.0 + jnp.exp(-jnp.abs(z))) for z in zs]
        spms = [jnp.where(causal, sp, 0.0) for sp in sps]
        his = [spm.astype(BF16) for spm in spms]
        los = [(spm - hi.astype(F32)).astype(BF16) for spm, hi in zip(spms, his)]
        suffixes = [jnp.dot(later, hi, preferred_element_type=F32)
                    + jnp.dot(later, lo, preferred_element_type=F32) for hi, lo in zip(his, los)]
        runs = [run_ref[g] for g in heads]
        weights = [jnp.where(causal, jnp.exp(zs[g] - sps[g] - suffixes[g] - runs[g]), 0.0).astype(BF16)
                   for g in heads]
        min_run = None
        for g in heads:
            vt = vt_ref[_head_cols(g), pl.ds(k0, tk)]
            acc_ref[g] += jnp.dot(vt, weights[g], preferred_element_type=F32)
            run_new = runs[g] + jnp.sum(spms[g], axis=0, keepdims=True)
            run_ref[g] = run_new
            head_min = jnp.min(run_new)
            min_run = head_min if min_run is None else jnp.minimum(min_run, head_min)
        return kb - 1, min_run > EXP_ZERO_BELOW

    def cond(carry):
        kb, dead = carry
        return jnp.logical_and(kb >= 0, jnp.logical_not(dead))

    lax.while_loop(cond, body, ((q0 + tq - 1) // tk, jnp.bool_(False)))
    for g in range(n_heads):
        o_ref[:, _head_cols(g)] = acc_ref[g].T.astype(o_ref.dtype)


def _attn_group_specs(S, tq, nq, G, q_blk, k_blk):
    assert q_blk % G == 0 and k_blk % G == 0
    gw = G * HEAD_DIM
    return [pl.BlockSpec((tq, gw), lambda b, hg, i: (b * nq + i, q_blk // G + hg)),
            pl.BlockSpec((S, gw), lambda b, hg, i: (b, k_blk // G + hg)),
            pl.BlockSpec((gw, S), lambda b, hg, i: (hg, b))]


def _sb_attn(z, vt, B, S, H, q_blk, k_blk):
    M = z.shape[0]
    tq = tk = min(S, ATTN_BLOCK)
    nq = S // tq
    G = min(H, HEAD_GROUP)
    gw = G * HEAD_DIM
    return pl.pallas_call(
        functools.partial(_sb_kernel, tq=tq, tk=tk, n_heads=G),
        grid=(B, H // G, nq),
        in_specs=_attn_group_specs(S, tq, nq, G, q_blk, k_blk),
        out_specs=pl.BlockSpec((tq, gw), lambda b, hg, i: (b * nq + i, hg)),
        out_shape=jax.ShapeDtypeStruct((M, H * HEAD_DIM), BF16),
        scratch_shapes=[pltpu.VMEM((G, 1, tq), F32), pltpu.VMEM((G, HEAD_DIM, tq), F32)],
        compiler_params=_params("arbitrary", "arbitrary", "arbitrary"),
        name="sb_attn",
    )(z, z, vt)


def _dsa_select_kernel(q_ref, k_ref, w_ref, mask_ref, hi_ref, lo_ref, *, tq, tk, n_sel, n_kblocks):
    qi = pl.program_id(1)
    q0 = qi * tq
    nkb = (q0 + tq + tk - 1) // tk
    lane = lax.broadcasted_iota(jnp.int32, (tk, 2 * IDX_DIM), 1)
    w_scale = (N_IDX_HEADS ** -0.5) * (IDX_DIM ** -0.5)
    wt = (w_ref[...].astype(F32) * w_scale).T

    def score_block(kb, carry):
        k0 = pl.multiple_of(kb * tk, tk)
        kk = k_ref[pl.ds(k0, tk), :]
        k_halves = (jnp.where(lane < IDX_DIM, kk, jnp.zeros_like(kk)),
                    jnp.where(lane >= IDX_DIM, kk, jnp.zeros_like(kk)))
        acc = jnp.zeros((tk, tq), F32)
        for p in range(N_IDX_HEADS // 2):
            q2 = q_ref[:, p * 2 * IDX_DIM:(p + 1) * 2 * IDX_DIM]
            for half in range(2):
                h = 2 * p + half
                ph = lax.dot_general(k_halves[half], q2, NT_DIMS, preferred_element_type=F32)
                acc = acc + jnp.maximum(ph, 0.0) * wt[h:h + 1, :]
        bits = lax.bitcast_convert_type(acc, jnp.int32)
        key = bits ^ ((bits >> 31) & 0x7FFFFFFF)
        key_pos, query_pos = _key_query_iotas(k0, q0, tk, tq)
        key = jnp.where(key_pos <= query_pos, key, INT_MIN)
        hi_ref[pl.ds(k0, tk), :] = (key >> 16).astype(I16)
        lo_ref[pl.ds(k0, tk), :] = (key ^ 0x8000).astype(I16)
        return carry

    lax.fori_loop(0, nkb, score_block, 0)

    n_count = (nkb + COUNT_BLOCKS - 1) // COUNT_BLOCKS
    rows = COUNT_BLOCKS * tk

    def pad_block(kb, carry):
        k0 = pl.multiple_of(kb * tk, tk)
        hi_ref[pl.ds(k0, tk), :] = jnp.full((tk, tq), I16_MIN, I16)
        lo_ref[pl.ds(k0, tk), :] = jnp.full((tk, tq), I16_MIN, I16)
        return carry

    lax.fori_loop(nkb, n_count * COUNT_BLOCKS, pad_block, 0)

    def count(ref, pred_fn, thr):
        thr16 = thr.astype(I16)

        def blk(i, cnt):
            r0 = pl.multiple_of(i * rows, rows)
            groups = []
            for c in range(rows // COUNT_CHUNK):
                vals = ref[pl.ds(r0 + c * COUNT_CHUNK, COUNT_CHUNK), :]
                ones = jnp.where(pred_fn(vals, thr16), I16(1), I16(0))
                parts = [ones[r * BF16_ROWS:(r + 1) * BF16_ROWS, :]
                         for r in range(COUNT_CHUNK // BF16_ROWS)]
                while len(parts) > 1:
                    parts = [a + b for a, b in zip(parts[0::2], parts[1::2])]
                groups.append(parts[0])
            while len(groups) > 1:
                groups = [a + b for a, b in zip(groups[0::2], groups[1::2])]
            return cnt + groups[0]
        cnt = lax.fori_loop(0, n_count, blk, jnp.zeros((BF16_ROWS, tq), I16))
        return jnp.sum(cnt.astype(jnp.int32), axis=0, keepdims=True)

    def kth_largest(ref, k):
        def bisect(it, thr):
            cand = thr + jnp.left_shift(jnp.int32(1), 15 - it)
            return jnp.where(count(ref, lambda v, c: v >= c, cand) >= k, cand, thr)
        return lax.fori_loop(0, 16, bisect, jnp.full((1, tq), I16_MIN, jnp.int32))

    thr_hi = kth_largest(hi_ref, n_sel)
    need_lo = n_sel - count(hi_ref, lambda v, c: v > c, thr_hi)
    thr_hi16 = thr_hi.astype(I16)

    def park(kb, carry):
        k0 = pl.multiple_of(kb * tk, tk)
        lo_ref[pl.ds(k0, tk), :] = jnp.where(hi_ref[pl.ds(k0, tk), :] == thr_hi16,
                                             lo_ref[pl.ds(k0, tk), :], I16(I16_MIN))
        return carry

    lax.fori_loop(0, nkb, park, 0)
    thr_lo = kth_largest(lo_ref, need_lo)
    need_ties = (need_lo - count(lo_ref, lambda v, c: v > c, thr_lo)).astype(F32)
    thr_lo16 = thr_lo.astype(I16)

    earlier = (lax.broadcasted_iota(jnp.int32, (tk, tk), 1)
               < lax.broadcasted_iota(jnp.int32, (tk, tk), 0)).astype(BF16)
    ones_rows = jnp.ones((BF16_ROWS, tk), BF16)
    one, zero, neg = BF16(1.0), BF16(0.0), BF16(NEG)

    def emit(kb, ties_seen):
        k0 = pl.multiple_of(kb * tk, tk)
        hi = hi_ref[pl.ds(k0, tk), :]
        lo = lo_ref[pl.ds(k0, tk), :]
        above = jnp.where(hi > thr_hi16, one, jnp.where(lo > thr_lo16, one, zero))
        eq = jnp.where(hi == thr_hi16, jnp.where(lo == thr_lo16, one, zero), zero)
        rank = jnp.dot(earlier, eq, preferred_element_type=F32).astype(BF16)
        room = jnp.clip(need_ties - ties_seen, -1.0, float(tk)).astype(BF16)
        sel = jnp.where(rank < room, jnp.maximum(above, eq), above)
        key_pos, query_pos = _key_query_iotas(k0, q0, tk, tq)
        causal = (key_pos <= query_pos).astype(BF16)
        mask_ref[0, pl.ds(k0, tk), :] = jnp.where(sel * causal > zero, zero, neg)
        return ties_seen + jnp.dot(ones_rows, eq, preferred_element_type=F32)[0:1]

    lax.fori_loop(0, nkb, emit, jnp.zeros((1, tq), F32))

    def fill(kb, carry):
        k0 = pl.multiple_of(kb * tk, tk)
        mask_ref[0, pl.ds(k0, tk), :] = jnp.full((tk, tq), NEG, mask_ref.dtype)
        return carry

    lax.fori_loop(nkb, n_kblocks, fill, 0)


def _dsa_select(z, zs, B, S, qix_blk, n_sel):
    tq = tk = min(S, ATTN_BLOCK)
    nq = S // tq
    assert nq % COUNT_BLOCKS == 0
    qw = N_IDX_HEADS * IDX_DIM
    return pl.pallas_call(
        functools.partial(_dsa_select_kernel, tq=tq, tk=tk, n_sel=n_sel, n_kblocks=S // tk),
        grid=(B, nq),
        in_specs=[pl.BlockSpec((tq, qw), lambda b, i: (b * nq + i, qix_blk)),
                  pl.BlockSpec((S, 2 * IDX_DIM), lambda b, i: (b, 0)),
                  pl.BlockSpec((tq, 2 * IDX_DIM), lambda b, i: (b * nq + i, 1))],
        out_specs=pl.BlockSpec((1, S, tq), lambda b, i: (b * nq + i, 0, 0)),
        out_shape=jax.ShapeDtypeStruct((B * nq, S, tq), BF16),
        scratch_shapes=[pltpu.VMEM((S, tq), I16), pltpu.VMEM((S, tq), I16)],
        compiler_params=_params("arbitrary", "arbitrary"),
        name="dsa_select",
    )(z, zs, zs)


def _rel_bucket(dist):
    n = jnp.maximum(dist, 0)
    max_exact = N_BUCKETS // 2
    nf = jnp.maximum(n, 1).astype(F32)
    large = max_exact + (jnp.log(nf / max_exact) / math.log(MAX_DISTANCE / max_exact)
                         * (N_BUCKETS - max_exact)).astype(jnp.int32)
    large = jnp.minimum(large, N_BUCKETS - 1)
    return jnp.where(n < max_exact, n, large)


def _bias_tiles(rel_bias, tq):
    assert tq >= MAX_DISTANCE
    H = rel_bias.shape[1]
    period = 2 * tq
    rb = rel_bias.astype(F32)
    j = jnp.arange(period)
    dist = jnp.stack([jnp.where(j < tq, j, 0), jnp.where(j < tq, tq + j, j - tq)])
    seq = ((rb[_rel_bucket(dist)] - rb[N_BUCKETS - 1]) * LOG2_E).transpose(2, 0, 1)
    rows = jnp.tile(seq, (1, 1, tq))[:, :, :tq * (period - 1)].reshape(H, 2, tq, period - 1)
    return rows[:, :, :, :tq]


def _dsa_attn_kernel(q_ref, k_ref, vt_ref, mask_ref, bias_ref, o_ref, m_ref, acc_ref, *, tq, n_heads):
    qi = pl.program_id(2)
    m_ref[...] = jnp.full_like(m_ref, NEG)
    acc_ref[...] = jnp.zeros_like(acc_ref)
    ones_rows = jnp.ones((BF16_ROWS, tq), BF16)

    def step(blocks):
        heads = range(n_heads)
        k0s = [pl.multiple_of(kb * tq, tq) for kb, _ in blocks]
        scores = [[lax.dot_general(k_ref[pl.ds(k0, tq), _head_cols(g)], q_ref[:, _head_cols(g)],
                                   NT_DIMS, preferred_element_type=F32) for g in heads] for k0 in k0s]
        masks = [mask_ref[0, pl.ds(k0, tq), :].astype(F32) for k0 in k0s]
        probs, alphas = [], []
        for g in heads:
            logits = []
            for j, (_, bias_idx) in enumerate(blocks):
                s = scores[j][g] + masks[j]
                if bias_idx is not None:
                    s = s + bias_ref[g, bias_idx]
                logits.append(s)
            m_old = m_ref[g]
            m_new = m_old
            for s in logits:
                m_new = jnp.maximum(m_new, jnp.max(s, axis=0, keepdims=True))
            alphas.append(jnp.exp2(m_old - m_new))
            probs.append([jnp.exp2(s - m_new).astype(BF16) for s in logits])
            m_ref[g] = m_new
        for g in heads:
            acc = alphas[g] * acc_ref[g]
            for j, k0 in enumerate(k0s):
                vt = jnp.concatenate([vt_ref[_head_cols(g), pl.ds(k0, tq)], ones_rows], axis=0)
                acc = acc + jnp.dot(vt, probs[g][j], preferred_element_type=F32)
            acc_ref[g] = acc

    n_far = jnp.maximum(qi - 1, 0)

    def far_pair(i, carry):
        step([(2 * i, None), (2 * i + 1, None)])
        return carry

    lax.fori_loop(0, n_far // 2, far_pair, 0)

    @pl.when(n_far % 2 == 1)
    def _():
        step([(n_far - 1, None)])

    @pl.when(qi >= 1)
    def _():
        step([(qi - 1, 1), (qi, 0)])

    @pl.when(qi == 0)
    def _():
        step([(qi, 0)])
    for g in range(n_heads):
        acc = acc_ref[g]
        out_t = acc[:HEAD_DIM] / acc[HEAD_DIM:HEAD_DIM + 1]
        o_ref[:, _head_cols(g)] = out_t.T.astype(o_ref.dtype)


def _dsa_attn(z, vt, mask, bias_tiles, B, S, H, q_blk, k_blk):
    M = z.shape[0]
    tq = bias_tiles.shape[-1]
    nq = S // tq
    G = min(H, HEAD_GROUP)
    gw = G * HEAD_DIM
    return pl.pallas_call(
        functools.partial(_dsa_attn_kernel, tq=tq, n_heads=G),
        grid=(B, H // G, nq),
        in_specs=_attn_group_specs(S, tq, nq, G, q_blk, k_blk) + [
            pl.BlockSpec((1, S, tq), lambda b, hg, i: (b * nq + i, 0, 0)),
            pl.BlockSpec((G, 2, tq, tq), lambda b, hg, i: (hg, 0, 0, 0))],
        out_specs=pl.BlockSpec((tq, gw), lambda b, hg, i: (b * nq + i, hg)),
        out_shape=jax.ShapeDtypeStruct((M, H * HEAD_DIM), BF16),
        scratch_shapes=[pltpu.VMEM((G, 1, tq), F32),
                        pltpu.VMEM((G, HEAD_DIM + BF16_ROWS, tq), F32)],
        compiler_params=_params("arbitrary", "arbitrary", "arbitrary"),
        name="dsa_attn",
    )(z, z, vt, mask, bias_tiles)


def _mix_out_kernel(osb_ref, ods_ref, gsb_ref, gds_ref, x_ref, wsb_ref, wds_ref, wout_ref,
                    gt_ref, g_ref, sc_ref, sh_ref, x1_ref, h2_ref):
    t_sb = jnp.dot(osb_ref[...], wsb_ref[...], preferred_element_type=F32)
    t_ds = jnp.dot(ods_ref[...], wds_ref[...], preferred_element_type=F32)
    merged = (_sigmoid(gsb_ref[...].astype(F32)) * t_sb
              + _sigmoid(gds_ref[...].astype(F32)) * t_ds)
    y = jnp.dot(merged.astype(BF16), wout_ref[...], preferred_element_type=F32)
    x1 = x_ref[...] + gt_ref[0] * y
    x1_ref[...] = x1
    h2_ref[...] = _rms_mod(x1, g_ref[...], sc_ref[0], sh_ref[0]).astype(h2_ref.dtype)


def _mix_out(o_sb, o_ds, z, gate_blk, x2, w_sb, w_ds, w_out, gt, g, sc, sh, S):
    M, D = x2.shape
    B = gt.shape[0]
    W = o_sb.shape[1]
    tm = min(S, 512)
    per_b = S // tm
    row = lambda i: (i, 0)
    const = lambda i: (0, 0)
    per_batch = lambda i: (i // per_b, 0, 0)
    return pl.pallas_call(
        _mix_out_kernel,
        grid=(M // tm,),
        in_specs=[pl.BlockSpec((tm, W), row),
                  pl.BlockSpec((tm, W), row),
                  pl.BlockSpec((tm, D), lambda i: (i, gate_blk)),
                  pl.BlockSpec((tm, D), lambda i: (i, gate_blk + 1)),
                  pl.BlockSpec((tm, D), row),
                  pl.BlockSpec((W, D), const),
                  pl.BlockSpec((W, D), const),
                  pl.BlockSpec((D, D), const),
                  pl.BlockSpec((1, 1, D), per_batch),
                  pl.BlockSpec((1, D), const),
                  pl.BlockSpec((1, 1, D), per_batch),
                  pl.BlockSpec((1, 1, D), per_batch)],
        out_specs=[pl.BlockSpec((tm, D), row), pl.BlockSpec((tm, D), row)],
        out_shape=[jax.ShapeDtypeStruct((M, D), F32), jax.ShapeDtypeStruct((M, D), BF16)],
        compiler_params=_params("arbitrary"),
        name="mix_out",
    )(o_sb, o_ds, z, z, x2, w_sb, w_ds, w_out, gt.reshape(B, 1, D), g.reshape(1, D),
      sc.reshape(B, 1, D), sh.reshape(B, 1, D))


HALO = BF16_ROWS


def _conv_ffn_kernel(h_ref, halo_ref, x1_ref, wg_ref, wu_ref, wd_ref, cw_ref, cb_ref, gt_ref,
                     gf_ref, o_ref, acc_ref, *, tiles_per_seq):
    i = pl.program_id(0)
    f = pl.program_id(1)

    @pl.when(f == 0)
    def _():
        acc_ref[...] = jnp.zeros_like(acc_ref)

    h = h_ref[...]
    wg = wg_ref[...]
    g0 = jnp.dot(h, wg, preferred_element_type=F32)
    g_prev = jnp.dot(halo_ref[...], wg, preferred_element_type=F32)
    g_prev = jnp.where(i % tiles_per_seq == 0, 0.0, g_prev)
    ridx = lax.broadcasted_iota(jnp.int32, g0.shape, 0)
    g1 = jnp.where(ridx == 0, g_prev[HALO - 1:HALO, :], pltpu.roll(g0, 1, 0))
    g2 = jnp.where(ridx == 0, g_prev[HALO - 2:HALO - 1, :],
                   jnp.where(ridx == 1, g_prev[HALO - 1:HALO, :], pltpu.roll(g0, 2, 0)))
    cw = cw_ref[...]
    a = cb_ref[...] + g2 * cw[0:1, :] + g1 * cw[1:2, :] + g0 * cw[2:3, :]
    u = jnp.dot(h, wu_ref[...], preferred_element_type=F32)
    act = (a * _sigmoid(a) * u).astype(BF16)
    acc_ref[...] += jnp.dot(act, wd_ref[...], preferred_element_type=F32)

    @pl.when(f == pl.num_programs(1) - 1)
    def _():
        x = x1_ref[...] + gt_ref[0] * acc_ref[...]
        ms = jnp.mean(x * x, axis=-1, keepdims=True)
        o_ref[...] = x * lax.rsqrt(ms + EPS) * gf_ref[...]


def _conv_ffn(h2, x1, w_gate, w_up, w_down, conv_w, conv_b, gt, g_final, S):
    M, D = x1.shape
    B = gt.shape[0]
    F = w_gate.shape[1]
    tm = min(S, 512)
    tf = min(F, 512)
    per_b = S // tm
    halo_per_tile = tm // HALO
    return pl.pallas_call(
        functools.partial(_conv_ffn_kernel, tiles_per_seq=per_b),
        grid=(M // tm, F // tf),
        in_specs=[pl.BlockSpec((tm, D), lambda i, f: (i, 0)),
                  pl.BlockSpec((HALO, D), lambda i, f: (jnp.maximum(i * halo_per_tile - 1, 0), 0)),
                  pl.BlockSpec((tm, D), lambda i, f: (i, 0)),
                  pl.BlockSpec((D, tf), lambda i, f: (0, f)),
                  pl.BlockSpec((D, tf), lambda i, f: (0, f)),
                  pl.BlockSpec((tf, D), lambda i, f: (f, 0)),
                  pl.BlockSpec((CONV_WIDTH, tf), lambda i, f: (0, f)),
                  pl.BlockSpec((1, tf), lambda i, f: (0, f)),
                  pl.BlockSpec((1, 1, D), lambda i, f: (i // per_b, 0, 0)),
                  pl.BlockSpec((1, D), lambda i, f: (0, 0))],
        out_specs=pl.BlockSpec((tm, D), lambda i, f: (i, 0)),
        out_shape=jax.ShapeDtypeStruct((M, D), F32),
        scratch_shapes=[pltpu.VMEM((tm, D), F32)],
        compiler_params=_params("arbitrary", "arbitrary"),
        name="conv_ffn",
    )(h2, h2, x1, w_gate, w_up, w_down, conv_w, conv_b.reshape(1, F), gt.reshape(B, 1, D),
      g_final.reshape(1, D))


def kernel(x, c, w_ada, b_ada, g_mix, w_in, w_o_sb, w_o_dsa, w_out, rel_bias, g_ffn, w_gate,
           w_up, conv_w, conv_b, w_down, g_final):
    B, S, D = x.shape
    depth = w_ada.shape[0]
    W = w_o_sb.shape[1]
    H = W // HEAD_DIM
    assert w_o_dsa.shape[1] == W and D % W == 0 and S % ATTN_BLOCK == 0
    n_sel = min(TOPK_MAX, S // 4)
    qw = N_IDX_HEADS * IDX_DIM
    scale = HEAD_DIM ** -0.5
    x2 = x.reshape(B * S, D)

    for l in range(depth):
        mod = _adaln(c, w_ada[l], b_ada[l])
        sh1, sc1, gt1, sh2, sc2, gt2 = jnp.split(mod, 6, axis=-1)

        wl = w_in[l]
        o_qix = 6 * W
        o_kix = o_qix + qw
        o_wix = o_kix + IDX_DIM
        o_gate = o_wix + N_IDX_HEADS
        w_main = jnp.concatenate([wl[:, :o_qix], wl[:, o_gate:], wl[:, o_qix:o_kix]], axis=1).astype(BF16)
        k_cols = wl[:, o_kix:o_wix]
        w_tail = jnp.concatenate(
            [k_cols, k_cols, wl[:, o_wix:o_gate],
             jnp.zeros((D, 2 * IDX_DIM - N_IDX_HEADS), wl.dtype)], axis=1).astype(BF16)

        h1 = _norm_mod(x2, g_mix[l], sc1, sh1, S)
        z = _in_proj(h1, w_main, W, ((0, scale), (3, scale * LOG2_E)), "in_proj")
        zs = _in_proj(h1, w_tail, 4 * IDX_DIM, (), "in_proj_idx")

        o_sb = _sb_attn(z, z[:, 2 * W:3 * W].T, B, S, H, 0, H)
        mask = _dsa_select(z, zs, B, S, (6 * W + 2 * D) // qw, n_sel)
        o_ds = _dsa_attn(z, z[:, 5 * W:6 * W].T, mask, _bias_tiles(rel_bias, min(S, ATTN_BLOCK)),
                         B, S, H, 3 * H, 4 * H)

        x2, h2 = _mix_out(o_sb, o_ds, z, 6 * W // D, x2, w_o_sb[l].astype(BF16),
                          w_o_dsa[l].astype(BF16), w_out[l].astype(BF16), gt1, g_ffn[l], sc2, sh2, S)
        last = l == depth - 1
        assert last, "the final rms_norm is fused into the last layer's FFN"
        x2 = _conv_ffn(h2, x2, w_gate[l].astype(BF16), w_up[l].astype(BF16), w_down[l].astype(BF16),
                       conv_w[l], conv_b[l], gt2, g_final, S)
    return x2.reshape(B, S, D)
```

```python
import functools
import math

import jax
import jax.numpy as jnp
from jax import lax
from jax.experimental import pallas as pl
from jax.experimental.pallas import tpu as pltpu

HEAD_DIM = 128
N_IDX_HEADS = 16
IDX_DIM = 64
TOPK_MAX = 256
N_BUCKETS = 32
MAX_DISTANCE = 128
CONV_WIDTH = 3
EPS = 1e-6

F32 = jnp.float32
BF16 = jnp.bfloat16
NEG = -1e30
I16 = jnp.int16
INT_MIN = -2 ** 31
I16_MIN = -2 ** 15
EXP_ZERO_BELOW = 104.0
LOG2_E = math.log2(math.e)
V7X_VMEM_LIMIT = 56 * 1024 * 1024
SUBLANES = 8
BF16_ROWS = 16
ATTN_BLOCK = 256
SB_HEAD_GROUP = 4
DSA_HEAD_GROUP = 8
COUNT_BLOCKS = 4
COUNT_CHUNK = 64
NT_DIMS = (((1,), (1,)), ((), ()))


def _params(*sem):
    return pltpu.CompilerParams(dimension_semantics=sem, vmem_limit_bytes=V7X_VMEM_LIMIT)


def _sigmoid(x):
    return 1.0 / (1.0 + jnp.exp(-x))


def _adaln_kernel(ct_ref, w_ref, b_ref, o_ref):
    ct = ct_ref[...]
    act = ct * _sigmoid(ct)
    w = w_ref[...]
    for b in range(ct.shape[1]):
        o_ref[b:b + 1, :] = jnp.sum(act[:, b:b + 1] * w, axis=0, keepdims=True) + b_ref[...]


def _adaln(c, w, bias):
    B, D = c.shape
    N = w.shape[1]
    tn = min(N, 1024)
    return pl.pallas_call(
        _adaln_kernel,
        grid=(N // tn,),
        in_specs=[pl.BlockSpec((D, B), lambda j: (0, 0)),
                  pl.BlockSpec((D, tn), lambda j: (0, j)),
                  pl.BlockSpec((1, tn), lambda j: (0, j))],
        out_specs=pl.BlockSpec((B, tn), lambda j: (0, j)),
        out_shape=jax.ShapeDtypeStruct((B, N), F32),
        compiler_params=_params("arbitrary"),
        name="adaln",
    )(c.T, w, bias.reshape(1, N))


def _rms_mod(x, g, sc, sh):
    ms = jnp.mean(x * x, axis=-1, keepdims=True)
    y = x * lax.rsqrt(ms + EPS) * g
    return y * (1.0 + sc) + sh


def _norm_mod_kernel(x_ref, g_ref, sc_ref, sh_ref, o_ref):
    o_ref[...] = _rms_mod(x_ref[...], g_ref[...], sc_ref[0], sh_ref[0]).astype(o_ref.dtype)


def _norm_mod(x2, g, sc, sh, S):
    M, D = x2.shape
    B = sc.shape[0]
    tm = min(S, 512)
    per_b = S // tm
    return pl.pallas_call(
        _norm_mod_kernel,
        grid=(M // tm,),
        in_specs=[pl.BlockSpec((tm, D), lambda i: (i, 0)),
                  pl.BlockSpec((1, D), lambda i: (0, 0)),
                  pl.BlockSpec((1, 1, D), lambda i: (i // per_b, 0, 0)),
                  pl.BlockSpec((1, 1, D), lambda i: (i // per_b, 0, 0))],
        out_specs=pl.BlockSpec((tm, D), lambda i: (i, 0)),
        out_shape=jax.ShapeDtypeStruct((M, D), BF16),
        compiler_params=_params("arbitrary"),
        name="norm_mod",
    )(x2, g.reshape(1, D), sc.reshape(B, 1, D), sh.reshape(B, 1, D))


def _in_proj_kernel(a_ref, b_ref, o_ref, *, tile_scales):
    j = pl.program_id(0)
    acc = jnp.dot(a_ref[...], b_ref[...], preferred_element_type=F32)
    scale = jnp.float32(1.0)
    for tile, tile_scale in tile_scales:
        scale = jnp.where(j == tile, tile_scale, scale)
    o_ref[...] = (acc * scale).astype(o_ref.dtype)


def _in_proj(h, w, tn, tile_scales, name):
    M, K = h.shape
    N = w.shape[1]
    tm = min(M, 1024)
    return pl.pallas_call(
        functools.partial(_in_proj_kernel, tile_scales=tile_scales),
        grid=(N // tn, M // tm),
        in_specs=[pl.BlockSpec((tm, K), lambda j, i: (i, 0)),
                  pl.BlockSpec((K, tn), lambda j, i: (0, j))],
        out_specs=pl.BlockSpec((tm, tn), lambda j, i: (i, j)),
        out_shape=jax.ShapeDtypeStruct((M, N), BF16),
        compiler_params=_params("arbitrary", "arbitrary"),
        name=name,
    )(h, w)


def _key_query_iotas(k0, q0, tk, tq):
    key_pos = k0 + lax.broadcasted_iota(jnp.int32, (tk, tq), 0)
    query_pos = q0 + lax.broadcasted_iota(jnp.int32, (tk, tq), 1)
    return key_pos, query_pos


def _head_cols(g):
    return slice(g * HEAD_DIM, (g + 1) * HEAD_DIM)


def _sb_kernel(q_ref, k_ref, vt_ref, o_ref, run_ref, acc_ref, *, tq, tk, n_heads):
    qi = pl.program_id(2)
    q0 = qi * tq
    run_ref[...] = jnp.zeros_like(run_ref)
    acc_ref[...] = jnp.zeros_like(acc_ref)
    later = (lax.broadcasted_iota(jnp.int32, (tk, tk), 1)
             > lax.broadcasted_iota(jnp.int32, (tk, tk), 0)).astype(BF16)

    def body(carry):
        kb, _ = carry
        k0 = pl.multiple_of(kb * tk, tk)
        key_pos, query_pos = _key_query_iotas(k0, q0, tk, tq)
        causal = key_pos < query_pos
        heads = range(n_heads)
        zs = [lax.dot_general(k_ref[pl.ds(k0, tk), _head_cols(g)], q_ref[:, _head_cols(g)],
                              NT_DIMS, preferred_element_type=F32) for g in heads]
        sps = [jnp.maximum(z, 0.0) + jnp.log(1.0 + jnp.exp(-jnp.abs(z))) for z in zs]
        spms = [jnp.where(causal, sp, 0.0) for sp in sps]
        his = [spm.astype(BF16) for spm in spms]
        los = [(spm - hi.astype(F32)).astype(BF16) for spm, hi in zip(spms, his)]
        suffixes = [jnp.dot(later, hi, preferred_element_type=F32)
                    + jnp.dot(later, lo, preferred_element_type=F32) for hi, lo in zip(his, los)]
        runs = [run_ref[g] for g in heads]
        weights = [jnp.where(causal, jnp.exp(zs[g] - sps[g] - suffixes[g] - runs[g]), 0.0).astype(BF16)
                   for g in heads]
        min_run = None
        for g in heads:
            vt = vt_ref[_head_cols(g), pl.ds(k0, tk)]
            acc_ref[g] += jnp.dot(vt, weights[g], preferred_element_type=F32)
            run_new = runs[g] + jnp.sum(spms[g], axis=0, keepdims=True)
            run_ref[g] = run_new
            head_min = jnp.min(run_new)
            min_run = head_min if min_run is None else jnp.minimum(min_run, head_min)
        return kb - 1, min_run > EXP_ZERO_BELOW

    def cond(carry):
        kb, dead = carry
        return jnp.logical_and(kb >= 0, jnp.logical_not(dead))

    lax.while_loop(cond, body, ((q0 + tq - 1) // tk, jnp.bool_(False)))
    for g in range(n_heads):
        o_ref[:, _head_cols(g)] = acc_ref[g].T.astype(o_ref.dtype)


def _attn_group_specs(S, tq, nq, G, q_blk, k_blk, kv_buffers):
    assert q_blk % G == 0 and k_blk % G == 0
    gw = G * HEAD_DIM
    mode = pl.Buffered(kv_buffers)
    return [pl.BlockSpec((tq, gw), lambda b, hg, i: (b * nq + i, q_blk // G + hg)),
            pl.BlockSpec((S, gw), lambda b, hg, i: (b, k_blk // G + hg), pipeline_mode=mode),
            pl.BlockSpec((gw, S), lambda b, hg, i: (hg, b), pipeline_mode=mode)]


def _sb_attn(z, vt, B, S, H, q_blk, k_blk):
    M = z.shape[0]
    tq = tk = min(S, ATTN_BLOCK)
    nq = S // tq
    G = min(H, SB_HEAD_GROUP)
    gw = G * HEAD_DIM
    return pl.pallas_call(
        functools.partial(_sb_kernel, tq=tq, tk=tk, n_heads=G),
        grid=(B, H // G, nq),
        in_specs=_attn_group_specs(S, tq, nq, G, q_blk, k_blk, kv_buffers=2),
        out_specs=pl.BlockSpec((tq, gw), lambda b, hg, i: (b * nq + i, hg)),
        out_shape=jax.ShapeDtypeStruct((M, H * HEAD_DIM), BF16),
        scratch_shapes=[pltpu.VMEM((G, 1, tq), F32), pltpu.VMEM((G, HEAD_DIM, tq), F32)],
        compiler_params=_params("arbitrary", "arbitrary", "arbitrary"),
        name="sb_attn",
    )(z, z, vt)


def _dsa_select_kernel(q_ref, k_ref, w_ref, mask_ref, hi_ref, lo_ref, *, tq, tk, n_sel, n_kblocks):
    qi = pl.program_id(1)
    q0 = qi * tq
    nkb = (q0 + tq + tk - 1) // tk
    lane = lax.broadcasted_iota(jnp.int32, (tk, 2 * IDX_DIM), 1)
    w_scale = (N_IDX_HEADS ** -0.5) * (IDX_DIM ** -0.5)
    wt = (w_ref[...].astype(F32) * w_scale).T

    def score_block(kb, carry):
        k0 = pl.multiple_of(kb * tk, tk)
        kk = k_ref[pl.ds(k0, tk), :]
        k_halves = (jnp.where(lane < IDX_DIM, kk, jnp.zeros_like(kk)),
                    jnp.where(lane >= IDX_DIM, kk, jnp.zeros_like(kk)))
        acc = jnp.zeros((tk, tq), F32)
        for p in range(N_IDX_HEADS // 2):
            q2 = q_ref[:, p * 2 * IDX_DIM:(p + 1) * 2 * IDX_DIM]
            for half in range(2):
                h = 2 * p + half
                ph = lax.dot_general(k_halves[half], q2, NT_DIMS, preferred_element_type=F32)
                acc = acc + jnp.maximum(ph, 0.0) * wt[h:h + 1, :]
        bits = lax.bitcast_convert_type(acc, jnp.int32)
        key = bits ^ ((bits >> 31) & 0x7FFFFFFF)
        key_pos, query_pos = _key_query_iotas(k0, q0, tk, tq)
        key = jnp.where(key_pos <= query_pos, key, INT_MIN)
        hi_ref[pl.ds(k0, tk), :] = (key >> 16).astype(I16)
        lo_ref[pl.ds(k0, tk), :] = (key ^ 0x8000).astype(I16)
        return carry

    lax.fori_loop(0, nkb, score_block, 0)

    n_count = (nkb + COUNT_BLOCKS - 1) // COUNT_BLOCKS
    rows = COUNT_BLOCKS * tk

    def pad_block(kb, carry):
        k0 = pl.multiple_of(kb * tk, tk)
        hi_ref[pl.ds(k0, tk), :] = jnp.full((tk, tq), I16_MIN, I16)
        lo_ref[pl.ds(k0, tk), :] = jnp.full((tk, tq), I16_MIN, I16)
        return carry

    lax.fori_loop(nkb, n_count * COUNT_BLOCKS, pad_block, 0)

    def count(flags_fn):
        def blk(i, cnt):
            r0 = pl.multiple_of(i * rows, rows)
            groups = []
            for c in range(rows // COUNT_CHUNK):
                flags = flags_fn(r0 + c * COUNT_CHUNK, COUNT_CHUNK)
                parts = [flags[r * BF16_ROWS:(r + 1) * BF16_ROWS, :]
                         for r in range(COUNT_CHUNK // BF16_ROWS)]
                while len(parts) > 1:
                    parts = [a + b for a, b in zip(parts[0::2], parts[1::2])]
                groups.append(parts[0])
            while len(groups) > 1:
                groups = [a + b for a, b in zip(groups[0::2], groups[1::2])]
            return cnt + groups[0]
        cnt = lax.fori_loop(0, n_count, blk, jnp.zeros((BF16_ROWS, tq), I16))
        return jnp.sum(cnt.astype(jnp.int32), axis=0, keepdims=True)

    flag, no_flag = I16(1), I16(0)

    def count_ge(ref, thr16):
        return count(lambda r, n: jnp.where(ref[pl.ds(r, n), :] >= thr16, flag, no_flag))

    def count_gt(ref, thr16):
        return count(lambda r, n: jnp.where(ref[pl.ds(r, n), :] > thr16, flag, no_flag))

    def kth_largest(ref, k):
        def bisect(it, thr):
            cand = thr + jnp.left_shift(jnp.int32(1), 15 - it)
            return jnp.where(count_ge(ref, cand.astype(I16)) >= k, cand, thr)
        return lax.fori_loop(0, 16, bisect, jnp.full((1, tq), I16_MIN, jnp.int32)).astype(I16)

    thr_hi = kth_largest(hi_ref, n_sel)
    need_lo = n_sel - count_gt(hi_ref, thr_hi)

    def park(kb, carry):
        k0 = pl.multiple_of(kb * tk, tk)
        lo_ref[pl.ds(k0, tk), :] = jnp.where(hi_ref[pl.ds(k0, tk), :] == thr_hi,
                                             lo_ref[pl.ds(k0, tk), :], I16(I16_MIN))
        return carry

    lax.fori_loop(0, nkb, park, 0)
    thr_lo = kth_largest(lo_ref, need_lo)
    need_ties = need_lo - count_gt(lo_ref, thr_lo)
    n_ties = count(lambda r, n: jnp.where(hi_ref[pl.ds(r, n), :] == thr_hi,
                                          jnp.where(lo_ref[pl.ds(r, n), :] == thr_lo, flag, no_flag),
                                          no_flag))
    all_ties_fit = jnp.max(n_ties - need_ties) <= 0

    one, zero, neg = BF16(1.0), BF16(0.0), BF16(NEG)

    def store_mask(kb, sel, diagonal):
        k0 = pl.multiple_of(kb * tk, tk)
        if diagonal:
            key_pos, query_pos = _key_query_iotas(k0, q0, tk, tq)
            sel = sel * (key_pos <= query_pos).astype(BF16)
        mask_ref[0, pl.ds(k0, tk), :] = jnp.where(sel > zero, zero, neg)

    def emit_all_ties(kb, diagonal):
        k0 = pl.multiple_of(kb * tk, tk)
        hi = hi_ref[pl.ds(k0, tk), :]
        lo = lo_ref[pl.ds(k0, tk), :]
        store_mask(kb, jnp.where(hi > thr_hi, one,
                                 jnp.where(hi == thr_hi, jnp.where(lo >= thr_lo, one, zero), zero)),
                   diagonal)

    earlier = (lax.broadcasted_iota(jnp.int32, (tk, tk), 1)
               < lax.broadcasted_iota(jnp.int32, (tk, tk), 0)).astype(BF16)
    ones_rows = jnp.ones((BF16_ROWS, tk), BF16)
    need_ties_f = need_ties.astype(F32)

    def emit_ranked_ties(kb, ties_seen, diagonal):
        k0 = pl.multiple_of(kb * tk, tk)
        hi = hi_ref[pl.ds(k0, tk), :]
        lo = lo_ref[pl.ds(k0, tk), :]
        above = jnp.where(hi > thr_hi, one, jnp.where(lo > thr_lo, one, zero))
        eq = jnp.where(hi == thr_hi, jnp.where(lo == thr_lo, one, zero), zero)
        rank = jnp.dot(earlier, eq, preferred_element_type=F32).astype(BF16)
        room = jnp.clip(need_ties_f - ties_seen, -1.0, float(tk)).astype(BF16)
        store_mask(kb, jnp.where(rank < room, jnp.maximum(above, eq), above), diagonal)
        return ties_seen + jnp.dot(ones_rows, eq, preferred_element_type=F32)[0:1]

    def emit_fast():
        def body(kb, carry):
            emit_all_ties(kb, False)
            return carry
        lax.fori_loop(0, nkb - 1, body, 0)
        emit_all_ties(nkb - 1, True)

    def emit_slow():
        ties_seen = lax.fori_loop(0, nkb - 1, lambda kb, seen: emit_ranked_ties(kb, seen, False),
                                  jnp.zeros((1, tq), F32))
        emit_ranked_ties(nkb - 1, ties_seen, True)

    lax.cond(all_ties_fit, emit_fast, emit_slow)

    def fill(kb, carry):
        k0 = pl.multiple_of(kb * tk, tk)
        mask_ref[0, pl.ds(k0, tk), :] = jnp.full((tk, tq), NEG, mask_ref.dtype)
        return carry

    lax.fori_loop(nkb, n_kblocks, fill, 0)


def _dsa_select(z, zs, B, S, qix_blk, n_sel):
    tq = tk = min(S, ATTN_BLOCK)
    nq = S // tq
    assert nq % COUNT_BLOCKS == 0
    qw = N_IDX_HEADS * IDX_DIM
    return pl.pallas_call(
        functools.partial(_dsa_select_kernel, tq=tq, tk=tk, n_sel=n_sel, n_kblocks=S // tk),
        grid=(B, nq),
        in_specs=[pl.BlockSpec((tq, qw), lambda b, i: (b * nq + i, qix_blk)),
                  pl.BlockSpec((S, 2 * IDX_DIM), lambda b, i: (b, 0)),
                  pl.BlockSpec((tq, 2 * IDX_DIM), lambda b, i: (b * nq + i, 1))],
        out_specs=pl.BlockSpec((1, S, tq), lambda b, i: (b * nq + i, 0, 0)),
        out_shape=jax.ShapeDtypeStruct((B * nq, S, tq), BF16),
        scratch_shapes=[pltpu.VMEM((S, tq), I16), pltpu.VMEM((S, tq), I16)],
        compiler_params=_params("arbitrary", "arbitrary"),
        name="dsa_select",
    )(z, zs, zs)


def _rel_bucket(dist):
    n = jnp.maximum(dist, 0)
    max_exact = N_BUCKETS // 2
    nf = jnp.maximum(n, 1).astype(F32)
    large = max_exact + (jnp.log(nf / max_exact) / math.log(MAX_DISTANCE / max_exact)
                         * (N_BUCKETS - max_exact)).astype(jnp.int32)
    large = jnp.minimum(large, N_BUCKETS - 1)
    return jnp.where(n < max_exact, n, large)


def _bias_tiles(rel_bias, tq):
    assert tq >= MAX_DISTANCE
    rb = rel_bias.astype(F32)
    key = jnp.arange(tq)[:, None]
    query = jnp.arange(tq)[None, :]
    bucket = _rel_bucket(jnp.stack([query - key, tq + query - key]))
    onehot = (bucket[None] == jnp.arange(N_BUCKETS)[:, None, None, None]).astype(F32)
    return jnp.einsum("nh,nikq->hikq", (rb - rb[N_BUCKETS - 1]) * LOG2_E, onehot,
                      precision=lax.Precision.HIGHEST)


def _dsa_attn_kernel(q_ref, k_ref, vt_ref, mask_ref, bias_ref, o_ref, m_ref, acc_ref, *, tq, n_heads):
    qi = pl.program_id(2)
    m_ref[...] = jnp.full_like(m_ref, NEG)
    acc_ref[...] = jnp.zeros_like(acc_ref)
    ones_rows = jnp.ones((BF16_ROWS, tq), BF16)

    def step(blocks):
        k0s = [pl.multiple_of(kb * tq, tq) for kb, _ in blocks]
        masks = [mask_ref[0, pl.ds(k0, tq), :].astype(F32) for k0 in k0s]

        def head_scores(g):
            return [lax.dot_general(k_ref[pl.ds(k0, tq), _head_cols(g)], q_ref[:, _head_cols(g)],
                                    NT_DIMS, preferred_element_type=F32) for k0 in k0s]

        def head_probs(g, scores):
            logits = []
            for j, (_, bias_idx) in enumerate(blocks):
                s = scores[j] + masks[j]
                if bias_idx is not None:
                    s = s + bias_ref[g, bias_idx]
                logits.append(s)
            m_old = m_ref[g]
            m_new = m_old
            for s in logits:
                m_new = jnp.maximum(m_new, jnp.max(s, axis=0, keepdims=True))
            m_ref[g] = m_new
            return jnp.exp2(m_old - m_new), [jnp.exp2(s - m_new).astype(BF16) for s in logits]

        def head_accumulate(g, alpha, probs):
            acc = alpha * acc_ref[g]
            for p, k0 in zip(probs, k0s):
                vt = jnp.concatenate([vt_ref[_head_cols(g), pl.ds(k0, tq)], ones_rows], axis=0)
                acc = acc + jnp.dot(vt, p, preferred_element_type=F32)
            acc_ref[g] = acc

        scores = [head_scores(g) for g in range(n_heads)]
        weights = [head_probs(g, scores[g]) for g in range(n_heads)]
        for g in range(n_heads):
            head_accumulate(g, *weights[g])

    n_far = jnp.maximum(qi - 1, 0)

    def far_pair(i, carry):
        step([(2 * i, None), (2 * i + 1, None)])
        return carry

    lax.fori_loop(0, n_far // 2, far_pair, 0)

    @pl.when(n_far % 2 == 1)
    def _():
        step([(n_far - 1, None)])

    @pl.when(qi >= 1)
    def _():
        step([(qi - 1, 1), (qi, 0)])

    @pl.when(qi == 0)
    def _():
        step([(qi, 0)])
    for g in range(n_heads):
        acc = acc_ref[g]
        out_t = acc[:HEAD_DIM] / acc[HEAD_DIM:HEAD_DIM + 1]
        o_ref[:, _head_cols(g)] = out_t.T.astype(o_ref.dtype)


def _dsa_attn(z, vt, mask, bias_tiles, B, S, H, q_blk, k_blk):
    M = z.shape[0]
    tq = bias_tiles.shape[-1]
    nq = S // tq
    G = min(H, DSA_HEAD_GROUP)
    gw = G * HEAD_DIM
    return pl.pallas_call(
        functools.partial(_dsa_attn_kernel, tq=tq, n_heads=G),
        grid=(B, H // G, nq),
        in_specs=_attn_group_specs(S, tq, nq, G, q_blk, k_blk, kv_buffers=1) + [
            pl.BlockSpec((1, S, tq), lambda b, hg, i: (b * nq + i, 0, 0)),
            pl.BlockSpec((G, 2, tq, tq), lambda b, hg, i: (hg, 0, 0, 0),
                         pipeline_mode=pl.Buffered(1))],
        out_specs=pl.BlockSpec((tq, gw), lambda b, hg, i: (b * nq + i, hg)),
        out_shape=jax.ShapeDtypeStruct((M, H * HEAD_DIM), BF16),
        scratch_shapes=[pltpu.VMEM((G, 1, tq), F32),
                        pltpu.VMEM((G, HEAD_DIM + BF16_ROWS, tq), F32)],
        compiler_params=_params("arbitrary", "arbitrary", "arbitrary"),
        name="dsa_attn",
    )(z, z, vt, mask, bias_tiles)


def _mix_out_kernel(osb_ref, ods_ref, gsb_ref, gds_ref, x_ref, wsb_ref, wds_ref, wout_ref,
                    gt_ref, g_ref, sc_ref, sh_ref, x1_ref, h2_ref):
    t_sb = jnp.dot(osb_ref[...], wsb_ref[...], preferred_element_type=F32)
    t_ds = jnp.dot(ods_ref[...], wds_ref[...], preferred_element_type=F32)
    merged = (_sigmoid(gsb_ref[...].astype(F32)) * t_sb
              + _sigmoid(gds_ref[...].astype(F32)) * t_ds)
    y = jnp.dot(merged.astype(BF16), wout_ref[...], preferred_element_type=F32)
    x1 = x_ref[...] + gt_ref[0] * y
    x1_ref[...] = x1
    h2_ref[...] = _rms_mod(x1, g_ref[...], sc_ref[0], sh_ref[0]).astype(h2_ref.dtype)


def _mix_out(o_sb, o_ds, z, gate_blk, x2, w_sb, w_ds, w_out, gt, g, sc, sh, S):
    M, D = x2.shape
    B = gt.shape[0]
    W = o_sb.shape[1]
    tm = min(S, 512)
    per_b = S // tm
    row = lambda i: (i, 0)
    const = lambda i: (0, 0)
    per_batch = lambda i: (i // per_b, 0, 0)
    return pl.pallas_call(
        _mix_out_kernel,
        grid=(M // tm,),
        in_specs=[pl.BlockSpec((tm, W), row),
                  pl.BlockSpec((tm, W), row),
                  pl.BlockSpec((tm, D), lambda i: (i, gate_blk)),
                  pl.BlockSpec((tm, D), lambda i: (i, gate_blk + 1)),
                  pl.BlockSpec((tm, D), row),
                  pl.BlockSpec((W, D), const),
                  pl.BlockSpec((W, D), const),
                  pl.BlockSpec((D, D), const),
                  pl.BlockSpec((1, 1, D), per_batch),
                  pl.BlockSpec((1, D), const),
                  pl.BlockSpec((1, 1, D), per_batch),
                  pl.BlockSpec((1, 1, D), per_batch)],
        out_specs=[pl.BlockSpec((tm, D), row), pl.BlockSpec((tm, D), row)],
        out_shape=[jax.ShapeDtypeStruct((M, D), F32), jax.ShapeDtypeStruct((M, D), BF16)],
        compiler_params=_params("arbitrary"),
        name="mix_out",
    )(o_sb, o_ds, z, z, x2, w_sb, w_ds, w_out, gt.reshape(B, 1, D), g.reshape(1, D),
      sc.reshape(B, 1, D), sh.reshape(B, 1, D))


HALO = BF16_ROWS


def _conv_ffn_kernel(h_ref, halo_ref, x1_ref, wg_ref, wu_ref, wd_ref, cw_ref, cb_ref, gt_ref,
                     gf_ref, o_ref, acc_ref, *, tiles_per_seq):
    i = pl.program_id(0)
    f = pl.program_id(1)

    @pl.when(f == 0)
    def _():
        acc_ref[...] = jnp.zeros_like(acc_ref)

    h = h_ref[...]
    wg = wg_ref[...]
    g0 = jnp.dot(h, wg, preferred_element_type=F32)
    g_prev = jnp.dot(halo_ref[...], wg, preferred_element_type=F32)
    g_prev = jnp.where(i % tiles_per_seq == 0, 0.0, g_prev)
    ridx = lax.broadcasted_iota(jnp.int32, g0.shape, 0)
    g1 = jnp.where(ridx == 0, g_prev[HALO - 1:HALO, :], pltpu.roll(g0, 1, 0))
    g2 = jnp.where(ridx == 0, g_prev[HALO - 2:HALO - 1, :],
                   jnp.where(ridx == 1, g_prev[HALO - 1:HALO, :], pltpu.roll(g0, 2, 0)))
    cw = cw_ref[...]
    a = cb_ref[...] + g2 * cw[0:1, :] + g1 * cw[1:2, :] + g0 * cw[2:3, :]
    u = jnp.dot(h, wu_ref[...], preferred_element_type=F32)
    act = (a * _sigmoid(a) * u).astype(BF16)
    acc_ref[...] += jnp.dot(act, wd_ref[...], preferred_element_type=F32)

    @pl.when(f == pl.num_programs(1) - 1)
    def _():
        x = x1_ref[...] + gt_ref[0] * acc_ref[...]
        ms = jnp.mean(x * x, axis=-1, keepdims=True)
        o_ref[...] = x * lax.rsqrt(ms + EPS) * gf_ref[...]


def _conv_ffn(h2, x1, w_gate, w_up, w_down, conv_w, conv_b, gt, g_final, S):
    M, D = x1.shape
    B = gt.shape[0]
    F = w_gate.shape[1]
    tm = min(S, 512)
    tf = min(F, 512)
    per_b = S // tm
    halo_per_tile = tm // HALO
    return pl.pallas_call(
        functools.partial(_conv_ffn_kernel, tiles_per_seq=per_b),
        grid=(M // tm, F // tf),
        in_specs=[pl.BlockSpec((tm, D), lambda i, f: (i, 0)),
                  pl.BlockSpec((HALO, D), lambda i, f: (jnp.maximum(i * halo_per_tile - 1, 0), 0)),
                  pl.BlockSpec((tm, D), lambda i, f: (i, 0)),
                  pl.BlockSpec((D, tf), lambda i, f: (0, f)),
                  pl.BlockSpec((D, tf), lambda i, f: (0, f)),
                  pl.BlockSpec((tf, D), lambda i, f: (f, 0)),
                  pl.BlockSpec((CONV_WIDTH, tf), lambda i, f: (0, f)),
                  pl.BlockSpec((1, tf), lambda i, f: (0, f)),
                  pl.BlockSpec((1, 1, D), lambda i, f: (i // per_b, 0, 0)),
                  pl.BlockSpec((1, D), lambda i, f: (0, 0))],
        out_specs=pl.BlockSpec((tm, D), lambda i, f: (i, 0)),
        out_shape=jax.ShapeDtypeStruct((M, D), F32),
        scratch_shapes=[pltpu.VMEM((tm, D), F32)],
        compiler_params=_params("arbitrary", "arbitrary"),
        name="conv_ffn",
    )(h2, h2, x1, w_gate, w_up, w_down, conv_w, conv_b.reshape(1, F), gt.reshape(B, 1, D),
      g_final.reshape(1, D))


def kernel(x, c, w_ada, b_ada, g_mix, w_in, w_o_sb, w_o_dsa, w_out, rel_bias, g_ffn, w_gate,
           w_up, conv_w, conv_b, w_down, g_final):
    B, S, D = x.shape
    depth = w_ada.shape[0]
    W = w_o_sb.shape[1]
    H = W // HEAD_DIM
    qw = N_IDX_HEADS * IDX_DIM
    assert w_o_dsa.shape[1] == W and D % W == 0 and S % ATTN_BLOCK == 0 and (6 * W) % qw == 0
    n_sel = min(TOPK_MAX, S // 4)
    scale = HEAD_DIM ** -0.5
    x2 = x.reshape(B * S, D)

    for l in range(depth):
        mod = _adaln(c, w_ada[l], b_ada[l])
        sh1, sc1, gt1, sh2, sc2, gt2 = jnp.split(mod, 6, axis=-1)

        wl = w_in[l]
        o_qix = 6 * W
        o_kix = o_qix + qw
        o_wix = o_kix + IDX_DIM
        o_gate = o_wix + N_IDX_HEADS
        w_qkv = wl[:, :o_kix].astype(BF16)
        w_gates = wl[:, o_gate:].astype(BF16)
        k_cols = wl[:, o_kix:o_wix]
        w_tail = jnp.concatenate(
            [k_cols, k_cols, wl[:, o_wix:o_gate],
             jnp.zeros((D, 2 * IDX_DIM - N_IDX_HEADS), wl.dtype)], axis=1).astype(BF16)

        h1 = _norm_mod(x2, g_mix[l], sc1, sh1, S)
        z = _in_proj(h1, w_qkv, W, ((0, scale), (3, scale * LOG2_E)), "in_proj")
        zg = _in_proj(h1, w_gates, W, (), "in_proj_gates")
        zs = _in_proj(h1, w_tail, 4 * IDX_DIM, (), "in_proj_idx")

        o_sb = _sb_attn(z, z[:, 2 * W:3 * W].T, B, S, H, 0, H)
        mask = _dsa_select(z, zs, B, S, o_qix // qw, n_sel)
        o_ds = _dsa_attn(z, z[:, 5 * W:6 * W].T, mask, _bias_tiles(rel_bias, min(S, ATTN_BLOCK)),
                         B, S, H, 3 * H, 4 * H)

        x2, h2 = _mix_out(o_sb, o_ds, zg, 0, x2, w_o_sb[l].astype(BF16),
                          w_o_dsa[l].astype(BF16), w_out[l].astype(BF16), gt1, g_ffn[l], sc2, sh2, S)
        last = l == depth - 1
        assert last, "the final rms_norm is fused into the last layer's FFN"
        x2 = _conv_ffn(h2, x2, w_gate[l].astype(BF16), w_up[l].astype(BF16), w_down[l].astype(BF16),
                       conv_w[l], conv_b[l], gt2, g_final, S)
    return x2.reshape(B, S, D)
```

```python
import functools
import math

import jax
import jax.numpy as jnp
from jax import lax
from jax.experimental import pallas as pl
from jax.experimental.pallas import tpu as pltpu

HEAD_DIM = 128
N_IDX_HEADS = 16
IDX_DIM = 64
TOPK_MAX = 256
N_BUCKETS = 32
MAX_DISTANCE = 128
CONV_WIDTH = 3
EPS = 1e-6

F32 = jnp.float32
BF16 = jnp.bfloat16
NEG = -1e30
I16 = jnp.int16
INT_MIN = -2 ** 31
I16_MIN = -2 ** 15
EXP_ZERO_BELOW = 104.0
LOG2_E = math.log2(math.e)
V7X_VMEM_LIMIT = 56 * 1024 * 1024
SUBLANES = 8
BF16_ROWS = 16
ATTN_BLOCK = 256
SB_HEAD_GROUP = 4
DSA_HEAD_GROUP = 8
COUNT_BLOCKS = 4
COUNT_CHUNK = 64
NT_DIMS = (((1,), (1,)), ((), ()))


def _params(*sem):
    return pltpu.CompilerParams(dimension_semantics=sem, vmem_limit_bytes=V7X_VMEM_LIMIT)


def _sigmoid(x):
    return 1.0 / (1.0 + jnp.exp(-x))


def _adaln_kernel(ct_ref, w_ref, b_ref, o_ref):
    ct = ct_ref[...]
    act = ct * _sigmoid(ct)
    w = w_ref[...]
    for b in range(ct.shape[1]):
        o_ref[b:b + 1, :] = jnp.sum(act[:, b:b + 1] * w, axis=0, keepdims=True) + b_ref[...]


def _adaln(c, w, bias):
    B, D = c.shape
    N = w.shape[1]
    tn = min(N, 1024)
    return pl.pallas_call(
        _adaln_kernel,
        grid=(N // tn,),
        in_specs=[pl.BlockSpec((D, B), lambda j: (0, 0)),
                  pl.BlockSpec((D, tn), lambda j: (0, j)),
                  pl.BlockSpec((1, tn), lambda j: (0, j))],
        out_specs=pl.BlockSpec((B, tn), lambda j: (0, j)),
        out_shape=jax.ShapeDtypeStruct((B, N), F32),
        compiler_params=_params("arbitrary"),
        name="adaln",
    )(c.T, w, bias.reshape(1, N))


def _rms_mod(x, g, sc, sh):
    ms = jnp.mean(x * x, axis=-1, keepdims=True)
    y = x * lax.rsqrt(ms + EPS) * g
    return y * (1.0 + sc) + sh


def _norm_mod_kernel(x_ref, g_ref, sc_ref, sh_ref, o_ref):
    o_ref[...] = _rms_mod(x_ref[...], g_ref[...], sc_ref[0], sh_ref[0]).astype(o_ref.dtype)


def _norm_mod(x2, g, sc, sh, S):
    M, D = x2.shape
    B = sc.shape[0]
    tm = min(S, 512)
    per_b = S // tm
    return pl.pallas_call(
        _norm_mod_kernel,
        grid=(M // tm,),
        in_specs=[pl.BlockSpec((tm, D), lambda i: (i, 0)),
                  pl.BlockSpec((1, D), lambda i: (0, 0)),
                  pl.BlockSpec((1, 1, D), lambda i: (i // per_b, 0, 0)),
                  pl.BlockSpec((1, 1, D), lambda i: (i // per_b, 0, 0))],
        out_specs=pl.BlockSpec((tm, D), lambda i: (i, 0)),
        out_shape=jax.ShapeDtypeStruct((M, D), BF16),
        compiler_params=_params("arbitrary"),
        name="norm_mod",
    )(x2, g.reshape(1, D), sc.reshape(B, 1, D), sh.reshape(B, 1, D))


def _in_proj_kernel(a_ref, b_ref, o_ref, *, tile_scales):
    j = pl.program_id(0)
    acc = jnp.dot(a_ref[...], b_ref[...], preferred_element_type=F32)
    scale = jnp.float32(1.0)
    for tile, tile_scale in tile_scales:
        scale = jnp.where(j == tile, tile_scale, scale)
    o_ref[...] = (acc * scale).astype(o_ref.dtype)


def _in_proj(h, w, tn, tile_scales, name):
    M, K = h.shape
    N = w.shape[1]
    tm = min(M, 1024)
    return pl.pallas_call(
        functools.partial(_in_proj_kernel, tile_scales=tile_scales),
        grid=(N // tn, M // tm),
        in_specs=[pl.BlockSpec((tm, K), lambda j, i: (i, 0)),
                  pl.BlockSpec((K, tn), lambda j, i: (0, j))],
        out_specs=pl.BlockSpec((tm, tn), lambda j, i: (i, j)),
        out_shape=jax.ShapeDtypeStruct((M, N), BF16),
        compiler_params=_params("arbitrary", "arbitrary"),
        name=name,
    )(h, w)


def _key_query_iotas(k0, q0, tk, tq):
    key_pos = k0 + lax.broadcasted_iota(jnp.int32, (tk, tq), 0)
    query_pos = q0 + lax.broadcasted_iota(jnp.int32, (tk, tq), 1)
    return key_pos, query_pos


def _head_cols(g):
    return slice(g * HEAD_DIM, (g + 1) * HEAD_DIM)


def _sb_kernel(q_ref, k_ref, vt_ref, o_ref, run_ref, acc_ref, *, tq, tk, n_heads):
    qi = pl.program_id(2)
    q0 = qi * tq
    run_ref[...] = jnp.zeros_like(run_ref)
    acc_ref[...] = jnp.zeros_like(acc_ref)
    later = (lax.broadcasted_iota(jnp.int32, (tk, tk), 1)
             > lax.broadcasted_iota(jnp.int32, (tk, tk), 0)).astype(BF16)

    def body(carry):
        kb, _ = carry
        k0 = pl.multiple_of(kb * tk, tk)
        key_pos, query_pos = _key_query_iotas(k0, q0, tk, tq)
        causal = key_pos < query_pos
        heads = range(n_heads)
        zs = [lax.dot_general(k_ref[pl.ds(k0, tk), _head_cols(g)], q_ref[:, _head_cols(g)],
                              NT_DIMS, preferred_element_type=F32) for g in heads]
        sps = [jnp.maximum(z, 0.0) + jnp.log(1.0 + jnp.exp(-jnp.abs(z))) for z in zs]
        spms = [jnp.where(causal, sp, 0.0) for sp in sps]
        his = [spm.astype(BF16) for spm in spms]
        los = [(spm - hi.astype(F32)).astype(BF16) for spm, hi in zip(spms, his)]
        suffixes = [jnp.dot(later, hi, preferred_element_type=F32)
                    + jnp.dot(later, lo, preferred_element_type=F32) for hi, lo in zip(his, los)]
        runs = [run_ref[g] for g in heads]
        weights = [jnp.where(causal, jnp.exp(zs[g] - sps[g] - suffixes[g] - runs[g]), 0.0).astype(BF16)
                   for g in heads]
        min_run = None
        for g in heads:
            vt = vt_ref[_head_cols(g), pl.ds(k0, tk)]
            acc_ref[g] += jnp.dot(vt, weights[g], preferred_element_type=F32)
            run_new = runs[g] + jnp.sum(spms[g], axis=0, keepdims=True)
            run_ref[g] = run_new
            head_min = jnp.min(run_new)
            min_run = head_min if min_run is None else jnp.minimum(min_run, head_min)
        return kb - 1, min_run > EXP_ZERO_BELOW

    def cond(carry):
        kb, dead = carry
        return jnp.logical_and(kb >= 0, jnp.logical_not(dead))

    lax.while_loop(cond, body, ((q0 + tq - 1) // tk, jnp.bool_(False)))
    for g in range(n_heads):
        o_ref[:, _head_cols(g)] = acc_ref[g].T.astype(o_ref.dtype)


def _attn_group_specs(S, tq, nq, G, q_blk, k_blk, kv_buffers):
    assert q_blk % G == 0 and k_blk % G == 0
    gw = G * HEAD_DIM
    mode = pl.Buffered(kv_buffers)
    return [pl.BlockSpec((tq, gw), lambda b, hg, i: (b * nq + i, q_blk // G + hg)),
            pl.BlockSpec((S, gw), lambda b, hg, i: (b, k_blk // G + hg), pipeline_mode=mode),
            pl.BlockSpec((gw, S), lambda b, hg, i: (hg, b), pipeline_mode=mode)]


def _sb_attn(z, vt, B, S, H, q_blk, k_blk):
    M = z.shape[0]
    tq = tk = min(S, ATTN_BLOCK)
    nq = S // tq
    G = min(H, SB_HEAD_GROUP)
    gw = G * HEAD_DIM
    return pl.pallas_call(
        functools.partial(_sb_kernel, tq=tq, tk=tk, n_heads=G),
        grid=(B, H // G, nq),
        in_specs=_attn_group_specs(S, tq, nq, G, q_blk, k_blk, kv_buffers=2),
        out_specs=pl.BlockSpec((tq, gw), lambda b, hg, i: (b * nq + i, hg)),
        out_shape=jax.ShapeDtypeStruct((M, H * HEAD_DIM), BF16),
        scratch_shapes=[pltpu.VMEM((G, 1, tq), F32), pltpu.VMEM((G, HEAD_DIM, tq), F32)],
        compiler_params=_params("arbitrary", "arbitrary", "arbitrary"),
        name="sb_attn",
    )(z, z, vt)


def _dsa_select_kernel(q_ref, k_ref, w_ref, mask_ref, hi_ref, lo_ref, *, tq, tk, n_sel, n_kblocks):
    qi = pl.program_id(1)
    q0 = qi * tq
    nkb = (q0 + tq + tk - 1) // tk
    lane = lax.broadcasted_iota(jnp.int32, (tk, 2 * IDX_DIM), 1)
    w_scale = (N_IDX_HEADS ** -0.5) * (IDX_DIM ** -0.5)
    wt = (w_ref[...].astype(F32) * w_scale).T

    def score_block(kb, diagonal):
        k0 = pl.multiple_of(kb * tk, tk)
        kk = k_ref[pl.ds(k0, tk), :]
        k_halves = (jnp.where(lane < IDX_DIM, kk, jnp.zeros_like(kk)),
                    jnp.where(lane >= IDX_DIM, kk, jnp.zeros_like(kk)))
        acc = jnp.zeros((tk, tq), F32)
        for p in range(N_IDX_HEADS // 2):
            q2 = q_ref[:, p * 2 * IDX_DIM:(p + 1) * 2 * IDX_DIM]
            for half in range(2):
                h = 2 * p + half
                ph = lax.dot_general(k_halves[half], q2, NT_DIMS, preferred_element_type=F32)
                acc = acc + jnp.maximum(ph, 0.0) * wt[h:h + 1, :]
        bits = lax.bitcast_convert_type(acc, jnp.int32)
        key = bits ^ ((bits >> 31) & 0x7FFFFFFF)
        if diagonal:
            key_pos, query_pos = _key_query_iotas(k0, q0, tk, tq)
            key = jnp.where(key_pos <= query_pos, key, INT_MIN)
        hi_ref[pl.ds(k0, tk), :] = (key >> 16).astype(I16)
        lo_ref[pl.ds(k0, tk), :] = (key ^ 0x8000).astype(I16)

    def score_pair(i, carry):
        score_block(2 * i, False)
        score_block(2 * i + 1, False)
        return carry

    lax.fori_loop(0, (nkb - 1) // 2, score_pair, 0)

    @pl.when((nkb - 1) % 2 == 1)
    def _():
        score_block(nkb - 2, False)

    score_block(nkb - 1, True)

    n_count = (nkb + COUNT_BLOCKS - 1) // COUNT_BLOCKS
    rows = COUNT_BLOCKS * tk

    def pad_block(kb, carry):
        k0 = pl.multiple_of(kb * tk, tk)
        hi_ref[pl.ds(k0, tk), :] = jnp.full((tk, tq), I16_MIN, I16)
        lo_ref[pl.ds(k0, tk), :] = jnp.full((tk, tq), I16_MIN, I16)
        return carry

    lax.fori_loop(nkb, n_count * COUNT_BLOCKS, pad_block, 0)

    def count(flags_fn):
        def blk(i, cnt):
            r0 = pl.multiple_of(i * rows, rows)
            groups = []
            for c in range(rows // COUNT_CHUNK):
                flags = flags_fn(r0 + c * COUNT_CHUNK, COUNT_CHUNK)
                parts = [flags[r * BF16_ROWS:(r + 1) * BF16_ROWS, :]
                         for r in range(COUNT_CHUNK // BF16_ROWS)]
                while len(parts) > 1:
                    parts = [a + b for a, b in zip(parts[0::2], parts[1::2])]
                groups.append(parts[0])
            while len(groups) > 1:
                groups = [a + b for a, b in zip(groups[0::2], groups[1::2])]
            return cnt + groups[0]
        cnt = lax.fori_loop(0, n_count, blk, jnp.zeros((BF16_ROWS, tq), I16))
        return jnp.sum(cnt.astype(jnp.int32), axis=0, keepdims=True)

    flag, no_flag = I16(1), I16(0)

    def count_ge(ref, thr16):
        return count(lambda r, n: jnp.where(ref[pl.ds(r, n), :] >= thr16, flag, no_flag))

    def kth_largest(ref, k):
        def bisect(it, carry):
            thr, n_ge, n_gt = carry
            cand = thr + jnp.left_shift(jnp.int32(1), 15 - it)
            cnt = count_ge(ref, cand.astype(I16))
            ok = cnt >= k
            return jnp.where(ok, cand, thr), jnp.where(ok, cnt, n_ge), jnp.where(ok, n_gt, cnt)
        zeros = jnp.zeros((1, tq), jnp.int32)
        thr, n_ge, n_gt = lax.fori_loop(0, 16, bisect, (zeros + I16_MIN, zeros, zeros))
        return thr.astype(I16), n_ge, n_gt

    thr_hi, _, above_hi = kth_largest(hi_ref, n_sel)
    need_lo = n_sel - above_hi

    def park(kb, carry):
        k0 = pl.multiple_of(kb * tk, tk)
        lo_ref[pl.ds(k0, tk), :] = jnp.where(hi_ref[pl.ds(k0, tk), :] == thr_hi,
                                             lo_ref[pl.ds(k0, tk), :], I16(I16_MIN))
        return carry

    lax.fori_loop(0, nkb, park, 0)
    thr_lo, at_least_lo, above_lo = kth_largest(lo_ref, need_lo)
    need_ties = need_lo - above_lo
    n_ties = jnp.where(thr_lo == I16(I16_MIN), tk * n_kblocks, at_least_lo - above_lo)
    all_ties_fit = jnp.max(n_ties - need_ties) <= 0

    one, zero, neg = BF16(1.0), BF16(0.0), BF16(NEG)

    def store_mask(kb, sel, diagonal):
        k0 = pl.multiple_of(kb * tk, tk)
        if diagonal:
            key_pos, query_pos = _key_query_iotas(k0, q0, tk, tq)
            sel = sel * (key_pos <= query_pos).astype(BF16)
        mask_ref[0, pl.ds(k0, tk), :] = jnp.where(sel > zero, zero, neg)

    def emit_all_ties(kb, diagonal):
        k0 = pl.multiple_of(kb * tk, tk)
        hi = hi_ref[pl.ds(k0, tk), :]
        lo = lo_ref[pl.ds(k0, tk), :]
        store_mask(kb, jnp.where(hi > thr_hi, one,
                                 jnp.where(hi == thr_hi, jnp.where(lo >= thr_lo, one, zero), zero)),
                   diagonal)

    earlier = (lax.broadcasted_iota(jnp.int32, (tk, tk), 1)
               < lax.broadcasted_iota(jnp.int32, (tk, tk), 0)).astype(BF16)
    ones_rows = jnp.ones((BF16_ROWS, tk), BF16)
    need_ties_f = need_ties.astype(F32)

    def emit_ranked_ties(kb, ties_seen, diagonal):
        k0 = pl.multiple_of(kb * tk, tk)
        hi = hi_ref[pl.ds(k0, tk), :]
        lo = lo_ref[pl.ds(k0, tk), :]
        above = jnp.where(hi > thr_hi, one, jnp.where(lo > thr_lo, one, zero))
        eq = jnp.where(hi == thr_hi, jnp.where(lo == thr_lo, one, zero), zero)
        rank = jnp.dot(earlier, eq, preferred_element_type=F32).astype(BF16)
        room = jnp.clip(need_ties_f - ties_seen, -1.0, float(tk)).astype(BF16)
        store_mask(kb, jnp.where(rank < room, jnp.maximum(above, eq), above), diagonal)
        return ties_seen + jnp.dot(ones_rows, eq, preferred_element_type=F32)[0:1]

    def emit_fast():
        def body(kb, carry):
            emit_all_ties(kb, False)
            return carry
        lax.fori_loop(0, nkb - 1, body, 0)
        emit_all_ties(nkb - 1, True)

    def emit_slow():
        ties_seen = lax.fori_loop(0, nkb - 1, lambda kb, seen: emit_ranked_ties(kb, seen, False),
                                  jnp.zeros((1, tq), F32))
        emit_ranked_ties(nkb - 1, ties_seen, True)

    lax.cond(all_ties_fit, emit_fast, emit_slow)

    def fill(kb, carry):
        k0 = pl.multiple_of(kb * tk, tk)
        mask_ref[0, pl.ds(k0, tk), :] = jnp.full((tk, tq), NEG, mask_ref.dtype)
        return carry

    lax.fori_loop(nkb, n_kblocks, fill, 0)


def _dsa_select(z, zs, B, S, qix_blk, n_sel):
    tq = tk = min(S, ATTN_BLOCK)
    nq = S // tq
    assert nq % COUNT_BLOCKS == 0
    qw = N_IDX_HEADS * IDX_DIM
    return pl.pallas_call(
        functools.partial(_dsa_select_kernel, tq=tq, tk=tk, n_sel=n_sel, n_kblocks=S // tk),
        grid=(B, nq),
        in_specs=[pl.BlockSpec((tq, qw), lambda b, i: (b * nq + i, qix_blk)),
                  pl.BlockSpec((S, 2 * IDX_DIM), lambda b, i: (b, 0)),
                  pl.BlockSpec((tq, 2 * IDX_DIM), lambda b, i: (b * nq + i, 1))],
        out_specs=pl.BlockSpec((1, S, tq), lambda b, i: (b * nq + i, 0, 0)),
        out_shape=jax.ShapeDtypeStruct((B * nq, S, tq), BF16),
        scratch_shapes=[pltpu.VMEM((S, tq), I16), pltpu.VMEM((S, tq), I16)],
        compiler_params=_params("arbitrary", "arbitrary"),
        name="dsa_select",
    )(z, zs, zs)


def _rel_bucket(dist):
    n = jnp.maximum(dist, 0)
    max_exact = N_BUCKETS // 2
    nf = jnp.maximum(n, 1).astype(F32)
    large = max_exact + (jnp.log(nf / max_exact) / math.log(MAX_DISTANCE / max_exact)
                         * (N_BUCKETS - max_exact)).astype(jnp.int32)
    large = jnp.minimum(large, N_BUCKETS - 1)
    return jnp.where(n < max_exact, n, large)


def _bias_tiles(rel_bias, tq):
    assert tq >= MAX_DISTANCE
    rb = rel_bias.astype(F32)
    key = jnp.arange(tq)[:, None]
    query = jnp.arange(tq)[None, :]
    bucket = _rel_bucket(jnp.stack([query - key, tq + query - key]))
    onehot = (bucket[None] == jnp.arange(N_BUCKETS)[:, None, None, None]).astype(F32)
    return jnp.einsum("nh,nikq->hikq", (rb - rb[N_BUCKETS - 1]) * LOG2_E, onehot,
                      precision=lax.Precision.HIGHEST)


def _dsa_attn_kernel(q_ref, k_ref, vt_ref, mask_ref, bias_ref, o_ref, m_ref, acc_ref, *, tq, n_heads):
    qi = pl.program_id(2)
    m_ref[...] = jnp.full_like(m_ref, NEG)
    acc_ref[...] = jnp.zeros_like(acc_ref)
    ones_rows = jnp.ones((BF16_ROWS, tq), BF16)

    def step(blocks):
        k0s = [pl.multiple_of(kb * tq, tq) for kb, _ in blocks]
        masks = [mask_ref[0, pl.ds(k0, tq), :].astype(F32) for k0 in k0s]

        def head_scores(g):
            return [lax.dot_general(k_ref[pl.ds(k0, tq), _head_cols(g)], q_ref[:, _head_cols(g)],
                                    NT_DIMS, preferred_element_type=F32) for k0 in k0s]

        def head_probs(g, scores):
            logits = []
            for j, (_, bias_idx) in enumerate(blocks):
                s = scores[j] + masks[j]
                if bias_idx is not None:
                    s = s + bias_ref[g, bias_idx]
                logits.append(s)
            m_old = m_ref[g]
            m_new = m_old
            for s in logits:
                m_new = jnp.maximum(m_new, jnp.max(s, axis=0, keepdims=True))
            m_ref[g] = m_new
            return jnp.exp2(m_old - m_new), [jnp.exp2(s - m_new).astype(BF16) for s in logits]

        def head_accumulate(g, alpha, probs):
            acc = alpha * acc_ref[g]
            for p, k0 in zip(probs, k0s):
                vt = jnp.concatenate([vt_ref[_head_cols(g), pl.ds(k0, tq)], ones_rows], axis=0)
                acc = acc + jnp.dot(vt, p, preferred_element_type=F32)
            acc_ref[g] = acc

        scores = [head_scores(g) for g in range(n_heads)]
        weights = [head_probs(g, scores[g]) for g in range(n_heads)]
        for g in range(n_heads):
            head_accumulate(g, *weights[g])

    n_far = jnp.maximum(qi - 1, 0)

    def far_pair(i, carry):
        step([(2 * i, None), (2 * i + 1, None)])
        return carry

    lax.fori_loop(0, n_far // 2, far_pair, 0)

    @pl.when(n_far % 2 == 1)
    def _():
        step([(n_far - 1, None)])

    @pl.when(qi >= 1)
    def _():
        step([(qi - 1, 1), (qi, 0)])

    @pl.when(qi == 0)
    def _():
        step([(qi, 0)])
    for g in range(n_heads):
        acc = acc_ref[g]
        out_t = acc[:HEAD_DIM] / acc[HEAD_DIM:HEAD_DIM + 1]
        o_ref[:, _head_cols(g)] = out_t.T.astype(o_ref.dtype)


def _dsa_attn(z, vt, mask, bias_tiles, B, S, H, q_blk, k_blk):
    M = z.shape[0]
    tq = bias_tiles.shape[-1]
    nq = S // tq
    G = min(H, DSA_HEAD_GROUP)
    gw = G * HEAD_DIM
    return pl.pallas_call(
        functools.partial(_dsa_attn_kernel, tq=tq, n_heads=G),
        grid=(B, H // G, nq),
        in_specs=_attn_group_specs(S, tq, nq, G, q_blk, k_blk, kv_buffers=1) + [
            pl.BlockSpec((1, S, tq), lambda b, hg, i: (b * nq + i, 0, 0)),
            pl.BlockSpec((G, 2, tq, tq), lambda b, hg, i: (hg, 0, 0, 0),
                         pipeline_mode=pl.Buffered(1))],
        out_specs=pl.BlockSpec((tq, gw), lambda b, hg, i: (b * nq + i, hg)),
        out_shape=jax.ShapeDtypeStruct((M, H * HEAD_DIM), BF16),
        scratch_shapes=[pltpu.VMEM((G, 1, tq), F32),
                        pltpu.VMEM((G, HEAD_DIM + BF16_ROWS, tq), F32)],
        compiler_params=_params("arbitrary", "arbitrary", "arbitrary"),
        name="dsa_attn",
    )(z, z, vt, mask, bias_tiles)


def _mix_out_kernel(osb_ref, ods_ref, gsb_ref, gds_ref, x_ref, wsb_ref, wds_ref, wout_ref,
                    gt_ref, g_ref, sc_ref, sh_ref, x1_ref, h2_ref):
    t_sb = jnp.dot(osb_ref[...], wsb_ref[...], preferred_element_type=F32)
    t_ds = jnp.dot(ods_ref[...], wds_ref[...], preferred_element_type=F32)
    merged = (_sigmoid(gsb_ref[...].astype(F32)) * t_sb
              + _sigmoid(gds_ref[...].astype(F32)) * t_ds)
    y = jnp.dot(merged.astype(BF16), wout_ref[...], preferred_element_type=F32)
    x1 = x_ref[...] + gt_ref[0] * y
    x1_ref[...] = x1
    h2_ref[...] = _rms_mod(x1, g_ref[...], sc_ref[0], sh_ref[0]).astype(h2_ref.dtype)


def _mix_out(o_sb, o_ds, z, gate_blk, x2, w_sb, w_ds, w_out, gt, g, sc, sh, S):
    M, D = x2.shape
    B = gt.shape[0]
    W = o_sb.shape[1]
    tm = min(S, 512)
    per_b = S // tm
    row = lambda i: (i, 0)
    const = lambda i: (0, 0)
    per_batch = lambda i: (i // per_b, 0, 0)
    return pl.pallas_call(
        _mix_out_kernel,
        grid=(M // tm,),
        in_specs=[pl.BlockSpec((tm, W), row),
                  pl.BlockSpec((tm, W), row),
                  pl.BlockSpec((tm, D), lambda i: (i, gate_blk)),
                  pl.BlockSpec((tm, D), lambda i: (i, gate_blk + 1)),
                  pl.BlockSpec((tm, D), row),
                  pl.BlockSpec((W, D), const),
                  pl.BlockSpec((W, D), const),
                  pl.BlockSpec((D, D), const),
                  pl.BlockSpec((1, 1, D), per_batch),
                  pl.BlockSpec((1, D), const),
                  pl.BlockSpec((1, 1, D), per_batch),
                  pl.BlockSpec((1, 1, D), per_batch)],
        out_specs=[pl.BlockSpec((tm, D), row), pl.BlockSpec((tm, D), row)],
        out_shape=[jax.ShapeDtypeStruct((M, D), F32), jax.ShapeDtypeStruct((M, D), BF16)],
        compiler_params=_params("arbitrary"),
        name="mix_out",
    )(o_sb, o_ds, z, z, x2, w_sb, w_ds, w_out, gt.reshape(B, 1, D), g.reshape(1, D),
      sc.reshape(B, 1, D), sh.reshape(B, 1, D))


HALO = BF16_ROWS


def _ffn_act_kernel(h_ref, halo_ref, wg_ref, wu_ref, cw_ref, cb_ref, o_ref, *, tiles_per_seq):
    i = pl.program_id(0)
    h = h_ref[...]
    wg = wg_ref[...]
    g0 = jnp.dot(h, wg, preferred_element_type=F32)
    g_prev = jnp.dot(halo_ref[...], wg, preferred_element_type=F32)
    g_prev = jnp.where(i % tiles_per_seq == 0, 0.0, g_prev)
    ridx = lax.broadcasted_iota(jnp.int32, g0.shape, 0)
    g1 = jnp.where(ridx == 0, g_prev[HALO - 1:HALO, :], pltpu.roll(g0, 1, 0))
    g2 = jnp.where(ridx == 0, g_prev[HALO - 2:HALO - 1, :],
                   jnp.where(ridx == 1, g_prev[HALO - 1:HALO, :], pltpu.roll(g0, 2, 0)))
    cw = cw_ref[...]
    a = cb_ref[...] + g2 * cw[0:1, :] + g1 * cw[1:2, :] + g0 * cw[2:3, :]
    u = jnp.dot(h, wu_ref[...], preferred_element_type=F32)
    o_ref[...] = (a * _sigmoid(a) * u).astype(o_ref.dtype)


def _ffn_out_kernel(a_ref, wd_ref, x1_ref, gt_ref, gf_ref, o_ref):
    k = pl.program_id(1)

    @pl.when(k == 0)
    def _():
        o_ref[...] = jnp.zeros_like(o_ref)

    o_ref[...] += jnp.dot(a_ref[...], wd_ref[...], preferred_element_type=F32)

    @pl.when(k == pl.num_programs(1) - 1)
    def _():
        x = x1_ref[...] + gt_ref[0] * o_ref[...]
        ms = jnp.mean(x * x, axis=-1, keepdims=True)
        o_ref[...] = x * lax.rsqrt(ms + EPS) * gf_ref[...]


def _conv_ffn(h2, x1, w_gate, w_up, w_down, conv_w, conv_b, gt, g_final, S):
    M, D = x1.shape
    B = gt.shape[0]
    F = w_gate.shape[1]
    tm = min(S, 1024)
    tf = min(F, 512)
    per_b = S // tm
    halo_per_tile = tm // HALO
    act = pl.pallas_call(
        functools.partial(_ffn_act_kernel, tiles_per_seq=per_b),
        grid=(M // tm, F // tf),
        in_specs=[pl.BlockSpec((tm, D), lambda i, f: (i, 0)),
                  pl.BlockSpec((HALO, D), lambda i, f: (jnp.maximum(i * halo_per_tile - 1, 0), 0)),
                  pl.BlockSpec((D, tf), lambda i, f: (0, f)),
                  pl.BlockSpec((D, tf), lambda i, f: (0, f)),
                  pl.BlockSpec((CONV_WIDTH, tf), lambda i, f: (0, f)),
                  pl.BlockSpec((1, tf), lambda i, f: (0, f))],
        out_specs=pl.BlockSpec((tm, tf), lambda i, f: (i, f)),
        out_shape=jax.ShapeDtypeStruct((M, F), BF16),
        compiler_params=_params("arbitrary", "arbitrary"),
        name="ffn_act",
    )(h2, h2, w_gate, w_up, conv_w, conv_b.reshape(1, F))
    return pl.pallas_call(
        _ffn_out_kernel,
        grid=(M // tm, F // tf),
        in_specs=[pl.BlockSpec((tm, tf), lambda i, k: (i, k)),
                  pl.BlockSpec((tf, D), lambda i, k: (k, 0)),
                  pl.BlockSpec((tm, D), lambda i, k: (i, 0)),
                  pl.BlockSpec((1, 1, D), lambda i, k: (i // per_b, 0, 0)),
                  pl.BlockSpec((1, D), lambda i, k: (0, 0))],
        out_specs=pl.BlockSpec((tm, D), lambda i, k: (i, 0)),
        out_shape=jax.ShapeDtypeStruct((M, D), F32),
        compiler_params=_params("arbitrary", "arbitrary"),
        name="ffn_out",
    )(act, w_down, x1, gt.reshape(B, 1, D), g_final.reshape(1, D))


def kernel(x, c, w_ada, b_ada, g_mix, w_in, w_o_sb, w_o_dsa, w_out, rel_bias, g_ffn, w_gate,
           w_up, conv_w, conv_b, w_down, g_final):
    B, S, D = x.shape
    depth = w_ada.shape[0]
    W = w_o_sb.shape[1]
    H = W // HEAD_DIM
    qw = N_IDX_HEADS * IDX_DIM
    assert w_o_dsa.shape[1] == W and D % W == 0 and S % ATTN_BLOCK == 0 and (6 * W) % qw == 0
    n_sel = min(TOPK_MAX, S // 4)
    scale = HEAD_DIM ** -0.5
    x2 = x.reshape(B * S, D)

    for l in range(depth):
        mod = _adaln(c, w_ada[l], b_ada[l])
        sh1, sc1, gt1, sh2, sc2, gt2 = jnp.split(mod, 6, axis=-1)

        wl = w_in[l]
        o_qix = 6 * W
        o_kix = o_qix + qw
        o_wix = o_kix + IDX_DIM
        o_gate = o_wix + N_IDX_HEADS
        w_qkv = wl[:, :o_kix].astype(BF16)
        w_gates = wl[:, o_gate:].astype(BF16)
        k_cols = wl[:, o_kix:o_wix]
        w_tail = jnp.concatenate(
            [k_cols, k_cols, wl[:, o_wix:o_gate],
             jnp.zeros((D, 2 * IDX_DIM - N_IDX_HEADS), wl.dtype)], axis=1).astype(BF16)

        h1 = _norm_mod(x2, g_mix[l], sc1, sh1, S)
        z = _in_proj(h1, w_qkv, W, ((0, scale), (3, scale * LOG2_E)), "in_proj")
        zg = _in_proj(h1, w_gates, W, (), "in_proj_gates")
        zs = _in_proj(h1, w_tail, 4 * IDX_DIM, (), "in_proj_idx")

        o_sb = _sb_attn(z, z[:, 2 * W:3 * W].T, B, S, H, 0, H)
        mask = _dsa_select(z, zs, B, S, o_qix // qw, n_sel)
        o_ds = _dsa_attn(z, z[:, 5 * W:6 * W].T, mask, _bias_tiles(rel_bias, min(S, ATTN_BLOCK)),
                         B, S, H, 3 * H, 4 * H)

        x2, h2 = _mix_out(o_sb, o_ds, zg, 0, x2, w_o_sb[l].astype(BF16),
                          w_o_dsa[l].astype(BF16), w_out[l].astype(BF16), gt1, g_ffn[l], sc2, sh2, S)
        last = l == depth - 1
        assert last, "the final rms_norm is fused into the last layer's FFN"
        x2 = _conv_ffn(h2, x2, w_gate[l].astype(BF16), w_up[l].astype(BF16), w_down[l].astype(BF16),
                       conv_w[l], conv_b[l], gt2, g_final, S)
    return x2.reshape(B, S, D)
```

```python
import functools
import math

import jax
import jax.numpy as jnp
from jax import lax
from jax.experimental import pallas as pl
from jax.experimental.pallas import tpu as pltpu

HEAD_DIM = 128
N_IDX_HEADS = 16
IDX_DIM = 64
TOPK_MAX = 256
N_BUCKETS = 32
MAX_DISTANCE = 128
CONV_WIDTH = 3
EPS = 1e-6

F32 = jnp.float32
BF16 = jnp.bfloat16
NEG = -1e30
I16 = jnp.int16
INT_MIN = -2 ** 31
I16_MIN = -2 ** 15
EXP_ZERO_BELOW = 104.0
LOG2_E = math.log2(math.e)
V7X_VMEM_LIMIT = 56 * 1024 * 1024
SUBLANES = 8
BF16_ROWS = 16
ATTN_BLOCK = 256
SB_HEAD_GROUP = 4
DSA_HEAD_GROUP = 8
COUNT_BLOCKS = 4
COUNT_CHUNK = 64
NT_DIMS = (((1,), (1,)), ((), ()))


def _params(*sem):
    return pltpu.CompilerParams(dimension_semantics=sem, vmem_limit_bytes=V7X_VMEM_LIMIT)


def _sigmoid(x):
    return 1.0 / (1.0 + jnp.exp(-x))


def _adaln_kernel(ct_ref, w_ref, b_ref, o_ref):
    ct = ct_ref[...]
    act = ct * _sigmoid(ct)
    w = w_ref[...]
    for b in range(ct.shape[1]):
        o_ref[b:b + 1, :] = jnp.sum(act[:, b:b + 1] * w, axis=0, keepdims=True) + b_ref[...]


def _adaln(c, w, bias):
    B, D = c.shape
    N = w.shape[1]
    tn = min(N, 1024)
    return pl.pallas_call(
        _adaln_kernel,
        grid=(N // tn,),
        in_specs=[pl.BlockSpec((D, B), lambda j: (0, 0)),
                  pl.BlockSpec((D, tn), lambda j: (0, j)),
                  pl.BlockSpec((1, tn), lambda j: (0, j))],
        out_specs=pl.BlockSpec((B, tn), lambda j: (0, j)),
        out_shape=jax.ShapeDtypeStruct((B, N), F32),
        compiler_params=_params("arbitrary"),
        name="adaln",
    )(c.T, w, bias.reshape(1, N))


def _rms_mod(x, g, sc, sh):
    ms = jnp.mean(x * x, axis=-1, keepdims=True)
    y = x * lax.rsqrt(ms + EPS) * g
    return y * (1.0 + sc) + sh


def _norm_mod_kernel(x_ref, g_ref, sc_ref, sh_ref, o_ref):
    o_ref[...] = _rms_mod(x_ref[...], g_ref[...], sc_ref[0], sh_ref[0]).astype(o_ref.dtype)


def _norm_mod(x2, g, sc, sh, S):
    M, D = x2.shape
    B = sc.shape[0]
    tm = min(S, 512)
    per_b = S // tm
    return pl.pallas_call(
        _norm_mod_kernel,
        grid=(M // tm,),
        in_specs=[pl.BlockSpec((tm, D), lambda i: (i, 0)),
                  pl.BlockSpec((1, D), lambda i: (0, 0)),
                  pl.BlockSpec((1, 1, D), lambda i: (i // per_b, 0, 0)),
                  pl.BlockSpec((1, 1, D), lambda i: (i // per_b, 0, 0))],
        out_specs=pl.BlockSpec((tm, D), lambda i: (i, 0)),
        out_shape=jax.ShapeDtypeStruct((M, D), BF16),
        compiler_params=_params("arbitrary"),
        name="norm_mod",
    )(x2, g.reshape(1, D), sc.reshape(B, 1, D), sh.reshape(B, 1, D))


def _in_proj_kernel(a_ref, b_ref, o_ref, *, tile_scales):
    j = pl.program_id(0)
    acc = jnp.dot(a_ref[...], b_ref[...], preferred_element_type=F32)
    scale = jnp.float32(1.0)
    for tile, tile_scale in tile_scales:
        scale = jnp.where(j == tile, tile_scale, scale)
    o_ref[...] = (acc * scale).astype(o_ref.dtype)


def _in_proj(h, w, tn, tile_scales, name):
    M, K = h.shape
    N = w.shape[1]
    tm = min(M, 1024)
    return pl.pallas_call(
        functools.partial(_in_proj_kernel, tile_scales=tile_scales),
        grid=(N // tn, M // tm),
        in_specs=[pl.BlockSpec((tm, K), lambda j, i: (i, 0)),
                  pl.BlockSpec((K, tn), lambda j, i: (0, j))],
        out_specs=pl.BlockSpec((tm, tn), lambda j, i: (i, j)),
        out_shape=jax.ShapeDtypeStruct((M, N), BF16),
        compiler_params=_params("arbitrary", "arbitrary"),
        name=name,
    )(h, w)


def _in_proj_f32w_kernel(a_ref, w_ref, o_ref, wb_ref, *, tile_scales):
    @pl.when(pl.program_id(1) == 0)
    def _():
        wb_ref[...] = w_ref[...].astype(BF16)

    _in_proj_kernel(a_ref, wb_ref, o_ref, tile_scales=tile_scales)


def _in_proj_f32w(h, w_all, layer, n_tiles, tn, tile_scales, name):
    M, K = h.shape
    tm = min(M, 1024)
    return pl.pallas_call(
        functools.partial(_in_proj_f32w_kernel, tile_scales=tile_scales),
        grid=(n_tiles, M // tm),
        in_specs=[pl.BlockSpec((tm, K), lambda j, i: (i, 0)),
                  pl.BlockSpec((None, K, tn), lambda j, i: (layer, 0, j))],
        out_specs=pl.BlockSpec((tm, tn), lambda j, i: (i, j)),
        out_shape=jax.ShapeDtypeStruct((M, n_tiles * tn), BF16),
        scratch_shapes=[pltpu.VMEM((K, tn), BF16)],
        compiler_params=_params("arbitrary", "arbitrary"),
        name=name,
    )(h, w_all)


def _key_query_iotas(k0, q0, tk, tq):
    key_pos = k0 + lax.broadcasted_iota(jnp.int32, (tk, tq), 0)
    query_pos = q0 + lax.broadcasted_iota(jnp.int32, (tk, tq), 1)
    return key_pos, query_pos


def _head_cols(g):
    return slice(g * HEAD_DIM, (g + 1) * HEAD_DIM)


def _sb_kernel(q_ref, k_ref, vt_ref, o_ref, run_ref, acc_ref, *, tq, tk, n_heads):
    qi = pl.program_id(2)
    q0 = qi * tq
    run_ref[...] = jnp.zeros_like(run_ref)
    acc_ref[...] = jnp.zeros_like(acc_ref)
    later = (lax.broadcasted_iota(jnp.int32, (tk, tk), 1)
             > lax.broadcasted_iota(jnp.int32, (tk, tk), 0)).astype(BF16)

    def step(kbs, first_is_diagonal):
        heads = range(n_heads)
        k0s = [pl.multiple_of(kb * tk, tk) for kb in kbs]
        causal = None
        if first_is_diagonal:
            key_pos, query_pos = _key_query_iotas(k0s[0], q0, tk, tq)
            causal = key_pos < query_pos
        masked = [first_is_diagonal and j == 0 for j in range(len(kbs))]
        zs = [[lax.dot_general(k_ref[pl.ds(k0, tk), _head_cols(g)], q_ref[:, _head_cols(g)],
                               NT_DIMS, preferred_element_type=F32) for g in heads] for k0 in k0s]
        sps = [[jnp.maximum(z, 0.0) + jnp.log(1.0 + jnp.exp(-jnp.abs(z))) for z in zj] for zj in zs]
        spms = [[jnp.where(causal, sp, 0.0) if masked[j] else sp for sp in spj]
                for j, spj in enumerate(sps)]
        his = [[spm.astype(BF16) for spm in spj] for spj in spms]
        los = [[(spm - hi.astype(F32)).astype(BF16) for spm, hi in zip(spj, hij)]
               for spj, hij in zip(spms, his)]
        suffixes = [[jnp.dot(later, hi, preferred_element_type=F32)
                     + jnp.dot(later, lo, preferred_element_type=F32) for hi, lo in zip(hij, loj)]
                    for hij, loj in zip(his, los)]
        weights, new_runs = [], []
        for g in heads:
            run = run_ref[g]
            head_weights = []
            for j in range(len(kbs)):
                a = jnp.exp(zs[j][g] - sps[j][g] - suffixes[j][g] - run)
                if masked[j]:
                    a = jnp.where(causal, a, 0.0)
                head_weights.append(a.astype(BF16))
                run = run + jnp.sum(spms[j][g], axis=0, keepdims=True)
            weights.append(head_weights)
            new_runs.append(run)
        min_run = None
        for g in heads:
            acc = acc_ref[g]
            for j, k0 in enumerate(k0s):
                acc = acc + jnp.dot(vt_ref[_head_cols(g), pl.ds(k0, tk)], weights[g][j],
                                    preferred_element_type=F32)
            acc_ref[g] = acc
            run_ref[g] = new_runs[g]
            head_min = jnp.min(new_runs[g])
            min_run = head_min if min_run is None else jnp.minimum(min_run, head_min)
        return min_run > EXP_ZERO_BELOW

    def body(carry):
        kb, _ = carry
        return kb - 1, step([kb], False)

    def cond(carry):
        kb, dead = carry
        return jnp.logical_and(kb >= 0, jnp.logical_not(dead))

    @pl.when(qi == 0)
    def _():
        step([qi], True)

    @pl.when(qi >= 1)
    def _():
        dead = step([qi, qi - 1], True)
        lax.while_loop(cond, body, (qi - 2, dead))

    for g in range(n_heads):
        o_ref[:, _head_cols(g)] = acc_ref[g].T.astype(o_ref.dtype)


def _attn_group_specs(S, tq, nq, G, q_blk, k_blk, kv_buffers):
    assert q_blk % G == 0 and k_blk % G == 0
    gw = G * HEAD_DIM
    mode = pl.Buffered(kv_buffers)
    return [pl.BlockSpec((tq, gw), lambda b, hg, i: (b * nq + i, q_blk // G + hg)),
            pl.BlockSpec((S, gw), lambda b, hg, i: (b, k_blk // G + hg), pipeline_mode=mode),
            pl.BlockSpec((gw, S), lambda b, hg, i: (hg, b), pipeline_mode=mode)]


def _sb_attn(z, vt, B, S, H, q_blk, k_blk):
    M = z.shape[0]
    tq = tk = min(S, ATTN_BLOCK)
    nq = S // tq
    G = min(H, SB_HEAD_GROUP)
    gw = G * HEAD_DIM
    return pl.pallas_call(
        functools.partial(_sb_kernel, tq=tq, tk=tk, n_heads=G),
        grid=(B, H // G, nq),
        in_specs=_attn_group_specs(S, tq, nq, G, q_blk, k_blk, kv_buffers=2),
        out_specs=pl.BlockSpec((tq, gw), lambda b, hg, i: (b * nq + i, hg)),
        out_shape=jax.ShapeDtypeStruct((M, H * HEAD_DIM), BF16),
        scratch_shapes=[pltpu.VMEM((G, 1, tq), F32), pltpu.VMEM((G, HEAD_DIM, tq), F32)],
        compiler_params=_params("arbitrary", "arbitrary", "arbitrary"),
        name="sb_attn",
    )(z, z, vt)


def _dsa_select_kernel(q_ref, k_ref, w_ref, mask_ref, hi_ref, lo_ref, *, tq, tk, n_sel, n_kblocks):
    qi = pl.program_id(1)
    q0 = qi * tq
    nkb = (q0 + tq + tk - 1) // tk
    lane = lax.broadcasted_iota(jnp.int32, (tk, 2 * IDX_DIM), 1)
    w_scale = (N_IDX_HEADS ** -0.5) * (IDX_DIM ** -0.5)
    wt = (w_ref[...].astype(F32) * w_scale).T

    def score_block(kb, diagonal):
        k0 = pl.multiple_of(kb * tk, tk)
        kk = k_ref[pl.ds(k0, tk), :]
        k_halves = (jnp.where(lane < IDX_DIM, kk, jnp.zeros_like(kk)),
                    jnp.where(lane >= IDX_DIM, kk, jnp.zeros_like(kk)))
        acc = jnp.zeros((tk, tq), F32)
        for p in range(N_IDX_HEADS // 2):
            q2 = q_ref[:, p * 2 * IDX_DIM:(p + 1) * 2 * IDX_DIM]
            for half in range(2):
                h = 2 * p + half
                ph = lax.dot_general(k_halves[half], q2, NT_DIMS, preferred_element_type=F32)
                acc = acc + jnp.maximum(ph, 0.0) * wt[h:h + 1, :]
        bits = lax.bitcast_convert_type(acc, jnp.int32)
        key = bits ^ ((bits >> 31) & 0x7FFFFFFF)
        if diagonal:
            key_pos, query_pos = _key_query_iotas(k0, q0, tk, tq)
            key = jnp.where(key_pos <= query_pos, key, INT_MIN)
        hi_ref[pl.ds(k0, tk), :] = (key >> 16).astype(I16)
        lo_ref[pl.ds(k0, tk), :] = (key ^ 0x8000).astype(I16)

    def score_pair(i, carry):
        score_block(2 * i, False)
        score_block(2 * i + 1, False)
        return carry

    lax.fori_loop(0, (nkb - 1) // 2, score_pair, 0)

    @pl.when((nkb - 1) % 2 == 1)
    def _():
        score_block(nkb - 2, False)

    score_block(nkb - 1, True)

    n_count = (nkb + COUNT_BLOCKS - 1) // COUNT_BLOCKS
    rows = COUNT_BLOCKS * tk

    def pad_block(kb, carry):
        k0 = pl.multiple_of(kb * tk, tk)
        hi_ref[pl.ds(k0, tk), :] = jnp.full((tk, tq), I16_MIN, I16)
        lo_ref[pl.ds(k0, tk), :] = jnp.full((tk, tq), I16_MIN, I16)
        return carry

    lax.fori_loop(nkb, n_count * COUNT_BLOCKS, pad_block, 0)

    def count(flags_fn):
        def blk(i, cnt):
            r0 = pl.multiple_of(i * rows, rows)
            groups = []
            for c in range(rows // COUNT_CHUNK):
                flags = flags_fn(r0 + c * COUNT_CHUNK, COUNT_CHUNK)
                parts = [flags[r * BF16_ROWS:(r + 1) * BF16_ROWS, :]
                         for r in range(COUNT_CHUNK // BF16_ROWS)]
                while len(parts) > 1:
                    parts = [a + b for a, b in zip(parts[0::2], parts[1::2])]
                groups.append(parts[0])
            while len(groups) > 1:
                groups = [a + b for a, b in zip(groups[0::2], groups[1::2])]
            return cnt + groups[0]
        cnt = lax.fori_loop(0, n_count, blk, jnp.zeros((BF16_ROWS, tq), I16))
        return jnp.sum(cnt.astype(jnp.int32), axis=0, keepdims=True)

    flag, no_flag = I16(1), I16(0)

    def count_ge(ref, thr16):
        return count(lambda r, n: jnp.where(ref[pl.ds(r, n), :] >= thr16, flag, no_flag))

    def kth_largest(ref, k):
        def bisect(it, carry):
            thr, n_ge, n_gt = carry
            cand = thr + jnp.left_shift(jnp.int32(1), 15 - it)
            cnt = count_ge(ref, cand.astype(I16))
            ok = cnt >= k
            return jnp.where(ok, cand, thr), jnp.where(ok, cnt, n_ge), jnp.where(ok, n_gt, cnt)
        zeros = jnp.zeros((1, tq), jnp.int32)
        thr, n_ge, n_gt = lax.fori_loop(0, 16, bisect, (zeros + I16_MIN, zeros, zeros))
        return thr.astype(I16), n_ge, n_gt

    thr_hi, _, above_hi = kth_largest(hi_ref, n_sel)
    need_lo = n_sel - above_hi

    def park(kb, carry):
        k0 = pl.multiple_of(kb * tk, tk)
        lo_ref[pl.ds(k0, tk), :] = jnp.where(hi_ref[pl.ds(k0, tk), :] == thr_hi,
                                             lo_ref[pl.ds(k0, tk), :], I16(I16_MIN))
        return carry

    lax.fori_loop(0, nkb, park, 0)
    thr_lo, at_least_lo, above_lo = kth_largest(lo_ref, need_lo)
    need_ties = need_lo - above_lo
    n_ties = jnp.where(thr_lo == I16(I16_MIN), tk * n_kblocks, at_least_lo - above_lo)
    all_ties_fit = jnp.max(n_ties - need_ties) <= 0

    one, zero, neg = BF16(1.0), BF16(0.0), BF16(NEG)

    def store_mask(kb, sel, diagonal):
        k0 = pl.multiple_of(kb * tk, tk)
        if diagonal:
            key_pos, query_pos = _key_query_iotas(k0, q0, tk, tq)
            sel = sel * (key_pos <= query_pos).astype(BF16)
        mask_ref[0, pl.ds(k0, tk), :] = jnp.where(sel > zero, zero, neg)

    def emit_all_ties(kb, diagonal):
        k0 = pl.multiple_of(kb * tk, tk)
        hi = hi_ref[pl.ds(k0, tk), :]
        lo = lo_ref[pl.ds(k0, tk), :]
        store_mask(kb, jnp.where(hi > thr_hi, one,
                                 jnp.where(hi == thr_hi, jnp.where(lo >= thr_lo, one, zero), zero)),
                   diagonal)

    earlier = (lax.broadcasted_iota(jnp.int32, (tk, tk), 1)
               < lax.broadcasted_iota(jnp.int32, (tk, tk), 0)).astype(BF16)
    ones_rows = jnp.ones((BF16_ROWS, tk), BF16)
    need_ties_f = need_ties.astype(F32)

    def emit_ranked_ties(kb, ties_seen, diagonal):
        k0 = pl.multiple_of(kb * tk, tk)
        hi = hi_ref[pl.ds(k0, tk), :]
        lo = lo_ref[pl.ds(k0, tk), :]
        above = jnp.where(hi > thr_hi, one, jnp.where(lo > thr_lo, one, zero))
        eq = jnp.where(hi == thr_hi, jnp.where(lo == thr_lo, one, zero), zero)
        rank = jnp.dot(earlier, eq, preferred_element_type=F32).astype(BF16)
        room = jnp.clip(need_ties_f - ties_seen, -1.0, float(tk)).astype(BF16)
        store_mask(kb, jnp.where(rank < room, jnp.maximum(above, eq), above), diagonal)
        return ties_seen + jnp.dot(ones_rows, eq, preferred_element_type=F32)[0:1]

    def emit_fast():
        def body(kb, carry):
            emit_all_ties(kb, False)
            return carry
        lax.fori_loop(0, nkb - 1, body, 0)
        emit_all_ties(nkb - 1, True)

    def emit_slow():
        ties_seen = lax.fori_loop(0, nkb - 1, lambda kb, seen: emit_ranked_ties(kb, seen, False),
                                  jnp.zeros((1, tq), F32))
        emit_ranked_ties(nkb - 1, ties_seen, True)

    lax.cond(all_ties_fit, emit_fast, emit_slow)

    def fill(kb, carry):
        k0 = pl.multiple_of(kb * tk, tk)
        mask_ref[0, pl.ds(k0, tk), :] = jnp.full((tk, tq), NEG, mask_ref.dtype)
        return carry

    lax.fori_loop(nkb, n_kblocks, fill, 0)


def _dsa_select(z, zs, B, S, qix_blk, n_sel):
    tq = tk = min(S, ATTN_BLOCK)
    nq = S // tq
    assert nq % COUNT_BLOCKS == 0
    qw = N_IDX_HEADS * IDX_DIM
    return pl.pallas_call(
        functools.partial(_dsa_select_kernel, tq=tq, tk=tk, n_sel=n_sel, n_kblocks=S // tk),
        grid=(B, nq),
        in_specs=[pl.BlockSpec((tq, qw), lambda b, i: (b * nq + i, qix_blk)),
                  pl.BlockSpec((S, 2 * IDX_DIM), lambda b, i: (b, 0)),
                  pl.BlockSpec((tq, 2 * IDX_DIM), lambda b, i: (b * nq + i, 1))],
        out_specs=pl.BlockSpec((1, S, tq), lambda b, i: (b * nq + i, 0, 0)),
        out_shape=jax.ShapeDtypeStruct((B * nq, S, tq), BF16),
        scratch_shapes=[pltpu.VMEM((S, tq), I16), pltpu.VMEM((S, tq), I16)],
        compiler_params=_params("arbitrary", "arbitrary"),
        name="dsa_select",
    )(z, zs, zs)


def _rel_bucket(dist):
    n = jnp.maximum(dist, 0)
    max_exact = N_BUCKETS // 2
    nf = jnp.maximum(n, 1).astype(F32)
    large = max_exact + (jnp.log(nf / max_exact) / math.log(MAX_DISTANCE / max_exact)
                         * (N_BUCKETS - max_exact)).astype(jnp.int32)
    large = jnp.minimum(large, N_BUCKETS - 1)
    return jnp.where(n < max_exact, n, large)


def _bias_tiles(rel_bias, tq):
    assert tq >= MAX_DISTANCE
    rb = rel_bias.astype(F32)
    key = jnp.arange(tq)[:, None]
    query = jnp.arange(tq)[None, :]
    bucket = _rel_bucket(jnp.stack([query - key, tq + query - key]))
    onehot = (bucket[None] == jnp.arange(N_BUCKETS)[:, None, None, None]).astype(F32)
    return jnp.einsum("nh,nikq->hikq", (rb - rb[N_BUCKETS - 1]) * LOG2_E, onehot,
                      precision=lax.Precision.HIGHEST)


def _dsa_attn_kernel(q_ref, k_ref, vt_ref, mask_ref, bias_ref, o_ref, m_ref, acc_ref, logit_ref, *,
                     tq, n_heads):
    qi = pl.program_id(2)
    m_ref[...] = jnp.full_like(m_ref, NEG)
    acc_ref[...] = jnp.zeros_like(acc_ref)
    ones_rows = jnp.ones((BF16_ROWS, tq), BF16)

    def step(blocks):
        k0s = [pl.multiple_of(kb * tq, tq) for kb, _ in blocks]
        masks = [mask_ref[0, pl.ds(k0, tq), :].astype(F32) for k0 in k0s]

        def head_scores(g):
            return [lax.dot_general(k_ref[pl.ds(k0, tq), _head_cols(g)], q_ref[:, _head_cols(g)],
                                    NT_DIMS, preferred_element_type=F32) for k0 in k0s]

        def head_logits(g, scores):
            m_new = m_ref[g]
            for j, (_, bias_idx) in enumerate(blocks):
                s = scores[j] + masks[j]
                if bias_idx is not None:
                    s = s + bias_ref[g, bias_idx]
                logit_ref[g, j] = s
                m_new = jnp.maximum(m_new, jnp.max(s, axis=0, keepdims=True))
            return m_new

        def head_accumulate(g, m_new):
            acc = jnp.exp2(m_ref[g] - m_new) * acc_ref[g]
            for j, k0 in enumerate(k0s):
                p = jnp.exp2(logit_ref[g, j] - m_new).astype(BF16)
                vt = jnp.concatenate([vt_ref[_head_cols(g), pl.ds(k0, tq)], ones_rows], axis=0)
                acc = acc + jnp.dot(vt, p, preferred_element_type=F32)
            acc_ref[g] = acc
            m_ref[g] = m_new

        scores = [head_scores(g) for g in range(n_heads)]
        maxima = [head_logits(g, scores[g]) for g in range(n_heads)]
        for g in range(n_heads):
            head_accumulate(g, maxima[g])

    n_far = jnp.maximum(qi - 1, 0)

    def far_pair(i, carry):
        step([(2 * i, None), (2 * i + 1, None)])
        return carry

    lax.fori_loop(0, n_far // 2, far_pair, 0)

    @pl.when(n_far % 2 == 1)
    def _():
        step([(n_far - 1, None)])

    @pl.when(qi >= 1)
    def _():
        step([(qi - 1, 1), (qi, 0)])

    @pl.when(qi == 0)
    def _():
        step([(qi, 0)])
    for g in range(n_heads):
        acc = acc_ref[g]
        out_t = acc[:HEAD_DIM] / acc[HEAD_DIM:HEAD_DIM + 1]
        o_ref[:, _head_cols(g)] = out_t.T.astype(o_ref.dtype)


def _dsa_attn(z, vt, mask, bias_tiles, B, S, H, q_blk, k_blk):
    M = z.shape[0]
    tq = bias_tiles.shape[-1]
    nq = S // tq
    G = min(H, DSA_HEAD_GROUP)
    gw = G * HEAD_DIM
    return pl.pallas_call(
        functools.partial(_dsa_attn_kernel, tq=tq, n_heads=G),
        grid=(B, H // G, nq),
        in_specs=_attn_group_specs(S, tq, nq, G, q_blk, k_blk, kv_buffers=1) + [
            pl.BlockSpec((1, S, tq), lambda b, hg, i: (b * nq + i, 0, 0)),
            pl.BlockSpec((G, 2, tq, tq), lambda b, hg, i: (hg, 0, 0, 0),
                         pipeline_mode=pl.Buffered(1))],
        out_specs=pl.BlockSpec((tq, gw), lambda b, hg, i: (b * nq + i, hg)),
        out_shape=jax.ShapeDtypeStruct((M, H * HEAD_DIM), BF16),
        scratch_shapes=[pltpu.VMEM((G, 1, tq), F32),
                        pltpu.VMEM((G, HEAD_DIM + BF16_ROWS, tq), F32),
                        pltpu.VMEM((G, 2, tq, tq), F32)],
        compiler_params=_params("arbitrary", "arbitrary", "arbitrary"),
        name="dsa_attn",
    )(z, z, vt, mask, bias_tiles)


def _mix_out_kernel(osb_ref, ods_ref, gsb_ref, gds_ref, x_ref, wsb_ref, wds_ref, wout_ref,
                    gt_ref, g_ref, sc_ref, sh_ref, x1_ref, h2_ref):
    t_sb = jnp.dot(osb_ref[...], wsb_ref[...], preferred_element_type=F32)
    t_ds = jnp.dot(ods_ref[...], wds_ref[...], preferred_element_type=F32)
    merged = (_sigmoid(gsb_ref[...].astype(F32)) * t_sb
              + _sigmoid(gds_ref[...].astype(F32)) * t_ds)
    y = jnp.dot(merged.astype(BF16), wout_ref[...], preferred_element_type=F32)
    x1 = x_ref[...] + gt_ref[0] * y
    x1_ref[...] = x1
    h2_ref[...] = _rms_mod(x1, g_ref[...], sc_ref[0], sh_ref[0]).astype(h2_ref.dtype)


def _mix_out(o_sb, o_ds, z, gate_blk, x2, w_sb, w_ds, w_out, gt, g, sc, sh, S):
    M, D = x2.shape
    B = gt.shape[0]
    W = o_sb.shape[1]
    tm = min(S, 512)
    per_b = S // tm
    row = lambda i: (i, 0)
    const = lambda i: (0, 0)
    per_batch = lambda i: (i // per_b, 0, 0)
    return pl.pallas_call(
        _mix_out_kernel,
        grid=(M // tm,),
        in_specs=[pl.BlockSpec((tm, W), row),
                  pl.BlockSpec((tm, W), row),
                  pl.BlockSpec((tm, D), lambda i: (i, gate_blk)),
                  pl.BlockSpec((tm, D), lambda i: (i, gate_blk + 1)),
                  pl.BlockSpec((tm, D), row),
                  pl.BlockSpec((W, D), const),
                  pl.BlockSpec((W, D), const),
                  pl.BlockSpec((D, D), const),
                  pl.BlockSpec((1, 1, D), per_batch),
                  pl.BlockSpec((1, D), const),
                  pl.BlockSpec((1, 1, D), per_batch),
                  pl.BlockSpec((1, 1, D), per_batch)],
        out_specs=[pl.BlockSpec((tm, D), row), pl.BlockSpec((tm, D), row)],
        out_shape=[jax.ShapeDtypeStruct((M, D), F32), jax.ShapeDtypeStruct((M, D), BF16)],
        compiler_params=_params("arbitrary"),
        name="mix_out",
    )(o_sb, o_ds, z, z, x2, w_sb, w_ds, w_out, gt.reshape(B, 1, D), g.reshape(1, D),
      sc.reshape(B, 1, D), sh.reshape(B, 1, D))


HALO = BF16_ROWS


def _ffn_act_kernel(h_ref, halo_ref, wg_ref, wu_ref, cw_ref, cb_ref, o_ref, *, tiles_per_seq):
    i = pl.program_id(0)
    h = h_ref[...]
    wg = wg_ref[...]
    g0 = jnp.dot(h, wg, preferred_element_type=F32)
    g_prev = jnp.dot(halo_ref[...], wg, preferred_element_type=F32)
    g_prev = jnp.where(i % tiles_per_seq == 0, 0.0, g_prev)
    ridx = lax.broadcasted_iota(jnp.int32, g0.shape, 0)
    g1 = jnp.where(ridx == 0, g_prev[HALO - 1:HALO, :], pltpu.roll(g0, 1, 0))
    g2 = jnp.where(ridx == 0, g_prev[HALO - 2:HALO - 1, :],
                   jnp.where(ridx == 1, g_prev[HALO - 1:HALO, :], pltpu.roll(g0, 2, 0)))
    cw = cw_ref[...]
    a = cb_ref[...] + g2 * cw[0:1, :] + g1 * cw[1:2, :] + g0 * cw[2:3, :]
    u = jnp.dot(h, wu_ref[...], preferred_element_type=F32)
    o_ref[...] = (a * _sigmoid(a) * u).astype(o_ref.dtype)


def _ffn_out_kernel(a_ref, wd_ref, x1_ref, gt_ref, gf_ref, o_ref):
    k = pl.program_id(1)

    @pl.when(k == 0)
    def _():
        o_ref[...] = jnp.zeros_like(o_ref)

    o_ref[...] += jnp.dot(a_ref[...], wd_ref[...], preferred_element_type=F32)

    @pl.when(k == pl.num_programs(1) - 1)
    def _():
        x = x1_ref[...] + gt_ref[0] * o_ref[...]
        ms = jnp.mean(x * x, axis=-1, keepdims=True)
        o_ref[...] = x * lax.rsqrt(ms + EPS) * gf_ref[...]


def _conv_ffn(h2, x1, w_gate, w_up, w_down, conv_w, conv_b, gt, g_final, S):
    M, D = x1.shape
    B = gt.shape[0]
    F = w_gate.shape[1]
    tm = min(S, 1024)
    tf = min(F, 512)
    per_b = S // tm
    halo_per_tile = tm // HALO
    act = pl.pallas_call(
        functools.partial(_ffn_act_kernel, tiles_per_seq=per_b),
        grid=(M // tm, F // tf),
        in_specs=[pl.BlockSpec((tm, D), lambda i, f: (i, 0)),
                  pl.BlockSpec((HALO, D), lambda i, f: (jnp.maximum(i * halo_per_tile - 1, 0), 0)),
                  pl.BlockSpec((D, tf), lambda i, f: (0, f)),
                  pl.BlockSpec((D, tf), lambda i, f: (0, f)),
                  pl.BlockSpec((CONV_WIDTH, tf), lambda i, f: (0, f)),
                  pl.BlockSpec((1, tf), lambda i, f: (0, f))],
        out_specs=pl.BlockSpec((tm, tf), lambda i, f: (i, f)),
        out_shape=jax.ShapeDtypeStruct((M, F), BF16),
        compiler_params=_params("arbitrary", "arbitrary"),
        name="ffn_act",
    )(h2, h2, w_gate, w_up, conv_w, conv_b.reshape(1, F))
    return pl.pallas_call(
        _ffn_out_kernel,
        grid=(M // tm, F // tf),
        in_specs=[pl.BlockSpec((tm, tf), lambda i, k: (i, k)),
                  pl.BlockSpec((tf, D), lambda i, k: (k, 0)),
                  pl.BlockSpec((tm, D), lambda i, k: (i, 0)),
                  pl.BlockSpec((1, 1, D), lambda i, k: (i // per_b, 0, 0)),
                  pl.BlockSpec((1, D), lambda i, k: (0, 0))],
        out_specs=pl.BlockSpec((tm, D), lambda i, k: (i, 0)),
        out_shape=jax.ShapeDtypeStruct((M, D), F32),
        compiler_params=_params("arbitrary", "arbitrary"),
        name="ffn_out",
    )(act, w_down, x1, gt.reshape(B, 1, D), g_final.reshape(1, D))


def kernel(x, c, w_ada, b_ada, g_mix, w_in, w_o_sb, w_o_dsa, w_out, rel_bias, g_ffn, w_gate,
           w_up, conv_w, conv_b, w_down, g_final):
    B, S, D = x.shape
    depth = w_ada.shape[0]
    W = w_o_sb.shape[1]
    H = W // HEAD_DIM
    qw = N_IDX_HEADS * IDX_DIM
    assert w_o_dsa.shape[1] == W and D % W == 0 and S % ATTN_BLOCK == 0 and (6 * W) % qw == 0
    n_sel = min(TOPK_MAX, S // 4)
    scale = HEAD_DIM ** -0.5
    x2 = x.reshape(B * S, D)

    for l in range(depth):
        mod = _adaln(c, w_ada[l], b_ada[l])
        sh1, sc1, gt1, sh2, sc2, gt2 = jnp.split(mod, 6, axis=-1)

        wl = w_in[l]
        o_qix = 6 * W
        o_kix = o_qix + qw
        o_wix = o_kix + IDX_DIM
        o_gate = o_wix + N_IDX_HEADS
        w_gates = wl[:, o_gate:].astype(BF16)
        k_cols = wl[:, o_kix:o_wix]
        w_tail = jnp.concatenate(
            [k_cols, k_cols, wl[:, o_wix:o_gate],
             jnp.zeros((D, 2 * IDX_DIM - N_IDX_HEADS), wl.dtype)], axis=1).astype(BF16)

        h1 = _norm_mod(x2, g_mix[l], sc1, sh1, S)
        z = _in_proj_f32w(h1, w_in, l, o_kix // W, W, ((0, scale), (3, scale * LOG2_E)), "in_proj")
        zg = _in_proj(h1, w_gates, W, (), "in_proj_gates")
        zs = _in_proj(h1, w_tail, 4 * IDX_DIM, (), "in_proj_idx")

        o_sb = _sb_attn(z, z[:, 2 * W:3 * W].T, B, S, H, 0, H)
        mask = _dsa_select(z, zs, B, S, o_qix // qw, n_sel)
        o_ds = _dsa_attn(z, z[:, 5 * W:6 * W].T, mask, _bias_tiles(rel_bias, min(S, ATTN_BLOCK)),
                         B, S, H, 3 * H, 4 * H)

        x2, h2 = _mix_out(o_sb, o_ds, zg, 0, x2, w_o_sb[l].astype(BF16),
                          w_o_dsa[l].astype(BF16), w_out[l].astype(BF16), gt1, g_ffn[l], sc2, sh2, S)
        last = l == depth - 1
        assert last, "the final rms_norm is fused into the last layer's FFN"
        x2 = _conv_ffn(h2, x2, w_gate[l].astype(BF16), w_up[l].astype(BF16), w_down[l].astype(BF16),
                       conv_w[l], conv_b[l], gt2, g_final, S)
    return x2.reshape(B, S, D)
```

```python
import functools
import math

import jax
import jax.numpy as jnp
from jax import lax
from jax.experimental import pallas as pl
from jax.experimental.pallas import tpu as pltpu

HEAD_DIM = 128
N_IDX_HEADS = 16
IDX_DIM = 64
TOPK_MAX = 256
N_BUCKETS = 32
MAX_DISTANCE = 128
CONV_WIDTH = 3
EPS = 1e-6

F32 = jnp.float32
BF16 = jnp.bfloat16
NEG = -1e30
I16 = jnp.int16
INT_MIN = -2 ** 31
I16_MIN = -2 ** 15
EXP_ZERO_BELOW = 104.0
LOG2_E = math.log2(math.e)
V7X_VMEM_LIMIT = 56 * 1024 * 1024
SUBLANES = 8
BF16_ROWS = 16
ATTN_BLOCK = 256
SB_HEAD_GROUP = 4
DSA_HEAD_GROUP = 8
COUNT_BLOCKS = 4
COUNT_CHUNK = 64
NT_DIMS = (((1,), (1,)), ((), ()))


def _params(*sem):
    return pltpu.CompilerParams(dimension_semantics=sem, vmem_limit_bytes=V7X_VMEM_LIMIT)


def _sigmoid(x):
    return 1.0 / (1.0 + jnp.exp(-x))


def _adaln_kernel(ct_ref, w_ref, b_ref, o_ref):
    ct = ct_ref[...]
    act = ct * _sigmoid(ct)
    w = w_ref[...]
    for b in range(ct.shape[1]):
        o_ref[b:b + 1, :] = jnp.sum(act[:, b:b + 1] * w, axis=0, keepdims=True) + b_ref[...]


def _adaln(c, w, bias):
    B, D = c.shape
    N = w.shape[1]
    tn = min(N, 1024)
    return pl.pallas_call(
        _adaln_kernel,
        grid=(N // tn,),
        in_specs=[pl.BlockSpec((D, B), lambda j: (0, 0)),
                  pl.BlockSpec((D, tn), lambda j: (0, j)),
                  pl.BlockSpec((1, tn), lambda j: (0, j))],
        out_specs=pl.BlockSpec((B, tn), lambda j: (0, j)),
        out_shape=jax.ShapeDtypeStruct((B, N), F32),
        compiler_params=_params("arbitrary"),
        name="adaln",
    )(c.T, w, bias.reshape(1, N))


def _rms_mod(x, g, sc, sh):
    ms = jnp.mean(x * x, axis=-1, keepdims=True)
    y = x * lax.rsqrt(ms + EPS) * g
    return y * (1.0 + sc) + sh


def _norm_mod_kernel(x_ref, g_ref, sc_ref, sh_ref, o_ref):
    o_ref[...] = _rms_mod(x_ref[...], g_ref[...], sc_ref[0], sh_ref[0]).astype(o_ref.dtype)


def _norm_mod(x2, g, sc, sh, S):
    M, D = x2.shape
    B = sc.shape[0]
    tm = min(S, 512)
    per_b = S // tm
    return pl.pallas_call(
        _norm_mod_kernel,
        grid=(M // tm,),
        in_specs=[pl.BlockSpec((tm, D), lambda i: (i, 0)),
                  pl.BlockSpec((1, D), lambda i: (0, 0)),
                  pl.BlockSpec((1, 1, D), lambda i: (i // per_b, 0, 0)),
                  pl.BlockSpec((1, 1, D), lambda i: (i // per_b, 0, 0))],
        out_specs=pl.BlockSpec((tm, D), lambda i: (i, 0)),
        out_shape=jax.ShapeDtypeStruct((M, D), BF16),
        compiler_params=_params("arbitrary"),
        name="norm_mod",
    )(x2, g.reshape(1, D), sc.reshape(B, 1, D), sh.reshape(B, 1, D))


def _in_proj_kernel(a_ref, b_ref, o_ref, *, tile_scales):
    j = pl.program_id(0)
    acc = jnp.dot(a_ref[...], b_ref[...], preferred_element_type=F32)
    scale = jnp.float32(1.0)
    for tile, tile_scale in tile_scales:
        scale = jnp.where(j == tile, tile_scale, scale)
    o_ref[...] = (acc * scale).astype(o_ref.dtype)


def _in_proj_wt_kernel(a_ref, wt_ref, o_ref, wb_ref, *, tile_scales):
    @pl.when(pl.program_id(1) == 0)
    def _():
        wb_ref[...] = wt_ref[...].T.astype(BF16)

    _in_proj_kernel(a_ref, wb_ref, o_ref, tile_scales=tile_scales)


def _in_proj_t_kernel(a_ref, wt_ref, o_ref, wb_ref):
    @pl.when(pl.program_id(1) == 0)
    def _():
        wb_ref[...] = wt_ref[...].astype(BF16)

    o_ref[...] = lax.dot_general(wb_ref[...], a_ref[...], NT_DIMS,
                                 preferred_element_type=F32).astype(o_ref.dtype)


def _tile_index_fn(tiles):
    steps = []
    for j, t in enumerate(tiles):
        while len(steps) < t - j:
            steps.append(j)
    return lambda j: j + sum(jnp.where(j >= first, 1, 0) for first in steps)


def _in_proj_wt(h, wt, tiles, tn, tile_scales, name, transposed=False):
    M, K = h.shape
    tm = min(M, 1024)
    n = len(tiles)
    tile_of = _tile_index_fn(tiles)
    in_specs = [pl.BlockSpec((tm, K), lambda j, i: (i, 0)),
                pl.BlockSpec((tn, K), lambda j, i: (tile_of(j), 0))]
    if transposed:
        assert not tile_scales
        body, wb_shape = _in_proj_t_kernel, (tn, K)
        out_spec = pl.BlockSpec((tn, tm), lambda j, i: (j, i))
        out_shape = jax.ShapeDtypeStruct((n * tn, M), BF16)
    else:
        body, wb_shape = functools.partial(_in_proj_wt_kernel, tile_scales=tile_scales), (K, tn)
        out_spec = pl.BlockSpec((tm, tn), lambda j, i: (i, j))
        out_shape = jax.ShapeDtypeStruct((M, n * tn), BF16)
    return pl.pallas_call(
        body,
        grid=(n, M // tm),
        in_specs=in_specs,
        out_specs=out_spec,
        out_shape=out_shape,
        scratch_shapes=[pltpu.VMEM(wb_shape, BF16)],
        compiler_params=_params("arbitrary", "arbitrary"),
        name=name,
    )(h, wt)


def _key_query_iotas(k0, q0, tk, tq):
    key_pos = k0 + lax.broadcasted_iota(jnp.int32, (tk, tq), 0)
    query_pos = q0 + lax.broadcasted_iota(jnp.int32, (tk, tq), 1)
    return key_pos, query_pos


def _head_cols(g):
    return slice(g * HEAD_DIM, (g + 1) * HEAD_DIM)


def _sb_kernel(q_ref, k_ref, vt_ref, o_ref, run_ref, acc_ref, *, tq, tk, n_heads):
    qi = pl.program_id(2)
    q0 = qi * tq
    run_ref[...] = jnp.zeros_like(run_ref)
    acc_ref[...] = jnp.zeros_like(acc_ref)
    later = (lax.broadcasted_iota(jnp.int32, (tk, tk), 1)
             > lax.broadcasted_iota(jnp.int32, (tk, tk), 0)).astype(BF16)

    def step(kbs, first_is_diagonal):
        heads = range(n_heads)
        k0s = [pl.multiple_of(kb * tk, tk) for kb in kbs]
        causal = None
        if first_is_diagonal:
            key_pos, query_pos = _key_query_iotas(k0s[0], q0, tk, tq)
            causal = key_pos < query_pos
        masked = [first_is_diagonal and j == 0 for j in range(len(kbs))]
        zs = [[lax.dot_general(k_ref[pl.ds(k0, tk), _head_cols(g)], q_ref[:, _head_cols(g)],
                               NT_DIMS, preferred_element_type=F32) for g in heads] for k0 in k0s]
        sps = [[jnp.maximum(z, 0.0) + jnp.log(1.0 + jnp.exp(-jnp.abs(z))) for z in zj] for zj in zs]
        spms = [[jnp.where(causal, sp, 0.0) if masked[j] else sp for sp in spj]
                for j, spj in enumerate(sps)]
        his = [[spm.astype(BF16) for spm in spj] for spj in spms]
        los = [[(spm - hi.astype(F32)).astype(BF16) for spm, hi in zip(spj, hij)]
               for spj, hij in zip(spms, his)]
        suffixes = [[jnp.dot(later, hi, preferred_element_type=F32)
                     + jnp.dot(later, lo, preferred_element_type=F32) for hi, lo in zip(hij, loj)]
                    for hij, loj in zip(his, los)]
        weights, new_runs = [], []
        for g in heads:
            run = run_ref[g]
            head_weights = []
            for j in range(len(kbs)):
                a = jnp.exp(zs[j][g] - sps[j][g] - suffixes[j][g] - run)
                if masked[j]:
                    a = jnp.where(causal, a, 0.0)
                head_weights.append(a.astype(BF16))
                run = run + jnp.sum(spms[j][g], axis=0, keepdims=True)
            weights.append(head_weights)
            new_runs.append(run)
        min_run = None
        for g in heads:
            acc = acc_ref[g]
            for j, k0 in enumerate(k0s):
                acc = acc + jnp.dot(vt_ref[_head_cols(g), pl.ds(k0, tk)], weights[g][j],
                                    preferred_element_type=F32)
            acc_ref[g] = acc
            run_ref[g] = new_runs[g]
            head_min = jnp.min(new_runs[g])
            min_run = head_min if min_run is None else jnp.minimum(min_run, head_min)
        return min_run > EXP_ZERO_BELOW

    def body(carry):
        kb, _ = carry
        return kb - 1, step([kb], False)

    def cond(carry):
        kb, dead = carry
        return jnp.logical_and(kb >= 0, jnp.logical_not(dead))

    @pl.when(qi == 0)
    def _():
        step([qi], True)

    @pl.when(qi >= 1)
    def _():
        dead = step([qi, qi - 1], True)
        lax.while_loop(cond, body, (qi - 2, dead))

    for g in range(n_heads):
        o_ref[:, _head_cols(g)] = acc_ref[g].T.astype(o_ref.dtype)


def _attn_group_specs(S, tq, nq, G, q_blk, k_blk, v_blk, kv_buffers):
    assert q_blk % G == 0 and k_blk % G == 0 and v_blk % G == 0
    gw = G * HEAD_DIM
    mode = pl.Buffered(kv_buffers)
    return [pl.BlockSpec((tq, gw), lambda b, hg, i: (b * nq + i, q_blk // G + hg)),
            pl.BlockSpec((S, gw), lambda b, hg, i: (b, k_blk // G + hg), pipeline_mode=mode),
            pl.BlockSpec((gw, S), lambda b, hg, i: (v_blk // G + hg, b), pipeline_mode=mode)]


def _sb_attn(z, vt, B, S, H, q_blk, k_blk, v_blk):
    M = z.shape[0]
    tq = tk = min(S, ATTN_BLOCK)
    nq = S // tq
    G = min(H, SB_HEAD_GROUP)
    gw = G * HEAD_DIM
    return pl.pallas_call(
        functools.partial(_sb_kernel, tq=tq, tk=tk, n_heads=G),
        grid=(B, H // G, nq),
        in_specs=_attn_group_specs(S, tq, nq, G, q_blk, k_blk, v_blk, kv_buffers=2),
        out_specs=pl.BlockSpec((tq, gw), lambda b, hg, i: (b * nq + i, hg)),
        out_shape=jax.ShapeDtypeStruct((M, H * HEAD_DIM), BF16),
        scratch_shapes=[pltpu.VMEM((G, 1, tq), F32), pltpu.VMEM((G, HEAD_DIM, tq), F32)],
        compiler_params=_params("arbitrary", "arbitrary", "arbitrary"),
        name="sb_attn",
    )(z, z, vt)


def _dsa_select_kernel(q_ref, k_ref, w_ref, mask_ref, hi_ref, lo_ref, *, tq, tk, n_sel, n_kblocks):
    qi = pl.program_id(1)
    q0 = qi * tq
    nkb = (q0 + tq + tk - 1) // tk
    lane = lax.broadcasted_iota(jnp.int32, (tk, 2 * IDX_DIM), 1)
    w_scale = (N_IDX_HEADS ** -0.5) * (IDX_DIM ** -0.5)
    wt = (w_ref[...].astype(F32) * w_scale).T

    def score_block(kb, diagonal):
        k0 = pl.multiple_of(kb * tk, tk)
        kk = k_ref[pl.ds(k0, tk), :]
        k_halves = (jnp.where(lane < IDX_DIM, kk, jnp.zeros_like(kk)),
                    jnp.where(lane >= IDX_DIM, kk, jnp.zeros_like(kk)))
        acc = jnp.zeros((tk, tq), F32)
        for p in range(N_IDX_HEADS // 2):
            q2 = q_ref[:, p * 2 * IDX_DIM:(p + 1) * 2 * IDX_DIM]
            for half in range(2):
                h = 2 * p + half
                ph = lax.dot_general(k_halves[half], q2, NT_DIMS, preferred_element_type=F32)
                acc = acc + jnp.maximum(ph, 0.0) * wt[h:h + 1, :]
        bits = lax.bitcast_convert_type(acc, jnp.int32)
        key = bits ^ ((bits >> 31) & 0x7FFFFFFF)
        if diagonal:
            key_pos, query_pos = _key_query_iotas(k0, q0, tk, tq)
            key = jnp.where(key_pos <= query_pos, key, INT_MIN)
        hi_ref[pl.ds(k0, tk), :] = (key >> 16).astype(I16)
        lo_ref[pl.ds(k0, tk), :] = (key ^ 0x8000).astype(I16)

    def score_pair(i, carry):
        score_block(2 * i, False)
        score_block(2 * i + 1, False)
        return carry

    lax.fori_loop(0, (nkb - 1) // 2, score_pair, 0)

    @pl.when((nkb - 1) % 2 == 1)
    def _():
        score_block(nkb - 2, False)

    score_block(nkb - 1, True)

    n_count = (nkb + COUNT_BLOCKS - 1) // COUNT_BLOCKS
    rows = COUNT_BLOCKS * tk

    def pad_block(kb, carry):
        k0 = pl.multiple_of(kb * tk, tk)
        hi_ref[pl.ds(k0, tk), :] = jnp.full((tk, tq), I16_MIN, I16)
        lo_ref[pl.ds(k0, tk), :] = jnp.full((tk, tq), I16_MIN, I16)
        return carry

    lax.fori_loop(nkb, n_count * COUNT_BLOCKS, pad_block, 0)

    def count(flags_fn):
        def blk(i, cnt):
            r0 = pl.multiple_of(i * rows, rows)
            groups = []
            for c in range(rows // COUNT_CHUNK):
                flags = flags_fn(r0 + c * COUNT_CHUNK, COUNT_CHUNK)
                parts = [flags[r * BF16_ROWS:(r + 1) * BF16_ROWS, :]
                         for r in range(COUNT_CHUNK // BF16_ROWS)]
                while len(parts) > 1:
                    parts = [a + b for a, b in zip(parts[0::2], parts[1::2])]
                groups.append(parts[0])
            while len(groups) > 1:
                groups = [a + b for a, b in zip(groups[0::2], groups[1::2])]
            return cnt + groups[0]
        cnt = lax.fori_loop(0, n_count, blk, jnp.zeros((BF16_ROWS, tq), I16))
        return jnp.sum(cnt.astype(jnp.int32), axis=0, keepdims=True)

    flag, no_flag = I16(1), I16(0)

    def count_ge(ref, thr16):
        return count(lambda r, n: jnp.where(ref[pl.ds(r, n), :] >= thr16, flag, no_flag))

    def kth_largest(ref, k):
        def bisect(it, carry):
            thr, n_ge, n_gt = carry
            cand = thr + jnp.left_shift(jnp.int32(1), 15 - it)
            cnt = count_ge(ref, cand.astype(I16))
            ok = cnt >= k
            return jnp.where(ok, cand, thr), jnp.where(ok, cnt, n_ge), jnp.where(ok, n_gt, cnt)
        zeros = jnp.zeros((1, tq), jnp.int32)
        thr, n_ge, n_gt = lax.fori_loop(0, 16, bisect, (zeros + I16_MIN, zeros, zeros))
        return thr.astype(I16), n_ge, n_gt

    thr_hi, _, above_hi = kth_largest(hi_ref, n_sel)
    need_lo = n_sel - above_hi

    def park(kb, carry):
        k0 = pl.multiple_of(kb * tk, tk)
        lo_ref[pl.ds(k0, tk), :] = jnp.where(hi_ref[pl.ds(k0, tk), :] == thr_hi,
                                             lo_ref[pl.ds(k0, tk), :], I16(I16_MIN))
        return carry

    lax.fori_loop(0, nkb, park, 0)
    thr_lo, at_least_lo, above_lo = kth_largest(lo_ref, need_lo)
    need_ties = need_lo - above_lo
    n_ties = jnp.where(thr_lo == I16(I16_MIN), tk * n_kblocks, at_least_lo - above_lo)
    all_ties_fit = jnp.max(n_ties - need_ties) <= 0

    one, zero, neg = BF16(1.0), BF16(0.0), BF16(NEG)

    def store_mask(kb, sel, diagonal):
        k0 = pl.multiple_of(kb * tk, tk)
        if diagonal:
            key_pos, query_pos = _key_query_iotas(k0, q0, tk, tq)
            sel = sel * (key_pos <= query_pos).astype(BF16)
        mask_ref[0, pl.ds(k0, tk), :] = jnp.where(sel > zero, zero, neg)

    def emit_all_ties(kb, diagonal):
        k0 = pl.multiple_of(kb * tk, tk)
        hi = hi_ref[pl.ds(k0, tk), :]
        lo = lo_ref[pl.ds(k0, tk), :]
        store_mask(kb, jnp.where(hi > thr_hi, one,
                                 jnp.where(hi == thr_hi, jnp.where(lo >= thr_lo, one, zero), zero)),
                   diagonal)

    earlier = (lax.broadcasted_iota(jnp.int32, (tk, tk), 1)
               < lax.broadcasted_iota(jnp.int32, (tk, tk), 0)).astype(BF16)
    ones_rows = jnp.ones((BF16_ROWS, tk), BF16)
    need_ties_f = need_ties.astype(F32)

    def emit_ranked_ties(kb, ties_seen, diagonal):
        k0 = pl.multiple_of(kb * tk, tk)
        hi = hi_ref[pl.ds(k0, tk), :]
        lo = lo_ref[pl.ds(k0, tk), :]
        above = jnp.where(hi > thr_hi, one, jnp.where(lo > thr_lo, one, zero))
        eq = jnp.where(hi == thr_hi, jnp.where(lo == thr_lo, one, zero), zero)
        rank = jnp.dot(earlier, eq, preferred_element_type=F32).astype(BF16)
        room = jnp.clip(need_ties_f - ties_seen, -1.0, float(tk)).astype(BF16)
        store_mask(kb, jnp.where(rank < room, jnp.maximum(above, eq), above), diagonal)
        return ties_seen + jnp.dot(ones_rows, eq, preferred_element_type=F32)[0:1]

    def emit_fast():
        def body(kb, carry):
            emit_all_ties(kb, False)
            return carry
        lax.fori_loop(0, nkb - 1, body, 0)
        emit_all_ties(nkb - 1, True)

    def emit_slow():
        ties_seen = lax.fori_loop(0, nkb - 1, lambda kb, seen: emit_ranked_ties(kb, seen, False),
                                  jnp.zeros((1, tq), F32))
        emit_ranked_ties(nkb - 1, ties_seen, True)

    lax.cond(all_ties_fit, emit_fast, emit_slow)

    def fill(kb, carry):
        k0 = pl.multiple_of(kb * tk, tk)
        mask_ref[0, pl.ds(k0, tk), :] = jnp.full((tk, tq), NEG, mask_ref.dtype)
        return carry

    lax.fori_loop(nkb, n_kblocks, fill, 0)


def _dsa_select(z, zs, B, S, qix_blk, n_sel):
    tq = tk = min(S, ATTN_BLOCK)
    nq = S // tq
    assert nq % COUNT_BLOCKS == 0
    qw = N_IDX_HEADS * IDX_DIM
    return pl.pallas_call(
        functools.partial(_dsa_select_kernel, tq=tq, tk=tk, n_sel=n_sel, n_kblocks=S // tk),
        grid=(B, nq),
        in_specs=[pl.BlockSpec((tq, qw), lambda b, i: (b * nq + i, qix_blk)),
                  pl.BlockSpec((S, 2 * IDX_DIM), lambda b, i: (b, 0)),
                  pl.BlockSpec((tq, 2 * IDX_DIM), lambda b, i: (b * nq + i, 1))],
        out_specs=pl.BlockSpec((1, S, tq), lambda b, i: (b * nq + i, 0, 0)),
        out_shape=jax.ShapeDtypeStruct((B * nq, S, tq), BF16),
        scratch_shapes=[pltpu.VMEM((S, tq), I16), pltpu.VMEM((S, tq), I16)],
        compiler_params=_params("arbitrary", "arbitrary"),
        name="dsa_select",
    )(z, zs, zs)


def _rel_bucket(dist):
    n = jnp.maximum(dist, 0)
    max_exact = N_BUCKETS // 2
    nf = jnp.maximum(n, 1).astype(F32)
    large = max_exact + (jnp.log(nf / max_exact) / math.log(MAX_DISTANCE / max_exact)
                         * (N_BUCKETS - max_exact)).astype(jnp.int32)
    large = jnp.minimum(large, N_BUCKETS - 1)
    return jnp.where(n < max_exact, n, large)


def _bias_tiles(rel_bias, tq):
    assert tq >= MAX_DISTANCE
    rb = rel_bias.astype(F32)
    key = jnp.arange(tq)[:, None]
    query = jnp.arange(tq)[None, :]
    bucket = _rel_bucket(jnp.stack([query - key, tq + query - key]))
    onehot = (bucket[None] == jnp.arange(N_BUCKETS)[:, None, None, None]).astype(F32)
    return jnp.einsum("nh,nikq->hikq", (rb - rb[N_BUCKETS - 1]) * LOG2_E, onehot,
                      precision=lax.Precision.HIGHEST)


def _dsa_attn_kernel(q_ref, k_ref, vt_ref, mask_ref, bias_ref, o_ref, m_ref, acc_ref, logit_ref, *,
                     tq, n_heads):
    qi = pl.program_id(2)
    m_ref[...] = jnp.full_like(m_ref, NEG)
    acc_ref[...] = jnp.zeros_like(acc_ref)
    ones_rows = jnp.ones((BF16_ROWS, tq), BF16)

    def step(blocks):
        k0s = [pl.multiple_of(kb * tq, tq) for kb, _ in blocks]
        masks = [mask_ref[0, pl.ds(k0, tq), :].astype(F32) for k0 in k0s]

        def head_scores(g):
            return [lax.dot_general(k_ref[pl.ds(k0, tq), _head_cols(g)], q_ref[:, _head_cols(g)],
                                    NT_DIMS, preferred_element_type=F32) for k0 in k0s]

        def head_logits(g, scores):
            m_new = m_ref[g]
            for j, (_, bias_idx) in enumerate(blocks):
                s = scores[j] + masks[j]
                if bias_idx is not None:
                    s = s + bias_ref[g, bias_idx]
                logit_ref[g, j] = s
                m_new = jnp.maximum(m_new, jnp.max(s, axis=0, keepdims=True))
            return m_new

        def head_accumulate(g, m_new):
            acc = jnp.exp2(m_ref[g] - m_new) * acc_ref[g]
            for j, k0 in enumerate(k0s):
                p = jnp.exp2(logit_ref[g, j] - m_new).astype(BF16)
                vt = jnp.concatenate([vt_ref[_head_cols(g), pl.ds(k0, tq)], ones_rows], axis=0)
                acc = acc + jnp.dot(vt, p, preferred_element_type=F32)
            acc_ref[g] = acc
            m_ref[g] = m_new

        scores = [head_scores(g) for g in range(n_heads)]
        maxima = [head_logits(g, scores[g]) for g in range(n_heads)]
        for g in range(n_heads):
            head_accumulate(g, maxima[g])

    n_far = jnp.maximum(qi - 1, 0)

    def far_pair(i, carry):
        step([(2 * i, None), (2 * i + 1, None)])
        return carry

    lax.fori_loop(0, n_far // 2, far_pair, 0)

    @pl.when(n_far % 2 == 1)
    def _():
        step([(n_far - 1, None)])

    @pl.when(qi >= 1)
    def _():
        step([(qi - 1, 1), (qi, 0)])

    @pl.when(qi == 0)
    def _():
        step([(qi, 0)])
    for g in range(n_heads):
        acc = acc_ref[g]
        out_t = acc[:HEAD_DIM] / acc[HEAD_DIM:HEAD_DIM + 1]
        o_ref[:, _head_cols(g)] = out_t.T.astype(o_ref.dtype)


def _dsa_attn(z, vt, mask, bias_tiles, B, S, H, q_blk, k_blk, v_blk):
    M = z.shape[0]
    tq = bias_tiles.shape[-1]
    nq = S // tq
    G = min(H, DSA_HEAD_GROUP)
    gw = G * HEAD_DIM
    return pl.pallas_call(
        functools.partial(_dsa_attn_kernel, tq=tq, n_heads=G),
        grid=(B, H // G, nq),
        in_specs=_attn_group_specs(S, tq, nq, G, q_blk, k_blk, v_blk, kv_buffers=1) + [
            pl.BlockSpec((1, S, tq), lambda b, hg, i: (b * nq + i, 0, 0)),
            pl.BlockSpec((G, 2, tq, tq), lambda b, hg, i: (hg, 0, 0, 0),
                         pipeline_mode=pl.Buffered(1))],
        out_specs=pl.BlockSpec((tq, gw), lambda b, hg, i: (b * nq + i, hg)),
        out_shape=jax.ShapeDtypeStruct((M, H * HEAD_DIM), BF16),
        scratch_shapes=[pltpu.VMEM((G, 1, tq), F32),
                        pltpu.VMEM((G, HEAD_DIM + BF16_ROWS, tq), F32),
                        pltpu.VMEM((G, 2, tq, tq), F32)],
        compiler_params=_params("arbitrary", "arbitrary", "arbitrary"),
        name="dsa_attn",
    )(z, z, vt, mask, bias_tiles)


def _mix_out_kernel(osb_ref, ods_ref, gsb_ref, gds_ref, x_ref, wsb_ref, wds_ref, wout_ref,
                    gt_ref, g_ref, sc_ref, sh_ref, x1_ref, h2_ref):
    t_sb = jnp.dot(osb_ref[...], wsb_ref[...], preferred_element_type=F32)
    t_ds = jnp.dot(ods_ref[...], wds_ref[...], preferred_element_type=F32)
    merged = (_sigmoid(gsb_ref[...].astype(F32)) * t_sb
              + _sigmoid(gds_ref[...].astype(F32)) * t_ds)
    y = jnp.dot(merged.astype(BF16), wout_ref[...], preferred_element_type=F32)
    x1 = x_ref[...] + gt_ref[0] * y
    x1_ref[...] = x1
    h2_ref[...] = _rms_mod(x1, g_ref[...], sc_ref[0], sh_ref[0]).astype(h2_ref.dtype)


def _mix_out(o_sb, o_ds, z, gate_blk, x2, w_sb, w_ds, w_out, gt, g, sc, sh, S):
    M, D = x2.shape
    B = gt.shape[0]
    W = o_sb.shape[1]
    tm = min(S, 512)
    per_b = S // tm
    row = lambda i: (i, 0)
    const = lambda i: (0, 0)
    per_batch = lambda i: (i // per_b, 0, 0)
    return pl.pallas_call(
        _mix_out_kernel,
        grid=(M // tm,),
        in_specs=[pl.BlockSpec((tm, W), row),
                  pl.BlockSpec((tm, W), row),
                  pl.BlockSpec((tm, D), lambda i: (i, gate_blk)),
                  pl.BlockSpec((tm, D), lambda i: (i, gate_blk + 1)),
                  pl.BlockSpec((tm, D), row),
                  pl.BlockSpec((W, D), const),
                  pl.BlockSpec((W, D), const),
                  pl.BlockSpec((D, D), const),
                  pl.BlockSpec((1, 1, D), per_batch),
                  pl.BlockSpec((1, D), const),
                  pl.BlockSpec((1, 1, D), per_batch),
                  pl.BlockSpec((1, 1, D), per_batch)],
        out_specs=[pl.BlockSpec((tm, D), row), pl.BlockSpec((tm, D), row)],
        out_shape=[jax.ShapeDtypeStruct((M, D), F32), jax.ShapeDtypeStruct((M, D), BF16)],
        compiler_params=_params("arbitrary"),
        name="mix_out",
    )(o_sb, o_ds, z, z, x2, w_sb, w_ds, w_out, gt.reshape(B, 1, D), g.reshape(1, D),
      sc.reshape(B, 1, D), sh.reshape(B, 1, D))


HALO = BF16_ROWS


def _ffn_act_kernel(h_ref, halo_ref, wg_ref, wu_ref, cw_ref, cb_ref, o_ref, *, tiles_per_seq):
    i = pl.program_id(0)
    h = h_ref[...]
    wg = wg_ref[...]
    g0 = jnp.dot(h, wg, preferred_element_type=F32)
    g_prev = jnp.dot(halo_ref[...], wg, preferred_element_type=F32)
    g_prev = jnp.where(i % tiles_per_seq == 0, 0.0, g_prev)
    ridx = lax.broadcasted_iota(jnp.int32, g0.shape, 0)
    g1 = jnp.where(ridx == 0, g_prev[HALO - 1:HALO, :], pltpu.roll(g0, 1, 0))
    g2 = jnp.where(ridx == 0, g_prev[HALO - 2:HALO - 1, :],
                   jnp.where(ridx == 1, g_prev[HALO - 1:HALO, :], pltpu.roll(g0, 2, 0)))
    cw = cw_ref[...]
    a = cb_ref[...] + g2 * cw[0:1, :] + g1 * cw[1:2, :] + g0 * cw[2:3, :]
    u = jnp.dot(h, wu_ref[...], preferred_element_type=F32)
    o_ref[...] = (a * _sigmoid(a) * u).astype(o_ref.dtype)


def _ffn_out_kernel(a_ref, wd_ref, x1_ref, gt_ref, gf_ref, o_ref):
    k = pl.program_id(1)

    @pl.when(k == 0)
    def _():
        o_ref[...] = jnp.zeros_like(o_ref)

    o_ref[...] += jnp.dot(a_ref[...], wd_ref[...], preferred_element_type=F32)

    @pl.when(k == pl.num_programs(1) - 1)
    def _():
        x = x1_ref[...] + gt_ref[0] * o_ref[...]
        ms = jnp.mean(x * x, axis=-1, keepdims=True)
        o_ref[...] = x * lax.rsqrt(ms + EPS) * gf_ref[...]


def _conv_ffn(h2, x1, w_gate, w_up, w_down, conv_w, conv_b, gt, g_final, S):
    M, D = x1.shape
    B = gt.shape[0]
    F = w_gate.shape[1]
    tm = min(S, 1024)
    tf = min(F, 512)
    per_b = S // tm
    halo_per_tile = tm // HALO
    act = pl.pallas_call(
        functools.partial(_ffn_act_kernel, tiles_per_seq=per_b),
        grid=(M // tm, F // tf),
        in_specs=[pl.BlockSpec((tm, D), lambda i, f: (i, 0)),
                  pl.BlockSpec((HALO, D), lambda i, f: (jnp.maximum(i * halo_per_tile - 1, 0), 0)),
                  pl.BlockSpec((D, tf), lambda i, f: (0, f)),
                  pl.BlockSpec((D, tf), lambda i, f: (0, f)),
                  pl.BlockSpec((CONV_WIDTH, tf), lambda i, f: (0, f)),
                  pl.BlockSpec((1, tf), lambda i, f: (0, f))],
        out_specs=pl.BlockSpec((tm, tf), lambda i, f: (i, f)),
        out_shape=jax.ShapeDtypeStruct((M, F), BF16),
        compiler_params=_params("arbitrary", "arbitrary"),
        name="ffn_act",
    )(h2, h2, w_gate, w_up, conv_w, conv_b.reshape(1, F))
    return pl.pallas_call(
        _ffn_out_kernel,
        grid=(M // tm, F // tf),
        in_specs=[pl.BlockSpec((tm, tf), lambda i, k: (i, k)),
                  pl.BlockSpec((tf, D), lambda i, k: (k, 0)),
                  pl.BlockSpec((tm, D), lambda i, k: (i, 0)),
                  pl.BlockSpec((1, 1, D), lambda i, k: (i // per_b, 0, 0)),
                  pl.BlockSpec((1, D), lambda i, k: (0, 0))],
        out_specs=pl.BlockSpec((tm, D), lambda i, k: (i, 0)),
        out_shape=jax.ShapeDtypeStruct((M, D), F32),
        compiler_params=_params("arbitrary", "arbitrary"),
        name="ffn_out",
    )(act, w_down, x1, gt.reshape(B, 1, D), g_final.reshape(1, D))


def kernel(x, c, w_ada, b_ada, g_mix, w_in, w_o_sb, w_o_dsa, w_out, rel_bias, g_ffn, w_gate,
           w_up, conv_w, conv_b, w_down, g_final):
    B, S, D = x.shape
    depth = w_ada.shape[0]
    W = w_o_sb.shape[1]
    H = W // HEAD_DIM
    qw = N_IDX_HEADS * IDX_DIM
    assert w_o_dsa.shape[1] == W and D % W == 0 and S % ATTN_BLOCK == 0 and W % qw == 0
    n_sel = min(TOPK_MAX, S // 4)
    scale = HEAD_DIM ** -0.5
    x2 = x.reshape(B * S, D)

    for l in range(depth):
        mod = _adaln(c, w_ada[l], b_ada[l])
        sh1, sc1, gt1, sh2, sc2, gt2 = jnp.split(mod, 6, axis=-1)

        wt = w_in[l].T
        o_kix = 6 * W + qw
        o_wix = o_kix + IDX_DIM
        o_gate = o_wix + N_IDX_HEADS
        k_rows = wt[o_kix:o_wix]
        wt_tail = jnp.concatenate(
            [k_rows, k_rows, wt[o_wix:o_gate],
             jnp.zeros((2 * IDX_DIM - N_IDX_HEADS, D), wt.dtype)], axis=0)

        h1 = _norm_mod(x2, g_mix[l], sc1, sh1, S)
        z = _in_proj_wt(h1, wt, (0, 1, 3, 4, 6), W, ((0, scale), (2, scale * LOG2_E)), "in_proj")
        vt = _in_proj_wt(h1, wt, (2, 5), W, (), "in_proj_values", transposed=True)
        zg = _in_proj_wt(h1, wt[o_gate:], tuple(range(2 * D // W)), W, (), "in_proj_gates")
        zs = _in_proj_wt(h1, wt_tail, (0,), 4 * IDX_DIM, (), "in_proj_idx")

        o_sb = _sb_attn(z, vt, B, S, H, 0, H, 0)
        mask = _dsa_select(z, zs, B, S, 4 * W // qw, n_sel)
        o_ds = _dsa_attn(z, vt, mask, _bias_tiles(rel_bias, min(S, ATTN_BLOCK)),
                         B, S, H, 2 * H, 3 * H, H)

        x2, h2 = _mix_out(o_sb, o_ds, zg, 0, x2, w_o_sb[l].astype(BF16),
                          w_o_dsa[l].astype(BF16), w_out[l].astype(BF16), gt1, g_ffn[l], sc2, sh2, S)
        last = l == depth - 1
        assert last, "the final rms_norm is fused into the last layer's FFN"
        x2 = _conv_ffn(h2, x2, w_gate[l].astype(BF16), w_up[l].astype(BF16), w_down[l].astype(BF16),
                       conv_w[l], conv_b[l], gt2, g_final, S)
    return x2.reshape(B, S, D)
```

```python
import functools
import math

import jax
import jax.numpy as jnp
from jax import lax
from jax.experimental import pallas as pl
from jax.experimental.pallas import tpu as pltpu

HEAD_DIM = 128
N_IDX_HEADS = 16
IDX_DIM = 64
TOPK_MAX = 256
N_BUCKETS = 32
MAX_DISTANCE = 128
CONV_WIDTH = 3
EPS = 1e-6

F32 = jnp.float32
BF16 = jnp.bfloat16
NEG = -1e30
I16 = jnp.int16
INT_MIN = -2 ** 31
I16_MIN = -2 ** 15
EXP_ZERO_BELOW = 104.0
LOG2_E = math.log2(math.e)
V7X_VMEM_LIMIT = 56 * 1024 * 1024
SUBLANES = 8
BF16_ROWS = 16
ATTN_BLOCK = 256
SB_HEAD_GROUP = 4
DSA_HEAD_GROUP = 8
COUNT_BLOCKS = 4
COUNT_CHUNK = 64
NT_DIMS = (((1,), (1,)), ((), ()))


def _params(*sem):
    return pltpu.CompilerParams(dimension_semantics=sem, vmem_limit_bytes=V7X_VMEM_LIMIT)


def _sigmoid(x):
    return 1.0 / (1.0 + jnp.exp(-x))


def _adaln_kernel(ct_ref, w_ref, b_ref, o_ref):
    ct = ct_ref[...]
    act = ct * _sigmoid(ct)
    w = w_ref[...]
    for b in range(ct.shape[1]):
        o_ref[b:b + 1, :] = jnp.sum(act[:, b:b + 1] * w, axis=0, keepdims=True) + b_ref[...]


def _adaln(c, w, bias):
    B, D = c.shape
    N = w.shape[1]
    tn = min(N, 1024)
    return pl.pallas_call(
        _adaln_kernel,
        grid=(N // tn,),
        in_specs=[pl.BlockSpec((D, B), lambda j: (0, 0)),
                  pl.BlockSpec((D, tn), lambda j: (0, j)),
                  pl.BlockSpec((1, tn), lambda j: (0, j))],
        out_specs=pl.BlockSpec((B, tn), lambda j: (0, j)),
        out_shape=jax.ShapeDtypeStruct((B, N), F32),
        compiler_params=_params("arbitrary"),
        name="adaln",
    )(c.T, w, bias.reshape(1, N))


def _rms_mod(x, g, sc, sh):
    ms = jnp.mean(x * x, axis=-1, keepdims=True)
    y = x * lax.rsqrt(ms + EPS) * g
    return y * (1.0 + sc) + sh


def _norm_mod_kernel(x_ref, g_ref, sc_ref, sh_ref, o_ref):
    o_ref[...] = _rms_mod(x_ref[...], g_ref[...], sc_ref[0], sh_ref[0]).astype(o_ref.dtype)


def _norm_mod(x2, g, sc, sh, S):
    M, D = x2.shape
    B = sc.shape[0]
    tm = min(S, 512)
    per_b = S // tm
    return pl.pallas_call(
        _norm_mod_kernel,
        grid=(M // tm,),
        in_specs=[pl.BlockSpec((tm, D), lambda i: (i, 0)),
                  pl.BlockSpec((1, D), lambda i: (0, 0)),
                  pl.BlockSpec((1, 1, D), lambda i: (i // per_b, 0, 0)),
                  pl.BlockSpec((1, 1, D), lambda i: (i // per_b, 0, 0))],
        out_specs=pl.BlockSpec((tm, D), lambda i: (i, 0)),
        out_shape=jax.ShapeDtypeStruct((M, D), BF16),
        compiler_params=_params("arbitrary"),
        name="norm_mod",
    )(x2, g.reshape(1, D), sc.reshape(B, 1, D), sh.reshape(B, 1, D))


def _in_proj_kernel(a_ref, b_ref, o_ref, *, tile_scales):
    j = pl.program_id(0)
    acc = jnp.dot(a_ref[...], b_ref[...], preferred_element_type=F32)
    scale = jnp.float32(1.0)
    for tile, tile_scale in tile_scales:
        scale = jnp.where(j == tile, tile_scale, scale)
    o_ref[...] = (acc * scale).astype(o_ref.dtype)


def _in_proj_wt_kernel(a_ref, wt_ref, o_ref, wb_ref, *, tile_scales):
    @pl.when(pl.program_id(1) == 0)
    def _():
        wb_ref[...] = wt_ref[...].T.astype(BF16)

    _in_proj_kernel(a_ref, wb_ref, o_ref, tile_scales=tile_scales)


def _in_proj_t_kernel(a_ref, wt_ref, o_ref, wb_ref):
    @pl.when(pl.program_id(1) == 0)
    def _():
        wb_ref[...] = wt_ref[...].astype(BF16)

    o_ref[...] = lax.dot_general(wb_ref[...], a_ref[...], NT_DIMS,
                                 preferred_element_type=F32).astype(o_ref.dtype)


def _tile_index_fn(tiles):
    steps = []
    for j, t in enumerate(tiles):
        while len(steps) < t - j:
            steps.append(j)
    return lambda j: j + sum(jnp.where(j >= first, 1, 0) for first in steps)


def _in_proj_wt(h, wt, tiles, tn, tile_scales, name, transposed=False):
    M, K = h.shape
    tm = min(M, 1024)
    n = len(tiles)
    tile_of = _tile_index_fn(tiles)
    in_specs = [pl.BlockSpec((tm, K), lambda j, i: (i, 0)),
                pl.BlockSpec((tn, K), lambda j, i: (tile_of(j), 0))]
    if transposed:
        assert not tile_scales
        body, wb_shape = _in_proj_t_kernel, (tn, K)
        out_spec = pl.BlockSpec((tn, tm), lambda j, i: (j, i))
        out_shape = jax.ShapeDtypeStruct((n * tn, M), BF16)
    else:
        body, wb_shape = functools.partial(_in_proj_wt_kernel, tile_scales=tile_scales), (K, tn)
        out_spec = pl.BlockSpec((tm, tn), lambda j, i: (i, j))
        out_shape = jax.ShapeDtypeStruct((M, n * tn), BF16)
    return pl.pallas_call(
        body,
        grid=(n, M // tm),
        in_specs=in_specs,
        out_specs=out_spec,
        out_shape=out_shape,
        scratch_shapes=[pltpu.VMEM(wb_shape, BF16)],
        compiler_params=_params("arbitrary", "arbitrary"),
        name=name,
    )(h, wt)


def _key_query_iotas(k0, q0, tk, tq):
    key_pos = k0 + lax.broadcasted_iota(jnp.int32, (tk, tq), 0)
    query_pos = q0 + lax.broadcasted_iota(jnp.int32, (tk, tq), 1)
    return key_pos, query_pos


def _head_cols(g):
    return slice(g * HEAD_DIM, (g + 1) * HEAD_DIM)


def _sb_kernel(q_ref, k_ref, vt_ref, o_ref, run_ref, acc_ref, *, tq, tk, n_heads):
    qi = pl.program_id(2)
    q0 = qi * tq
    run_ref[...] = jnp.zeros_like(run_ref)
    acc_ref[...] = jnp.zeros_like(acc_ref)
    later = (lax.broadcasted_iota(jnp.int32, (tk, tk), 1)
             > lax.broadcasted_iota(jnp.int32, (tk, tk), 0)).astype(BF16)

    def step(kbs, first_is_diagonal):
        heads = range(n_heads)
        k0s = [pl.multiple_of(kb * tk, tk) for kb in kbs]
        causal = None
        if first_is_diagonal:
            key_pos, query_pos = _key_query_iotas(k0s[0], q0, tk, tq)
            causal = key_pos < query_pos
        masked = [first_is_diagonal and j == 0 for j in range(len(kbs))]
        zs = [[lax.dot_general(k_ref[pl.ds(k0, tk), _head_cols(g)], q_ref[:, _head_cols(g)],
                               NT_DIMS, preferred_element_type=F32) for g in heads] for k0 in k0s]
        sps = [[jnp.maximum(z, 0.0) + jnp.log(1.0 + jnp.exp(-jnp.abs(z))) for z in zj] for zj in zs]
        spms = [[jnp.where(causal, sp, 0.0) if masked[j] else sp for sp in spj]
                for j, spj in enumerate(sps)]
        his = [[spm.astype(BF16) for spm in spj] for spj in spms]
        los = [[(spm - hi.astype(F32)).astype(BF16) for spm, hi in zip(spj, hij)]
               for spj, hij in zip(spms, his)]
        suffixes = [[jnp.dot(later, hi, preferred_element_type=F32)
                     + jnp.dot(later, lo, preferred_element_type=F32) for hi, lo in zip(hij, loj)]
                    for hij, loj in zip(his, los)]
        weights, new_runs = [], []
        for g in heads:
            run = run_ref[g]
            head_weights = []
            for j in range(len(kbs)):
                a = jnp.exp(zs[j][g] - sps[j][g] - suffixes[j][g] - run)
                if masked[j]:
                    a = jnp.where(causal, a, 0.0)
                head_weights.append(a.astype(BF16))
                run = run + jnp.sum(spms[j][g], axis=0, keepdims=True)
            weights.append(head_weights)
            new_runs.append(run)
        min_run = None
        for g in heads:
            acc = acc_ref[g]
            for j, k0 in enumerate(k0s):
                acc = acc + jnp.dot(vt_ref[_head_cols(g), pl.ds(k0, tk)], weights[g][j],
                                    preferred_element_type=F32)
            acc_ref[g] = acc
            run_ref[g] = new_runs[g]
            head_min = jnp.min(new_runs[g])
            min_run = head_min if min_run is None else jnp.minimum(min_run, head_min)
        return min_run > EXP_ZERO_BELOW

    def body(carry):
        kb, _ = carry
        return kb - 1, step([kb], False)

    def cond(carry):
        kb, dead = carry
        return jnp.logical_and(kb >= 0, jnp.logical_not(dead))

    @pl.when(qi == 0)
    def _():
        step([qi], True)

    @pl.when(qi >= 1)
    def _():
        dead = step([qi, qi - 1], True)
        lax.while_loop(cond, body, (qi - 2, dead))

    for g in range(n_heads):
        o_ref[:, _head_cols(g)] = acc_ref[g].T.astype(o_ref.dtype)


def _attn_group_specs(S, tq, nq, G, q_blk, k_blk, v_blk, kv_buffers):
    assert q_blk % G == 0 and k_blk % G == 0 and v_blk % G == 0
    gw = G * HEAD_DIM
    mode = pl.Buffered(kv_buffers)
    return [pl.BlockSpec((tq, gw), lambda b, hg, i: (b * nq + i, q_blk // G + hg)),
            pl.BlockSpec((S, gw), lambda b, hg, i: (b, k_blk // G + hg), pipeline_mode=mode),
            pl.BlockSpec((gw, S), lambda b, hg, i: (v_blk // G + hg, b), pipeline_mode=mode)]


def _sb_attn(z, vt, B, S, H, q_blk, k_blk, v_blk):
    M = z.shape[0]
    tq = tk = min(S, ATTN_BLOCK)
    nq = S // tq
    G = min(H, SB_HEAD_GROUP)
    gw = G * HEAD_DIM
    return pl.pallas_call(
        functools.partial(_sb_kernel, tq=tq, tk=tk, n_heads=G),
        grid=(B, H // G, nq),
        in_specs=_attn_group_specs(S, tq, nq, G, q_blk, k_blk, v_blk, kv_buffers=2),
        out_specs=pl.BlockSpec((tq, gw), lambda b, hg, i: (b * nq + i, hg)),
        out_shape=jax.ShapeDtypeStruct((M, H * HEAD_DIM), BF16),
        scratch_shapes=[pltpu.VMEM((G, 1, tq), F32), pltpu.VMEM((G, HEAD_DIM, tq), F32)],
        compiler_params=_params("arbitrary", "arbitrary", "arbitrary"),
        name="sb_attn",
    )(z, z, vt)


def _dsa_select_kernel(q_ref, k_ref, w_ref, mask_ref, hi_ref, lo_ref, *, tq, tk, n_sel, n_kblocks):
    qi = pl.program_id(1)
    q0 = qi * tq
    nkb = (q0 + tq + tk - 1) // tk
    lane = lax.broadcasted_iota(jnp.int32, (tk, 2 * IDX_DIM), 1)
    w_scale = (N_IDX_HEADS ** -0.5) * (IDX_DIM ** -0.5)
    wt = (w_ref[...].astype(F32) * w_scale).T

    def score_block(kb, diagonal):
        k0 = pl.multiple_of(kb * tk, tk)
        kk = k_ref[pl.ds(k0, tk), :]
        k_halves = (jnp.where(lane < IDX_DIM, kk, jnp.zeros_like(kk)),
                    jnp.where(lane >= IDX_DIM, kk, jnp.zeros_like(kk)))
        acc = jnp.zeros((tk, tq), F32)
        for p in range(N_IDX_HEADS // 2):
            q2 = q_ref[:, p * 2 * IDX_DIM:(p + 1) * 2 * IDX_DIM]
            for half in range(2):
                h = 2 * p + half
                ph = lax.dot_general(k_halves[half], q2, NT_DIMS, preferred_element_type=F32)
                acc = acc + jnp.maximum(ph, 0.0) * wt[h:h + 1, :]
        bits = lax.bitcast_convert_type(acc, jnp.int32)
        key = bits ^ ((bits >> 31) & 0x7FFFFFFF)
        if diagonal:
            key_pos, query_pos = _key_query_iotas(k0, q0, tk, tq)
            key = jnp.where(key_pos <= query_pos, key, INT_MIN)
        hi_ref[pl.ds(k0, tk), :] = (key >> 16).astype(I16)
        lo_ref[pl.ds(k0, tk), :] = (key ^ 0x8000).astype(I16)

    def score_pair(i, carry):
        score_block(2 * i, False)
        score_block(2 * i + 1, False)
        return carry

    lax.fori_loop(0, (nkb - 1) // 2, score_pair, 0)

    @pl.when((nkb - 1) % 2 == 1)
    def _():
        score_block(nkb - 2, False)

    score_block(nkb - 1, True)

    n_count = (nkb + COUNT_BLOCKS - 1) // COUNT_BLOCKS
    rows = COUNT_BLOCKS * tk

    def pad_block(kb, carry):
        k0 = pl.multiple_of(kb * tk, tk)
        hi_ref[pl.ds(k0, tk), :] = jnp.full((tk, tq), I16_MIN, I16)
        lo_ref[pl.ds(k0, tk), :] = jnp.full((tk, tq), I16_MIN, I16)
        return carry

    lax.fori_loop(nkb, n_count * COUNT_BLOCKS, pad_block, 0)

    def count(flags_fn):
        def blk(i, cnt):
            r0 = pl.multiple_of(i * rows, rows)
            groups = []
            for c in range(rows // COUNT_CHUNK):
                flags = flags_fn(r0 + c * COUNT_CHUNK, COUNT_CHUNK)
                parts = [flags[r * BF16_ROWS:(r + 1) * BF16_ROWS, :]
                         for r in range(COUNT_CHUNK // BF16_ROWS)]
                while len(parts) > 1:
                    parts = [a + b for a, b in zip(parts[0::2], parts[1::2])]
                groups.append(parts[0])
            while len(groups) > 1:
                groups = [a + b for a, b in zip(groups[0::2], groups[1::2])]
            return cnt + groups[0]
        cnt = lax.fori_loop(0, n_count, blk, jnp.zeros((BF16_ROWS, tq), I16))
        return jnp.sum(cnt.astype(jnp.int32), axis=0, keepdims=True)

    flag, no_flag = I16(1), I16(0)

    def count_ge(ref, thr16):
        return count(lambda r, n: jnp.where(ref[pl.ds(r, n), :] >= thr16, flag, no_flag))

    def kth_largest(ref, k):
        def bisect(it, carry):
            thr, n_ge, n_gt = carry
            cand = thr + jnp.left_shift(jnp.int32(1), 15 - it)
            cnt = count_ge(ref, cand.astype(I16))
            ok = cnt >= k
            return jnp.where(ok, cand, thr), jnp.where(ok, cnt, n_ge), jnp.where(ok, n_gt, cnt)
        zeros = jnp.zeros((1, tq), jnp.int32)
        thr, n_ge, n_gt = lax.fori_loop(0, 16, bisect, (zeros + I16_MIN, zeros, zeros))
        return thr.astype(I16), n_ge, n_gt

    thr_hi, _, above_hi = kth_largest(hi_ref, n_sel)
    need_lo = n_sel - above_hi

    def park(kb, carry):
        k0 = pl.multiple_of(kb * tk, tk)
        lo_ref[pl.ds(k0, tk), :] = jnp.where(hi_ref[pl.ds(k0, tk), :] == thr_hi,
                                             lo_ref[pl.ds(k0, tk), :], I16(I16_MIN))
        return carry

    lax.fori_loop(0, nkb, park, 0)
    thr_lo, at_least_lo, above_lo = kth_largest(lo_ref, need_lo)
    need_ties = need_lo - above_lo
    n_ties = jnp.where(thr_lo == I16(I16_MIN), tk * n_kblocks, at_least_lo - above_lo)
    all_ties_fit = jnp.max(n_ties - need_ties) <= 0

    one, zero, neg = BF16(1.0), BF16(0.0), BF16(NEG)

    def store_mask(kb, sel, diagonal):
        k0 = pl.multiple_of(kb * tk, tk)
        if diagonal:
            key_pos, query_pos = _key_query_iotas(k0, q0, tk, tq)
            sel = sel * (key_pos <= query_pos).astype(BF16)
        mask_ref[0, pl.ds(k0, tk), :] = jnp.where(sel > zero, zero, neg)

    def emit_all_ties(kb, diagonal):
        k0 = pl.multiple_of(kb * tk, tk)
        hi = hi_ref[pl.ds(k0, tk), :]
        lo = lo_ref[pl.ds(k0, tk), :]
        store_mask(kb, jnp.where(hi > thr_hi, one,
                                 jnp.where(hi == thr_hi, jnp.where(lo >= thr_lo, one, zero), zero)),
                   diagonal)

    earlier = (lax.broadcasted_iota(jnp.int32, (tk, tk), 1)
               < lax.broadcasted_iota(jnp.int32, (tk, tk), 0)).astype(BF16)
    ones_rows = jnp.ones((BF16_ROWS, tk), BF16)
    need_ties_f = need_ties.astype(F32)

    def emit_ranked_ties(kb, ties_seen, diagonal):
        k0 = pl.multiple_of(kb * tk, tk)
        hi = hi_ref[pl.ds(k0, tk), :]
        lo = lo_ref[pl.ds(k0, tk), :]
        above = jnp.where(hi > thr_hi, one, jnp.where(lo > thr_lo, one, zero))
        eq = jnp.where(hi == thr_hi, jnp.where(lo == thr_lo, one, zero), zero)
        rank = jnp.dot(earlier, eq, preferred_element_type=F32).astype(BF16)
        room = jnp.clip(need_ties_f - ties_seen, -1.0, float(tk)).astype(BF16)
        store_mask(kb, jnp.where(rank < room, jnp.maximum(above, eq), above), diagonal)
        return ties_seen + jnp.dot(ones_rows, eq, preferred_element_type=F32)[0:1]

    def emit_fast():
        def body(kb, carry):
            emit_all_ties(kb, False)
            return carry
        lax.fori_loop(0, nkb - 1, body, 0)
        emit_all_ties(nkb - 1, True)

    def emit_slow():
        ties_seen = lax.fori_loop(0, nkb - 1, lambda kb, seen: emit_ranked_ties(kb, seen, False),
                                  jnp.zeros((1, tq), F32))
        emit_ranked_ties(nkb - 1, ties_seen, True)

    lax.cond(all_ties_fit, emit_fast, emit_slow)

    def fill(kb, carry):
        k0 = pl.multiple_of(kb * tk, tk)
        mask_ref[0, pl.ds(k0, tk), :] = jnp.full((tk, tq), NEG, mask_ref.dtype)
        return carry

    lax.fori_loop(nkb, n_kblocks, fill, 0)


def _dsa_select(z, zs, B, S, qix_blk, n_sel):
    tq = tk = min(S, ATTN_BLOCK)
    nq = S // tq
    assert nq % COUNT_BLOCKS == 0
    qw = N_IDX_HEADS * IDX_DIM
    return pl.pallas_call(
        functools.partial(_dsa_select_kernel, tq=tq, tk=tk, n_sel=n_sel, n_kblocks=S // tk),
        grid=(B, nq),
        in_specs=[pl.BlockSpec((tq, qw), lambda b, i: (b * nq + i, qix_blk)),
                  pl.BlockSpec((S, 2 * IDX_DIM), lambda b, i: (b, 0)),
                  pl.BlockSpec((tq, 2 * IDX_DIM), lambda b, i: (b * nq + i, 1))],
        out_specs=pl.BlockSpec((1, S, tq), lambda b, i: (b * nq + i, 0, 0)),
        out_shape=jax.ShapeDtypeStruct((B * nq, S, tq), BF16),
        scratch_shapes=[pltpu.VMEM((S, tq), I16), pltpu.VMEM((S, tq), I16)],
        compiler_params=_params("arbitrary", "arbitrary"),
        name="dsa_select",
    )(z, zs, zs)


def _rel_bucket(dist):
    n = jnp.maximum(dist, 0)
    max_exact = N_BUCKETS // 2
    nf = jnp.maximum(n, 1).astype(F32)
    large = max_exact + (jnp.log(nf / max_exact) / math.log(MAX_DISTANCE / max_exact)
                         * (N_BUCKETS - max_exact)).astype(jnp.int32)
    large = jnp.minimum(large, N_BUCKETS - 1)
    return jnp.where(n < max_exact, n, large)


def _bias_tiles(rel_bias, tq):
    assert tq >= MAX_DISTANCE
    rb = rel_bias.astype(F32)
    key = jnp.arange(tq)[:, None]
    query = jnp.arange(tq)[None, :]
    bucket = _rel_bucket(jnp.stack([query - key, tq + query - key]))
    onehot = (bucket[None] == jnp.arange(N_BUCKETS)[:, None, None, None]).astype(F32)
    return jnp.einsum("nh,nikq->hikq", (rb - rb[N_BUCKETS - 1]) * LOG2_E, onehot,
                      precision=lax.Precision.HIGHEST)


def _dsa_attn_kernel(q_ref, k_ref, vt_ref, mask_ref, bias_ref, o_ref, m_ref, acc_ref, logit_ref, *,
                     tq, n_heads):
    qi = pl.program_id(2)
    m_ref[...] = jnp.full_like(m_ref, NEG)
    acc_ref[...] = jnp.zeros_like(acc_ref)
    ones_rows = jnp.ones((BF16_ROWS, tq), BF16)

    def step(blocks):
        k0s = [pl.multiple_of(kb * tq, tq) for kb, _ in blocks]
        masks = [mask_ref[0, pl.ds(k0, tq), :].astype(F32) for k0 in k0s]

        def head_scores(g):
            return [lax.dot_general(k_ref[pl.ds(k0, tq), _head_cols(g)], q_ref[:, _head_cols(g)],
                                    NT_DIMS, preferred_element_type=F32) for k0 in k0s]

        def head_logits(g, scores):
            m_new = m_ref[g]
            for j, (_, bias_idx) in enumerate(blocks):
                s = scores[j] + masks[j]
                if bias_idx is not None:
                    s = s + bias_ref[g, bias_idx]
                logit_ref[g, j] = s
                m_new = jnp.maximum(m_new, jnp.max(s, axis=0, keepdims=True))
            return m_new

        def head_accumulate(g, m_new):
            acc = jnp.exp2(m_ref[g] - m_new) * acc_ref[g]
            for j, k0 in enumerate(k0s):
                p = jnp.exp2(logit_ref[g, j] - m_new).astype(BF16)
                vt = jnp.concatenate([vt_ref[_head_cols(g), pl.ds(k0, tq)], ones_rows], axis=0)
                acc = acc + jnp.dot(vt, p, preferred_element_type=F32)
            acc_ref[g] = acc
            m_ref[g] = m_new

        scores = [head_scores(g) for g in range(n_heads)]
        maxima = [head_logits(g, scores[g]) for g in range(n_heads)]
        for g in range(n_heads):
            head_accumulate(g, maxima[g])

    n_far = jnp.maximum(qi - 1, 0)

    def far_pair(i, carry):
        step([(2 * i, None), (2 * i + 1, None)])
        return carry

    lax.fori_loop(0, n_far // 2, far_pair, 0)

    @pl.when(n_far % 2 == 1)
    def _():
        step([(n_far - 1, None)])

    @pl.when(qi >= 1)
    def _():
        step([(qi - 1, 1), (qi, 0)])

    @pl.when(qi == 0)
    def _():
        step([(qi, 0)])
    for g in range(n_heads):
        acc = acc_ref[g]
        out_t = acc[:HEAD_DIM] / acc[HEAD_DIM:HEAD_DIM + 1]
        o_ref[:, _head_cols(g)] = out_t.T.astype(o_ref.dtype)


def _dsa_attn(z, vt, mask, bias_tiles, B, S, H, q_blk, k_blk, v_blk):
    M = z.shape[0]
    tq = bias_tiles.shape[-1]
    nq = S // tq
    G = min(H, DSA_HEAD_GROUP)
    gw = G * HEAD_DIM
    return pl.pallas_call(
        functools.partial(_dsa_attn_kernel, tq=tq, n_heads=G),
        grid=(B, H // G, nq),
        in_specs=_attn_group_specs(S, tq, nq, G, q_blk, k_blk, v_blk, kv_buffers=1) + [
            pl.BlockSpec((1, S, tq), lambda b, hg, i: (b * nq + i, 0, 0)),
            pl.BlockSpec((G, 2, tq, tq), lambda b, hg, i: (hg, 0, 0, 0),
                         pipeline_mode=pl.Buffered(1))],
        out_specs=pl.BlockSpec((tq, gw), lambda b, hg, i: (b * nq + i, hg)),
        out_shape=jax.ShapeDtypeStruct((M, H * HEAD_DIM), BF16),
        scratch_shapes=[pltpu.VMEM((G, 1, tq), F32),
                        pltpu.VMEM((G, HEAD_DIM + BF16_ROWS, tq), F32),
                        pltpu.VMEM((G, 2, tq, tq), F32)],
        compiler_params=_params("arbitrary", "arbitrary", "arbitrary"),
        name="dsa_attn",
    )(z, z, vt, mask, bias_tiles)


def _mix_out_kernel(osb_ref, ods_ref, gsb_ref, gds_ref, x_ref, wsb_ref, wds_ref, wout_ref,
                    gt_ref, g_ref, sc_ref, sh_ref, x1_ref, h2_ref):
    t_sb = jnp.dot(osb_ref[...], wsb_ref[...], preferred_element_type=F32)
    t_ds = jnp.dot(ods_ref[...], wds_ref[...], preferred_element_type=F32)
    merged = (_sigmoid(gsb_ref[...].astype(F32)) * t_sb
              + _sigmoid(gds_ref[...].astype(F32)) * t_ds)
    y = jnp.dot(merged.astype(BF16), wout_ref[...], preferred_element_type=F32)
    x1 = x_ref[...] + gt_ref[0] * y
    x1_ref[...] = x1
    h2_ref[...] = _rms_mod(x1, g_ref[...], sc_ref[0], sh_ref[0]).astype(h2_ref.dtype)


def _mix_out(o_sb, o_ds, z, gate_blk, x2, w_sb, w_ds, w_out, gt, g, sc, sh, S):
    M, D = x2.shape
    B = gt.shape[0]
    W = o_sb.shape[1]
    tm = min(S, 512)
    per_b = S // tm
    row = lambda i: (i, 0)
    const = lambda i: (0, 0)
    per_batch = lambda i: (i // per_b, 0, 0)
    return pl.pallas_call(
        _mix_out_kernel,
        grid=(M // tm,),
        in_specs=[pl.BlockSpec((tm, W), row),
                  pl.BlockSpec((tm, W), row),
                  pl.BlockSpec((tm, D), lambda i: (i, gate_blk)),
                  pl.BlockSpec((tm, D), lambda i: (i, gate_blk + 1)),
                  pl.BlockSpec((tm, D), row),
                  pl.BlockSpec((W, D), const),
                  pl.BlockSpec((W, D), const),
                  pl.BlockSpec((D, D), const),
                  pl.BlockSpec((1, 1, D), per_batch),
                  pl.BlockSpec((1, D), const),
                  pl.BlockSpec((1, 1, D), per_batch),
                  pl.BlockSpec((1, 1, D), per_batch)],
        out_specs=[pl.BlockSpec((tm, D), row), pl.BlockSpec((tm, D), row)],
        out_shape=[jax.ShapeDtypeStruct((M, D), F32), jax.ShapeDtypeStruct((M, D), BF16)],
        compiler_params=_params("arbitrary"),
        name="mix_out",
    )(o_sb, o_ds, z, z, x2, w_sb, w_ds, w_out, gt.reshape(B, 1, D), g.reshape(1, D),
      sc.reshape(B, 1, D), sh.reshape(B, 1, D))


HALO = BF16_ROWS


def _ffn_act_kernel(h_ref, halo_ref, wg_ref, wu_ref, cw_ref, cb_ref, o_ref, wgb_ref, wub_ref, *,
                    tiles_per_seq):
    i = pl.program_id(1)

    @pl.when(i == 0)
    def _():
        wgb_ref[...] = wg_ref[...].astype(BF16)
        wub_ref[...] = wu_ref[...].astype(BF16)

    h = h_ref[...]
    wg = wgb_ref[...]
    g0 = jnp.dot(h, wg, preferred_element_type=F32)
    g_prev = jnp.dot(halo_ref[...], wg, preferred_element_type=F32)
    g_prev = jnp.where(i % tiles_per_seq == 0, 0.0, g_prev)
    ridx = lax.broadcasted_iota(jnp.int32, g0.shape, 0)
    g1 = jnp.where(ridx == 0, g_prev[HALO - 1:HALO, :], pltpu.roll(g0, 1, 0))
    g2 = jnp.where(ridx == 0, g_prev[HALO - 2:HALO - 1, :],
                   jnp.where(ridx == 1, g_prev[HALO - 1:HALO, :], pltpu.roll(g0, 2, 0)))
    cw = cw_ref[...]
    a = cb_ref[...] + g2 * cw[0:1, :] + g1 * cw[1:2, :] + g0 * cw[2:3, :]
    u = jnp.dot(h, wub_ref[...], preferred_element_type=F32)
    o_ref[...] = (a * _sigmoid(a) * u).astype(o_ref.dtype)


def _ffn_out_kernel(a_ref, wd_ref, x1_ref, gt_ref, gf_ref, o_ref):
    k = pl.program_id(1)

    @pl.when(k == 0)
    def _():
        o_ref[...] = jnp.zeros_like(o_ref)

    o_ref[...] += jnp.dot(a_ref[...], wd_ref[...], preferred_element_type=F32)

    @pl.when(k == pl.num_programs(1) - 1)
    def _():
        x = x1_ref[...] + gt_ref[0] * o_ref[...]
        ms = jnp.mean(x * x, axis=-1, keepdims=True)
        o_ref[...] = x * lax.rsqrt(ms + EPS) * gf_ref[...]


def _conv_ffn(h2, x1, w_gate, w_up, layer, w_down, conv_w, conv_b, gt, g_final, S):
    M, D = x1.shape
    B = gt.shape[0]
    F = w_gate.shape[2]
    tm = min(S, 1024)
    tf = min(F, 512)
    per_b = S // tm
    halo_per_tile = tm // HALO
    act = pl.pallas_call(
        functools.partial(_ffn_act_kernel, tiles_per_seq=per_b),
        grid=(F // tf, M // tm),
        in_specs=[pl.BlockSpec((tm, D), lambda f, i: (i, 0)),
                  pl.BlockSpec((HALO, D), lambda f, i: (jnp.maximum(i * halo_per_tile - 1, 0), 0)),
                  pl.BlockSpec((None, D, tf), lambda f, i: (layer, 0, f)),
                  pl.BlockSpec((None, D, tf), lambda f, i: (layer, 0, f)),
                  pl.BlockSpec((CONV_WIDTH, tf), lambda f, i: (0, f)),
                  pl.BlockSpec((1, tf), lambda f, i: (0, f))],
        out_specs=pl.BlockSpec((tm, tf), lambda f, i: (i, f)),
        out_shape=jax.ShapeDtypeStruct((M, F), BF16),
        scratch_shapes=[pltpu.VMEM((D, tf), BF16), pltpu.VMEM((D, tf), BF16)],
        compiler_params=_params("arbitrary", "arbitrary"),
        name="ffn_act",
    )(h2, h2, w_gate, w_up, conv_w, conv_b.reshape(1, F))
    return pl.pallas_call(
        _ffn_out_kernel,
        grid=(M // tm, F // tf),
        in_specs=[pl.BlockSpec((tm, tf), lambda i, k: (i, k)),
                  pl.BlockSpec((tf, D), lambda i, k: (k, 0)),
                  pl.BlockSpec((tm, D), lambda i, k: (i, 0)),
                  pl.BlockSpec((1, 1, D), lambda i, k: (i // per_b, 0, 0)),
                  pl.BlockSpec((1, D), lambda i, k: (0, 0))],
        out_specs=pl.BlockSpec((tm, D), lambda i, k: (i, 0)),
        out_shape=jax.ShapeDtypeStruct((M, D), F32),
        compiler_params=_params("arbitrary", "arbitrary"),
        name="ffn_out",
    )(act, w_down, x1, gt.reshape(B, 1, D), g_final.reshape(1, D))


def kernel(x, c, w_ada, b_ada, g_mix, w_in, w_o_sb, w_o_dsa, w_out, rel_bias, g_ffn, w_gate,
           w_up, conv_w, conv_b, w_down, g_final):
    B, S, D = x.shape
    depth = w_ada.shape[0]
    W = w_o_sb.shape[1]
    H = W // HEAD_DIM
    qw = N_IDX_HEADS * IDX_DIM
    assert w_o_dsa.shape[1] == W and D % W == 0 and S % ATTN_BLOCK == 0 and W % qw == 0
    n_sel = min(TOPK_MAX, S // 4)
    scale = HEAD_DIM ** -0.5
    x2 = x.reshape(B * S, D)

    for l in range(depth):
        mod = _adaln(c, w_ada[l], b_ada[l])
        sh1, sc1, gt1, sh2, sc2, gt2 = jnp.split(mod, 6, axis=-1)

        wt = w_in[l].T
        o_kix = 6 * W + qw
        o_wix = o_kix + IDX_DIM
        o_gate = o_wix + N_IDX_HEADS
        k_rows = wt[o_kix:o_wix]
        wt_tail = jnp.concatenate(
            [k_rows, k_rows, wt[o_wix:o_gate],
             jnp.zeros((2 * IDX_DIM - N_IDX_HEADS, D), wt.dtype)], axis=0)

        h1 = _norm_mod(x2, g_mix[l], sc1, sh1, S)
        z = _in_proj_wt(h1, wt, (0, 1, 3, 4, 6), W, ((0, scale), (2, scale * LOG2_E)), "in_proj")
        vt = _in_proj_wt(h1, wt, (2, 5), W, (), "in_proj_values", transposed=True)
        zg = _in_proj_wt(h1, wt[o_gate:], tuple(range(2 * D // W)), W, (), "in_proj_gates")
        zs = _in_proj_wt(h1, wt_tail, (0,), 4 * IDX_DIM, (), "in_proj_idx")

        o_sb = _sb_attn(z, vt, B, S, H, 0, H, 0)
        mask = _dsa_select(z, zs, B, S, 4 * W // qw, n_sel)
        o_ds = _dsa_attn(z, vt, mask, _bias_tiles(rel_bias, min(S, ATTN_BLOCK)),
                         B, S, H, 2 * H, 3 * H, H)

        x2, h2 = _mix_out(o_sb, o_ds, zg, 0, x2, w_o_sb[l].astype(BF16),
                          w_o_dsa[l].astype(BF16), w_out[l].astype(BF16), gt1, g_ffn[l], sc2, sh2, S)
        last = l == depth - 1
        assert last, "the final rms_norm is fused into the last layer's FFN"
        x2 = _conv_ffn(h2, x2, w_gate, w_up, l, w_down[l].astype(BF16),
                       conv_w[l], conv_b[l], gt2, g_final, S)
    return x2.reshape(B, S, D)
```

```python
import functools
import math

import jax
import jax.numpy as jnp
from jax import lax
from jax.experimental import pallas as pl
from jax.experimental.pallas import tpu as pltpu

HEAD_DIM = 128
N_IDX_HEADS = 16
IDX_DIM = 64
TOPK_MAX = 256
N_BUCKETS = 32
MAX_DISTANCE = 128
CONV_WIDTH = 3
EPS = 1e-6

F32 = jnp.float32
BF16 = jnp.bfloat16
NEG = -1e30
I16 = jnp.int16
INT_MIN = -2 ** 31
I16_MIN = -2 ** 15
EXP_ZERO_BELOW = 104.0
LOG2_E = math.log2(math.e)
V7X_VMEM_LIMIT = 56 * 1024 * 1024
SUBLANES = 8
BF16_ROWS = 16
ATTN_BLOCK = 256
SB_HEAD_GROUP = 4
DSA_HEAD_GROUP = 8
SCORE_BLOCKS = 4
COUNT_BLOCKS = 4
COUNT_CHUNK = 64
NT_DIMS = (((1,), (1,)), ((), ()))


def _params(*sem):
    return pltpu.CompilerParams(dimension_semantics=sem, vmem_limit_bytes=V7X_VMEM_LIMIT)


def _sigmoid(x):
    return 1.0 / (1.0 + jnp.exp(-x))


def _adaln_kernel(ct_ref, w_ref, b_ref, o_ref):
    ct = ct_ref[...]
    act = ct * _sigmoid(ct)
    w = w_ref[...]
    for b in range(ct.shape[1]):
        o_ref[b:b + 1, :] = jnp.sum(act[:, b:b + 1] * w, axis=0, keepdims=True) + b_ref[...]


def _adaln(c, w, bias):
    B, D = c.shape
    N = w.shape[1]
    tn = min(N, 1024)
    return pl.pallas_call(
        _adaln_kernel,
        grid=(N // tn,),
        in_specs=[pl.BlockSpec((D, B), lambda j: (0, 0)),
                  pl.BlockSpec((D, tn), lambda j: (0, j)),
                  pl.BlockSpec((1, tn), lambda j: (0, j))],
        out_specs=pl.BlockSpec((B, tn), lambda j: (0, j)),
        out_shape=jax.ShapeDtypeStruct((B, N), F32),
        compiler_params=_params("arbitrary"),
        name="adaln",
    )(c.T, w, bias.reshape(1, N))


def _rms_mod(x, g, sc, sh):
    ms = jnp.mean(x * x, axis=-1, keepdims=True)
    y = x * lax.rsqrt(ms + EPS) * g
    return y * (1.0 + sc) + sh


def _norm_mod_kernel(x_ref, g_ref, sc_ref, sh_ref, o_ref):
    o_ref[...] = _rms_mod(x_ref[...], g_ref[...], sc_ref[0], sh_ref[0]).astype(o_ref.dtype)


def _norm_mod(x2, g, sc, sh, S):
    M, D = x2.shape
    B = sc.shape[0]
    tm = min(S, 1024)
    per_b = S // tm
    return pl.pallas_call(
        _norm_mod_kernel,
        grid=(M // tm,),
        in_specs=[pl.BlockSpec((tm, D), lambda i: (i, 0)),
                  pl.BlockSpec((1, D), lambda i: (0, 0)),
                  pl.BlockSpec((1, 1, D), lambda i: (i // per_b, 0, 0)),
                  pl.BlockSpec((1, 1, D), lambda i: (i // per_b, 0, 0))],
        out_specs=pl.BlockSpec((tm, D), lambda i: (i, 0)),
        out_shape=jax.ShapeDtypeStruct((M, D), BF16),
        compiler_params=_params("arbitrary"),
        name="norm_mod",
    )(x2, g.reshape(1, D), sc.reshape(B, 1, D), sh.reshape(B, 1, D))


def _in_proj_kernel(a_ref, b_ref, o_ref, *, tile_scales):
    j = pl.program_id(0)
    acc = jnp.dot(a_ref[...], b_ref[...], preferred_element_type=F32)
    scale = jnp.float32(1.0)
    for tile, tile_scale in tile_scales:
        scale = jnp.where(j == tile, tile_scale, scale)
    o_ref[...] = (acc * scale).astype(o_ref.dtype)


def _in_proj_wt_kernel(a_ref, wt_ref, o_ref, wb_ref, *, tile_scales):
    @pl.when(pl.program_id(1) == 0)
    def _():
        wb_ref[...] = wt_ref[...].T.astype(BF16)

    _in_proj_kernel(a_ref, wb_ref, o_ref, tile_scales=tile_scales)


def _in_proj_t_kernel(a_ref, wt_ref, o_ref, wb_ref):
    @pl.when(pl.program_id(1) == 0)
    def _():
        wb_ref[...] = wt_ref[...].astype(BF16)

    o_ref[...] = lax.dot_general(wb_ref[...], a_ref[...], NT_DIMS,
                                 preferred_element_type=F32).astype(o_ref.dtype)


def _tile_index_fn(tiles):
    steps = []
    for j, t in enumerate(tiles):
        while len(steps) < t - j:
            steps.append(j)
    return lambda j: j + sum(jnp.where(j >= first, 1, 0) for first in steps)


def _in_proj_wt(h, wt, tiles, tn, tile_scales, name, transposed=False):
    M, K = h.shape
    tm = min(M, 1024)
    n = len(tiles)
    tile_of = _tile_index_fn(tiles)
    in_specs = [pl.BlockSpec((tm, K), lambda j, i: (i, 0)),
                pl.BlockSpec((tn, K), lambda j, i: (tile_of(j), 0))]
    if transposed:
        assert not tile_scales
        body, wb_shape = _in_proj_t_kernel, (tn, K)
        out_spec = pl.BlockSpec((tn, tm), lambda j, i: (j, i))
        out_shape = jax.ShapeDtypeStruct((n * tn, M), BF16)
    else:
        body, wb_shape = functools.partial(_in_proj_wt_kernel, tile_scales=tile_scales), (K, tn)
        out_spec = pl.BlockSpec((tm, tn), lambda j, i: (i, j))
        out_shape = jax.ShapeDtypeStruct((M, n * tn), BF16)
    return pl.pallas_call(
        body,
        grid=(n, M // tm),
        in_specs=in_specs,
        out_specs=out_spec,
        out_shape=out_shape,
        scratch_shapes=[pltpu.VMEM(wb_shape, BF16)],
        compiler_params=_params("arbitrary", "arbitrary"),
        name=name,
    )(h, wt)


def _key_query_iotas(k0, q0, tk, tq):
    key_pos = k0 + lax.broadcasted_iota(jnp.int32, (tk, tq), 0)
    query_pos = q0 + lax.broadcasted_iota(jnp.int32, (tk, tq), 1)
    return key_pos, query_pos


def _head_cols(g):
    return slice(g * HEAD_DIM, (g + 1) * HEAD_DIM)


def _sb_kernel(q_ref, k_ref, vt_ref, o_ref, run_ref, acc_ref, *, tq, tk, n_heads):
    qi = pl.program_id(2)
    q0 = qi * tq
    run_ref[...] = jnp.zeros_like(run_ref)
    acc_ref[...] = jnp.zeros_like(acc_ref)
    later = (lax.broadcasted_iota(jnp.int32, (tk, tk), 1)
             > lax.broadcasted_iota(jnp.int32, (tk, tk), 0)).astype(BF16)

    def step(kbs, first_is_diagonal):
        heads = range(n_heads)
        k0s = [pl.multiple_of(kb * tk, tk) for kb in kbs]
        causal = None
        if first_is_diagonal:
            key_pos, query_pos = _key_query_iotas(k0s[0], q0, tk, tq)
            causal = key_pos < query_pos
        masked = [first_is_diagonal and j == 0 for j in range(len(kbs))]
        zs = [[lax.dot_general(k_ref[pl.ds(k0, tk), _head_cols(g)], q_ref[:, _head_cols(g)],
                               NT_DIMS, preferred_element_type=F32) for g in heads] for k0 in k0s]
        sps = [[jnp.maximum(z, 0.0) + jnp.log(1.0 + jnp.exp(-jnp.abs(z))) for z in zj] for zj in zs]
        spms = [[jnp.where(causal, sp, 0.0) if masked[j] else sp for sp in spj]
                for j, spj in enumerate(sps)]
        his = [[spm.astype(BF16) for spm in spj] for spj in spms]
        los = [[(spm - hi.astype(F32)).astype(BF16) for spm, hi in zip(spj, hij)]
               for spj, hij in zip(spms, his)]
        suffixes = [[jnp.dot(later, hi, preferred_element_type=F32)
                     + jnp.dot(later, lo, preferred_element_type=F32) for hi, lo in zip(hij, loj)]
                    for hij, loj in zip(his, los)]
        weights, new_runs = [], []
        for g in heads:
            run = run_ref[g]
            head_weights = []
            for j in range(len(kbs)):
                a = jnp.exp(zs[j][g] - sps[j][g] - suffixes[j][g] - run)
                if masked[j]:
                    a = jnp.where(causal, a, 0.0)
                head_weights.append(a.astype(BF16))
                run = run + jnp.sum(spms[j][g], axis=0, keepdims=True)
            weights.append(head_weights)
            new_runs.append(run)
        min_run = None
        for g in heads:
            acc = acc_ref[g]
            for j, k0 in enumerate(k0s):
                acc = acc + jnp.dot(vt_ref[_head_cols(g), pl.ds(k0, tk)], weights[g][j],
                                    preferred_element_type=F32)
            acc_ref[g] = acc
            run_ref[g] = new_runs[g]
            head_min = jnp.min(new_runs[g])
            min_run = head_min if min_run is None else jnp.minimum(min_run, head_min)
        return min_run > EXP_ZERO_BELOW

    def body(carry):
        kb, _ = carry
        return kb - 1, step([kb], False)

    def cond(carry):
        kb, dead = carry
        return jnp.logical_and(kb >= 0, jnp.logical_not(dead))

    @pl.when(qi == 0)
    def _():
        step([qi], True)

    @pl.when(qi >= 1)
    def _():
        dead = step([qi, qi - 1], True)
        lax.while_loop(cond, body, (qi - 2, dead))

    for g in range(n_heads):
        o_ref[:, _head_cols(g)] = acc_ref[g].T.astype(o_ref.dtype)


def _attn_group_specs(S, tq, nq, G, q_blk, k_blk, v_blk, kv_buffers):
    assert q_blk % G == 0 and k_blk % G == 0 and v_blk % G == 0
    gw = G * HEAD_DIM
    mode = pl.Buffered(kv_buffers)
    return [pl.BlockSpec((tq, gw), lambda b, hg, i: (b * nq + i, q_blk // G + hg)),
            pl.BlockSpec((S, gw), lambda b, hg, i: (b, k_blk // G + hg), pipeline_mode=mode),
            pl.BlockSpec((gw, S), lambda b, hg, i: (v_blk // G + hg, b), pipeline_mode=mode)]


def _sb_attn(z, vt, B, S, H, q_blk, k_blk, v_blk):
    M = z.shape[0]
    tq = tk = min(S, ATTN_BLOCK)
    nq = S // tq
    G = min(H, SB_HEAD_GROUP)
    gw = G * HEAD_DIM
    return pl.pallas_call(
        functools.partial(_sb_kernel, tq=tq, tk=tk, n_heads=G),
        grid=(B, H // G, nq),
        in_specs=_attn_group_specs(S, tq, nq, G, q_blk, k_blk, v_blk, kv_buffers=2),
        out_specs=pl.BlockSpec((tq, gw), lambda b, hg, i: (b * nq + i, hg)),
        out_shape=jax.ShapeDtypeStruct((M, H * HEAD_DIM), BF16),
        scratch_shapes=[pltpu.VMEM((G, 1, tq), F32), pltpu.VMEM((G, HEAD_DIM, tq), F32)],
        compiler_params=_params("arbitrary", "arbitrary", "arbitrary"),
        name="sb_attn",
    )(z, z, vt)


def _dsa_select_kernel(q_ref, k_ref, w_ref, mask_ref, hi_ref, lo_ref, *, tq, tk, n_sel, n_kblocks):
    qi = pl.program_id(1)
    q0 = qi * tq
    nkb = (q0 + tq + tk - 1) // tk
    lane = lax.broadcasted_iota(jnp.int32, (tk, 2 * IDX_DIM), 1)
    w_scale = (N_IDX_HEADS ** -0.5) * (IDX_DIM ** -0.5)
    wt = (w_ref[...].astype(F32) * w_scale).T

    def score_block(kb, diagonal):
        k0 = pl.multiple_of(kb * tk, tk)
        kk = k_ref[pl.ds(k0, tk), :]
        k_halves = (jnp.where(lane < IDX_DIM, kk, jnp.zeros_like(kk)),
                    jnp.where(lane >= IDX_DIM, kk, jnp.zeros_like(kk)))
        acc = jnp.zeros((tk, tq), F32)
        for p in range(N_IDX_HEADS // 2):
            q2 = q_ref[:, p * 2 * IDX_DIM:(p + 1) * 2 * IDX_DIM]
            for half in range(2):
                h = 2 * p + half
                ph = lax.dot_general(k_halves[half], q2, NT_DIMS, preferred_element_type=F32)
                acc = acc + jnp.maximum(ph, 0.0) * wt[h:h + 1, :]
        bits = lax.bitcast_convert_type(acc, jnp.int32)
        key = bits ^ ((bits >> 31) & 0x7FFFFFFF)
        if diagonal:
            key_pos, query_pos = _key_query_iotas(k0, q0, tk, tq)
            key = jnp.where(key_pos <= query_pos, key, INT_MIN)
        hi_ref[pl.ds(k0, tk), :] = (key >> 16).astype(I16)
        lo_ref[pl.ds(k0, tk), :] = (key ^ 0x8000).astype(I16)

    def score_group(i, carry):
        for j in range(SCORE_BLOCKS):
            score_block(SCORE_BLOCKS * i + j, False)
        return carry

    n_before = nkb - 1
    lax.fori_loop(0, n_before // SCORE_BLOCKS, score_group, 0)
    first_left = n_before - n_before % SCORE_BLOCKS
    for left in range(1, SCORE_BLOCKS):
        @pl.when(n_before % SCORE_BLOCKS == left)
        def _(left=left):
            for j in range(left):
                score_block(first_left + j, False)

    score_block(nkb - 1, True)

    n_count = (nkb + COUNT_BLOCKS - 1) // COUNT_BLOCKS
    rows = COUNT_BLOCKS * tk

    def pad_block(kb, carry):
        k0 = pl.multiple_of(kb * tk, tk)
        hi_ref[pl.ds(k0, tk), :] = jnp.full((tk, tq), I16_MIN, I16)
        lo_ref[pl.ds(k0, tk), :] = jnp.full((tk, tq), I16_MIN, I16)
        return carry

    lax.fori_loop(nkb, n_count * COUNT_BLOCKS, pad_block, 0)

    def count(flags_fn):
        def blk(i, cnt):
            r0 = pl.multiple_of(i * rows, rows)
            groups = []
            for c in range(rows // COUNT_CHUNK):
                flags = flags_fn(r0 + c * COUNT_CHUNK, COUNT_CHUNK)
                parts = [flags[r * BF16_ROWS:(r + 1) * BF16_ROWS, :]
                         for r in range(COUNT_CHUNK // BF16_ROWS)]
                while len(parts) > 1:
                    parts = [a + b for a, b in zip(parts[0::2], parts[1::2])]
                groups.append(parts[0])
            while len(groups) > 1:
                groups = [a + b for a, b in zip(groups[0::2], groups[1::2])]
            return cnt + groups[0]
        cnt = lax.fori_loop(0, n_count, blk, jnp.zeros((BF16_ROWS, tq), I16))
        return jnp.sum(cnt.astype(jnp.int32), axis=0, keepdims=True)

    flag, no_flag = I16(1), I16(0)

    def count_ge(ref, thr16):
        return count(lambda r, n: jnp.where(ref[pl.ds(r, n), :] >= thr16, flag, no_flag))

    def kth_largest(ref, k):
        def bisect(it, carry):
            thr, n_ge, n_gt = carry
            cand = thr + jnp.left_shift(jnp.int32(1), 15 - it)
            cnt = count_ge(ref, cand.astype(I16))
            ok = cnt >= k
            return jnp.where(ok, cand, thr), jnp.where(ok, cnt, n_ge), jnp.where(ok, n_gt, cnt)
        zeros = jnp.zeros((1, tq), jnp.int32)
        thr, n_ge, n_gt = lax.fori_loop(0, 16, bisect, (zeros + I16_MIN, zeros, zeros))
        return thr.astype(I16), n_ge, n_gt

    thr_hi, _, above_hi = kth_largest(hi_ref, n_sel)
    need_lo = n_sel - above_hi

    def park(kb, carry):
        k0 = pl.multiple_of(kb * tk, tk)
        lo_ref[pl.ds(k0, tk), :] = jnp.where(hi_ref[pl.ds(k0, tk), :] == thr_hi,
                                             lo_ref[pl.ds(k0, tk), :], I16(I16_MIN))
        return carry

    lax.fori_loop(0, nkb, park, 0)
    thr_lo, at_least_lo, above_lo = kth_largest(lo_ref, need_lo)
    need_ties = need_lo - above_lo
    n_ties = jnp.where(thr_lo == I16(I16_MIN), tk * n_kblocks, at_least_lo - above_lo)
    all_ties_fit = jnp.max(n_ties - need_ties) <= 0

    one, zero, neg = BF16(1.0), BF16(0.0), BF16(NEG)

    def store_mask(kb, sel, diagonal):
        k0 = pl.multiple_of(kb * tk, tk)
        if diagonal:
            key_pos, query_pos = _key_query_iotas(k0, q0, tk, tq)
            sel = sel * (key_pos <= query_pos).astype(BF16)
        mask_ref[0, pl.ds(k0, tk), :] = jnp.where(sel > zero, zero, neg)

    def emit_all_ties(kb, diagonal):
        k0 = pl.multiple_of(kb * tk, tk)
        hi = hi_ref[pl.ds(k0, tk), :]
        lo = lo_ref[pl.ds(k0, tk), :]
        store_mask(kb, jnp.where(hi > thr_hi, one,
                                 jnp.where(hi == thr_hi, jnp.where(lo >= thr_lo, one, zero), zero)),
                   diagonal)

    earlier = (lax.broadcasted_iota(jnp.int32, (tk, tk), 1)
               < lax.broadcasted_iota(jnp.int32, (tk, tk), 0)).astype(BF16)
    ones_rows = jnp.ones((BF16_ROWS, tk), BF16)
    need_ties_f = need_ties.astype(F32)

    def emit_ranked_ties(kb, ties_seen, diagonal):
        k0 = pl.multiple_of(kb * tk, tk)
        hi = hi_ref[pl.ds(k0, tk), :]
        lo = lo_ref[pl.ds(k0, tk), :]
        above = jnp.where(hi > thr_hi, one, jnp.where(lo > thr_lo, one, zero))
        eq = jnp.where(hi == thr_hi, jnp.where(lo == thr_lo, one, zero), zero)
        rank = jnp.dot(earlier, eq, preferred_element_type=F32).astype(BF16)
        room = jnp.clip(need_ties_f - ties_seen, -1.0, float(tk)).astype(BF16)
        store_mask(kb, jnp.where(rank < room, jnp.maximum(above, eq), above), diagonal)
        return ties_seen + jnp.dot(ones_rows, eq, preferred_element_type=F32)[0:1]

    def emit_fast():
        def body(kb, carry):
            emit_all_ties(kb, False)
            return carry
        lax.fori_loop(0, nkb - 1, body, 0)
        emit_all_ties(nkb - 1, True)

    def emit_slow():
        ties_seen = lax.fori_loop(0, nkb - 1, lambda kb, seen: emit_ranked_ties(kb, seen, False),
                                  jnp.zeros((1, tq), F32))
        emit_ranked_ties(nkb - 1, ties_seen, True)

    lax.cond(all_ties_fit, emit_fast, emit_slow)

    def fill(kb, carry):
        k0 = pl.multiple_of(kb * tk, tk)
        mask_ref[0, pl.ds(k0, tk), :] = jnp.full((tk, tq), NEG, mask_ref.dtype)
        return carry

    lax.fori_loop(nkb, n_kblocks, fill, 0)


def _dsa_select(z, zs, B, S, qix_blk, n_sel):
    tq = tk = min(S, ATTN_BLOCK)
    nq = S // tq
    assert nq % COUNT_BLOCKS == 0
    qw = N_IDX_HEADS * IDX_DIM
    return pl.pallas_call(
        functools.partial(_dsa_select_kernel, tq=tq, tk=tk, n_sel=n_sel, n_kblocks=S // tk),
        grid=(B, nq),
        in_specs=[pl.BlockSpec((tq, qw), lambda b, i: (b * nq + i, qix_blk)),
                  pl.BlockSpec((S, 2 * IDX_DIM), lambda b, i: (b, 0)),
                  pl.BlockSpec((tq, 2 * IDX_DIM), lambda b, i: (b * nq + i, 1))],
        out_specs=pl.BlockSpec((1, S, tq), lambda b, i: (b * nq + i, 0, 0)),
        out_shape=jax.ShapeDtypeStruct((B * nq, S, tq), BF16),
        scratch_shapes=[pltpu.VMEM((S, tq), I16), pltpu.VMEM((S, tq), I16)],
        compiler_params=_params("arbitrary", "arbitrary"),
        name="dsa_select",
    )(z, zs, zs)


def _rel_bucket(dist):
    n = jnp.maximum(dist, 0)
    max_exact = N_BUCKETS // 2
    nf = jnp.maximum(n, 1).astype(F32)
    large = max_exact + (jnp.log(nf / max_exact) / math.log(MAX_DISTANCE / max_exact)
                         * (N_BUCKETS - max_exact)).astype(jnp.int32)
    large = jnp.minimum(large, N_BUCKETS - 1)
    return jnp.where(n < max_exact, n, large)


def _bias_tiles(rel_bias, tq):
    assert tq >= MAX_DISTANCE
    rb = rel_bias.astype(F32)
    key = jnp.arange(tq)[:, None]
    query = jnp.arange(tq)[None, :]
    bucket = _rel_bucket(jnp.stack([query - key, tq + query - key]))
    onehot = (bucket[None] == jnp.arange(N_BUCKETS)[:, None, None, None]).astype(F32)
    return jnp.einsum("nh,nikq->hikq", (rb - rb[N_BUCKETS - 1]) * LOG2_E, onehot,
                      precision=lax.Precision.HIGHEST)


def _dsa_attn_kernel(q_ref, k_ref, vt_ref, mask_ref, bias_ref, o_ref, m_ref, acc_ref, logit_ref, *,
                     tq, n_heads):
    qi = pl.program_id(2)
    m_ref[...] = jnp.full_like(m_ref, NEG)
    acc_ref[...] = jnp.zeros_like(acc_ref)
    ones_rows = jnp.ones((BF16_ROWS, tq), BF16)

    def step(blocks):
        k0s = [pl.multiple_of(kb * tq, tq) for kb, _ in blocks]
        masks = [mask_ref[0, pl.ds(k0, tq), :].astype(F32) for k0 in k0s]

        def head_scores(g):
            return [lax.dot_general(k_ref[pl.ds(k0, tq), _head_cols(g)], q_ref[:, _head_cols(g)],
                                    NT_DIMS, preferred_element_type=F32) for k0 in k0s]

        def head_logits(g, scores):
            m_new = m_ref[g]
            for j, (_, bias_idx) in enumerate(blocks):
                s = scores[j] + masks[j]
                if bias_idx is not None:
                    s = s + bias_ref[g, bias_idx]
                logit_ref[g, j] = s
                m_new = jnp.maximum(m_new, jnp.max(s, axis=0, keepdims=True))
            return m_new

        def head_accumulate(g, m_new):
            acc = jnp.exp2(m_ref[g] - m_new) * acc_ref[g]
            for j, k0 in enumerate(k0s):
                p = jnp.exp2(logit_ref[g, j] - m_new).astype(BF16)
                vt = jnp.concatenate([vt_ref[_head_cols(g), pl.ds(k0, tq)], ones_rows], axis=0)
                acc = acc + jnp.dot(vt, p, preferred_element_type=F32)
            acc_ref[g] = acc
            m_ref[g] = m_new

        scores = [head_scores(g) for g in range(n_heads)]
        maxima = [head_logits(g, scores[g]) for g in range(n_heads)]
        for g in range(n_heads):
            head_accumulate(g, maxima[g])

    n_far = jnp.maximum(qi - 1, 0)

    def far_pair(i, carry):
        step([(2 * i, None), (2 * i + 1, None)])
        return carry

    lax.fori_loop(0, n_far // 2, far_pair, 0)

    @pl.when(n_far % 2 == 1)
    def _():
        step([(n_far - 1, None)])

    @pl.when(qi >= 1)
    def _():
        step([(qi - 1, 1), (qi, 0)])

    @pl.when(qi == 0)
    def _():
        step([(qi, 0)])
    for g in range(n_heads):
        acc = acc_ref[g]
        out_t = acc[:HEAD_DIM] / acc[HEAD_DIM:HEAD_DIM + 1]
        o_ref[:, _head_cols(g)] = out_t.T.astype(o_ref.dtype)


def _dsa_attn(z, vt, mask, bias_tiles, B, S, H, q_blk, k_blk, v_blk):
    M = z.shape[0]
    tq = bias_tiles.shape[-1]
    nq = S // tq
    G = min(H, DSA_HEAD_GROUP)
    gw = G * HEAD_DIM
    return pl.pallas_call(
        functools.partial(_dsa_attn_kernel, tq=tq, n_heads=G),
        grid=(B, H // G, nq),
        in_specs=_attn_group_specs(S, tq, nq, G, q_blk, k_blk, v_blk, kv_buffers=1) + [
            pl.BlockSpec((1, S, tq), lambda b, hg, i: (b * nq + i, 0, 0)),
            pl.BlockSpec((G, 2, tq, tq), lambda b, hg, i: (hg, 0, 0, 0),
                         pipeline_mode=pl.Buffered(1))],
        out_specs=pl.BlockSpec((tq, gw), lambda b, hg, i: (b * nq + i, hg)),
        out_shape=jax.ShapeDtypeStruct((M, H * HEAD_DIM), BF16),
        scratch_shapes=[pltpu.VMEM((G, 1, tq), F32),
                        pltpu.VMEM((G, HEAD_DIM + BF16_ROWS, tq), F32),
                        pltpu.VMEM((G, 2, tq, tq), F32)],
        compiler_params=_params("arbitrary", "arbitrary", "arbitrary"),
        name="dsa_attn",
    )(z, z, vt, mask, bias_tiles)


def _mix_out_kernel(osb_ref, ods_ref, gsb_ref, gds_ref, x_ref, wsb_ref, wds_ref, wout_ref,
                    gt_ref, g_ref, sc_ref, sh_ref, x1_ref, h2_ref):
    t_sb = jnp.dot(osb_ref[...], wsb_ref[...], preferred_element_type=F32)
    t_ds = jnp.dot(ods_ref[...], wds_ref[...], preferred_element_type=F32)
    merged = (_sigmoid(gsb_ref[...].astype(F32)) * t_sb
              + _sigmoid(gds_ref[...].astype(F32)) * t_ds)
    y = jnp.dot(merged.astype(BF16), wout_ref[...], preferred_element_type=F32)
    x1 = x_ref[...] + gt_ref[0] * y
    x1_ref[...] = x1
    h2_ref[...] = _rms_mod(x1, g_ref[...], sc_ref[0], sh_ref[0]).astype(h2_ref.dtype)


def _mix_out(o_sb, o_ds, z, gate_blk, x2, w_sb, w_ds, w_out, gt, g, sc, sh, S):
    M, D = x2.shape
    B = gt.shape[0]
    W = o_sb.shape[1]
    tm = min(S, 512)
    per_b = S // tm
    row = lambda i: (i, 0)
    const = lambda i: (0, 0)
    per_batch = lambda i: (i // per_b, 0, 0)
    return pl.pallas_call(
        _mix_out_kernel,
        grid=(M // tm,),
        in_specs=[pl.BlockSpec((tm, W), row),
                  pl.BlockSpec((tm, W), row),
                  pl.BlockSpec((tm, D), lambda i: (i, gate_blk)),
                  pl.BlockSpec((tm, D), lambda i: (i, gate_blk + 1)),
                  pl.BlockSpec((tm, D), row),
                  pl.BlockSpec((W, D), const),
                  pl.BlockSpec((W, D), const),
                  pl.BlockSpec((D, D), const),
                  pl.BlockSpec((1, 1, D), per_batch),
                  pl.BlockSpec((1, D), const),
                  pl.BlockSpec((1, 1, D), per_batch),
                  pl.BlockSpec((1, 1, D), per_batch)],
        out_specs=[pl.BlockSpec((tm, D), row), pl.BlockSpec((tm, D), row)],
        out_shape=[jax.ShapeDtypeStruct((M, D), F32), jax.ShapeDtypeStruct((M, D), BF16)],
        compiler_params=_params("arbitrary"),
        name="mix_out",
    )(o_sb, o_ds, z, z, x2, w_sb, w_ds, w_out, gt.reshape(B, 1, D), g.reshape(1, D),
      sc.reshape(B, 1, D), sh.reshape(B, 1, D))


HALO = BF16_ROWS


def _ffn_act_kernel(h_ref, halo_ref, wg_ref, wu_ref, cw_ref, cb_ref, o_ref, wgb_ref, wub_ref, *,
                    tiles_per_seq):
    i = pl.program_id(1)

    @pl.when(i == 0)
    def _():
        wgb_ref[...] = wg_ref[...].astype(BF16)
        wub_ref[...] = wu_ref[...].astype(BF16)

    h = h_ref[...]
    wg = wgb_ref[...]
    g0 = jnp.dot(h, wg, preferred_element_type=F32)
    g_prev = jnp.dot(halo_ref[...], wg, preferred_element_type=F32)
    g_prev = jnp.where(i % tiles_per_seq == 0, 0.0, g_prev)
    ridx = lax.broadcasted_iota(jnp.int32, g0.shape, 0)
    g1 = jnp.where(ridx == 0, g_prev[HALO - 1:HALO, :], pltpu.roll(g0, 1, 0))
    g2 = jnp.where(ridx == 0, g_prev[HALO - 2:HALO - 1, :],
                   jnp.where(ridx == 1, g_prev[HALO - 1:HALO, :], pltpu.roll(g0, 2, 0)))
    cw = cw_ref[...]
    a = cb_ref[...] + g2 * cw[0:1, :] + g1 * cw[1:2, :] + g0 * cw[2:3, :]
    u = jnp.dot(h, wub_ref[...], preferred_element_type=F32)
    o_ref[...] = (a * _sigmoid(a) * u).astype(o_ref.dtype)


def _ffn_out_kernel(a_ref, wd_ref, x1_ref, gt_ref, gf_ref, o_ref):
    k = pl.program_id(1)

    @pl.when(k == 0)
    def _():
        o_ref[...] = jnp.zeros_like(o_ref)

    o_ref[...] += jnp.dot(a_ref[...], wd_ref[...], preferred_element_type=F32)

    @pl.when(k == pl.num_programs(1) - 1)
    def _():
        x = x1_ref[...] + gt_ref[0] * o_ref[...]
        ms = jnp.mean(x * x, axis=-1, keepdims=True)
        o_ref[...] = x * lax.rsqrt(ms + EPS) * gf_ref[...]


def _conv_ffn(h2, x1, w_gate, w_up, layer, w_down, conv_w, conv_b, gt, g_final, S):
    M, D = x1.shape
    B = gt.shape[0]
    F = w_gate.shape[2]
    tm = min(S, 1024)
    tf = min(F, 512)
    per_b = S // tm
    halo_per_tile = tm // HALO
    act = pl.pallas_call(
        functools.partial(_ffn_act_kernel, tiles_per_seq=per_b),
        grid=(F // tf, M // tm),
        in_specs=[pl.BlockSpec((tm, D), lambda f, i: (i, 0)),
                  pl.BlockSpec((HALO, D), lambda f, i: (jnp.maximum(i * halo_per_tile - 1, 0), 0)),
                  pl.BlockSpec((None, D, tf), lambda f, i: (layer, 0, f)),
                  pl.BlockSpec((None, D, tf), lambda f, i: (layer, 0, f)),
                  pl.BlockSpec((CONV_WIDTH, tf), lambda f, i: (0, f)),
                  pl.BlockSpec((1, tf), lambda f, i: (0, f))],
        out_specs=pl.BlockSpec((tm, tf), lambda f, i: (i, f)),
        out_shape=jax.ShapeDtypeStruct((M, F), BF16),
        scratch_shapes=[pltpu.VMEM((D, tf), BF16), pltpu.VMEM((D, tf), BF16)],
        compiler_params=_params("arbitrary", "arbitrary"),
        name="ffn_act",
    )(h2, h2, w_gate, w_up, conv_w, conv_b.reshape(1, F))
    return pl.pallas_call(
        _ffn_out_kernel,
        grid=(M // tm, F // tf),
        in_specs=[pl.BlockSpec((tm, tf), lambda i, k: (i, k)),
                  pl.BlockSpec((tf, D), lambda i, k: (k, 0)),
                  pl.BlockSpec((tm, D), lambda i, k: (i, 0)),
                  pl.BlockSpec((1, 1, D), lambda i, k: (i // per_b, 0, 0)),
                  pl.BlockSpec((1, D), lambda i, k: (0, 0))],
        out_specs=pl.BlockSpec((tm, D), lambda i, k: (i, 0)),
        out_shape=jax.ShapeDtypeStruct((M, D), F32),
        compiler_params=_params("arbitrary", "arbitrary"),
        name="ffn_out",
    )(act, w_down, x1, gt.reshape(B, 1, D), g_final.reshape(1, D))


def kernel(x, c, w_ada, b_ada, g_mix, w_in, w_o_sb, w_o_dsa, w_out, rel_bias, g_ffn, w_gate,
           w_up, conv_w, conv_b, w_down, g_final):
    B, S, D = x.shape
    depth = w_ada.shape[0]
    W = w_o_sb.shape[1]
    H = W // HEAD_DIM
    qw = N_IDX_HEADS * IDX_DIM
    assert w_o_dsa.shape[1] == W and D % W == 0 and S % ATTN_BLOCK == 0 and W % qw == 0
    n_sel = min(TOPK_MAX, S // 4)
    scale = HEAD_DIM ** -0.5
    x2 = x.reshape(B * S, D)

    for l in range(depth):
        mod = _adaln(c, w_ada[l], b_ada[l])
        sh1, sc1, gt1, sh2, sc2, gt2 = jnp.split(mod, 6, axis=-1)

        wt = w_in[l].T
        o_kix = 6 * W + qw
        o_wix = o_kix + IDX_DIM
        o_gate = o_wix + N_IDX_HEADS
        k_rows = wt[o_kix:o_wix]
        wt_tail = jnp.concatenate(
            [k_rows, k_rows, wt[o_wix:o_gate],
             jnp.zeros((2 * IDX_DIM - N_IDX_HEADS, D), wt.dtype)], axis=0)

        h1 = _norm_mod(x2, g_mix[l], sc1, sh1, S)
        z = _in_proj_wt(h1, wt, (0, 1, 3, 4, 6), W, ((0, scale), (2, scale * LOG2_E)), "in_proj")
        vt = _in_proj_wt(h1, wt, (2, 5), W, (), "in_proj_values", transposed=True)
        zg = _in_proj_wt(h1, wt[o_gate:], tuple(range(2 * D // W)), W, (), "in_proj_gates")
        zs = _in_proj_wt(h1, wt_tail, (0,), 4 * IDX_DIM, (), "in_proj_idx")

        o_sb = _sb_attn(z, vt, B, S, H, 0, H, 0)
        mask = _dsa_select(z, zs, B, S, 4 * W // qw, n_sel)
        o_ds = _dsa_attn(z, vt, mask, _bias_tiles(rel_bias, min(S, ATTN_BLOCK)),
                         B, S, H, 2 * H, 3 * H, H)

        x2, h2 = _mix_out(o_sb, o_ds, zg, 0, x2, w_o_sb[l].astype(BF16),
                          w_o_dsa[l].astype(BF16), w_out[l].astype(BF16), gt1, g_ffn[l], sc2, sh2, S)
        last = l == depth - 1
        assert last, "the final rms_norm is fused into the last layer's FFN"
        x2 = _conv_ffn(h2, x2, w_gate, w_up, l, w_down[l].astype(BF16),
                       conv_w[l], conv_b[l], gt2, g_final, S)
    return x2.reshape(B, S, D)
```

```python
import functools
import math

import jax
import jax.numpy as jnp
from jax import lax
from jax.experimental import pallas as pl
from jax.experimental.pallas import tpu as pltpu

HEAD_DIM = 128
N_IDX_HEADS = 16
IDX_DIM = 64
TOPK_MAX = 256
N_BUCKETS = 32
MAX_DISTANCE = 128
CONV_WIDTH = 3
EPS = 1e-6

F32 = jnp.float32
BF16 = jnp.bfloat16
NEG = -1e30
I16 = jnp.int16
INT_MIN = -2 ** 31
I16_MIN = -2 ** 15
EXP_ZERO_BELOW = 104.0
LOG2_E = math.log2(math.e)
V7X_VMEM_LIMIT = 56 * 1024 * 1024
BF16_ROWS = 16
ATTN_BLOCK = 256
SB_HEAD_GROUP = 4
DSA_HEAD_GROUP = 8
SCORE_BLOCKS = 4
COUNT_BLOCKS = 4
COUNT_CHUNK = 64
NT_DIMS = (((1,), (1,)), ((), ()))


def _params(*sem):
    return pltpu.CompilerParams(dimension_semantics=sem, vmem_limit_bytes=V7X_VMEM_LIMIT)


def _sigmoid(x):
    return 1.0 / (1.0 + jnp.exp(-x))


def _adaln_kernel(ct_ref, w_ref, b_ref, o_ref):
    ct = ct_ref[...]
    act = ct * _sigmoid(ct)
    w = w_ref[...]
    for b in range(ct.shape[1]):
        o_ref[b:b + 1, :] = jnp.sum(act[:, b:b + 1] * w, axis=0, keepdims=True) + b_ref[...]


def _adaln(c, w, bias):
    B, D = c.shape
    N = w.shape[1]
    tn = min(N, 2048)
    return pl.pallas_call(
        _adaln_kernel,
        grid=(N // tn,),
        in_specs=[pl.BlockSpec((D, B), lambda j: (0, 0)),
                  pl.BlockSpec((D, tn), lambda j: (0, j)),
                  pl.BlockSpec((1, tn), lambda j: (0, j))],
        out_specs=pl.BlockSpec((B, tn), lambda j: (0, j)),
        out_shape=jax.ShapeDtypeStruct((B, N), F32),
        compiler_params=_params("arbitrary"),
        name="adaln",
    )(c.T, w, bias.reshape(1, N))


def _rms_mod(x, g, sc, sh):
    ms = jnp.mean(x * x, axis=-1, keepdims=True)
    y = x * lax.rsqrt(ms + EPS) * g
    return y * (1.0 + sc) + sh


def _norm_mod_kernel(x_ref, g_ref, sc_ref, sh_ref, o_ref):
    o_ref[...] = _rms_mod(x_ref[...], g_ref[...], sc_ref[0], sh_ref[0]).astype(o_ref.dtype)


def _norm_mod(x2, g, sc, sh, S):
    M, D = x2.shape
    B = sc.shape[0]
    tm = min(S, 1024)
    per_b = S // tm
    return pl.pallas_call(
        _norm_mod_kernel,
        grid=(M // tm,),
        in_specs=[pl.BlockSpec((tm, D), lambda i: (i, 0)),
                  pl.BlockSpec((1, D), lambda i: (0, 0)),
                  pl.BlockSpec((1, 1, D), lambda i: (i // per_b, 0, 0)),
                  pl.BlockSpec((1, 1, D), lambda i: (i // per_b, 0, 0))],
        out_specs=pl.BlockSpec((tm, D), lambda i: (i, 0)),
        out_shape=jax.ShapeDtypeStruct((M, D), BF16),
        compiler_params=_params("arbitrary"),
        name="norm_mod",
    )(x2, g.reshape(1, D), sc.reshape(B, 1, D), sh.reshape(B, 1, D))


def _in_proj_kernel(a_ref, b_ref, o_ref, *, tile_scales):
    j = pl.program_id(0)
    acc = jnp.dot(a_ref[...], b_ref[...], preferred_element_type=F32)
    scale = jnp.float32(1.0)
    for tile, tile_scale in tile_scales:
        scale = jnp.where(j == tile, tile_scale, scale)
    o_ref[...] = (acc * scale).astype(o_ref.dtype)


def _in_proj_wt_kernel(a_ref, wt_ref, o_ref, wb_ref, *, tile_scales):
    @pl.when(pl.program_id(1) == 0)
    def _():
        wb_ref[...] = wt_ref[...].T.astype(BF16)

    _in_proj_kernel(a_ref, wb_ref, o_ref, tile_scales=tile_scales)


def _in_proj_t_kernel(a_ref, wt_ref, o_ref, wb_ref):
    @pl.when(pl.program_id(1) == 0)
    def _():
        wb_ref[...] = wt_ref[...].astype(BF16)

    o_ref[...] = lax.dot_general(wb_ref[...], a_ref[...], NT_DIMS,
                                 preferred_element_type=F32).astype(o_ref.dtype)


def _tile_index_fn(tiles):
    steps = []
    for j, t in enumerate(tiles):
        while len(steps) < t - j:
            steps.append(j)
    return lambda j: j + sum(jnp.where(j >= first, 1, 0) for first in steps)


def _in_proj_wt(h, wt, tiles, tn, tile_scales, name, transposed=False):
    M, K = h.shape
    tm = min(M, 1024)
    n = len(tiles)
    tile_of = _tile_index_fn(tiles)
    in_specs = [pl.BlockSpec((tm, K), lambda j, i: (i, 0)),
                pl.BlockSpec((tn, K), lambda j, i: (tile_of(j), 0))]
    if transposed:
        assert not tile_scales
        body, wb_shape = _in_proj_t_kernel, (tn, K)
        out_spec = pl.BlockSpec((tn, tm), lambda j, i: (j, i))
        out_shape = jax.ShapeDtypeStruct((n * tn, M), BF16)
    else:
        body, wb_shape = functools.partial(_in_proj_wt_kernel, tile_scales=tile_scales), (K, tn)
        out_spec = pl.BlockSpec((tm, tn), lambda j, i: (i, j))
        out_shape = jax.ShapeDtypeStruct((M, n * tn), BF16)
    return pl.pallas_call(
        body,
        grid=(n, M // tm),
        in_specs=in_specs,
        out_specs=out_spec,
        out_shape=out_shape,
        scratch_shapes=[pltpu.VMEM(wb_shape, BF16)],
        compiler_params=_params("arbitrary", "arbitrary"),
        name=name,
    )(h, wt)


def _key_query_iotas(k0, q0, tk, tq):
    key_pos = k0 + lax.broadcasted_iota(jnp.int32, (tk, tq), 0)
    query_pos = q0 + lax.broadcasted_iota(jnp.int32, (tk, tq), 1)
    return key_pos, query_pos


def _head_cols(g):
    return slice(g * HEAD_DIM, (g + 1) * HEAD_DIM)


def _sb_kernel(q_ref, k_ref, vt_ref, o_ref, run_ref, acc_ref, *, tq, tk, n_heads):
    qi = pl.program_id(2)
    q0 = qi * tq
    run_ref[...] = jnp.zeros_like(run_ref)
    acc_ref[...] = jnp.zeros_like(acc_ref)
    later = (lax.broadcasted_iota(jnp.int32, (tk, tk), 1)
             > lax.broadcasted_iota(jnp.int32, (tk, tk), 0)).astype(BF16)

    def step(kbs, first_is_diagonal):
        heads = range(n_heads)
        k0s = [pl.multiple_of(kb * tk, tk) for kb in kbs]
        causal = None
        if first_is_diagonal:
            key_pos, query_pos = _key_query_iotas(k0s[0], q0, tk, tq)
            causal = key_pos < query_pos
        masked = [first_is_diagonal and j == 0 for j in range(len(kbs))]
        zs = [[lax.dot_general(k_ref[pl.ds(k0, tk), _head_cols(g)], q_ref[:, _head_cols(g)],
                               NT_DIMS, preferred_element_type=F32) for g in heads] for k0 in k0s]
        sps = [[jnp.maximum(z, 0.0) + jnp.log(1.0 + jnp.exp(-jnp.abs(z))) for z in zj] for zj in zs]
        spms = [[jnp.where(causal, sp, 0.0) if masked[j] else sp for sp in spj]
                for j, spj in enumerate(sps)]
        his = [[spm.astype(BF16) for spm in spj] for spj in spms]
        los = [[(spm - hi.astype(F32)).astype(BF16) for spm, hi in zip(spj, hij)]
               for spj, hij in zip(spms, his)]
        suffixes = [[jnp.dot(later, hi, preferred_element_type=F32)
                     + jnp.dot(later, lo, preferred_element_type=F32) for hi, lo in zip(hij, loj)]
                    for hij, loj in zip(his, los)]
        weights, new_runs = [], []
        for g in heads:
            run = run_ref[g]
            head_weights = []
            for j in range(len(kbs)):
                a = jnp.exp(zs[j][g] - sps[j][g] - suffixes[j][g] - run)
                if masked[j]:
                    a = jnp.where(causal, a, 0.0)
                head_weights.append(a.astype(BF16))
                run = run + jnp.sum(spms[j][g], axis=0, keepdims=True)
            weights.append(head_weights)
            new_runs.append(run)
        min_run = None
        for g in heads:
            acc = acc_ref[g]
            for j, k0 in enumerate(k0s):
                acc = acc + jnp.dot(vt_ref[_head_cols(g), pl.ds(k0, tk)], weights[g][j],
                                    preferred_element_type=F32)
            acc_ref[g] = acc
            run_ref[g] = new_runs[g]
            head_min = jnp.min(new_runs[g])
            min_run = head_min if min_run is None else jnp.minimum(min_run, head_min)
        return min_run > EXP_ZERO_BELOW

    def body(carry):
        kb, _ = carry
        return kb - 1, step([kb], False)

    def cond(carry):
        kb, dead = carry
        return jnp.logical_and(kb >= 0, jnp.logical_not(dead))

    @pl.when(qi == 0)
    def _():
        step([qi], True)

    @pl.when(qi >= 1)
    def _():
        dead = step([qi, qi - 1], True)
        lax.while_loop(cond, body, (qi - 2, dead))

    for g in range(n_heads):
        o_ref[:, _head_cols(g)] = acc_ref[g].T.astype(o_ref.dtype)


def _attn_group_specs(S, tq, nq, G, q_blk, k_blk, v_blk, kv_buffers):
    assert q_blk % G == 0 and k_blk % G == 0 and v_blk % G == 0
    gw = G * HEAD_DIM
    mode = pl.Buffered(kv_buffers)
    return [pl.BlockSpec((tq, gw), lambda b, hg, i: (b * nq + i, q_blk // G + hg)),
            pl.BlockSpec((S, gw), lambda b, hg, i: (b, k_blk // G + hg), pipeline_mode=mode),
            pl.BlockSpec((gw, S), lambda b, hg, i: (v_blk // G + hg, b), pipeline_mode=mode)]


def _sb_attn(z, vt, B, S, H, q_blk, k_blk, v_blk):
    M = z.shape[0]
    tq = tk = min(S, ATTN_BLOCK)
    nq = S // tq
    G = min(H, SB_HEAD_GROUP)
    gw = G * HEAD_DIM
    return pl.pallas_call(
        functools.partial(_sb_kernel, tq=tq, tk=tk, n_heads=G),
        grid=(B, H // G, nq),
        in_specs=_attn_group_specs(S, tq, nq, G, q_blk, k_blk, v_blk, kv_buffers=2),
        out_specs=pl.BlockSpec((tq, gw), lambda b, hg, i: (b * nq + i, hg)),
        out_shape=jax.ShapeDtypeStruct((M, H * HEAD_DIM), BF16),
        scratch_shapes=[pltpu.VMEM((G, 1, tq), F32), pltpu.VMEM((G, HEAD_DIM, tq), F32)],
        compiler_params=_params("arbitrary", "arbitrary", "arbitrary"),
        name="sb_attn",
    )(z, z, vt)


def _dsa_select_kernel(q_ref, k_ref, w_ref, mask_ref, hi_ref, lo_ref, *, tq, tk, n_sel, n_kblocks):
    qi = pl.program_id(1)
    q0 = qi * tq
    nkb = (q0 + tq + tk - 1) // tk
    lane = lax.broadcasted_iota(jnp.int32, (tk, 2 * IDX_DIM), 1)
    w_scale = (N_IDX_HEADS ** -0.5) * (IDX_DIM ** -0.5)
    wt = (w_ref[...].astype(F32) * w_scale).T

    def score_block(kb, diagonal):
        k0 = pl.multiple_of(kb * tk, tk)
        kk = k_ref[pl.ds(k0, tk), :]
        k_halves = (jnp.where(lane < IDX_DIM, kk, jnp.zeros_like(kk)),
                    jnp.where(lane >= IDX_DIM, kk, jnp.zeros_like(kk)))
        acc = jnp.zeros((tk, tq), F32)
        for p in range(N_IDX_HEADS // 2):
            q2 = q_ref[:, p * 2 * IDX_DIM:(p + 1) * 2 * IDX_DIM]
            for half in range(2):
                h = 2 * p + half
                ph = lax.dot_general(k_halves[half], q2, NT_DIMS, preferred_element_type=F32)
                acc = acc + jnp.maximum(ph, 0.0) * wt[h:h + 1, :]
        bits = lax.bitcast_convert_type(acc, jnp.int32)
        key = bits ^ ((bits >> 31) & 0x7FFFFFFF)
        if diagonal:
            key_pos, query_pos = _key_query_iotas(k0, q0, tk, tq)
            key = jnp.where(key_pos <= query_pos, key, INT_MIN)
        hi_ref[pl.ds(k0, tk), :] = (key >> 16).astype(I16)
        lo_ref[pl.ds(k0, tk), :] = (key ^ 0x8000).astype(I16)

    def score_group(i, carry):
        for j in range(SCORE_BLOCKS):
            score_block(SCORE_BLOCKS * i + j, False)
        return carry

    n_before = nkb - 1
    lax.fori_loop(0, n_before // SCORE_BLOCKS, score_group, 0)
    first_left = n_before - n_before % SCORE_BLOCKS
    for left in range(1, SCORE_BLOCKS):
        @pl.when(n_before % SCORE_BLOCKS == left)
        def _(left=left):
            for j in range(left):
                score_block(first_left + j, False)

    score_block(nkb - 1, True)

    n_count = (nkb + COUNT_BLOCKS - 1) // COUNT_BLOCKS
    rows = COUNT_BLOCKS * tk

    def pad_block(kb, carry):
        k0 = pl.multiple_of(kb * tk, tk)
        hi_ref[pl.ds(k0, tk), :] = jnp.full((tk, tq), I16_MIN, I16)
        lo_ref[pl.ds(k0, tk), :] = jnp.full((tk, tq), I16_MIN, I16)
        return carry

    lax.fori_loop(nkb, n_count * COUNT_BLOCKS, pad_block, 0)

    def count(flags_fn):
        def blk(i, cnt):
            r0 = pl.multiple_of(i * rows, rows)
            groups = []
            for c in range(rows // COUNT_CHUNK):
                flags = flags_fn(r0 + c * COUNT_CHUNK, COUNT_CHUNK)
                parts = [flags[r * BF16_ROWS:(r + 1) * BF16_ROWS, :]
                         for r in range(COUNT_CHUNK // BF16_ROWS)]
                while len(parts) > 1:
                    parts = [a + b for a, b in zip(parts[0::2], parts[1::2])]
                groups.append(parts[0])
            while len(groups) > 1:
                groups = [a + b for a, b in zip(groups[0::2], groups[1::2])]
            return cnt + groups[0]
        cnt = lax.fori_loop(0, n_count, blk, jnp.zeros((BF16_ROWS, tq), I16))
        return jnp.sum(cnt.astype(jnp.int32), axis=0, keepdims=True)

    flag, no_flag = I16(1), I16(0)

    def count_ge(ref, thr16):
        return count(lambda r, n: jnp.where(ref[pl.ds(r, n), :] >= thr16, flag, no_flag))

    def kth_largest(ref, k):
        def bisect(it, carry):
            thr, n_ge, n_gt = carry
            cand = thr + jnp.left_shift(jnp.int32(1), 15 - it)
            cnt = count_ge(ref, cand.astype(I16))
            ok = cnt >= k
            return jnp.where(ok, cand, thr), jnp.where(ok, cnt, n_ge), jnp.where(ok, n_gt, cnt)
        zeros = jnp.zeros((1, tq), jnp.int32)
        thr, n_ge, n_gt = lax.fori_loop(0, 16, bisect, (zeros + I16_MIN, zeros, zeros))
        return thr.astype(I16), n_ge, n_gt

    thr_hi, _, above_hi = kth_largest(hi_ref, n_sel)
    need_lo = n_sel - above_hi

    def park(kb, carry):
        k0 = pl.multiple_of(kb * tk, tk)
        lo_ref[pl.ds(k0, tk), :] = jnp.where(hi_ref[pl.ds(k0, tk), :] == thr_hi,
                                             lo_ref[pl.ds(k0, tk), :], I16(I16_MIN))
        return carry

    lax.fori_loop(0, nkb, park, 0)
    thr_lo, at_least_lo, above_lo = kth_largest(lo_ref, need_lo)
    need_ties = need_lo - above_lo
    n_ties = jnp.where(thr_lo == I16(I16_MIN), tk * n_kblocks, at_least_lo - above_lo)
    all_ties_fit = jnp.max(n_ties - need_ties) <= 0

    one, zero, neg = BF16(1.0), BF16(0.0), BF16(NEG)

    def store_mask(kb, sel, diagonal):
        k0 = pl.multiple_of(kb * tk, tk)
        if diagonal:
            key_pos, query_pos = _key_query_iotas(k0, q0, tk, tq)
            sel = sel * (key_pos <= query_pos).astype(BF16)
        mask_ref[0, pl.ds(k0, tk), :] = jnp.where(sel > zero, zero, neg)

    def emit_all_ties(kb, diagonal):
        k0 = pl.multiple_of(kb * tk, tk)
        hi = hi_ref[pl.ds(k0, tk), :]
        lo = lo_ref[pl.ds(k0, tk), :]
        store_mask(kb, jnp.where(hi > thr_hi, one,
                                 jnp.where(hi == thr_hi, jnp.where(lo >= thr_lo, one, zero), zero)),
                   diagonal)

    earlier = (lax.broadcasted_iota(jnp.int32, (tk, tk), 1)
               < lax.broadcasted_iota(jnp.int32, (tk, tk), 0)).astype(BF16)
    ones_rows = jnp.ones((BF16_ROWS, tk), BF16)
    need_ties_f = need_ties.astype(F32)

    def emit_ranked_ties(kb, ties_seen, diagonal):
        k0 = pl.multiple_of(kb * tk, tk)
        hi = hi_ref[pl.ds(k0, tk), :]
        lo = lo_ref[pl.ds(k0, tk), :]
        above = jnp.where(hi > thr_hi, one, jnp.where(lo > thr_lo, one, zero))
        eq = jnp.where(hi == thr_hi, jnp.where(lo == thr_lo, one, zero), zero)
        rank = jnp.dot(earlier, eq, preferred_element_type=F32).astype(BF16)
        room = jnp.clip(need_ties_f - ties_seen, -1.0, float(tk)).astype(BF16)
        store_mask(kb, jnp.where(rank < room, jnp.maximum(above, eq), above), diagonal)
        return ties_seen + jnp.dot(ones_rows, eq, preferred_element_type=F32)[0:1]

    def emit_fast():
        def body(kb, carry):
            emit_all_ties(kb, False)
            return carry
        lax.fori_loop(0, nkb - 1, body, 0)
        emit_all_ties(nkb - 1, True)

    def emit_slow():
        ties_seen = lax.fori_loop(0, nkb - 1, lambda kb, seen: emit_ranked_ties(kb, seen, False),
                                  jnp.zeros((1, tq), F32))
        emit_ranked_ties(nkb - 1, ties_seen, True)

    lax.cond(all_ties_fit, emit_fast, emit_slow)

    def fill(kb, carry):
        k0 = pl.multiple_of(kb * tk, tk)
        mask_ref[0, pl.ds(k0, tk), :] = jnp.full((tk, tq), NEG, mask_ref.dtype)
        return carry

    lax.fori_loop(nkb, n_kblocks, fill, 0)


def _dsa_select(z, zs, B, S, qix_blk, n_sel):
    tq = tk = min(S, ATTN_BLOCK)
    nq = S // tq
    assert nq % COUNT_BLOCKS == 0
    qw = N_IDX_HEADS * IDX_DIM
    return pl.pallas_call(
        functools.partial(_dsa_select_kernel, tq=tq, tk=tk, n_sel=n_sel, n_kblocks=S // tk),
        grid=(B, nq),
        in_specs=[pl.BlockSpec((tq, qw), lambda b, i: (b * nq + i, qix_blk)),
                  pl.BlockSpec((S, 2 * IDX_DIM), lambda b, i: (b, 0)),
                  pl.BlockSpec((tq, 2 * IDX_DIM), lambda b, i: (b * nq + i, 1))],
        out_specs=pl.BlockSpec((1, S, tq), lambda b, i: (b * nq + i, 0, 0)),
        out_shape=jax.ShapeDtypeStruct((B * nq, S, tq), BF16),
        scratch_shapes=[pltpu.VMEM((S, tq), I16), pltpu.VMEM((S, tq), I16)],
        compiler_params=_params("arbitrary", "arbitrary"),
        name="dsa_select",
    )(z, zs, zs)


def _rel_bucket(dist):
    n = jnp.maximum(dist, 0)
    max_exact = N_BUCKETS // 2
    nf = jnp.maximum(n, 1).astype(F32)
    large = max_exact + (jnp.log(nf / max_exact) / math.log(MAX_DISTANCE / max_exact)
                         * (N_BUCKETS - max_exact)).astype(jnp.int32)
    large = jnp.minimum(large, N_BUCKETS - 1)
    return jnp.where(n < max_exact, n, large)


def _bias_tiles(rel_bias, tq):
    assert tq >= MAX_DISTANCE
    rb = rel_bias.astype(F32)
    key = jnp.arange(tq)[:, None]
    query = jnp.arange(tq)[None, :]
    bucket = _rel_bucket(jnp.stack([query - key, tq + query - key]))
    onehot = (bucket[None] == jnp.arange(N_BUCKETS)[:, None, None, None]).astype(F32)
    return jnp.einsum("nh,nikq->hikq", (rb - rb[N_BUCKETS - 1]) * LOG2_E, onehot,
                      precision=lax.Precision.HIGHEST)


def _dsa_attn_kernel(q_ref, k_ref, vt_ref, mask_ref, bias_ref, o_ref, m_ref, acc_ref, logit_ref, *,
                     tq, n_heads):
    qi = pl.program_id(2)
    m_ref[...] = jnp.full_like(m_ref, NEG)
    acc_ref[...] = jnp.zeros_like(acc_ref)
    ones_rows = jnp.ones((BF16_ROWS, tq), BF16)

    def step(blocks):
        k0s = [pl.multiple_of(kb * tq, tq) for kb, _ in blocks]
        masks = [mask_ref[0, pl.ds(k0, tq), :].astype(F32) for k0 in k0s]

        def head_scores(g):
            return [lax.dot_general(k_ref[pl.ds(k0, tq), _head_cols(g)], q_ref[:, _head_cols(g)],
                                    NT_DIMS, preferred_element_type=F32) for k0 in k0s]

        def head_logits(g, scores):
            m_new = m_ref[g]
            for j, (_, bias_idx) in enumerate(blocks):
                s = scores[j] + masks[j]
                if bias_idx is not None:
                    s = s + bias_ref[g, bias_idx]
                logit_ref[g, j] = s
                m_new = jnp.maximum(m_new, jnp.max(s, axis=0, keepdims=True))
            return m_new

        def head_accumulate(g, m_new):
            acc = jnp.exp2(m_ref[g] - m_new) * acc_ref[g]
            for j, k0 in enumerate(k0s):
                p = jnp.exp2(logit_ref[g, j] - m_new).astype(BF16)
                vt = jnp.concatenate([vt_ref[_head_cols(g), pl.ds(k0, tq)], ones_rows], axis=0)
                acc = acc + jnp.dot(vt, p, preferred_element_type=F32)
            acc_ref[g] = acc
            m_ref[g] = m_new

        scores = [head_scores(g) for g in range(n_heads)]
        maxima = [head_logits(g, scores[g]) for g in range(n_heads)]
        for g in range(n_heads):
            head_accumulate(g, maxima[g])

    n_far = jnp.maximum(qi - 1, 0)

    def far_pair(i, carry):
        step([(2 * i, None), (2 * i + 1, None)])
        return carry

    lax.fori_loop(0, n_far // 2, far_pair, 0)

    @pl.when(n_far % 2 == 1)
    def _():
        step([(n_far - 1, None)])

    @pl.when(qi >= 1)
    def _():
        step([(qi - 1, 1), (qi, 0)])

    @pl.when(qi == 0)
    def _():
        step([(qi, 0)])
    for g in range(n_heads):
        acc = acc_ref[g]
        out_t = acc[:HEAD_DIM] / acc[HEAD_DIM:HEAD_DIM + 1]
        o_ref[:, _head_cols(g)] = out_t.T.astype(o_ref.dtype)


def _dsa_attn(z, vt, mask, bias_tiles, B, S, H, q_blk, k_blk, v_blk):
    M = z.shape[0]
    tq = bias_tiles.shape[-1]
    nq = S // tq
    G = min(H, DSA_HEAD_GROUP)
    gw = G * HEAD_DIM
    return pl.pallas_call(
        functools.partial(_dsa_attn_kernel, tq=tq, n_heads=G),
        grid=(B, H // G, nq),
        in_specs=_attn_group_specs(S, tq, nq, G, q_blk, k_blk, v_blk, kv_buffers=1) + [
            pl.BlockSpec((1, S, tq), lambda b, hg, i: (b * nq + i, 0, 0)),
            pl.BlockSpec((G, 2, tq, tq), lambda b, hg, i: (hg, 0, 0, 0),
                         pipeline_mode=pl.Buffered(1))],
        out_specs=pl.BlockSpec((tq, gw), lambda b, hg, i: (b * nq + i, hg)),
        out_shape=jax.ShapeDtypeStruct((M, H * HEAD_DIM), BF16),
        scratch_shapes=[pltpu.VMEM((G, 1, tq), F32),
                        pltpu.VMEM((G, HEAD_DIM + BF16_ROWS, tq), F32),
                        pltpu.VMEM((G, 2, tq, tq), F32)],
        compiler_params=_params("arbitrary", "arbitrary", "arbitrary"),
        name="dsa_attn",
    )(z, z, vt, mask, bias_tiles)


def _mix_out_kernel(osb_ref, ods_ref, gsb_ref, gds_ref, x_ref, wsb_ref, wds_ref, wout_ref,
                    gt_ref, g_ref, sc_ref, sh_ref, x1_ref, h2_ref):
    t_sb = jnp.dot(osb_ref[...], wsb_ref[...], preferred_element_type=F32)
    t_ds = jnp.dot(ods_ref[...], wds_ref[...], preferred_element_type=F32)
    merged = (_sigmoid(gsb_ref[...].astype(F32)) * t_sb
              + _sigmoid(gds_ref[...].astype(F32)) * t_ds)
    y = jnp.dot(merged.astype(BF16), wout_ref[...], preferred_element_type=F32)
    x1 = x_ref[...] + gt_ref[0] * y
    x1_ref[...] = x1
    h2_ref[...] = _rms_mod(x1, g_ref[...], sc_ref[0], sh_ref[0]).astype(h2_ref.dtype)


def _mix_out(o_sb, o_ds, z, gate_blk, x2, w_sb, w_ds, w_out, gt, g, sc, sh, S):
    M, D = x2.shape
    B = gt.shape[0]
    W = o_sb.shape[1]
    tm = min(S, 512)
    per_b = S // tm
    row = lambda i: (i, 0)
    const = lambda i: (0, 0)
    per_batch = lambda i: (i // per_b, 0, 0)
    return pl.pallas_call(
        _mix_out_kernel,
        grid=(M // tm,),
        in_specs=[pl.BlockSpec((tm, W), row),
                  pl.BlockSpec((tm, W), row),
                  pl.BlockSpec((tm, D), lambda i: (i, gate_blk)),
                  pl.BlockSpec((tm, D), lambda i: (i, gate_blk + 1)),
                  pl.BlockSpec((tm, D), row),
                  pl.BlockSpec((W, D), const),
                  pl.BlockSpec((W, D), const),
                  pl.BlockSpec((D, D), const),
                  pl.BlockSpec((1, 1, D), per_batch),
                  pl.BlockSpec((1, D), const),
                  pl.BlockSpec((1, 1, D), per_batch),
                  pl.BlockSpec((1, 1, D), per_batch)],
        out_specs=[pl.BlockSpec((tm, D), row), pl.BlockSpec((tm, D), row)],
        out_shape=[jax.ShapeDtypeStruct((M, D), F32), jax.ShapeDtypeStruct((M, D), BF16)],
        compiler_params=_params("arbitrary"),
        name="mix_out",
    )(o_sb, o_ds, z, z, x2, w_sb, w_ds, w_out, gt.reshape(B, 1, D), g.reshape(1, D),
      sc.reshape(B, 1, D), sh.reshape(B, 1, D))


HALO = BF16_ROWS


def _ffn_act_kernel(h_ref, halo_ref, wg_ref, wu_ref, cw_ref, cb_ref, o_ref, wgb_ref, wub_ref, *,
                    tiles_per_seq):
    i = pl.program_id(1)

    @pl.when(i == 0)
    def _():
        wgb_ref[...] = wg_ref[...].astype(BF16)
        wub_ref[...] = wu_ref[...].astype(BF16)

    h = h_ref[...]
    wg = wgb_ref[...]
    g0 = jnp.dot(h, wg, preferred_element_type=F32)
    g_prev = jnp.dot(halo_ref[...], wg, preferred_element_type=F32)
    g_prev = jnp.where(i % tiles_per_seq == 0, 0.0, g_prev)
    ridx = lax.broadcasted_iota(jnp.int32, g0.shape, 0)
    g1 = jnp.where(ridx == 0, g_prev[HALO - 1:HALO, :], pltpu.roll(g0, 1, 0))
    g2 = jnp.where(ridx == 0, g_prev[HALO - 2:HALO - 1, :],
                   jnp.where(ridx == 1, g_prev[HALO - 1:HALO, :], pltpu.roll(g0, 2, 0)))
    cw = cw_ref[...]
    a = cb_ref[...] + g2 * cw[0:1, :] + g1 * cw[1:2, :] + g0 * cw[2:3, :]
    u = jnp.dot(h, wub_ref[...], preferred_element_type=F32)
    o_ref[...] = (a * _sigmoid(a) * u).astype(o_ref.dtype)


def _ffn_out_kernel(a_ref, wd_ref, x1_ref, gt_ref, gf_ref, o_ref):
    k = pl.program_id(1)

    @pl.when(k == 0)
    def _():
        o_ref[...] = jnp.zeros_like(o_ref)

    o_ref[...] += jnp.dot(a_ref[...], wd_ref[...], preferred_element_type=F32)

    @pl.when(k == pl.num_programs(1) - 1)
    def _():
        x = x1_ref[...] + gt_ref[0] * o_ref[...]
        ms = jnp.mean(x * x, axis=-1, keepdims=True)
        o_ref[...] = x * lax.rsqrt(ms + EPS) * gf_ref[...]


def _conv_ffn(h2, x1, w_gate, w_up, layer, w_down, conv_w, conv_b, gt, g_final, S):
    M, D = x1.shape
    B = gt.shape[0]
    F = w_gate.shape[2]
    tm = min(S, 1024)
    tf = min(F, 512)
    per_b = S // tm
    halo_per_tile = tm // HALO
    act = pl.pallas_call(
        functools.partial(_ffn_act_kernel, tiles_per_seq=per_b),
        grid=(F // tf, M // tm),
        in_specs=[pl.BlockSpec((tm, D), lambda f, i: (i, 0)),
                  pl.BlockSpec((HALO, D), lambda f, i: (jnp.maximum(i * halo_per_tile - 1, 0), 0)),
                  pl.BlockSpec((None, D, tf), lambda f, i: (layer, 0, f)),
                  pl.BlockSpec((None, D, tf), lambda f, i: (layer, 0, f)),
                  pl.BlockSpec((CONV_WIDTH, tf), lambda f, i: (0, f)),
                  pl.BlockSpec((1, tf), lambda f, i: (0, f))],
        out_specs=pl.BlockSpec((tm, tf), lambda f, i: (i, f)),
        out_shape=jax.ShapeDtypeStruct((M, F), BF16),
        scratch_shapes=[pltpu.VMEM((D, tf), BF16), pltpu.VMEM((D, tf), BF16)],
        compiler_params=_params("arbitrary", "arbitrary"),
        name="ffn_act",
    )(h2, h2, w_gate, w_up, conv_w, conv_b.reshape(1, F))
    return pl.pallas_call(
        _ffn_out_kernel,
        grid=(M // tm, F // tf),
        in_specs=[pl.BlockSpec((tm, tf), lambda i, k: (i, k)),
                  pl.BlockSpec((tf, D), lambda i, k: (k, 0)),
                  pl.BlockSpec((tm, D), lambda i, k: (i, 0)),
                  pl.BlockSpec((1, 1, D), lambda i, k: (i // per_b, 0, 0)),
                  pl.BlockSpec((1, D), lambda i, k: (0, 0))],
        out_specs=pl.BlockSpec((tm, D), lambda i, k: (i, 0)),
        out_shape=jax.ShapeDtypeStruct((M, D), F32),
        compiler_params=_params("arbitrary", "arbitrary"),
        name="ffn_out",
    )(act, w_down, x1, gt.reshape(B, 1, D), g_final.reshape(1, D))


def kernel(x, c, w_ada, b_ada, g_mix, w_in, w_o_sb, w_o_dsa, w_out, rel_bias, g_ffn, w_gate,
           w_up, conv_w, conv_b, w_down, g_final):
    B, S, D = x.shape
    depth = w_ada.shape[0]
    W = w_o_sb.shape[1]
    H = W // HEAD_DIM
    qw = N_IDX_HEADS * IDX_DIM
    assert w_o_dsa.shape[1] == W and D % W == 0 and S % ATTN_BLOCK == 0 and W % qw == 0
    n_sel = min(TOPK_MAX, S // 4)
    scale = HEAD_DIM ** -0.5
    x2 = x.reshape(B * S, D)

    for l in range(depth):
        mod = _adaln(c, w_ada[l], b_ada[l])
        sh1, sc1, gt1, sh2, sc2, gt2 = jnp.split(mod, 6, axis=-1)

        wt = w_in[l].T
        o_kix = 6 * W + qw
        o_wix = o_kix + IDX_DIM
        o_gate = o_wix + N_IDX_HEADS
        k_rows = wt[o_kix:o_wix]
        wt_tail = jnp.concatenate(
            [k_rows, k_rows, wt[o_wix:o_gate],
             jnp.zeros((2 * IDX_DIM - N_IDX_HEADS, D), wt.dtype)], axis=0)

        h1 = _norm_mod(x2, g_mix[l], sc1, sh1, S)
        z = _in_proj_wt(h1, wt, (0, 1, 3, 4, 6), W, ((0, scale), (2, scale * LOG2_E)), "in_proj")
        vt = _in_proj_wt(h1, wt, (2, 5), W, (), "in_proj_values", transposed=True)
        zg = _in_proj_wt(h1, wt[o_gate:], tuple(range(2 * D // W)), W, (), "in_proj_gates")
        zs = _in_proj_wt(h1, wt_tail, (0,), 4 * IDX_DIM, (), "in_proj_idx")

        o_sb = _sb_attn(z, vt, B, S, H, 0, H, 0)
        mask = _dsa_select(z, zs, B, S, 4 * W // qw, n_sel)
        o_ds = _dsa_attn(z, vt, mask, _bias_tiles(rel_bias, min(S, ATTN_BLOCK)),
                         B, S, H, 2 * H, 3 * H, H)

        x2, h2 = _mix_out(o_sb, o_ds, zg, 0, x2, w_o_sb[l].astype(BF16),
                          w_o_dsa[l].astype(BF16), w_out[l].astype(BF16), gt1, g_ffn[l], sc2, sh2, S)
        last = l == depth - 1
        assert last, "the final rms_norm is fused into the last layer's FFN"
        x2 = _conv_ffn(h2, x2, w_gate, w_up, l, w_down[l].astype(BF16),
                       conv_w[l], conv_b[l], gt2, g_final, S)
    return x2.reshape(B, S, D)
```

```python
import functools
import math

import jax
import jax.numpy as jnp
from jax import lax
from jax.experimental import pallas as pl
from jax.experimental.pallas import tpu as pltpu

HEAD_DIM = 128
N_IDX_HEADS = 16
IDX_DIM = 64
TOPK_MAX = 256
N_BUCKETS = 32
MAX_DISTANCE = 128
CONV_WIDTH = 3
EPS = 1e-6

F32 = jnp.float32
BF16 = jnp.bfloat16
NEG = -1e30
I16 = jnp.int16
INT_MIN = -2 ** 31
I16_MIN = -2 ** 15
EXP_ZERO_BELOW = 104.0
LOG2_E = math.log2(math.e)
V7X_VMEM_LIMIT = 56 * 1024 * 1024
BF16_ROWS = 16
ATTN_BLOCK = 256
SB_HEAD_GROUP = 4
DSA_HEAD_GROUP = 8
SCORE_BLOCKS = 4
COUNT_BLOCKS = 4
COUNT_CHUNK = 64
NT_DIMS = (((1,), (1,)), ((), ()))


def _params(*sem):
    return pltpu.CompilerParams(dimension_semantics=sem, vmem_limit_bytes=V7X_VMEM_LIMIT)


def _sigmoid(x):
    return 1.0 / (1.0 + jnp.exp(-x))


def _adaln_kernel(ct_ref, w_ref, b_ref, o_ref):
    ct = ct_ref[...]
    act = ct * _sigmoid(ct)
    w = w_ref[...]
    for b in range(ct.shape[1]):
        o_ref[b:b + 1, :] = jnp.sum(act[:, b:b + 1] * w, axis=0, keepdims=True) + b_ref[...]


def _adaln(c, w, bias):
    B, D = c.shape
    N = w.shape[1]
    tn = min(N, 1024)
    return pl.pallas_call(
        _adaln_kernel,
        grid=(N // tn,),
        in_specs=[pl.BlockSpec((D, B), lambda j: (0, 0)),
                  pl.BlockSpec((D, tn), lambda j: (0, j)),
                  pl.BlockSpec((1, tn), lambda j: (0, j))],
        out_specs=pl.BlockSpec((B, tn), lambda j: (0, j)),
        out_shape=jax.ShapeDtypeStruct((B, N), F32),
        compiler_params=_params("arbitrary"),
        name="adaln",
    )(c.T, w, bias.reshape(1, N))


def _rms_mod(x, g, sc, sh):
    ms = jnp.mean(x * x, axis=-1, keepdims=True)
    y = x * lax.rsqrt(ms + EPS) * g
    return y * (1.0 + sc) + sh


def _norm_mod_kernel(x_ref, g_ref, sc_ref, sh_ref, o_ref):
    o_ref[...] = _rms_mod(x_ref[...], g_ref[...], sc_ref[0], sh_ref[0]).astype(o_ref.dtype)


def _norm_mod(x2, g, sc, sh, S):
    M, D = x2.shape
    B = sc.shape[0]
    tm = min(S, 1024)
    per_b = S // tm
    return pl.pallas_call(
        _norm_mod_kernel,
        grid=(M // tm,),
        in_specs=[pl.BlockSpec((tm, D), lambda i: (i, 0)),
                  pl.BlockSpec((1, D), lambda i: (0, 0)),
                  pl.BlockSpec((1, 1, D), lambda i: (i // per_b, 0, 0)),
                  pl.BlockSpec((1, 1, D), lambda i: (i // per_b, 0, 0))],
        out_specs=pl.BlockSpec((tm, D), lambda i: (i, 0)),
        out_shape=jax.ShapeDtypeStruct((M, D), BF16),
        compiler_params=_params("arbitrary"),
        name="norm_mod",
    )(x2, g.reshape(1, D), sc.reshape(B, 1, D), sh.reshape(B, 1, D))


def _in_proj_kernel(a_ref, b_ref, o_ref, *, tile_scales):
    j = pl.program_id(0)
    acc = jnp.dot(a_ref[...], b_ref[...], preferred_element_type=F32)
    scale = jnp.float32(1.0)
    for tile, tile_scale in tile_scales:
        scale = jnp.where(j == tile, tile_scale, scale)
    o_ref[...] = (acc * scale).astype(o_ref.dtype)


def _in_proj_wt_kernel(a_ref, wt_ref, o_ref, wb_ref, *, tile_scales):
    @pl.when(pl.program_id(1) == 0)
    def _():
        wb_ref[...] = wt_ref[...].T.astype(BF16)

    _in_proj_kernel(a_ref, wb_ref, o_ref, tile_scales=tile_scales)


def _in_proj_t_kernel(a_ref, wt_ref, o_ref, wb_ref):
    @pl.when(pl.program_id(1) == 0)
    def _():
        wb_ref[...] = wt_ref[...].astype(BF16)

    o_ref[...] = lax.dot_general(wb_ref[...], a_ref[...], NT_DIMS,
                                 preferred_element_type=F32).astype(o_ref.dtype)


def _tile_index_fn(tiles):
    steps = []
    for j, t in enumerate(tiles):
        while len(steps) < t - j:
            steps.append(j)
    return lambda j: j + sum(jnp.where(j >= first, 1, 0) for first in steps)


def _in_proj_wt(h, wt, tiles, tn, tile_scales, name, transposed=False):
    M, K = h.shape
    tm = min(M, 1024)
    n = len(tiles)
    tile_of = _tile_index_fn(tiles)
    in_specs = [pl.BlockSpec((tm, K), lambda j, i: (i, 0)),
                pl.BlockSpec((tn, K), lambda j, i: (tile_of(j), 0))]
    if transposed:
        assert not tile_scales
        body, wb_shape = _in_proj_t_kernel, (tn, K)
        out_spec = pl.BlockSpec((tn, tm), lambda j, i: (j, i))
        out_shape = jax.ShapeDtypeStruct((n * tn, M), BF16)
    else:
        body, wb_shape = functools.partial(_in_proj_wt_kernel, tile_scales=tile_scales), (K, tn)
        out_spec = pl.BlockSpec((tm, tn), lambda j, i: (i, j))
        out_shape = jax.ShapeDtypeStruct((M, n * tn), BF16)
    return pl.pallas_call(
        body,
        grid=(n, M // tm),
        in_specs=in_specs,
        out_specs=out_spec,
        out_shape=out_shape,
        scratch_shapes=[pltpu.VMEM(wb_shape, BF16)],
        compiler_params=_params("arbitrary", "arbitrary"),
        name=name,
    )(h, wt)


def _key_query_iotas(k0, q0, tk, tq):
    key_pos = k0 + lax.broadcasted_iota(jnp.int32, (tk, tq), 0)
    query_pos = q0 + lax.broadcasted_iota(jnp.int32, (tk, tq), 1)
    return key_pos, query_pos


def _head_cols(g):
    return slice(g * HEAD_DIM, (g + 1) * HEAD_DIM)


def _sb_kernel(q_ref, k_ref, vt_ref, o_ref, run_ref, acc_ref, *, tq, tk, n_heads):
    qi = pl.program_id(2)
    q0 = qi * tq
    run_ref[...] = jnp.zeros_like(run_ref)
    acc_ref[...] = jnp.zeros_like(acc_ref)
    later = (lax.broadcasted_iota(jnp.int32, (tk, tk), 1)
             > lax.broadcasted_iota(jnp.int32, (tk, tk), 0)).astype(BF16)

    def step(kbs, first_is_diagonal):
        heads = range(n_heads)
        k0s = [pl.multiple_of(kb * tk, tk) for kb in kbs]
        causal = None
        if first_is_diagonal:
            key_pos, query_pos = _key_query_iotas(k0s[0], q0, tk, tq)
            causal = key_pos < query_pos
        masked = [first_is_diagonal and j == 0 for j in range(len(kbs))]
        zs = [[lax.dot_general(k_ref[pl.ds(k0, tk), _head_cols(g)], q_ref[:, _head_cols(g)],
                               NT_DIMS, preferred_element_type=F32) for g in heads] for k0 in k0s]
        sps = [[jnp.maximum(z, 0.0) + jnp.log(1.0 + jnp.exp(-jnp.abs(z))) for z in zj] for zj in zs]
        spms = [[jnp.where(causal, sp, 0.0) if masked[j] else sp for sp in spj]
                for j, spj in enumerate(sps)]
        his = [[spm.astype(BF16) for spm in spj] for spj in spms]
        los = [[(spm - hi.astype(F32)).astype(BF16) for spm, hi in zip(spj, hij)]
               for spj, hij in zip(spms, his)]
        suffixes = [[jnp.dot(later, hi, preferred_element_type=F32)
                     + jnp.dot(later, lo, preferred_element_type=F32) for hi, lo in zip(hij, loj)]
                    for hij, loj in zip(his, los)]
        weights, new_runs = [], []
        for g in heads:
            run = run_ref[g]
            head_weights = []
            for j in range(len(kbs)):
                a = jnp.exp(zs[j][g] - sps[j][g] - suffixes[j][g] - run)
                if masked[j]:
                    a = jnp.where(causal, a, 0.0)
                head_weights.append(a.astype(BF16))
                run = run + jnp.sum(spms[j][g], axis=0, keepdims=True)
            weights.append(head_weights)
            new_runs.append(run)
        min_run = None
        for g in heads:
            acc = acc_ref[g]
            for j, k0 in enumerate(k0s):
                acc = acc + jnp.dot(vt_ref[_head_cols(g), pl.ds(k0, tk)], weights[g][j],
                                    preferred_element_type=F32)
            acc_ref[g] = acc
            run_ref[g] = new_runs[g]
            head_min = jnp.min(new_runs[g])
            min_run = head_min if min_run is None else jnp.minimum(min_run, head_min)
        return min_run > EXP_ZERO_BELOW

    def body(carry):
        kb, _ = carry
        return kb - 1, step([kb], False)

    def cond(carry):
        kb, dead = carry
        return jnp.logical_and(kb >= 0, jnp.logical_not(dead))

    @pl.when(qi == 0)
    def _():
        step([qi], True)

    @pl.when(qi >= 1)
    def _():
        dead = step([qi, qi - 1], True)
        lax.while_loop(cond, body, (qi - 2, dead))

    for g in range(n_heads):
        o_ref[:, _head_cols(g)] = acc_ref[g].T.astype(o_ref.dtype)


def _attn_group_specs(S, tq, nq, G, q_blk, k_blk, v_blk, kv_buffers):
    assert q_blk % G == 0 and k_blk % G == 0 and v_blk % G == 0
    gw = G * HEAD_DIM
    mode = pl.Buffered(kv_buffers)
    return [pl.BlockSpec((tq, gw), lambda b, hg, i: (b * nq + i, q_blk // G + hg)),
            pl.BlockSpec((S, gw), lambda b, hg, i: (b, k_blk // G + hg), pipeline_mode=mode),
            pl.BlockSpec((gw, S), lambda b, hg, i: (v_blk // G + hg, b), pipeline_mode=mode)]


def _sb_attn(z, vt, B, S, H, q_blk, k_blk, v_blk):
    M = z.shape[0]
    tq = tk = min(S, ATTN_BLOCK)
    nq = S // tq
    G = min(H, SB_HEAD_GROUP)
    gw = G * HEAD_DIM
    return pl.pallas_call(
        functools.partial(_sb_kernel, tq=tq, tk=tk, n_heads=G),
        grid=(B, H // G, nq),
        in_specs=_attn_group_specs(S, tq, nq, G, q_blk, k_blk, v_blk, kv_buffers=2),
        out_specs=pl.BlockSpec((tq, gw), lambda b, hg, i: (b * nq + i, hg)),
        out_shape=jax.ShapeDtypeStruct((M, H * HEAD_DIM), BF16),
        scratch_shapes=[pltpu.VMEM((G, 1, tq), F32), pltpu.VMEM((G, HEAD_DIM, tq), F32)],
        compiler_params=_params("arbitrary", "arbitrary", "arbitrary"),
        name="sb_attn",
    )(z, z, vt)


def _dsa_select_kernel(q_ref, k_ref, w_ref, mask_ref, hi_ref, lo_ref, *, tq, tk, n_sel, n_kblocks):
    qi = pl.program_id(1)
    q0 = qi * tq
    nkb = (q0 + tq + tk - 1) // tk
    lane = lax.broadcasted_iota(jnp.int32, (tk, 2 * IDX_DIM), 1)
    w_scale = (N_IDX_HEADS ** -0.5) * (IDX_DIM ** -0.5)
    wt = (w_ref[...].astype(F32) * w_scale).T

    def score_block(kb, diagonal):
        k0 = pl.multiple_of(kb * tk, tk)
        kk = k_ref[pl.ds(k0, tk), :]
        k_halves = (jnp.where(lane < IDX_DIM, kk, jnp.zeros_like(kk)),
                    jnp.where(lane >= IDX_DIM, kk, jnp.zeros_like(kk)))
        acc = jnp.zeros((tk, tq), F32)
        for p in range(N_IDX_HEADS // 2):
            q2 = q_ref[:, p * 2 * IDX_DIM:(p + 1) * 2 * IDX_DIM]
            for half in range(2):
                h = 2 * p + half
                ph = lax.dot_general(k_halves[half], q2, NT_DIMS, preferred_element_type=F32)
                acc = acc + jnp.maximum(ph, 0.0) * wt[h:h + 1, :]
        bits = lax.bitcast_convert_type(acc, jnp.int32)
        key = bits ^ ((bits >> 31) & 0x7FFFFFFF)
        if diagonal:
            key_pos, query_pos = _key_query_iotas(k0, q0, tk, tq)
            key = jnp.where(key_pos <= query_pos, key, INT_MIN)
        hi_ref[pl.ds(k0, tk), :] = (key >> 16).astype(I16)
        lo_ref[pl.ds(k0, tk), :] = (key ^ 0x8000).astype(I16)

    def score_group(i, carry):
        for j in range(SCORE_BLOCKS):
            score_block(SCORE_BLOCKS * i + j, False)
        return carry

    n_before = nkb - 1
    lax.fori_loop(0, n_before // SCORE_BLOCKS, score_group, 0)
    first_left = n_before - n_before % SCORE_BLOCKS
    for left in range(1, SCORE_BLOCKS):
        @pl.when(n_before % SCORE_BLOCKS == left)
        def _(left=left):
            for j in range(left):
                score_block(first_left + j, False)

    score_block(nkb - 1, True)

    n_count = (nkb + COUNT_BLOCKS - 1) // COUNT_BLOCKS
    rows = COUNT_BLOCKS * tk

    def pad_block(kb, carry):
        k0 = pl.multiple_of(kb * tk, tk)
        hi_ref[pl.ds(k0, tk), :] = jnp.full((tk, tq), I16_MIN, I16)
        lo_ref[pl.ds(k0, tk), :] = jnp.full((tk, tq), I16_MIN, I16)
        return carry

    lax.fori_loop(nkb, n_count * COUNT_BLOCKS, pad_block, 0)

    def count(flags_fn):
        def blk(i, cnt):
            r0 = pl.multiple_of(i * rows, rows)
            groups = []
            for c in range(rows // COUNT_CHUNK):
                flags = flags_fn(r0 + c * COUNT_CHUNK, COUNT_CHUNK)
                parts = [flags[r * BF16_ROWS:(r + 1) * BF16_ROWS, :]
                         for r in range(COUNT_CHUNK // BF16_ROWS)]
                while len(parts) > 1:
                    parts = [a + b for a, b in zip(parts[0::2], parts[1::2])]
                groups.append(parts[0])
            while len(groups) > 1:
                groups = [a + b for a, b in zip(groups[0::2], groups[1::2])]
            return cnt + groups[0]
        cnt = lax.fori_loop(0, n_count, blk, jnp.zeros((BF16_ROWS, tq), I16))
        return jnp.sum(cnt.astype(jnp.int32), axis=0, keepdims=True)

    flag, no_flag = I16(1), I16(0)

    def count_ge(ref, thr16):
        return count(lambda r, n: jnp.where(ref[pl.ds(r, n), :] >= thr16, flag, no_flag))

    def kth_largest(ref, k):
        def bisect(it, carry):
            thr, n_ge, n_gt = carry
            cand = thr + jnp.left_shift(jnp.int32(1), 15 - it)
            cnt = count_ge(ref, cand.astype(I16))
            ok = cnt >= k
            return jnp.where(ok, cand, thr), jnp.where(ok, cnt, n_ge), jnp.where(ok, n_gt, cnt)
        zeros = jnp.zeros((1, tq), jnp.int32)
        thr, n_ge, n_gt = lax.fori_loop(0, 16, bisect, (zeros + I16_MIN, zeros, zeros))
        return thr.astype(I16), n_ge, n_gt

    thr_hi, _, above_hi = kth_largest(hi_ref, n_sel)
    need_lo = n_sel - above_hi

    def park(kb, carry):
        k0 = pl.multiple_of(kb * tk, tk)
        lo_ref[pl.ds(k0, tk), :] = jnp.where(hi_ref[pl.ds(k0, tk), :] == thr_hi,
                                             lo_ref[pl.ds(k0, tk), :], I16(I16_MIN))
        return carry

    lax.fori_loop(0, nkb, park, 0)
    thr_lo, at_least_lo, above_lo = kth_largest(lo_ref, need_lo)
    need_ties = need_lo - above_lo
    n_ties = jnp.where(thr_lo == I16(I16_MIN), tk * n_kblocks, at_least_lo - above_lo)
    all_ties_fit = jnp.max(n_ties - need_ties) <= 0

    one, zero, neg = BF16(1.0), BF16(0.0), BF16(NEG)

    def store_mask(kb, sel, diagonal):
        k0 = pl.multiple_of(kb * tk, tk)
        if diagonal:
            key_pos, query_pos = _key_query_iotas(k0, q0, tk, tq)
            sel = sel * (key_pos <= query_pos).astype(BF16)
        mask_ref[0, pl.ds(k0, tk), :] = jnp.where(sel > zero, zero, neg)

    def emit_all_ties(kb, diagonal):
        k0 = pl.multiple_of(kb * tk, tk)
        hi = hi_ref[pl.ds(k0, tk), :]
        lo = lo_ref[pl.ds(k0, tk), :]
        store_mask(kb, jnp.where(hi > thr_hi, one,
                                 jnp.where(hi == thr_hi, jnp.where(lo >= thr_lo, one, zero), zero)),
                   diagonal)

    earlier = (lax.broadcasted_iota(jnp.int32, (tk, tk), 1)
               < lax.broadcasted_iota(jnp.int32, (tk, tk), 0)).astype(BF16)
    ones_rows = jnp.ones((BF16_ROWS, tk), BF16)
    need_ties_f = need_ties.astype(F32)

    def emit_ranked_ties(kb, ties_seen, diagonal):
        k0 = pl.multiple_of(kb * tk, tk)
        hi = hi_ref[pl.ds(k0, tk), :]
        lo = lo_ref[pl.ds(k0, tk), :]
        above = jnp.where(hi > thr_hi, one, jnp.where(lo > thr_lo, one, zero))
        eq = jnp.where(hi == thr_hi, jnp.where(lo == thr_lo, one, zero), zero)
        rank = jnp.dot(earlier, eq, preferred_element_type=F32).astype(BF16)
        room = jnp.clip(need_ties_f - ties_seen, -1.0, float(tk)).astype(BF16)
        store_mask(kb, jnp.where(rank < room, jnp.maximum(above, eq), above), diagonal)
        return ties_seen + jnp.dot(ones_rows, eq, preferred_element_type=F32)[0:1]

    def emit_fast():
        def body(kb, carry):
            emit_all_ties(kb, False)
            return carry
        lax.fori_loop(0, nkb - 1, body, 0)
        emit_all_ties(nkb - 1, True)

    def emit_slow():
        ties_seen = lax.fori_loop(0, nkb - 1, lambda kb, seen: emit_ranked_ties(kb, seen, False),
                                  jnp.zeros((1, tq), F32))
        emit_ranked_ties(nkb - 1, ties_seen, True)

    lax.cond(all_ties_fit, emit_fast, emit_slow)

    def fill(kb, carry):
        k0 = pl.multiple_of(kb * tk, tk)
        mask_ref[0, pl.ds(k0, tk), :] = jnp.full((tk, tq), NEG, mask_ref.dtype)
        return carry

    lax.fori_loop(nkb, n_kblocks, fill, 0)


def _dsa_select(z, zs, B, S, qix_blk, n_sel):
    tq = tk = min(S, ATTN_BLOCK)
    nq = S // tq
    assert nq % COUNT_BLOCKS == 0
    qw = N_IDX_HEADS * IDX_DIM
    return pl.pallas_call(
        functools.partial(_dsa_select_kernel, tq=tq, tk=tk, n_sel=n_sel, n_kblocks=S // tk),
        grid=(B, nq),
        in_specs=[pl.BlockSpec((tq, qw), lambda b, i: (b * nq + i, qix_blk)),
                  pl.BlockSpec((S, 2 * IDX_DIM), lambda b, i: (b, 0)),
                  pl.BlockSpec((tq, 2 * IDX_DIM), lambda b, i: (b * nq + i, 1))],
        out_specs=pl.BlockSpec((1, S, tq), lambda b, i: (b * nq + i, 0, 0)),
        out_shape=jax.ShapeDtypeStruct((B * nq, S, tq), BF16),
        scratch_shapes=[pltpu.VMEM((S, tq), I16), pltpu.VMEM((S, tq), I16)],
        compiler_params=_params("arbitrary", "arbitrary"),
        name="dsa_select",
    )(z, zs, zs)


def _rel_bucket(dist):
    n = jnp.maximum(dist, 0)
    max_exact = N_BUCKETS // 2
    nf = jnp.maximum(n, 1).astype(F32)
    large = max_exact + (jnp.log(nf / max_exact) / math.log(MAX_DISTANCE / max_exact)
                         * (N_BUCKETS - max_exact)).astype(jnp.int32)
    large = jnp.minimum(large, N_BUCKETS - 1)
    return jnp.where(n < max_exact, n, large)


def _bias_tiles(rel_bias, tq):
    assert tq >= MAX_DISTANCE
    rb = rel_bias.astype(F32)
    key = jnp.arange(tq)[:, None]
    query = jnp.arange(tq)[None, :]
    bucket = _rel_bucket(jnp.stack([query - key, tq + query - key]))
    onehot = (bucket[None] == jnp.arange(N_BUCKETS)[:, None, None, None]).astype(F32)
    return jnp.einsum("nh,nikq->hikq", (rb - rb[N_BUCKETS - 1]) * LOG2_E, onehot,
                      precision=lax.Precision.HIGHEST)


def _dsa_attn_kernel(q_ref, k_ref, vt_ref, mask_ref, bias_ref, o_ref, m_ref, acc_ref, logit_ref, *,
                     tq, n_heads):
    qi = pl.program_id(2)
    m_ref[...] = jnp.full_like(m_ref, NEG)
    acc_ref[...] = jnp.zeros_like(acc_ref)
    ones_rows = jnp.ones((BF16_ROWS, tq), BF16)

    def step(blocks):
        k0s = [pl.multiple_of(kb * tq, tq) for kb, _ in blocks]
        masks = [mask_ref[0, pl.ds(k0, tq), :].astype(F32) for k0 in k0s]

        def head_scores(g):
            return [lax.dot_general(k_ref[pl.ds(k0, tq), _head_cols(g)], q_ref[:, _head_cols(g)],
                                    NT_DIMS, preferred_element_type=F32) for k0 in k0s]

        def head_logits(g, scores):
            m_new = m_ref[g]
            for j, (_, bias_idx) in enumerate(blocks):
                s = scores[j] + masks[j]
                if bias_idx is not None:
                    s = s + bias_ref[g, bias_idx]
                logit_ref[g, j] = s
                m_new = jnp.maximum(m_new, jnp.max(s, axis=0, keepdims=True))
            return m_new

        def head_accumulate(g, m_new):
            acc = jnp.exp2(m_ref[g] - m_new) * acc_ref[g]
            for j, k0 in enumerate(k0s):
                p = jnp.exp2(logit_ref[g, j] - m_new).astype(BF16)
                vt = jnp.concatenate([vt_ref[_head_cols(g), pl.ds(k0, tq)], ones_rows], axis=0)
                acc = acc + jnp.dot(vt, p, preferred_element_type=F32)
            acc_ref[g] = acc
            m_ref[g] = m_new

        scores = [head_scores(g) for g in range(n_heads)]
        maxima = [head_logits(g, scores[g]) for g in range(n_heads)]
        for g in range(n_heads):
            head_accumulate(g, maxima[g])

    n_far = jnp.maximum(qi - 1, 0)

    def far_pair(i, carry):
        step([(2 * i, None), (2 * i + 1, None)])
        return carry

    lax.fori_loop(0, n_far // 2, far_pair, 0)

    @pl.when(n_far % 2 == 1)
    def _():
        step([(n_far - 1, None)])

    @pl.when(qi >= 1)
    def _():
        step([(qi - 1, 1), (qi, 0)])

    @pl.when(qi == 0)
    def _():
        step([(qi, 0)])
    for g in range(n_heads):
        acc = acc_ref[g]
        out_t = acc[:HEAD_DIM] / acc[HEAD_DIM:HEAD_DIM + 1]
        o_ref[:, _head_cols(g)] = out_t.T.astype(o_ref.dtype)


def _dsa_attn(z, vt, mask, bias_tiles, B, S, H, q_blk, k_blk, v_blk):
    M = z.shape[0]
    tq = bias_tiles.shape[-1]
    nq = S // tq
    G = min(H, DSA_HEAD_GROUP)
    gw = G * HEAD_DIM
    return pl.pallas_call(
        functools.partial(_dsa_attn_kernel, tq=tq, n_heads=G),
        grid=(B, H // G, nq),
        in_specs=_attn_group_specs(S, tq, nq, G, q_blk, k_blk, v_blk, kv_buffers=1) + [
            pl.BlockSpec((1, S, tq), lambda b, hg, i: (b * nq + i, 0, 0)),
            pl.BlockSpec((G, 2, tq, tq), lambda b, hg, i: (hg, 0, 0, 0),
                         pipeline_mode=pl.Buffered(1))],
        out_specs=pl.BlockSpec((tq, gw), lambda b, hg, i: (b * nq + i, hg)),
        out_shape=jax.ShapeDtypeStruct((M, H * HEAD_DIM), BF16),
        scratch_shapes=[pltpu.VMEM((G, 1, tq), F32),
                        pltpu.VMEM((G, HEAD_DIM + BF16_ROWS, tq), F32),
                        pltpu.VMEM((G, 2, tq, tq), F32)],
        compiler_params=_params("arbitrary", "arbitrary", "arbitrary"),
        name="dsa_attn",
    )(z, z, vt, mask, bias_tiles)


def _mix_out_kernel(osb_ref, ods_ref, gsb_ref, gds_ref, x_ref, wsb_ref, wds_ref, wout_ref,
                    gt_ref, g_ref, sc_ref, sh_ref, x1_ref, h2_ref):
    t_sb = jnp.dot(osb_ref[...], wsb_ref[...], preferred_element_type=F32)
    t_ds = jnp.dot(ods_ref[...], wds_ref[...], preferred_element_type=F32)
    merged = (_sigmoid(gsb_ref[...].astype(F32)) * t_sb
              + _sigmoid(gds_ref[...].astype(F32)) * t_ds)
    y = jnp.dot(merged.astype(BF16), wout_ref[...], preferred_element_type=F32)
    x1 = x_ref[...] + gt_ref[0] * y
    x1_ref[...] = x1
    h2_ref[...] = _rms_mod(x1, g_ref[...], sc_ref[0], sh_ref[0]).astype(h2_ref.dtype)


def _mix_out(o_sb, o_ds, z, gate_blk, x2, w_sb, w_ds, w_out, gt, g, sc, sh, S):
    M, D = x2.shape
    B = gt.shape[0]
    W = o_sb.shape[1]
    tm = min(S, 512)
    per_b = S // tm
    row = lambda i: (i, 0)
    const = lambda i: (0, 0)
    per_batch = lambda i: (i // per_b, 0, 0)
    return pl.pallas_call(
        _mix_out_kernel,
        grid=(M // tm,),
        in_specs=[pl.BlockSpec((tm, W), row),
                  pl.BlockSpec((tm, W), row),
                  pl.BlockSpec((tm, D), lambda i: (i, gate_blk)),
                  pl.BlockSpec((tm, D), lambda i: (i, gate_blk + 1)),
                  pl.BlockSpec((tm, D), row),
                  pl.BlockSpec((W, D), const),
                  pl.BlockSpec((W, D), const),
                  pl.BlockSpec((D, D), const),
                  pl.BlockSpec((1, 1, D), per_batch),
                  pl.BlockSpec((1, D), const),
                  pl.BlockSpec((1, 1, D), per_batch),
                  pl.BlockSpec((1, 1, D), per_batch)],
        out_specs=[pl.BlockSpec((tm, D), row), pl.BlockSpec((tm, D), row)],
        out_shape=[jax.ShapeDtypeStruct((M, D), F32), jax.ShapeDtypeStruct((M, D), BF16)],
        compiler_params=_params("arbitrary"),
        name="mix_out",
    )(o_sb, o_ds, z, z, x2, w_sb, w_ds, w_out, gt.reshape(B, 1, D), g.reshape(1, D),
      sc.reshape(B, 1, D), sh.reshape(B, 1, D))


HALO = BF16_ROWS


def _ffn_act_kernel(h_ref, halo_ref, wg_ref, wu_ref, cw_ref, cb_ref, o_ref, wgb_ref, wub_ref, *,
                    tiles_per_seq):
    i = pl.program_id(1)

    @pl.when(i == 0)
    def _():
        wgb_ref[...] = wg_ref[...].astype(BF16)
        wub_ref[...] = wu_ref[...].astype(BF16)

    h = h_ref[...]
    wg = wgb_ref[...]
    g0 = jnp.dot(h, wg, preferred_element_type=F32)
    g_prev = jnp.dot(halo_ref[...], wg, preferred_element_type=F32)
    g_prev = jnp.where(i % tiles_per_seq == 0, 0.0, g_prev)
    ridx = lax.broadcasted_iota(jnp.int32, g0.shape, 0)
    g1 = jnp.where(ridx == 0, g_prev[HALO - 1:HALO, :], pltpu.roll(g0, 1, 0))
    g2 = jnp.where(ridx == 0, g_prev[HALO - 2:HALO - 1, :],
                   jnp.where(ridx == 1, g_prev[HALO - 1:HALO, :], pltpu.roll(g0, 2, 0)))
    cw = cw_ref[...]
    a = cb_ref[...] + g2 * cw[0:1, :] + g1 * cw[1:2, :] + g0 * cw[2:3, :]
    u = jnp.dot(h, wub_ref[...], preferred_element_type=F32)
    o_ref[...] = (a * _sigmoid(a) * u).astype(o_ref.dtype)


def _ffn_out_kernel(a_ref, wd_ref, x1_ref, gt_ref, gf_ref, o_ref):
    k = pl.program_id(1)

    @pl.when(k == 0)
    def _():
        o_ref[...] = jnp.zeros_like(o_ref)

    o_ref[...] += jnp.dot(a_ref[...], wd_ref[...], preferred_element_type=F32)

    @pl.when(k == pl.num_programs(1) - 1)
    def _():
        x = x1_ref[...] + gt_ref[0] * o_ref[...]
        ms = jnp.mean(x * x, axis=-1, keepdims=True)
        o_ref[...] = x * lax.rsqrt(ms + EPS) * gf_ref[...]


def _conv_ffn(h2, x1, w_gate, w_up, layer, w_down, conv_w, conv_b, gt, g_final, S):
    M, D = x1.shape
    B = gt.shape[0]
    F = w_gate.shape[2]
    tm = min(S, 1024)
    tf = min(F, 512)
    per_b = S // tm
    halo_per_tile = tm // HALO
    act = pl.pallas_call(
        functools.partial(_ffn_act_kernel, tiles_per_seq=per_b),
        grid=(F // tf, M // tm),
        in_specs=[pl.BlockSpec((tm, D), lambda f, i: (i, 0)),
                  pl.BlockSpec((HALO, D), lambda f, i: (jnp.maximum(i * halo_per_tile - 1, 0), 0)),
                  pl.BlockSpec((None, D, tf), lambda f, i: (layer, 0, f)),
                  pl.BlockSpec((None, D, tf), lambda f, i: (layer, 0, f)),
                  pl.BlockSpec((CONV_WIDTH, tf), lambda f, i: (0, f)),
                  pl.BlockSpec((1, tf), lambda f, i: (0, f))],
        out_specs=pl.BlockSpec((tm, tf), lambda f, i: (i, f)),
        out_shape=jax.ShapeDtypeStruct((M, F), BF16),
        scratch_shapes=[pltpu.VMEM((D, tf), BF16), pltpu.VMEM((D, tf), BF16)],
        compiler_params=_params("arbitrary", "arbitrary"),
        name="ffn_act",
    )(h2, h2, w_gate, w_up, conv_w, conv_b.reshape(1, F))
    return pl.pallas_call(
        _ffn_out_kernel,
        grid=(M // tm, F // tf),
        in_specs=[pl.BlockSpec((tm, tf), lambda i, k: (i, k)),
                  pl.BlockSpec((tf, D), lambda i, k: (k, 0)),
                  pl.BlockSpec((tm, D), lambda i, k: (i, 0)),
                  pl.BlockSpec((1, 1, D), lambda i, k: (i // per_b, 0, 0)),
                  pl.BlockSpec((1, D), lambda i, k: (0, 0))],
        out_specs=pl.BlockSpec((tm, D), lambda i, k: (i, 0)),
        out_shape=jax.ShapeDtypeStruct((M, D), F32),
        compiler_params=_params("arbitrary", "arbitrary"),
        name="ffn_out",
    )(act, w_down, x1, gt.reshape(B, 1, D), g_final.reshape(1, D))


def kernel(x, c, w_ada, b_ada, g_mix, w_in, w_o_sb, w_o_dsa, w_out, rel_bias, g_ffn, w_gate,
           w_up, conv_w, conv_b, w_down, g_final):
    B, S, D = x.shape
    depth = w_ada.shape[0]
    W = w_o_sb.shape[1]
    H = W // HEAD_DIM
    qw = N_IDX_HEADS * IDX_DIM
    assert w_o_dsa.shape[1] == W and D % W == 0 and S % ATTN_BLOCK == 0 and W % qw == 0
    n_sel = min(TOPK_MAX, S // 4)
    scale = HEAD_DIM ** -0.5
    x2 = x.reshape(B * S, D)

    for l in range(depth):
        mod = _adaln(c, w_ada[l], b_ada[l])
        sh1, sc1, gt1, sh2, sc2, gt2 = jnp.split(mod, 6, axis=-1)

        wt = w_in[l].T
        o_kix = 6 * W + qw
        o_wix = o_kix + IDX_DIM
        o_gate = o_wix + N_IDX_HEADS
        k_rows = wt[o_kix:o_wix]
        wt_tail = jnp.concatenate(
            [k_rows, k_rows, wt[o_wix:o_gate],
             jnp.zeros((2 * IDX_DIM - N_IDX_HEADS, D), wt.dtype)], axis=0)

        h1 = _norm_mod(x2, g_mix[l], sc1, sh1, S)
        z = _in_proj_wt(h1, wt, (0, 1, 3, 4, 6), W, ((0, scale), (2, scale * LOG2_E)), "in_proj")
        vt = _in_proj_wt(h1, wt, (2, 5), W, (), "in_proj_values", transposed=True)
        zg = _in_proj_wt(h1, wt[o_gate:], tuple(range(2 * D // W)), W, (), "in_proj_gates")
        zs = _in_proj_wt(h1, wt_tail, (0,), 4 * IDX_DIM, (), "in_proj_idx")

        o_sb = _sb_attn(z, vt, B, S, H, 0, H, 0)
        mask = _dsa_select(z, zs, B, S, 4 * W // qw, n_sel)
        o_ds = _dsa_attn(z, vt, mask, _bias_tiles(rel_bias, min(S, ATTN_BLOCK)),
                         B, S, H, 2 * H, 3 * H, H)

        x2, h2 = _mix_out(o_sb, o_ds, zg, 0, x2, w_o_sb[l].astype(BF16),
                          w_o_dsa[l].astype(BF16), w_out[l].astype(BF16), gt1, g_ffn[l], sc2, sh2, S)
        last = l == depth - 1
        assert last, "the final rms_norm is fused into the last layer's FFN"
        x2 = _conv_ffn(h2, x2, w_gate, w_up, l, w_down[l].astype(BF16),
                       conv_w[l], conv_b[l], gt2, g_final, S)
    return x2.reshape(B, S, D)
```

```python
import functools
import math

import jax
import jax.numpy as jnp
from jax import lax
from jax.experimental import pallas as pl
from jax.experimental.pallas import tpu as pltpu

HEAD_DIM = 128
N_IDX_HEADS = 16
IDX_DIM = 64
TOPK_MAX = 256
N_BUCKETS = 32
MAX_DISTANCE = 128
CONV_WIDTH = 3
EPS = 1e-6

F32 = jnp.float32
BF16 = jnp.bfloat16
NEG = -1e30
I16 = jnp.int16
INT_MIN = -2 ** 31
I16_MIN = -2 ** 15
EXP_ZERO_BELOW = 104.0
LOG2_E = math.log2(math.e)
V7X_VMEM_LIMIT = 56 * 1024 * 1024
BF16_ROWS = 16
ATTN_BLOCK = 256
SB_HEAD_GROUP = 4
DSA_HEAD_GROUP = 8
SCORE_BLOCKS = 4
COUNT_BLOCKS = 4
COUNT_CHUNK = 64
NT_DIMS = (((1,), (1,)), ((), ()))


def _params(*sem):
    return pltpu.CompilerParams(dimension_semantics=sem, vmem_limit_bytes=V7X_VMEM_LIMIT)


def _sigmoid(x):
    return 1.0 / (1.0 + jnp.exp(-x))


def _adaln_kernel(ct_ref, w_ref, b_ref, o_ref):
    ct = ct_ref[...]
    act = ct * _sigmoid(ct)
    w = w_ref[...]
    for b in range(ct.shape[1]):
        o_ref[b:b + 1, :] = jnp.sum(act[:, b:b + 1] * w, axis=0, keepdims=True) + b_ref[...]


def _adaln(c, w, bias):
    B, D = c.shape
    N = w.shape[1]
    tn = min(N, 1024)
    return pl.pallas_call(
        _adaln_kernel,
        grid=(N // tn,),
        in_specs=[pl.BlockSpec((D, B), lambda j: (0, 0)),
                  pl.BlockSpec((D, tn), lambda j: (0, j)),
                  pl.BlockSpec((1, tn), lambda j: (0, j))],
        out_specs=pl.BlockSpec((B, tn), lambda j: (0, j)),
        out_shape=jax.ShapeDtypeStruct((B, N), F32),
        compiler_params=_params("arbitrary"),
        name="adaln",
    )(c.T, w, bias.reshape(1, N))


def _rms_mod(x, g, sc, sh):
    ms = jnp.mean(x * x, axis=-1, keepdims=True)
    y = x * lax.rsqrt(ms + EPS) * g
    return y * (1.0 + sc) + sh


def _norm_mod_kernel(x_ref, g_ref, sc_ref, sh_ref, o_ref):
    o_ref[...] = _rms_mod(x_ref[...], g_ref[...], sc_ref[0], sh_ref[0]).astype(o_ref.dtype)


def _norm_mod(x2, g, sc, sh, S):
    M, D = x2.shape
    B = sc.shape[0]
    tm = min(S, 1024)
    per_b = S // tm
    return pl.pallas_call(
        _norm_mod_kernel,
        grid=(M // tm,),
        in_specs=[pl.BlockSpec((tm, D), lambda i: (i, 0)),
                  pl.BlockSpec((1, D), lambda i: (0, 0)),
                  pl.BlockSpec((1, 1, D), lambda i: (i // per_b, 0, 0)),
                  pl.BlockSpec((1, 1, D), lambda i: (i // per_b, 0, 0))],
        out_specs=pl.BlockSpec((tm, D), lambda i: (i, 0)),
        out_shape=jax.ShapeDtypeStruct((M, D), BF16),
        compiler_params=_params("arbitrary"),
        name="norm_mod",
    )(x2, g.reshape(1, D), sc.reshape(B, 1, D), sh.reshape(B, 1, D))


def _in_proj_kernel(a_ref, b_ref, o_ref, *, tile_scales):
    acc = jnp.dot(a_ref[...], b_ref[...], preferred_element_type=F32)
    o_ref[...] = (acc * _tile_scale(tile_scales)).astype(o_ref.dtype)


def _in_proj_wt_kernel(a_ref, wt_ref, o_ref, wb_ref, *, tile_scales):
    @pl.when(pl.program_id(1) == 0)
    def _():
        wb_ref[...] = wt_ref[...].T.astype(BF16)

    _in_proj_kernel(a_ref, wb_ref, o_ref, tile_scales=tile_scales)


def _tile_scale(tile_scales):
    j = pl.program_id(0)
    scale = jnp.float32(1.0)
    for tile, tile_scale in tile_scales:
        scale = jnp.where(j == tile, tile_scale, scale)
    return scale


def _in_proj_t_kernel(a_ref, wt_ref, o_ref, wb_ref, *, tile_scales):
    @pl.when(pl.program_id(1) == 0)
    def _():
        wb_ref[...] = wt_ref[...].astype(BF16)

    acc = lax.dot_general(wb_ref[...], a_ref[...], NT_DIMS, preferred_element_type=F32)
    o_ref[...] = (acc * _tile_scale(tile_scales)).astype(o_ref.dtype)


def _tile_index_fn(tiles):
    steps = []
    for j, t in enumerate(tiles):
        while len(steps) < t - j:
            steps.append(j)
    return lambda j: j + sum(jnp.where(j >= first, 1, 0) for first in steps)


def _in_proj_wt(h, wt, tiles, tn, tile_scales, name, transposed=False):
    M, K = h.shape
    tm = min(M, 1024)
    n = len(tiles)
    tile_of = _tile_index_fn(tiles)
    in_specs = [pl.BlockSpec((tm, K), lambda j, i: (i, 0)),
                pl.BlockSpec((tn, K), lambda j, i: (tile_of(j), 0))]
    if transposed:
        body, wb_shape = functools.partial(_in_proj_t_kernel, tile_scales=tile_scales), (tn, K)
        out_spec = pl.BlockSpec((tn, tm), lambda j, i: (j, i))
        out_shape = jax.ShapeDtypeStruct((n * tn, M), BF16)
    else:
        body, wb_shape = functools.partial(_in_proj_wt_kernel, tile_scales=tile_scales), (K, tn)
        out_spec = pl.BlockSpec((tm, tn), lambda j, i: (i, j))
        out_shape = jax.ShapeDtypeStruct((M, n * tn), BF16)
    return pl.pallas_call(
        body,
        grid=(n, M // tm),
        in_specs=in_specs,
        out_specs=out_spec,
        out_shape=out_shape,
        scratch_shapes=[pltpu.VMEM(wb_shape, BF16)],
        compiler_params=_params("arbitrary", "arbitrary"),
        name=name,
    )(h, wt)


def _key_query_iotas(k0, q0, tk, tq):
    key_pos = k0 + lax.broadcasted_iota(jnp.int32, (tk, tq), 0)
    query_pos = q0 + lax.broadcasted_iota(jnp.int32, (tk, tq), 1)
    return key_pos, query_pos


def _head_cols(g):
    return slice(g * HEAD_DIM, (g + 1) * HEAD_DIM)


def _sb_kernel(q_ref, k_ref, vt_ref, o_ref, run_ref, acc_ref, *, tq, tk, n_heads):
    qi = pl.program_id(2)
    q0 = qi * tq
    run_ref[...] = jnp.zeros_like(run_ref)
    acc_ref[...] = jnp.zeros_like(acc_ref)
    later = (lax.broadcasted_iota(jnp.int32, (tk, tk), 1)
             > lax.broadcasted_iota(jnp.int32, (tk, tk), 0)).astype(BF16)

    def step(kbs, first_is_diagonal):
        heads = range(n_heads)
        k0s = [pl.multiple_of(kb * tk, tk) for kb in kbs]
        causal = None
        if first_is_diagonal:
            key_pos, query_pos = _key_query_iotas(k0s[0], q0, tk, tq)
            causal = key_pos < query_pos
        masked = [first_is_diagonal and j == 0 for j in range(len(kbs))]
        zs = [[jnp.dot(k_ref[pl.ds(k0, tk), _head_cols(g)], q_ref[_head_cols(g), :],
                       preferred_element_type=F32) for g in heads] for k0 in k0s]
        sps = [[jnp.maximum(z, 0.0) + jnp.log(1.0 + jnp.exp(-jnp.abs(z))) for z in zj] for zj in zs]
        spms = [[jnp.where(causal, sp, 0.0) if masked[j] else sp for sp in spj]
                for j, spj in enumerate(sps)]
        his = [[spm.astype(BF16) for spm in spj] for spj in spms]
        los = [[(spm - hi.astype(F32)).astype(BF16) for spm, hi in zip(spj, hij)]
               for spj, hij in zip(spms, his)]
        suffixes = [[jnp.dot(later, hi, preferred_element_type=F32)
                     + jnp.dot(later, lo, preferred_element_type=F32) for hi, lo in zip(hij, loj)]
                    for hij, loj in zip(his, los)]
        weights, new_runs = [], []
        for g in heads:
            run = run_ref[g]
            head_weights = []
            for j in range(len(kbs)):
                a = jnp.exp(zs[j][g] - sps[j][g] - suffixes[j][g] - run)
                if masked[j]:
                    a = jnp.where(causal, a, 0.0)
                head_weights.append(a.astype(BF16))
                run = run + jnp.sum(spms[j][g], axis=0, keepdims=True)
            weights.append(head_weights)
            new_runs.append(run)
        min_run = None
        for g in heads:
            acc = acc_ref[g]
            for j, k0 in enumerate(k0s):
                acc = acc + jnp.dot(vt_ref[_head_cols(g), pl.ds(k0, tk)], weights[g][j],
                                    preferred_element_type=F32)
            acc_ref[g] = acc
            run_ref[g] = new_runs[g]
            head_min = jnp.min(new_runs[g])
            min_run = head_min if min_run is None else jnp.minimum(min_run, head_min)
        return min_run > EXP_ZERO_BELOW

    def body(carry):
        kb, _ = carry
        return kb - 1, step([kb], False)

    def cond(carry):
        kb, dead = carry
        return jnp.logical_and(kb >= 0, jnp.logical_not(dead))

    @pl.when(qi == 0)
    def _():
        step([qi], True)

    @pl.when(qi >= 1)
    def _():
        dead = step([qi, qi - 1], True)
        lax.while_loop(cond, body, (qi - 2, dead))

    for g in range(n_heads):
        o_ref[:, _head_cols(g)] = acc_ref[g].T.astype(o_ref.dtype)


def _attn_group_specs(S, tq, nq, G, q_blk, k_blk, v_blk, kv_buffers):
    assert q_blk % G == 0 and k_blk % G == 0 and v_blk % G == 0
    gw = G * HEAD_DIM
    mode = pl.Buffered(kv_buffers)
    return [pl.BlockSpec((gw, tq), lambda b, hg, i: (q_blk // G + hg, b * nq + i)),
            pl.BlockSpec((S, gw), lambda b, hg, i: (b, k_blk // G + hg), pipeline_mode=mode),
            pl.BlockSpec((gw, S), lambda b, hg, i: (v_blk // G + hg, b), pipeline_mode=mode)]


def _sb_attn(z, zt, B, S, H, q_blk, k_blk, v_blk):
    M = z.shape[0]
    tq = tk = min(S, ATTN_BLOCK)
    nq = S // tq
    G = min(H, SB_HEAD_GROUP)
    gw = G * HEAD_DIM
    return pl.pallas_call(
        functools.partial(_sb_kernel, tq=tq, tk=tk, n_heads=G),
        grid=(B, H // G, nq),
        in_specs=_attn_group_specs(S, tq, nq, G, q_blk, k_blk, v_blk, kv_buffers=2),
        out_specs=pl.BlockSpec((tq, gw), lambda b, hg, i: (b * nq + i, hg)),
        out_shape=jax.ShapeDtypeStruct((M, H * HEAD_DIM), BF16),
        scratch_shapes=[pltpu.VMEM((G, 1, tq), F32), pltpu.VMEM((G, HEAD_DIM, tq), F32)],
        compiler_params=_params("arbitrary", "arbitrary", "arbitrary"),
        name="sb_attn",
    )(zt, z, zt)


def _dsa_select_kernel(q_ref, k_ref, w_ref, mask_ref, hi_ref, lo_ref, *, tq, tk, n_sel, n_kblocks):
    qi = pl.program_id(1)
    q0 = qi * tq
    nkb = (q0 + tq + tk - 1) // tk
    lane = lax.broadcasted_iota(jnp.int32, (tk, 2 * IDX_DIM), 1)
    w_scale = (N_IDX_HEADS ** -0.5) * (IDX_DIM ** -0.5)
    wt = (w_ref[...].astype(F32) * w_scale).T

    def score_block(kb, diagonal):
        k0 = pl.multiple_of(kb * tk, tk)
        kk = k_ref[pl.ds(k0, tk), :]
        k_halves = (jnp.where(lane < IDX_DIM, kk, jnp.zeros_like(kk)),
                    jnp.where(lane >= IDX_DIM, kk, jnp.zeros_like(kk)))
        acc = jnp.zeros((tk, tq), F32)
        for p in range(N_IDX_HEADS // 2):
            q2 = q_ref[:, p * 2 * IDX_DIM:(p + 1) * 2 * IDX_DIM]
            for half in range(2):
                h = 2 * p + half
                ph = lax.dot_general(k_halves[half], q2, NT_DIMS, preferred_element_type=F32)
                acc = acc + jnp.maximum(ph, 0.0) * wt[h:h + 1, :]
        bits = lax.bitcast_convert_type(acc, jnp.int32)
        key = bits ^ ((bits >> 31) & 0x7FFFFFFF)
        if diagonal:
            key_pos, query_pos = _key_query_iotas(k0, q0, tk, tq)
            key = jnp.where(key_pos <= query_pos, key, INT_MIN)
        hi_ref[pl.ds(k0, tk), :] = (key >> 16).astype(I16)
        lo_ref[pl.ds(k0, tk), :] = (key ^ 0x8000).astype(I16)

    def score_group(i, carry):
        for j in range(SCORE_BLOCKS):
            score_block(SCORE_BLOCKS * i + j, False)
        return carry

    n_before = nkb - 1
    lax.fori_loop(0, n_before // SCORE_BLOCKS, score_group, 0)
    first_left = n_before - n_before % SCORE_BLOCKS
    for left in range(1, SCORE_BLOCKS):
        @pl.when(n_before % SCORE_BLOCKS == left)
        def _(left=left):
            for j in range(left):
                score_block(first_left + j, False)

    score_block(nkb - 1, True)

    n_count = (nkb + COUNT_BLOCKS - 1) // COUNT_BLOCKS
    rows = COUNT_BLOCKS * tk

    def pad_block(kb, carry):
        k0 = pl.multiple_of(kb * tk, tk)
        hi_ref[pl.ds(k0, tk), :] = jnp.full((tk, tq), I16_MIN, I16)
        lo_ref[pl.ds(k0, tk), :] = jnp.full((tk, tq), I16_MIN, I16)
        return carry

    lax.fori_loop(nkb, n_count * COUNT_BLOCKS, pad_block, 0)

    def count(flags_fn):
        def blk(i, cnt):
            r0 = pl.multiple_of(i * rows, rows)
            groups = []
            for c in range(rows // COUNT_CHUNK):
                flags = flags_fn(r0 + c * COUNT_CHUNK, COUNT_CHUNK)
                parts = [flags[r * BF16_ROWS:(r + 1) * BF16_ROWS, :]
                         for r in range(COUNT_CHUNK // BF16_ROWS)]
                while len(parts) > 1:
                    parts = [a + b for a, b in zip(parts[0::2], parts[1::2])]
                groups.append(parts[0])
            while len(groups) > 1:
                groups = [a + b for a, b in zip(groups[0::2], groups[1::2])]
            return cnt + groups[0]
        cnt = lax.fori_loop(0, n_count, blk, jnp.zeros((BF16_ROWS, tq), I16))
        return jnp.sum(cnt.astype(jnp.int32), axis=0, keepdims=True)

    flag, no_flag = I16(1), I16(0)

    def count_ge(ref, thr16):
        return count(lambda r, n: jnp.where(ref[pl.ds(r, n), :] >= thr16, flag, no_flag))

    def kth_largest(ref, k):
        def bisect(it, carry):
            thr, n_ge, n_gt = carry
            cand = thr + jnp.left_shift(jnp.int32(1), 15 - it)
            cnt = count_ge(ref, cand.astype(I16))
            ok = cnt >= k
            return jnp.where(ok, cand, thr), jnp.where(ok, cnt, n_ge), jnp.where(ok, n_gt, cnt)
        zeros = jnp.zeros((1, tq), jnp.int32)
        thr, n_ge, n_gt = lax.fori_loop(0, 16, bisect, (zeros + I16_MIN, zeros, zeros))
        return thr.astype(I16), n_ge, n_gt

    thr_hi, _, above_hi = kth_largest(hi_ref, n_sel)
    need_lo = n_sel - above_hi

    def park(kb, carry):
        k0 = pl.multiple_of(kb * tk, tk)
        lo_ref[pl.ds(k0, tk), :] = jnp.where(hi_ref[pl.ds(k0, tk), :] == thr_hi,
                                             lo_ref[pl.ds(k0, tk), :], I16(I16_MIN))
        return carry

    lax.fori_loop(0, nkb, park, 0)
    thr_lo, at_least_lo, above_lo = kth_largest(lo_ref, need_lo)
    need_ties = need_lo - above_lo
    n_ties = jnp.where(thr_lo == I16(I16_MIN), tk * n_kblocks, at_least_lo - above_lo)
    all_ties_fit = jnp.max(n_ties - need_ties) <= 0

    one, zero, neg = BF16(1.0), BF16(0.0), BF16(NEG)

    def store_mask(kb, sel, diagonal):
        k0 = pl.multiple_of(kb * tk, tk)
        if diagonal:
            key_pos, query_pos = _key_query_iotas(k0, q0, tk, tq)
            sel = sel * (key_pos <= query_pos).astype(BF16)
        mask_ref[0, pl.ds(k0, tk), :] = jnp.where(sel > zero, zero, neg)

    def emit_all_ties(kb, diagonal):
        k0 = pl.multiple_of(kb * tk, tk)
        hi = hi_ref[pl.ds(k0, tk), :]
        lo = lo_ref[pl.ds(k0, tk), :]
        store_mask(kb, jnp.where(hi > thr_hi, one,
                                 jnp.where(hi == thr_hi, jnp.where(lo >= thr_lo, one, zero), zero)),
                   diagonal)

    earlier = (lax.broadcasted_iota(jnp.int32, (tk, tk), 1)
               < lax.broadcasted_iota(jnp.int32, (tk, tk), 0)).astype(BF16)
    ones_rows = jnp.ones((BF16_ROWS, tk), BF16)
    need_ties_f = need_ties.astype(F32)

    def emit_ranked_ties(kb, ties_seen, diagonal):
        k0 = pl.multiple_of(kb * tk, tk)
        hi = hi_ref[pl.ds(k0, tk), :]
        lo = lo_ref[pl.ds(k0, tk), :]
        above = jnp.where(hi > thr_hi, one, jnp.where(lo > thr_lo, one, zero))
        eq = jnp.where(hi == thr_hi, jnp.where(lo == thr_lo, one, zero), zero)
        rank = jnp.dot(earlier, eq, preferred_element_type=F32).astype(BF16)
        room = jnp.clip(need_ties_f - ties_seen, -1.0, float(tk)).astype(BF16)
        store_mask(kb, jnp.where(rank < room, jnp.maximum(above, eq), above), diagonal)
        return ties_seen + jnp.dot(ones_rows, eq, preferred_element_type=F32)[0:1]

    def emit_fast():
        def body(kb, carry):
            emit_all_ties(kb, False)
            return carry
        lax.fori_loop(0, nkb - 1, body, 0)
        emit_all_ties(nkb - 1, True)

    def emit_slow():
        ties_seen = lax.fori_loop(0, nkb - 1, lambda kb, seen: emit_ranked_ties(kb, seen, False),
                                  jnp.zeros((1, tq), F32))
        emit_ranked_ties(nkb - 1, ties_seen, True)

    lax.cond(all_ties_fit, emit_fast, emit_slow)

    def fill(kb, carry):
        k0 = pl.multiple_of(kb * tk, tk)
        mask_ref[0, pl.ds(k0, tk), :] = jnp.full((tk, tq), NEG, mask_ref.dtype)
        return carry

    lax.fori_loop(nkb, n_kblocks, fill, 0)


def _dsa_select(z, zs, B, S, qix_blk, n_sel):
    tq = tk = min(S, ATTN_BLOCK)
    nq = S // tq
    assert nq % COUNT_BLOCKS == 0
    qw = N_IDX_HEADS * IDX_DIM
    return pl.pallas_call(
        functools.partial(_dsa_select_kernel, tq=tq, tk=tk, n_sel=n_sel, n_kblocks=S // tk),
        grid=(B, nq),
        in_specs=[pl.BlockSpec((tq, qw), lambda b, i: (b * nq + i, qix_blk)),
                  pl.BlockSpec((S, 2 * IDX_DIM), lambda b, i: (b, 0)),
                  pl.BlockSpec((tq, 2 * IDX_DIM), lambda b, i: (b * nq + i, 1))],
        out_specs=pl.BlockSpec((1, S, tq), lambda b, i: (b * nq + i, 0, 0)),
        out_shape=jax.ShapeDtypeStruct((B * nq, S, tq), BF16),
        scratch_shapes=[pltpu.VMEM((S, tq), I16), pltpu.VMEM((S, tq), I16)],
        compiler_params=_params("arbitrary", "arbitrary"),
        name="dsa_select",
    )(z, zs, zs)


def _rel_bucket(dist):
    n = jnp.maximum(dist, 0)
    max_exact = N_BUCKETS // 2
    nf = jnp.maximum(n, 1).astype(F32)
    large = max_exact + (jnp.log(nf / max_exact) / math.log(MAX_DISTANCE / max_exact)
                         * (N_BUCKETS - max_exact)).astype(jnp.int32)
    large = jnp.minimum(large, N_BUCKETS - 1)
    return jnp.where(n < max_exact, n, large)


def _bias_tiles(rel_bias, tq):
    assert tq >= MAX_DISTANCE
    rb = rel_bias.astype(F32)
    key = jnp.arange(tq)[:, None]
    query = jnp.arange(tq)[None, :]
    bucket = _rel_bucket(jnp.stack([query - key, tq + query - key]))
    onehot = (bucket[None] == jnp.arange(N_BUCKETS)[:, None, None, None]).astype(F32)
    return jnp.einsum("nh,nikq->hikq", (rb - rb[N_BUCKETS - 1]) * LOG2_E, onehot,
                      precision=lax.Precision.HIGHEST)


def _dsa_attn_kernel(q_ref, k_ref, vt_ref, mask_ref, bias_ref, o_ref, m_ref, acc_ref, logit_ref, *,
                     tq, n_heads):
    qi = pl.program_id(2)
    m_ref[...] = jnp.full_like(m_ref, NEG)
    acc_ref[...] = jnp.zeros_like(acc_ref)
    ones_rows = jnp.ones((BF16_ROWS, tq), BF16)

    def step(blocks):
        k0s = [pl.multiple_of(kb * tq, tq) for kb, _ in blocks]
        masks = [mask_ref[0, pl.ds(k0, tq), :].astype(F32) for k0 in k0s]

        def head_scores(g):
            return [jnp.dot(k_ref[pl.ds(k0, tq), _head_cols(g)], q_ref[_head_cols(g), :],
                            preferred_element_type=F32) for k0 in k0s]

        def head_logits(g, scores):
            m_new = m_ref[g]
            for j, (_, bias_idx) in enumerate(blocks):
                s = scores[j] + masks[j]
                if bias_idx is not None:
                    s = s + bias_ref[g, bias_idx]
                logit_ref[g, j] = s
                m_new = jnp.maximum(m_new, jnp.max(s, axis=0, keepdims=True))
            return m_new

        def head_accumulate(g, m_new):
            acc = jnp.exp2(m_ref[g] - m_new) * acc_ref[g]
            for j, k0 in enumerate(k0s):
                p = jnp.exp2(logit_ref[g, j] - m_new).astype(BF16)
                vt = jnp.concatenate([vt_ref[_head_cols(g), pl.ds(k0, tq)], ones_rows], axis=0)
                acc = acc + jnp.dot(vt, p, preferred_element_type=F32)
            acc_ref[g] = acc
            m_ref[g] = m_new

        scores = [head_scores(g) for g in range(n_heads)]
        maxima = [head_logits(g, scores[g]) for g in range(n_heads)]
        for g in range(n_heads):
            head_accumulate(g, maxima[g])

    n_far = jnp.maximum(qi - 1, 0)

    def far_pair(i, carry):
        step([(2 * i, None), (2 * i + 1, None)])
        return carry

    lax.fori_loop(0, n_far // 2, far_pair, 0)

    @pl.when(n_far % 2 == 1)
    def _():
        step([(n_far - 1, None)])

    @pl.when(qi >= 1)
    def _():
        step([(qi - 1, 1), (qi, 0)])

    @pl.when(qi == 0)
    def _():
        step([(qi, 0)])
    for g in range(n_heads):
        acc = acc_ref[g]
        out_t = acc[:HEAD_DIM] / acc[HEAD_DIM:HEAD_DIM + 1]
        o_ref[:, _head_cols(g)] = out_t.T.astype(o_ref.dtype)


def _dsa_attn(z, zt, mask, bias_tiles, B, S, H, q_blk, k_blk, v_blk):
    M = z.shape[0]
    tq = bias_tiles.shape[-1]
    nq = S // tq
    G = min(H, DSA_HEAD_GROUP)
    gw = G * HEAD_DIM
    return pl.pallas_call(
        functools.partial(_dsa_attn_kernel, tq=tq, n_heads=G),
        grid=(B, H // G, nq),
        in_specs=_attn_group_specs(S, tq, nq, G, q_blk, k_blk, v_blk, kv_buffers=1) + [
            pl.BlockSpec((1, S, tq), lambda b, hg, i: (b * nq + i, 0, 0)),
            pl.BlockSpec((G, 2, tq, tq), lambda b, hg, i: (hg, 0, 0, 0),
                         pipeline_mode=pl.Buffered(1))],
        out_specs=pl.BlockSpec((tq, gw), lambda b, hg, i: (b * nq + i, hg)),
        out_shape=jax.ShapeDtypeStruct((M, H * HEAD_DIM), BF16),
        scratch_shapes=[pltpu.VMEM((G, 1, tq), F32),
                        pltpu.VMEM((G, HEAD_DIM + BF16_ROWS, tq), F32),
                        pltpu.VMEM((G, 2, tq, tq), F32)],
        compiler_params=_params("arbitrary", "arbitrary", "arbitrary"),
        name="dsa_attn",
    )(zt, z, zt, mask, bias_tiles)


def _mix_out_kernel(osb_ref, ods_ref, gsb_ref, gds_ref, x_ref, wsb_ref, wds_ref, wout_ref,
                    gt_ref, g_ref, sc_ref, sh_ref, x1_ref, h2_ref):
    t_sb = jnp.dot(osb_ref[...], wsb_ref[...], preferred_element_type=F32)
    t_ds = jnp.dot(ods_ref[...], wds_ref[...], preferred_element_type=F32)
    merged = (_sigmoid(gsb_ref[...].astype(F32)) * t_sb
              + _sigmoid(gds_ref[...].astype(F32)) * t_ds)
    y = jnp.dot(merged.astype(BF16), wout_ref[...], preferred_element_type=F32)
    x1 = x_ref[...] + gt_ref[0] * y
    x1_ref[...] = x1
    h2_ref[...] = _rms_mod(x1, g_ref[...], sc_ref[0], sh_ref[0]).astype(h2_ref.dtype)


def _mix_out(o_sb, o_ds, z, gate_blk, x2, w_sb, w_ds, w_out, gt, g, sc, sh, S):
    M, D = x2.shape
    B = gt.shape[0]
    W = o_sb.shape[1]
    tm = min(S, 512)
    per_b = S // tm
    row = lambda i: (i, 0)
    const = lambda i: (0, 0)
    per_batch = lambda i: (i // per_b, 0, 0)
    return pl.pallas_call(
        _mix_out_kernel,
        grid=(M // tm,),
        in_specs=[pl.BlockSpec((tm, W), row),
                  pl.BlockSpec((tm, W), row),
                  pl.BlockSpec((tm, D), lambda i: (i, gate_blk)),
                  pl.BlockSpec((tm, D), lambda i: (i, gate_blk + 1)),
                  pl.BlockSpec((tm, D), row),
                  pl.BlockSpec((W, D), const),
                  pl.BlockSpec((W, D), const),
                  pl.BlockSpec((D, D), const),
                  pl.BlockSpec((1, 1, D), per_batch),
                  pl.BlockSpec((1, D), const),
                  pl.BlockSpec((1, 1, D), per_batch),
                  pl.BlockSpec((1, 1, D), per_batch)],
        out_specs=[pl.BlockSpec((tm, D), row), pl.BlockSpec((tm, D), row)],
        out_shape=[jax.ShapeDtypeStruct((M, D), F32), jax.ShapeDtypeStruct((M, D), BF16)],
        compiler_params=_params("arbitrary"),
        name="mix_out",
    )(o_sb, o_ds, z, z, x2, w_sb, w_ds, w_out, gt.reshape(B, 1, D), g.reshape(1, D),
      sc.reshape(B, 1, D), sh.reshape(B, 1, D))


HALO = BF16_ROWS


def _ffn_act_kernel(h_ref, halo_ref, wg_ref, wu_ref, cw_ref, cb_ref, o_ref, wgb_ref, wub_ref, *,
                    tiles_per_seq):
    i = pl.program_id(1)

    @pl.when(i == 0)
    def _():
        wgb_ref[...] = wg_ref[...].astype(BF16)
        wub_ref[...] = wu_ref[...].astype(BF16)

    h = h_ref[...]
    wg = wgb_ref[...]
    g0 = jnp.dot(h, wg, preferred_element_type=F32)
    g_prev = jnp.dot(halo_ref[...], wg, preferred_element_type=F32)
    g_prev = jnp.where(i % tiles_per_seq == 0, 0.0, g_prev)
    ridx = lax.broadcasted_iota(jnp.int32, g0.shape, 0)
    g1 = jnp.where(ridx == 0, g_prev[HALO - 1:HALO, :], pltpu.roll(g0, 1, 0))
    g2 = jnp.where(ridx == 0, g_prev[HALO - 2:HALO - 1, :],
                   jnp.where(ridx == 1, g_prev[HALO - 1:HALO, :], pltpu.roll(g0, 2, 0)))
    cw = cw_ref[...]
    a = cb_ref[...] + g2 * cw[0:1, :] + g1 * cw[1:2, :] + g0 * cw[2:3, :]
    u = jnp.dot(h, wub_ref[...], preferred_element_type=F32)
    o_ref[...] = (a * _sigmoid(a) * u).astype(o_ref.dtype)


def _ffn_out_kernel(a_ref, wd_ref, x1_ref, gt_ref, gf_ref, o_ref):
    k = pl.program_id(1)

    @pl.when(k == 0)
    def _():
        o_ref[...] = jnp.zeros_like(o_ref)

    o_ref[...] += jnp.dot(a_ref[...], wd_ref[...], preferred_element_type=F32)

    @pl.when(k == pl.num_programs(1) - 1)
    def _():
        x = x1_ref[...] + gt_ref[0] * o_ref[...]
        ms = jnp.mean(x * x, axis=-1, keepdims=True)
        o_ref[...] = x * lax.rsqrt(ms + EPS) * gf_ref[...]


def _conv_ffn(h2, x1, w_gate, w_up, layer, w_down, conv_w, conv_b, gt, g_final, S):
    M, D = x1.shape
    B = gt.shape[0]
    F = w_gate.shape[2]
    tm = min(S, 1024)
    tf = min(F, 512)
    per_b = S // tm
    halo_per_tile = tm // HALO
    act = pl.pallas_call(
        functools.partial(_ffn_act_kernel, tiles_per_seq=per_b),
        grid=(F // tf, M // tm),
        in_specs=[pl.BlockSpec((tm, D), lambda f, i: (i, 0)),
                  pl.BlockSpec((HALO, D), lambda f, i: (jnp.maximum(i * halo_per_tile - 1, 0), 0)),
                  pl.BlockSpec((None, D, tf), lambda f, i: (layer, 0, f)),
                  pl.BlockSpec((None, D, tf), lambda f, i: (layer, 0, f)),
                  pl.BlockSpec((CONV_WIDTH, tf), lambda f, i: (0, f)),
                  pl.BlockSpec((1, tf), lambda f, i: (0, f))],
        out_specs=pl.BlockSpec((tm, tf), lambda f, i: (i, f)),
        out_shape=jax.ShapeDtypeStruct((M, F), BF16),
        scratch_shapes=[pltpu.VMEM((D, tf), BF16), pltpu.VMEM((D, tf), BF16)],
        compiler_params=_params("arbitrary", "arbitrary"),
        name="ffn_act",
    )(h2, h2, w_gate, w_up, conv_w, conv_b.reshape(1, F))
    return pl.pallas_call(
        _ffn_out_kernel,
        grid=(M // tm, F // tf),
        in_specs=[pl.BlockSpec((tm, tf), lambda i, k: (i, k)),
                  pl.BlockSpec((tf, D), lambda i, k: (k, 0)),
                  pl.BlockSpec((tm, D), lambda i, k: (i, 0)),
                  pl.BlockSpec((1, 1, D), lambda i, k: (i // per_b, 0, 0)),
                  pl.BlockSpec((1, D), lambda i, k: (0, 0))],
        out_specs=pl.BlockSpec((tm, D), lambda i, k: (i, 0)),
        out_shape=jax.ShapeDtypeStruct((M, D), F32),
        compiler_params=_params("arbitrary", "arbitrary"),
        name="ffn_out",
    )(act, w_down, x1, gt.reshape(B, 1, D), g_final.reshape(1, D))


def kernel(x, c, w_ada, b_ada, g_mix, w_in, w_o_sb, w_o_dsa, w_out, rel_bias, g_ffn, w_gate,
           w_up, conv_w, conv_b, w_down, g_final):
    B, S, D = x.shape
    depth = w_ada.shape[0]
    W = w_o_sb.shape[1]
    H = W // HEAD_DIM
    qw = N_IDX_HEADS * IDX_DIM
    assert w_o_dsa.shape[1] == W and D % W == 0 and S % ATTN_BLOCK == 0 and W % qw == 0
    n_sel = min(TOPK_MAX, S // 4)
    scale = HEAD_DIM ** -0.5
    x2 = x.reshape(B * S, D)

    for l in range(depth):
        mod = _adaln(c, w_ada[l], b_ada[l])
        sh1, sc1, gt1, sh2, sc2, gt2 = jnp.split(mod, 6, axis=-1)

        wt = w_in[l].T
        o_kix = 6 * W + qw
        o_wix = o_kix + IDX_DIM
        o_gate = o_wix + N_IDX_HEADS
        k_rows = wt[o_kix:o_wix]
        wt_tail = jnp.concatenate(
            [k_rows, k_rows, wt[o_wix:o_gate],
             jnp.zeros((2 * IDX_DIM - N_IDX_HEADS, D), wt.dtype)], axis=0)

        h1 = _norm_mod(x2, g_mix[l], sc1, sh1, S)
        z = _in_proj_wt(h1, wt, (1, 4, 6), W, (), "in_proj")
        zt = _in_proj_wt(h1, wt, (0, 2, 3, 5), W, ((0, scale), (2, scale * LOG2_E)),
                         "in_proj_t", transposed=True)
        zg = _in_proj_wt(h1, wt[o_gate:], tuple(range(2 * D // W)), W, (), "in_proj_gates")
        zs = _in_proj_wt(h1, wt_tail, (0,), 4 * IDX_DIM, (), "in_proj_idx")

        o_sb = _sb_attn(z, zt, B, S, H, 0, 0, H)
        mask = _dsa_select(z, zs, B, S, 2 * W // qw, n_sel)
        o_ds = _dsa_attn(z, zt, mask, _bias_tiles(rel_bias, min(S, ATTN_BLOCK)),
                         B, S, H, 2 * H, H, 3 * H)

        x2, h2 = _mix_out(o_sb, o_ds, zg, 0, x2, w_o_sb[l].astype(BF16),
                          w_o_dsa[l].astype(BF16), w_out[l].astype(BF16), gt1, g_ffn[l], sc2, sh2, S)
        last = l == depth - 1
        assert last, "the final rms_norm is fused into the last layer's FFN"
        x2 = _conv_ffn(h2, x2, w_gate, w_up, l, w_down[l].astype(BF16),
                       conv_w[l], conv_b[l], gt2, g_final, S)
    return x2.reshape(B, S, D)
```

```python
import functools
import math

import jax
import jax.numpy as jnp
from jax import lax
from jax.experimental import pallas as pl
from jax.experimental.pallas import tpu as pltpu

HEAD_DIM = 128
N_IDX_HEADS = 16
IDX_DIM = 64
TOPK_MAX = 256
N_BUCKETS = 32
MAX_DISTANCE = 128
CONV_WIDTH = 3
EPS = 1e-6

F32 = jnp.float32
BF16 = jnp.bfloat16
NEG = -1e30
I16 = jnp.int16
INT_MIN = -2 ** 31
I16_MIN = -2 ** 15
EXP_ZERO_BELOW = 104.0
LOG2_E = math.log2(math.e)
V7X_VMEM_LIMIT = 56 * 1024 * 1024
BF16_ROWS = 16
ATTN_BLOCK = 256
SB_HEAD_GROUP = 4
DSA_HEAD_GROUP = 8
SCORE_BLOCKS = 4
COUNT_BLOCKS = 4
COUNT_CHUNK = 64
NT_DIMS = (((1,), (1,)), ((), ()))


def _params(*sem):
    return pltpu.CompilerParams(dimension_semantics=sem, vmem_limit_bytes=V7X_VMEM_LIMIT)


def _sigmoid(x):
    return 1.0 / (1.0 + jnp.exp(-x))


def _adaln_kernel(ct_ref, w_ref, b_ref, o_ref):
    ct = ct_ref[...]
    act = ct * _sigmoid(ct)
    w = w_ref[...]
    for b in range(ct.shape[1]):
        o_ref[b:b + 1, :] = jnp.sum(act[:, b:b + 1] * w, axis=0, keepdims=True) + b_ref[...]


def _adaln(c, w, bias):
    B, D = c.shape
    N = w.shape[1]
    tn = min(N, 1024)
    return pl.pallas_call(
        _adaln_kernel,
        grid=(N // tn,),
        in_specs=[pl.BlockSpec((D, B), lambda j: (0, 0)),
                  pl.BlockSpec((D, tn), lambda j: (0, j)),
                  pl.BlockSpec((1, tn), lambda j: (0, j))],
        out_specs=pl.BlockSpec((B, tn), lambda j: (0, j)),
        out_shape=jax.ShapeDtypeStruct((B, N), F32),
        compiler_params=_params("arbitrary"),
        name="adaln",
    )(c.T, w, bias.reshape(1, N))


def _rms_mod(x, g, sc, sh):
    ms = jnp.mean(x * x, axis=-1, keepdims=True)
    y = x * lax.rsqrt(ms + EPS) * g
    return y * (1.0 + sc) + sh


def _norm_mod_kernel(x_ref, g_ref, sc_ref, sh_ref, o_ref):
    o_ref[...] = _rms_mod(x_ref[...], g_ref[...], sc_ref[0], sh_ref[0]).astype(o_ref.dtype)


def _norm_mod(x2, g, sc, sh, S):
    M, D = x2.shape
    B = sc.shape[0]
    tm = min(S, 1024)
    per_b = S // tm
    return pl.pallas_call(
        _norm_mod_kernel,
        grid=(M // tm,),
        in_specs=[pl.BlockSpec((tm, D), lambda i: (i, 0)),
                  pl.BlockSpec((1, D), lambda i: (0, 0)),
                  pl.BlockSpec((1, 1, D), lambda i: (i // per_b, 0, 0)),
                  pl.BlockSpec((1, 1, D), lambda i: (i // per_b, 0, 0))],
        out_specs=pl.BlockSpec((tm, D), lambda i: (i, 0)),
        out_shape=jax.ShapeDtypeStruct((M, D), BF16),
        compiler_params=_params("arbitrary"),
        name="norm_mod",
    )(x2, g.reshape(1, D), sc.reshape(B, 1, D), sh.reshape(B, 1, D))


def _in_proj_kernel(a_ref, b_ref, o_ref, *, tile_scales):
    acc = jnp.dot(a_ref[...], b_ref[...], preferred_element_type=F32)
    o_ref[...] = (acc * _tile_scale(tile_scales)).astype(o_ref.dtype)


def _in_proj_wt_kernel(a_ref, wt_ref, o_ref, wb_ref, *, tile_scales):
    @pl.when(pl.program_id(1) == 0)
    def _():
        wb_ref[...] = wt_ref[...].T.astype(BF16)

    _in_proj_kernel(a_ref, wb_ref, o_ref, tile_scales=tile_scales)


def _tile_scale(tile_scales):
    j = pl.program_id(0)
    scale = jnp.float32(1.0)
    for tile, tile_scale in tile_scales:
        scale = jnp.where(j == tile, tile_scale, scale)
    return scale


def _in_proj_t_kernel(a_ref, wt_ref, o_ref, wb_ref, *, tile_scales):
    @pl.when(pl.program_id(1) == 0)
    def _():
        wb_ref[...] = wt_ref[...].astype(BF16)

    acc = lax.dot_general(wb_ref[...], a_ref[...], NT_DIMS, preferred_element_type=F32)
    o_ref[...] = (acc * _tile_scale(tile_scales)).astype(o_ref.dtype)


def _tile_index_fn(tiles):
    steps = []
    for j, t in enumerate(tiles):
        while len(steps) < t - j:
            steps.append(j)
    return lambda j: j + sum(jnp.where(j >= first, 1, 0) for first in steps)


def _in_proj_wt(h, wt, tiles, tn, tile_scales, name, transposed=False):
    M, K = h.shape
    tm = min(M, 1024)
    n = len(tiles)
    tile_of = _tile_index_fn(tiles)
    in_specs = [pl.BlockSpec((tm, K), lambda j, i: (i, 0)),
                pl.BlockSpec((tn, K), lambda j, i: (tile_of(j), 0))]
    if transposed:
        body, wb_shape = functools.partial(_in_proj_t_kernel, tile_scales=tile_scales), (tn, K)
        out_spec = pl.BlockSpec((tn, tm), lambda j, i: (j, i))
        out_shape = jax.ShapeDtypeStruct((n * tn, M), BF16)
    else:
        body, wb_shape = functools.partial(_in_proj_wt_kernel, tile_scales=tile_scales), (K, tn)
        out_spec = pl.BlockSpec((tm, tn), lambda j, i: (i, j))
        out_shape = jax.ShapeDtypeStruct((M, n * tn), BF16)
    return pl.pallas_call(
        body,
        grid=(n, M // tm),
        in_specs=in_specs,
        out_specs=out_spec,
        out_shape=out_shape,
        scratch_shapes=[pltpu.VMEM(wb_shape, BF16)],
        compiler_params=_params("arbitrary", "arbitrary"),
        name=name,
    )(h, wt)


def _key_query_iotas(k0, q0, tk, tq):
    key_pos = k0 + lax.broadcasted_iota(jnp.int32, (tk, tq), 0)
    query_pos = q0 + lax.broadcasted_iota(jnp.int32, (tk, tq), 1)
    return key_pos, query_pos


def _head_cols(g):
    return slice(g * HEAD_DIM, (g + 1) * HEAD_DIM)


def _sb_kernel(q_ref, k_ref, vt_ref, o_ref, run_ref, acc_ref, *, tq, tk, n_heads):
    qi = pl.program_id(2)
    q0 = qi * tq
    run_ref[...] = jnp.zeros_like(run_ref)
    acc_ref[...] = jnp.zeros_like(acc_ref)
    later = (lax.broadcasted_iota(jnp.int32, (tk, tk), 1)
             > lax.broadcasted_iota(jnp.int32, (tk, tk), 0)).astype(BF16)

    def step(kbs, first_is_diagonal):
        heads = range(n_heads)
        k0s = [pl.multiple_of(kb * tk, tk) for kb in kbs]
        causal = None
        if first_is_diagonal:
            key_pos, query_pos = _key_query_iotas(k0s[0], q0, tk, tq)
            causal = key_pos < query_pos
        masked = [first_is_diagonal and j == 0 for j in range(len(kbs))]
        zs = [[jnp.dot(k_ref[pl.ds(k0, tk), _head_cols(g)], q_ref[_head_cols(g), :],
                       preferred_element_type=F32) for g in heads] for k0 in k0s]
        sps = [[jnp.maximum(z, 0.0) + jnp.log(1.0 + jnp.exp(-jnp.abs(z))) for z in zj] for zj in zs]
        spms = [[jnp.where(causal, sp, 0.0) if masked[j] else sp for sp in spj]
                for j, spj in enumerate(sps)]
        his = [[spm.astype(BF16) for spm in spj] for spj in spms]
        los = [[(spm - hi.astype(F32)).astype(BF16) for spm, hi in zip(spj, hij)]
               for spj, hij in zip(spms, his)]
        suffixes = [[jnp.dot(later, hi, preferred_element_type=F32)
                     + jnp.dot(later, lo, preferred_element_type=F32) for hi, lo in zip(hij, loj)]
                    for hij, loj in zip(his, los)]
        weights, new_runs = [], []
        for g in heads:
            run = run_ref[g]
            head_weights = []
            for j in range(len(kbs)):
                a = jnp.exp(zs[j][g] - sps[j][g] - suffixes[j][g] - run)
                if masked[j]:
                    a = jnp.where(causal, a, 0.0)
                head_weights.append(a.astype(BF16))
                run = run + jnp.sum(spms[j][g], axis=0, keepdims=True)
            weights.append(head_weights)
            new_runs.append(run)
        min_run = None
        for g in heads:
            acc = acc_ref[g]
            for j, k0 in enumerate(k0s):
                acc = acc + jnp.dot(vt_ref[_head_cols(g), pl.ds(k0, tk)], weights[g][j],
                                    preferred_element_type=F32)
            acc_ref[g] = acc
            run_ref[g] = new_runs[g]
            head_min = jnp.min(new_runs[g])
            min_run = head_min if min_run is None else jnp.minimum(min_run, head_min)
        return min_run > EXP_ZERO_BELOW

    def body(carry):
        kb, _ = carry
        return kb - 1, step([kb], False)

    def cond(carry):
        kb, dead = carry
        return jnp.logical_and(kb >= 0, jnp.logical_not(dead))

    @pl.when(qi == 0)
    def _():
        step([qi], True)

    @pl.when(qi >= 1)
    def _():
        dead = step([qi, qi - 1], True)
        lax.while_loop(cond, body, (qi - 2, dead))

    for g in range(n_heads):
        o_ref[:, _head_cols(g)] = acc_ref[g].T.astype(o_ref.dtype)


def _attn_group_specs(S, tq, nq, G, q_blk, k_blk, v_blk, kv_buffers):
    assert q_blk % G == 0 and k_blk % G == 0 and v_blk % G == 0
    gw = G * HEAD_DIM
    mode = pl.Buffered(kv_buffers)
    return [pl.BlockSpec((gw, tq), lambda b, hg, i: (q_blk // G + hg, b * nq + i)),
            pl.BlockSpec((S, gw), lambda b, hg, i: (b, k_blk // G + hg), pipeline_mode=mode),
            pl.BlockSpec((gw, S), lambda b, hg, i: (v_blk // G + hg, b), pipeline_mode=mode)]


def _sb_attn(z, zt, B, S, H, q_blk, k_blk, v_blk):
    M = z.shape[0]
    tq = tk = min(S, ATTN_BLOCK)
    nq = S // tq
    G = min(H, SB_HEAD_GROUP)
    gw = G * HEAD_DIM
    return pl.pallas_call(
        functools.partial(_sb_kernel, tq=tq, tk=tk, n_heads=G),
        grid=(B, H // G, nq),
        in_specs=_attn_group_specs(S, tq, nq, G, q_blk, k_blk, v_blk, kv_buffers=2),
        out_specs=pl.BlockSpec((tq, gw), lambda b, hg, i: (b * nq + i, hg)),
        out_shape=jax.ShapeDtypeStruct((M, H * HEAD_DIM), BF16),
        scratch_shapes=[pltpu.VMEM((G, 1, tq), F32), pltpu.VMEM((G, HEAD_DIM, tq), F32)],
        compiler_params=_params("arbitrary", "arbitrary", "arbitrary"),
        name="sb_attn",
    )(zt, z, zt)


def _dsa_select_kernel(q_ref, k_ref, w_ref, mask_ref, hi_ref, lo_ref, *, tq, tk, n_sel, n_kblocks):
    qi = pl.program_id(1)
    q0 = qi * tq
    nkb = (q0 + tq + tk - 1) // tk
    lane = lax.broadcasted_iota(jnp.int32, (tk, 2 * IDX_DIM), 1)
    w_scale = (N_IDX_HEADS ** -0.5) * (IDX_DIM ** -0.5)
    wt = (w_ref[...].astype(F32) * w_scale).T

    def score_block(kb, diagonal):
        k0 = pl.multiple_of(kb * tk, tk)
        kk = k_ref[pl.ds(k0, tk), :]
        k_halves = (jnp.where(lane < IDX_DIM, kk, jnp.zeros_like(kk)),
                    jnp.where(lane >= IDX_DIM, kk, jnp.zeros_like(kk)))
        acc = jnp.zeros((tk, tq), F32)
        for p in range(N_IDX_HEADS // 2):
            q2t = q_ref[p * 2 * IDX_DIM:(p + 1) * 2 * IDX_DIM, :]
            for half in range(2):
                h = 2 * p + half
                ph = jnp.dot(k_halves[half], q2t, preferred_element_type=F32)
                acc = acc + jnp.maximum(ph, 0.0) * wt[h:h + 1, :]
        bits = lax.bitcast_convert_type(acc, jnp.int32)
        key = bits ^ ((bits >> 31) & 0x7FFFFFFF)
        if diagonal:
            key_pos, query_pos = _key_query_iotas(k0, q0, tk, tq)
            key = jnp.where(key_pos <= query_pos, key, INT_MIN)
        hi_ref[pl.ds(k0, tk), :] = (key >> 16).astype(I16)
        lo_ref[pl.ds(k0, tk), :] = (key ^ 0x8000).astype(I16)

    def score_group(i, carry):
        for j in range(SCORE_BLOCKS):
            score_block(SCORE_BLOCKS * i + j, False)
        return carry

    n_before = nkb - 1
    lax.fori_loop(0, n_before // SCORE_BLOCKS, score_group, 0)
    first_left = n_before - n_before % SCORE_BLOCKS
    for left in range(1, SCORE_BLOCKS):
        @pl.when(n_before % SCORE_BLOCKS == left)
        def _(left=left):
            for j in range(left):
                score_block(first_left + j, False)

    score_block(nkb - 1, True)

    n_count = (nkb + COUNT_BLOCKS - 1) // COUNT_BLOCKS
    rows = COUNT_BLOCKS * tk

    def pad_block(kb, carry):
        k0 = pl.multiple_of(kb * tk, tk)
        hi_ref[pl.ds(k0, tk), :] = jnp.full((tk, tq), I16_MIN, I16)
        lo_ref[pl.ds(k0, tk), :] = jnp.full((tk, tq), I16_MIN, I16)
        return carry

    lax.fori_loop(nkb, n_count * COUNT_BLOCKS, pad_block, 0)

    def count(flags_fn):
        def blk(i, cnt):
            r0 = pl.multiple_of(i * rows, rows)
            groups = []
            for c in range(rows // COUNT_CHUNK):
                flags = flags_fn(r0 + c * COUNT_CHUNK, COUNT_CHUNK)
                parts = [flags[r * BF16_ROWS:(r + 1) * BF16_ROWS, :]
                         for r in range(COUNT_CHUNK // BF16_ROWS)]
                while len(parts) > 1:
                    parts = [a + b for a, b in zip(parts[0::2], parts[1::2])]
                groups.append(parts[0])
            while len(groups) > 1:
                groups = [a + b for a, b in zip(groups[0::2], groups[1::2])]
            return cnt + groups[0]
        cnt = lax.fori_loop(0, n_count, blk, jnp.zeros((BF16_ROWS, tq), I16))
        return jnp.sum(cnt.astype(jnp.int32), axis=0, keepdims=True)

    flag, no_flag = I16(1), I16(0)

    def count_ge(ref, thr16):
        return count(lambda r, n: jnp.where(ref[pl.ds(r, n), :] >= thr16, flag, no_flag))

    def kth_largest(ref, k):
        def bisect(it, carry):
            thr, n_ge, n_gt = carry
            cand = thr + jnp.left_shift(jnp.int32(1), 15 - it)
            cnt = count_ge(ref, cand.astype(I16))
            ok = cnt >= k
            return jnp.where(ok, cand, thr), jnp.where(ok, cnt, n_ge), jnp.where(ok, n_gt, cnt)
        zeros = jnp.zeros((1, tq), jnp.int32)
        thr, n_ge, n_gt = lax.fori_loop(0, 16, bisect, (zeros + I16_MIN, zeros, zeros))
        return thr.astype(I16), n_ge, n_gt

    thr_hi, _, above_hi = kth_largest(hi_ref, n_sel)
    need_lo = n_sel - above_hi

    def park(kb, carry):
        k0 = pl.multiple_of(kb * tk, tk)
        lo_ref[pl.ds(k0, tk), :] = jnp.where(hi_ref[pl.ds(k0, tk), :] == thr_hi,
                                             lo_ref[pl.ds(k0, tk), :], I16(I16_MIN))
        return carry

    lax.fori_loop(0, nkb, park, 0)
    thr_lo, at_least_lo, above_lo = kth_largest(lo_ref, need_lo)
    need_ties = need_lo - above_lo
    n_ties = jnp.where(thr_lo == I16(I16_MIN), tk * n_kblocks, at_least_lo - above_lo)
    all_ties_fit = jnp.max(n_ties - need_ties) <= 0

    one, zero, neg = BF16(1.0), BF16(0.0), BF16(NEG)

    def store_mask(kb, sel, diagonal):
        k0 = pl.multiple_of(kb * tk, tk)
        if diagonal:
            key_pos, query_pos = _key_query_iotas(k0, q0, tk, tq)
            sel = sel * (key_pos <= query_pos).astype(BF16)
        mask_ref[0, pl.ds(k0, tk), :] = jnp.where(sel > zero, zero, neg)

    def emit_all_ties(kb, diagonal):
        k0 = pl.multiple_of(kb * tk, tk)
        hi = hi_ref[pl.ds(k0, tk), :]
        lo = lo_ref[pl.ds(k0, tk), :]
        store_mask(kb, jnp.where(hi > thr_hi, one,
                                 jnp.where(hi == thr_hi, jnp.where(lo >= thr_lo, one, zero), zero)),
                   diagonal)

    earlier = (lax.broadcasted_iota(jnp.int32, (tk, tk), 1)
               < lax.broadcasted_iota(jnp.int32, (tk, tk), 0)).astype(BF16)
    ones_rows = jnp.ones((BF16_ROWS, tk), BF16)
    need_ties_f = need_ties.astype(F32)

    def emit_ranked_ties(kb, ties_seen, diagonal):
        k0 = pl.multiple_of(kb * tk, tk)
        hi = hi_ref[pl.ds(k0, tk), :]
        lo = lo_ref[pl.ds(k0, tk), :]
        above = jnp.where(hi > thr_hi, one, jnp.where(lo > thr_lo, one, zero))
        eq = jnp.where(hi == thr_hi, jnp.where(lo == thr_lo, one, zero), zero)
        rank = jnp.dot(earlier, eq, preferred_element_type=F32).astype(BF16)
        room = jnp.clip(need_ties_f - ties_seen, -1.0, float(tk)).astype(BF16)
        store_mask(kb, jnp.where(rank < room, jnp.maximum(above, eq), above), diagonal)
        return ties_seen + jnp.dot(ones_rows, eq, preferred_element_type=F32)[0:1]

    def emit_fast():
        def body(kb, carry):
            emit_all_ties(kb, False)
            return carry
        lax.fori_loop(0, nkb - 1, body, 0)
        emit_all_ties(nkb - 1, True)

    def emit_slow():
        ties_seen = lax.fori_loop(0, nkb - 1, lambda kb, seen: emit_ranked_ties(kb, seen, False),
                                  jnp.zeros((1, tq), F32))
        emit_ranked_ties(nkb - 1, ties_seen, True)

    lax.cond(all_ties_fit, emit_fast, emit_slow)

    def fill(kb, carry):
        k0 = pl.multiple_of(kb * tk, tk)
        mask_ref[0, pl.ds(k0, tk), :] = jnp.full((tk, tq), NEG, mask_ref.dtype)
        return carry

    lax.fori_loop(nkb, n_kblocks, fill, 0)


def _dsa_select(zt, zs, B, S, qix_blk, n_sel):
    tq = tk = min(S, ATTN_BLOCK)
    nq = S // tq
    assert nq % COUNT_BLOCKS == 0
    qw = N_IDX_HEADS * IDX_DIM
    return pl.pallas_call(
        functools.partial(_dsa_select_kernel, tq=tq, tk=tk, n_sel=n_sel, n_kblocks=S // tk),
        grid=(B, nq),
        in_specs=[pl.BlockSpec((qw, tq), lambda b, i: (qix_blk, b * nq + i)),
                  pl.BlockSpec((S, 2 * IDX_DIM), lambda b, i: (b, 0)),
                  pl.BlockSpec((tq, 2 * IDX_DIM), lambda b, i: (b * nq + i, 1))],
        out_specs=pl.BlockSpec((1, S, tq), lambda b, i: (b * nq + i, 0, 0)),
        out_shape=jax.ShapeDtypeStruct((B * nq, S, tq), BF16),
        scratch_shapes=[pltpu.VMEM((S, tq), I16), pltpu.VMEM((S, tq), I16)],
        compiler_params=_params("arbitrary", "arbitrary"),
        name="dsa_select",
    )(zt, zs, zs)


def _rel_bucket(dist):
    n = jnp.maximum(dist, 0)
    max_exact = N_BUCKETS // 2
    nf = jnp.maximum(n, 1).astype(F32)
    large = max_exact + (jnp.log(nf / max_exact) / math.log(MAX_DISTANCE / max_exact)
                         * (N_BUCKETS - max_exact)).astype(jnp.int32)
    large = jnp.minimum(large, N_BUCKETS - 1)
    return jnp.where(n < max_exact, n, large)


def _bias_tiles(rel_bias, tq):
    assert tq >= MAX_DISTANCE
    rb = rel_bias.astype(F32)
    key = jnp.arange(tq)[:, None]
    query = jnp.arange(tq)[None, :]
    bucket = _rel_bucket(jnp.stack([query - key, tq + query - key]))
    onehot = (bucket[None] == jnp.arange(N_BUCKETS)[:, None, None, None]).astype(F32)
    return jnp.einsum("nh,nikq->hikq", (rb - rb[N_BUCKETS - 1]) * LOG2_E, onehot,
                      precision=lax.Precision.HIGHEST)


def _dsa_attn_kernel(q_ref, k_ref, vt_ref, mask_ref, bias_ref, o_ref, m_ref, acc_ref, logit_ref, *,
                     tq, n_heads):
    qi = pl.program_id(2)
    m_ref[...] = jnp.full_like(m_ref, NEG)
    acc_ref[...] = jnp.zeros_like(acc_ref)
    ones_rows = jnp.ones((BF16_ROWS, tq), BF16)

    def step(blocks):
        k0s = [pl.multiple_of(kb * tq, tq) for kb, _ in blocks]
        masks = [mask_ref[0, pl.ds(k0, tq), :].astype(F32) for k0 in k0s]

        def head_scores(g):
            return [jnp.dot(k_ref[pl.ds(k0, tq), _head_cols(g)], q_ref[_head_cols(g), :],
                            preferred_element_type=F32) for k0 in k0s]

        def head_logits(g, scores):
            m_new = m_ref[g]
            for j, (_, bias_idx) in enumerate(blocks):
                s = scores[j] + masks[j]
                if bias_idx is not None:
                    s = s + bias_ref[g, bias_idx]
                logit_ref[g, j] = s
                m_new = jnp.maximum(m_new, jnp.max(s, axis=0, keepdims=True))
            return m_new

        def head_accumulate(g, m_new):
            acc = jnp.exp2(m_ref[g] - m_new) * acc_ref[g]
            for j, k0 in enumerate(k0s):
                p = jnp.exp2(logit_ref[g, j] - m_new).astype(BF16)
                vt = jnp.concatenate([vt_ref[_head_cols(g), pl.ds(k0, tq)], ones_rows], axis=0)
                acc = acc + jnp.dot(vt, p, preferred_element_type=F32)
            acc_ref[g] = acc
            m_ref[g] = m_new

        scores = [head_scores(g) for g in range(n_heads)]
        maxima = [head_logits(g, scores[g]) for g in range(n_heads)]
        for g in range(n_heads):
            head_accumulate(g, maxima[g])

    n_far = jnp.maximum(qi - 1, 0)

    def far_pair(i, carry):
        step([(2 * i, None), (2 * i + 1, None)])
        return carry

    lax.fori_loop(0, n_far // 2, far_pair, 0)

    @pl.when(n_far % 2 == 1)
    def _():
        step([(n_far - 1, None)])

    @pl.when(qi >= 1)
    def _():
        step([(qi - 1, 1), (qi, 0)])

    @pl.when(qi == 0)
    def _():
        step([(qi, 0)])
    for g in range(n_heads):
        acc = acc_ref[g]
        out_t = acc[:HEAD_DIM] / acc[HEAD_DIM:HEAD_DIM + 1]
        o_ref[:, _head_cols(g)] = out_t.T.astype(o_ref.dtype)


def _dsa_attn(z, zt, mask, bias_tiles, B, S, H, q_blk, k_blk, v_blk):
    M = z.shape[0]
    tq = bias_tiles.shape[-1]
    nq = S // tq
    G = min(H, DSA_HEAD_GROUP)
    gw = G * HEAD_DIM
    return pl.pallas_call(
        functools.partial(_dsa_attn_kernel, tq=tq, n_heads=G),
        grid=(B, H // G, nq),
        in_specs=_attn_group_specs(S, tq, nq, G, q_blk, k_blk, v_blk, kv_buffers=1) + [
            pl.BlockSpec((1, S, tq), lambda b, hg, i: (b * nq + i, 0, 0)),
            pl.BlockSpec((G, 2, tq, tq), lambda b, hg, i: (hg, 0, 0, 0),
                         pipeline_mode=pl.Buffered(1))],
        out_specs=pl.BlockSpec((tq, gw), lambda b, hg, i: (b * nq + i, hg)),
        out_shape=jax.ShapeDtypeStruct((M, H * HEAD_DIM), BF16),
        scratch_shapes=[pltpu.VMEM((G, 1, tq), F32),
                        pltpu.VMEM((G, HEAD_DIM + BF16_ROWS, tq), F32),
                        pltpu.VMEM((G, 2, tq, tq), F32)],
        compiler_params=_params("arbitrary", "arbitrary", "arbitrary"),
        name="dsa_attn",
    )(zt, z, zt, mask, bias_tiles)


def _mix_out_kernel(osb_ref, ods_ref, gsb_ref, gds_ref, x_ref, wsb_ref, wds_ref, wout_ref,
                    gt_ref, g_ref, sc_ref, sh_ref, x1_ref, h2_ref):
    t_sb = jnp.dot(osb_ref[...], wsb_ref[...], preferred_element_type=F32)
    t_ds = jnp.dot(ods_ref[...], wds_ref[...], preferred_element_type=F32)
    merged = (_sigmoid(gsb_ref[...].astype(F32)) * t_sb
              + _sigmoid(gds_ref[...].astype(F32)) * t_ds)
    y = jnp.dot(merged.astype(BF16), wout_ref[...], preferred_element_type=F32)
    x1 = x_ref[...] + gt_ref[0] * y
    x1_ref[...] = x1
    h2_ref[...] = _rms_mod(x1, g_ref[...], sc_ref[0], sh_ref[0]).astype(h2_ref.dtype)


def _mix_out(o_sb, o_ds, z, gate_blk, x2, w_sb, w_ds, w_out, gt, g, sc, sh, S):
    M, D = x2.shape
    B = gt.shape[0]
    W = o_sb.shape[1]
    tm = min(S, 512)
    per_b = S // tm
    row = lambda i: (i, 0)
    const = lambda i: (0, 0)
    per_batch = lambda i: (i // per_b, 0, 0)
    return pl.pallas_call(
        _mix_out_kernel,
        grid=(M // tm,),
        in_specs=[pl.BlockSpec((tm, W), row),
                  pl.BlockSpec((tm, W), row),
                  pl.BlockSpec((tm, D), lambda i: (i, gate_blk)),
                  pl.BlockSpec((tm, D), lambda i: (i, gate_blk + 1)),
                  pl.BlockSpec((tm, D), row),
                  pl.BlockSpec((W, D), const),
                  pl.BlockSpec((W, D), const),
                  pl.BlockSpec((D, D), const),
                  pl.BlockSpec((1, 1, D), per_batch),
                  pl.BlockSpec((1, D), const),
                  pl.BlockSpec((1, 1, D), per_batch),
                  pl.BlockSpec((1, 1, D), per_batch)],
        out_specs=[pl.BlockSpec((tm, D), row), pl.BlockSpec((tm, D), row)],
        out_shape=[jax.ShapeDtypeStruct((M, D), F32), jax.ShapeDtypeStruct((M, D), BF16)],
        compiler_params=_params("arbitrary"),
        name="mix_out",
    )(o_sb, o_ds, z, z, x2, w_sb, w_ds, w_out, gt.reshape(B, 1, D), g.reshape(1, D),
      sc.reshape(B, 1, D), sh.reshape(B, 1, D))


HALO = BF16_ROWS


def _ffn_act_kernel(h_ref, halo_ref, wg_ref, wu_ref, cw_ref, cb_ref, o_ref, wgb_ref, wub_ref, *,
                    tiles_per_seq):
    i = pl.program_id(1)

    @pl.when(i == 0)
    def _():
        wgb_ref[...] = wg_ref[...].astype(BF16)
        wub_ref[...] = wu_ref[...].astype(BF16)

    h = h_ref[...]
    wg = wgb_ref[...]
    g0 = jnp.dot(h, wg, preferred_element_type=F32)
    g_prev = jnp.dot(halo_ref[...], wg, preferred_element_type=F32)
    g_prev = jnp.where(i % tiles_per_seq == 0, 0.0, g_prev)
    ridx = lax.broadcasted_iota(jnp.int32, g0.shape, 0)
    g1 = jnp.where(ridx == 0, g_prev[HALO - 1:HALO, :], pltpu.roll(g0, 1, 0))
    g2 = jnp.where(ridx == 0, g_prev[HALO - 2:HALO - 1, :],
                   jnp.where(ridx == 1, g_prev[HALO - 1:HALO, :], pltpu.roll(g0, 2, 0)))
    cw = cw_ref[...]
    a = cb_ref[...] + g2 * cw[0:1, :] + g1 * cw[1:2, :] + g0 * cw[2:3, :]
    u = jnp.dot(h, wub_ref[...], preferred_element_type=F32)
    o_ref[...] = (a * _sigmoid(a) * u).astype(o_ref.dtype)


def _ffn_out_kernel(a_ref, wd_ref, x1_ref, gt_ref, gf_ref, o_ref):
    k = pl.program_id(1)

    @pl.when(k == 0)
    def _():
        o_ref[...] = jnp.zeros_like(o_ref)

    o_ref[...] += jnp.dot(a_ref[...], wd_ref[...], preferred_element_type=F32)

    @pl.when(k == pl.num_programs(1) - 1)
    def _():
        x = x1_ref[...] + gt_ref[0] * o_ref[...]
        ms = jnp.mean(x * x, axis=-1, keepdims=True)
        o_ref[...] = x * lax.rsqrt(ms + EPS) * gf_ref[...]


def _conv_ffn(h2, x1, w_gate, w_up, layer, w_down, conv_w, conv_b, gt, g_final, S):
    M, D = x1.shape
    B = gt.shape[0]
    F = w_gate.shape[2]
    tm = min(S, 1024)
    tf = min(F, 512)
    per_b = S // tm
    halo_per_tile = tm // HALO
    act = pl.pallas_call(
        functools.partial(_ffn_act_kernel, tiles_per_seq=per_b),
        grid=(F // tf, M // tm),
        in_specs=[pl.BlockSpec((tm, D), lambda f, i: (i, 0)),
                  pl.BlockSpec((HALO, D), lambda f, i: (jnp.maximum(i * halo_per_tile - 1, 0), 0)),
                  pl.BlockSpec((None, D, tf), lambda f, i: (layer, 0, f)),
                  pl.BlockSpec((None, D, tf), lambda f, i: (layer, 0, f)),
                  pl.BlockSpec((CONV_WIDTH, tf), lambda f, i: (0, f)),
                  pl.BlockSpec((1, tf), lambda f, i: (0, f))],
        out_specs=pl.BlockSpec((tm, tf), lambda f, i: (i, f)),
        out_shape=jax.ShapeDtypeStruct((M, F), BF16),
        scratch_shapes=[pltpu.VMEM((D, tf), BF16), pltpu.VMEM((D, tf), BF16)],
        compiler_params=_params("arbitrary", "arbitrary"),
        name="ffn_act",
    )(h2, h2, w_gate, w_up, conv_w, conv_b.reshape(1, F))
    return pl.pallas_call(
        _ffn_out_kernel,
        grid=(M // tm, F // tf),
        in_specs=[pl.BlockSpec((tm, tf), lambda i, k: (i, k)),
                  pl.BlockSpec((tf, D), lambda i, k: (k, 0)),
                  pl.BlockSpec((tm, D), lambda i, k: (i, 0)),
                  pl.BlockSpec((1, 1, D), lambda i, k: (i // per_b, 0, 0)),
                  pl.BlockSpec((1, D), lambda i, k: (0, 0))],
        out_specs=pl.BlockSpec((tm, D), lambda i, k: (i, 0)),
        out_shape=jax.ShapeDtypeStruct((M, D), F32),
        compiler_params=_params("arbitrary", "arbitrary"),
        name="ffn_out",
    )(act, w_down, x1, gt.reshape(B, 1, D), g_final.reshape(1, D))


def kernel(x, c, w_ada, b_ada, g_mix, w_in, w_o_sb, w_o_dsa, w_out, rel_bias, g_ffn, w_gate,
           w_up, conv_w, conv_b, w_down, g_final):
    B, S, D = x.shape
    depth = w_ada.shape[0]
    W = w_o_sb.shape[1]
    H = W // HEAD_DIM
    qw = N_IDX_HEADS * IDX_DIM
    assert w_o_dsa.shape[1] == W and D % W == 0 and S % ATTN_BLOCK == 0 and W % qw == 0
    n_sel = min(TOPK_MAX, S // 4)
    scale = HEAD_DIM ** -0.5
    x2 = x.reshape(B * S, D)

    for l in range(depth):
        mod = _adaln(c, w_ada[l], b_ada[l])
        sh1, sc1, gt1, sh2, sc2, gt2 = jnp.split(mod, 6, axis=-1)

        wt = w_in[l].T
        o_kix = 6 * W + qw
        o_wix = o_kix + IDX_DIM
        o_gate = o_wix + N_IDX_HEADS
        k_rows = wt[o_kix:o_wix]
        wt_tail = jnp.concatenate(
            [k_rows, k_rows, wt[o_wix:o_gate],
             jnp.zeros((2 * IDX_DIM - N_IDX_HEADS, D), wt.dtype)], axis=0)

        h1 = _norm_mod(x2, g_mix[l], sc1, sh1, S)
        z = _in_proj_wt(h1, wt, (1, 4), W, (), "in_proj")
        zt = _in_proj_wt(h1, wt, (0, 2, 3, 5, 6), W, ((0, scale), (2, scale * LOG2_E)),
                         "in_proj_t", transposed=True)
        zg = _in_proj_wt(h1, wt[o_gate:], tuple(range(2 * D // W)), W, (), "in_proj_gates")
        zs = _in_proj_wt(h1, wt_tail, (0,), 4 * IDX_DIM, (), "in_proj_idx")

        o_sb = _sb_attn(z, zt, B, S, H, 0, 0, H)
        mask = _dsa_select(zt, zs, B, S, 4 * W // qw, n_sel)
        o_ds = _dsa_attn(z, zt, mask, _bias_tiles(rel_bias, min(S, ATTN_BLOCK)),
                         B, S, H, 2 * H, H, 3 * H)

        x2, h2 = _mix_out(o_sb, o_ds, zg, 0, x2, w_o_sb[l].astype(BF16),
                          w_o_dsa[l].astype(BF16), w_out[l].astype(BF16), gt1, g_ffn[l], sc2, sh2, S)
        last = l == depth - 1
        assert last, "the final rms_norm is fused into the last layer's FFN"
        x2 = _conv_ffn(h2, x2, w_gate, w_up, l, w_down[l].astype(BF16),
                       conv_w[l], conv_b[l], gt2, g_final, S)
    return x2.reshape(B, S, D)
```

```python
import functools
import math

import jax
import jax.numpy as jnp
from jax import lax
from jax.experimental import pallas as pl
from jax.experimental.pallas import tpu as pltpu

HEAD_DIM = 128
N_IDX_HEADS = 16
IDX_DIM = 64
TOPK_MAX = 256
N_BUCKETS = 32
MAX_DISTANCE = 128
CONV_WIDTH = 3
EPS = 1e-6

F32 = jnp.float32
BF16 = jnp.bfloat16
NEG = -1e30
I16 = jnp.int16
INT_MIN = -2 ** 31
I16_MIN = -2 ** 15
EXP_ZERO_BELOW = 104.0
LOG2_E = math.log2(math.e)
V7X_VMEM_LIMIT = 56 * 1024 * 1024
BF16_ROWS = 16
ATTN_BLOCK = 256
SB_HEAD_GROUP = 4
DSA_HEAD_GROUP = 8
SCORE_BLOCKS = 4
COUNT_BLOCKS = 4
COUNT_CHUNK = 64
NT_DIMS = (((1,), (1,)), ((), ()))


def _params(*sem):
    return pltpu.CompilerParams(dimension_semantics=sem, vmem_limit_bytes=V7X_VMEM_LIMIT)


def _sigmoid(x):
    return 1.0 / (1.0 + jnp.exp(-x))


def _adaln_kernel(ct_ref, w_ref, b_ref, o_ref):
    ct = ct_ref[...]
    act = ct * _sigmoid(ct)
    w = w_ref[...]
    for b in range(ct.shape[1]):
        o_ref[b:b + 1, :] = jnp.sum(act[:, b:b + 1] * w, axis=0, keepdims=True) + b_ref[...]


def _adaln(c, w, bias):
    B, D = c.shape
    N = w.shape[1]
    tn = min(N, 1024)
    return pl.pallas_call(
        _adaln_kernel,
        grid=(N // tn,),
        in_specs=[pl.BlockSpec((D, B), lambda j: (0, 0)),
                  pl.BlockSpec((D, tn), lambda j: (0, j)),
                  pl.BlockSpec((1, tn), lambda j: (0, j))],
        out_specs=pl.BlockSpec((B, tn), lambda j: (0, j)),
        out_shape=jax.ShapeDtypeStruct((B, N), F32),
        compiler_params=_params("arbitrary"),
        name="adaln",
    )(c.T, w, bias.reshape(1, N))


def _rms_mod(x, g, sc, sh):
    ms = jnp.mean(x * x, axis=-1, keepdims=True)
    y = x * lax.rsqrt(ms + EPS) * g
    return y * (1.0 + sc) + sh


def _norm_mod_kernel(x_ref, g_ref, sc_ref, sh_ref, o_ref):
    o_ref[...] = _rms_mod(x_ref[...], g_ref[...], sc_ref[0], sh_ref[0]).astype(o_ref.dtype)


def _norm_mod(x2, g, sc, sh, S):
    M, D = x2.shape
    B = sc.shape[0]
    tm = min(S, 1024)
    per_b = S // tm
    return pl.pallas_call(
        _norm_mod_kernel,
        grid=(M // tm,),
        in_specs=[pl.BlockSpec((tm, D), lambda i: (i, 0)),
                  pl.BlockSpec((1, D), lambda i: (0, 0)),
                  pl.BlockSpec((1, 1, D), lambda i: (i // per_b, 0, 0)),
                  pl.BlockSpec((1, 1, D), lambda i: (i // per_b, 0, 0))],
        out_specs=pl.BlockSpec((tm, D), lambda i: (i, 0)),
        out_shape=jax.ShapeDtypeStruct((M, D), BF16),
        compiler_params=_params("arbitrary"),
        name="norm_mod",
    )(x2, g.reshape(1, D), sc.reshape(B, 1, D), sh.reshape(B, 1, D))


def _in_proj_kernel(a_ref, b_ref, o_ref, *, tile_scales):
    acc = jnp.dot(a_ref[...], b_ref[...], preferred_element_type=F32)
    o_ref[...] = (acc * _tile_scale(tile_scales)).astype(o_ref.dtype)


def _in_proj_wt_kernel(a_ref, wt_ref, o_ref, wb_ref, *, tile_scales):
    @pl.when(pl.program_id(1) == 0)
    def _():
        wb_ref[...] = wt_ref[...].T.astype(BF16)

    _in_proj_kernel(a_ref, wb_ref, o_ref, tile_scales=tile_scales)


def _tile_scale(tile_scales):
    j = pl.program_id(0)
    scale = jnp.float32(1.0)
    for tile, tile_scale in tile_scales:
        scale = jnp.where(j == tile, tile_scale, scale)
    return scale


def _in_proj_t_kernel(a_ref, wt_ref, o_ref, wb_ref, *, tile_scales):
    @pl.when(pl.program_id(1) == 0)
    def _():
        wb_ref[...] = wt_ref[...].astype(BF16)

    acc = lax.dot_general(wb_ref[...], a_ref[...], NT_DIMS, preferred_element_type=F32)
    o_ref[...] = (acc * _tile_scale(tile_scales)).astype(o_ref.dtype)


def _tile_index_fn(tiles):
    steps = []
    for j, t in enumerate(tiles):
        while len(steps) < t - j:
            steps.append(j)
    return lambda j: j + sum(jnp.where(j >= first, 1, 0) for first in steps)


def _in_proj_wt(h, wt, tiles, tn, tile_scales, name, transposed=False):
    M, K = h.shape
    tm = min(M, 1024)
    n = len(tiles)
    tile_of = _tile_index_fn(tiles)
    in_specs = [pl.BlockSpec((tm, K), lambda j, i: (i, 0)),
                pl.BlockSpec((tn, K), lambda j, i: (tile_of(j), 0))]
    if transposed:
        body, wb_shape = functools.partial(_in_proj_t_kernel, tile_scales=tile_scales), (tn, K)
        out_spec = pl.BlockSpec((tn, tm), lambda j, i: (j, i))
        out_shape = jax.ShapeDtypeStruct((n * tn, M), BF16)
    else:
        body, wb_shape = functools.partial(_in_proj_wt_kernel, tile_scales=tile_scales), (K, tn)
        out_spec = pl.BlockSpec((tm, tn), lambda j, i: (i, j))
        out_shape = jax.ShapeDtypeStruct((M, n * tn), BF16)
    return pl.pallas_call(
        body,
        grid=(n, M // tm),
        in_specs=in_specs,
        out_specs=out_spec,
        out_shape=out_shape,
        scratch_shapes=[pltpu.VMEM(wb_shape, BF16)],
        compiler_params=_params("arbitrary", "arbitrary"),
        name=name,
    )(h, wt)


def _key_query_iotas(k0, q0, tk, tq):
    key_pos = k0 + lax.broadcasted_iota(jnp.int32, (tk, tq), 0)
    query_pos = q0 + lax.broadcasted_iota(jnp.int32, (tk, tq), 1)
    return key_pos, query_pos


def _head_cols(g):
    return slice(g * HEAD_DIM, (g + 1) * HEAD_DIM)


def _sb_kernel(q_ref, k_ref, vt_ref, o_ref, run_ref, acc_ref, *, tq, tk, n_heads):
    qi = pl.program_id(2)
    q0 = qi * tq
    run_ref[...] = jnp.zeros_like(run_ref)
    acc_ref[...] = jnp.zeros_like(acc_ref)
    later = (lax.broadcasted_iota(jnp.int32, (tk, tk), 1)
             > lax.broadcasted_iota(jnp.int32, (tk, tk), 0)).astype(BF16)

    def step(kbs, first_is_diagonal):
        heads = range(n_heads)
        k0s = [pl.multiple_of(kb * tk, tk) for kb in kbs]
        causal = None
        if first_is_diagonal:
            key_pos, query_pos = _key_query_iotas(k0s[0], q0, tk, tq)
            causal = key_pos < query_pos
        masked = [first_is_diagonal and j == 0 for j in range(len(kbs))]
        zs = [[jnp.dot(k_ref[pl.ds(k0, tk), _head_cols(g)], q_ref[_head_cols(g), :],
                       preferred_element_type=F32) for g in heads] for k0 in k0s]
        sps = [[jnp.maximum(z, 0.0) + jnp.log(1.0 + jnp.exp(-jnp.abs(z))) for z in zj] for zj in zs]
        spms = [[jnp.where(causal, sp, 0.0) if masked[j] else sp for sp in spj]
                for j, spj in enumerate(sps)]
        his = [[spm.astype(BF16) for spm in spj] for spj in spms]
        los = [[(spm - hi.astype(F32)).astype(BF16) for spm, hi in zip(spj, hij)]
               for spj, hij in zip(spms, his)]
        suffixes = [[jnp.dot(later, hi, preferred_element_type=F32)
                     + jnp.dot(later, lo, preferred_element_type=F32) for hi, lo in zip(hij, loj)]
                    for hij, loj in zip(his, los)]
        weights, new_runs = [], []
        for g in heads:
            run = run_ref[g]
            head_weights = []
            for j in range(len(kbs)):
                a = jnp.exp(zs[j][g] - sps[j][g] - suffixes[j][g] - run)
                if masked[j]:
                    a = jnp.where(causal, a, 0.0)
                head_weights.append(a.astype(BF16))
                run = run + jnp.sum(spms[j][g], axis=0, keepdims=True)
            weights.append(head_weights)
            new_runs.append(run)
        min_run = None
        for g in heads:
            acc = acc_ref[g]
            for j, k0 in enumerate(k0s):
                acc = acc + jnp.dot(vt_ref[_head_cols(g), pl.ds(k0, tk)], weights[g][j],
                                    preferred_element_type=F32)
            acc_ref[g] = acc
            run_ref[g] = new_runs[g]
            head_min = jnp.min(new_runs[g])
            min_run = head_min if min_run is None else jnp.minimum(min_run, head_min)
        return min_run > EXP_ZERO_BELOW

    def body(carry):
        kb, _ = carry
        return kb - 1, step([kb], False)

    def cond(carry):
        kb, dead = carry
        return jnp.logical_and(kb >= 0, jnp.logical_not(dead))

    @pl.when(qi == 0)
    def _():
        step([qi], True)

    @pl.when(qi >= 1)
    def _():
        dead = step([qi, qi - 1], True)
        lax.while_loop(cond, body, (qi - 2, dead))

    for g in range(n_heads):
        o_ref[:, _head_cols(g)] = acc_ref[g].T.astype(o_ref.dtype)


def _attn_group_specs(S, tq, nq, G, q_blk, k_blk, v_blk, kv_buffers):
    assert q_blk % G == 0 and k_blk % G == 0 and v_blk % G == 0
    gw = G * HEAD_DIM
    mode = pl.Buffered(kv_buffers)
    return [pl.BlockSpec((gw, tq), lambda b, hg, i: (q_blk // G + hg, b * nq + i)),
            pl.BlockSpec((S, gw), lambda b, hg, i: (b, k_blk // G + hg), pipeline_mode=mode),
            pl.BlockSpec((gw, S), lambda b, hg, i: (v_blk // G + hg, b), pipeline_mode=mode)]


def _sb_attn(z, zt, B, S, H, q_blk, k_blk, v_blk):
    M = z.shape[0]
    tq = tk = min(S, ATTN_BLOCK)
    nq = S // tq
    G = min(H, SB_HEAD_GROUP)
    gw = G * HEAD_DIM
    return pl.pallas_call(
        functools.partial(_sb_kernel, tq=tq, tk=tk, n_heads=G),
        grid=(B, H // G, nq),
        in_specs=_attn_group_specs(S, tq, nq, G, q_blk, k_blk, v_blk, kv_buffers=2),
        out_specs=pl.BlockSpec((tq, gw), lambda b, hg, i: (b * nq + i, hg)),
        out_shape=jax.ShapeDtypeStruct((M, H * HEAD_DIM), BF16),
        scratch_shapes=[pltpu.VMEM((G, 1, tq), F32), pltpu.VMEM((G, HEAD_DIM, tq), F32)],
        compiler_params=_params("arbitrary", "arbitrary", "arbitrary"),
        name="sb_attn",
    )(zt, z, zt)


def _dsa_select_kernel(q_ref, k_ref, w_ref, mask_ref, hi_ref, lo_ref, *, tq, tk, n_sel, n_kblocks):
    qi = pl.program_id(1)
    q0 = qi * tq
    nkb = (q0 + tq + tk - 1) // tk
    lane = lax.broadcasted_iota(jnp.int32, (tk, 2 * IDX_DIM), 1)
    w_scale = (N_IDX_HEADS ** -0.5) * (IDX_DIM ** -0.5)
    wt = (w_ref[...].astype(F32) * w_scale).T

    def score_block(kb, diagonal):
        k0 = pl.multiple_of(kb * tk, tk)
        kk = k_ref[pl.ds(k0, tk), :]
        k_halves = (jnp.where(lane < IDX_DIM, kk, jnp.zeros_like(kk)),
                    jnp.where(lane >= IDX_DIM, kk, jnp.zeros_like(kk)))
        acc = jnp.zeros((tk, tq), F32)
        for p in range(N_IDX_HEADS // 2):
            q2t = q_ref[p * 2 * IDX_DIM:(p + 1) * 2 * IDX_DIM, :]
            for half in range(2):
                h = 2 * p + half
                ph = jnp.dot(k_halves[half], q2t, preferred_element_type=F32)
                acc = acc + jnp.maximum(ph, 0.0) * wt[h:h + 1, :]
        bits = lax.bitcast_convert_type(acc, jnp.int32)
        key = bits ^ ((bits >> 31) & 0x7FFFFFFF)
        if diagonal:
            key_pos, query_pos = _key_query_iotas(k0, q0, tk, tq)
            key = jnp.where(key_pos <= query_pos, key, INT_MIN)
        hi_ref[pl.ds(k0, tk), :] = (key >> 16).astype(I16)
        lo_ref[pl.ds(k0, tk), :] = (key ^ 0x8000).astype(I16)

    def score_group(i, carry):
        for j in range(SCORE_BLOCKS):
            score_block(SCORE_BLOCKS * i + j, False)
        return carry

    n_before = nkb - 1
    lax.fori_loop(0, n_before // SCORE_BLOCKS, score_group, 0)
    first_left = n_before - n_before % SCORE_BLOCKS
    for left in range(1, SCORE_BLOCKS):
        @pl.when(n_before % SCORE_BLOCKS == left)
        def _(left=left):
            for j in range(left):
                score_block(first_left + j, False)

    score_block(nkb - 1, True)

    n_count = (nkb + COUNT_BLOCKS - 1) // COUNT_BLOCKS
    rows = COUNT_BLOCKS * tk

    def pad_block(kb, carry):
        k0 = pl.multiple_of(kb * tk, tk)
        hi_ref[pl.ds(k0, tk), :] = jnp.full((tk, tq), I16_MIN, I16)
        lo_ref[pl.ds(k0, tk), :] = jnp.full((tk, tq), I16_MIN, I16)
        return carry

    lax.fori_loop(nkb, n_count * COUNT_BLOCKS, pad_block, 0)

    def count(flags_fn):
        def blk(i, cnt):
            r0 = pl.multiple_of(i * rows, rows)
            groups = []
            for c in range(rows // COUNT_CHUNK):
                flags = flags_fn(r0 + c * COUNT_CHUNK, COUNT_CHUNK)
                parts = [flags[r * BF16_ROWS:(r + 1) * BF16_ROWS, :]
                         for r in range(COUNT_CHUNK // BF16_ROWS)]
                while len(parts) > 1:
                    parts = [a + b for a, b in zip(parts[0::2], parts[1::2])]
                groups.append(parts[0])
            while len(groups) > 1:
                groups = [a + b for a, b in zip(groups[0::2], groups[1::2])]
            return cnt + groups[0]
        cnt = lax.fori_loop(0, n_count, blk, jnp.zeros((BF16_ROWS, tq), I16))
        return jnp.sum(cnt.astype(jnp.int32), axis=0, keepdims=True)

    flag, no_flag = I16(1), I16(0)

    def count_ge(ref, thr16):
        return count(lambda r, n: jnp.where(ref[pl.ds(r, n), :] >= thr16, flag, no_flag))

    def kth_largest(ref, k):
        def bisect(it, carry):
            thr, n_ge, n_gt = carry
            cand = thr + jnp.left_shift(jnp.int32(1), 15 - it)
            cnt = count_ge(ref, cand.astype(I16))
            ok = cnt >= k
            return jnp.where(ok, cand, thr), jnp.where(ok, cnt, n_ge), jnp.where(ok, n_gt, cnt)
        zeros = jnp.zeros((1, tq), jnp.int32)
        thr, n_ge, n_gt = lax.fori_loop(0, 16, bisect, (zeros + I16_MIN, zeros, zeros))
        return thr.astype(I16), n_ge, n_gt

    thr_hi, _, above_hi = kth_largest(hi_ref, n_sel)
    need_lo = n_sel - above_hi

    def park(kb, carry):
        k0 = pl.multiple_of(kb * tk, tk)
        lo_ref[pl.ds(k0, tk), :] = jnp.where(hi_ref[pl.ds(k0, tk), :] == thr_hi,
                                             lo_ref[pl.ds(k0, tk), :], I16(I16_MIN))
        return carry

    lax.fori_loop(0, nkb, park, 0)
    thr_lo, at_least_lo, above_lo = kth_largest(lo_ref, need_lo)
    need_ties = need_lo - above_lo
    n_ties = jnp.where(thr_lo == I16(I16_MIN), tk * n_kblocks, at_least_lo - above_lo)
    all_ties_fit = jnp.max(n_ties - need_ties) <= 0

    one, zero, neg = BF16(1.0), BF16(0.0), BF16(NEG)

    def store_mask(kb, sel, diagonal):
        k0 = pl.multiple_of(kb * tk, tk)
        if diagonal:
            key_pos, query_pos = _key_query_iotas(k0, q0, tk, tq)
            sel = sel * (key_pos <= query_pos).astype(BF16)
        mask_ref[0, pl.ds(k0, tk), :] = jnp.where(sel > zero, zero, neg)

    def emit_all_ties(kb, diagonal):
        k0 = pl.multiple_of(kb * tk, tk)
        hi = hi_ref[pl.ds(k0, tk), :]
        lo = lo_ref[pl.ds(k0, tk), :]
        store_mask(kb, jnp.where(hi > thr_hi, one,
                                 jnp.where(hi == thr_hi, jnp.where(lo >= thr_lo, one, zero), zero)),
                   diagonal)

    earlier = (lax.broadcasted_iota(jnp.int32, (tk, tk), 1)
               < lax.broadcasted_iota(jnp.int32, (tk, tk), 0)).astype(BF16)
    ones_rows = jnp.ones((BF16_ROWS, tk), BF16)
    need_ties_f = need_ties.astype(F32)

    def emit_ranked_ties(kb, ties_seen, diagonal):
        k0 = pl.multiple_of(kb * tk, tk)
        hi = hi_ref[pl.ds(k0, tk), :]
        lo = lo_ref[pl.ds(k0, tk), :]
        above = jnp.where(hi > thr_hi, one, jnp.where(lo > thr_lo, one, zero))
        eq = jnp.where(hi == thr_hi, jnp.where(lo == thr_lo, one, zero), zero)
        rank = jnp.dot(earlier, eq, preferred_element_type=F32).astype(BF16)
        room = jnp.clip(need_ties_f - ties_seen, -1.0, float(tk)).astype(BF16)
        store_mask(kb, jnp.where(rank < room, jnp.maximum(above, eq), above), diagonal)
        return ties_seen + jnp.dot(ones_rows, eq, preferred_element_type=F32)[0:1]

    def emit_fast():
        def body(kb, carry):
            emit_all_ties(kb, False)
            return carry
        lax.fori_loop(0, nkb - 1, body, 0)
        emit_all_ties(nkb - 1, True)

    def emit_slow():
        ties_seen = lax.fori_loop(0, nkb - 1, lambda kb, seen: emit_ranked_ties(kb, seen, False),
                                  jnp.zeros((1, tq), F32))
        emit_ranked_ties(nkb - 1, ties_seen, True)

    lax.cond(all_ties_fit, emit_fast, emit_slow)

    def fill(kb, carry):
        k0 = pl.multiple_of(kb * tk, tk)
        mask_ref[0, pl.ds(k0, tk), :] = jnp.full((tk, tq), NEG, mask_ref.dtype)
        return carry

    lax.fori_loop(nkb, n_kblocks, fill, 0)


def _dsa_select(zt, zs, B, S, qix_blk, n_sel):
    tq = tk = min(S, ATTN_BLOCK)
    nq = S // tq
    assert nq % COUNT_BLOCKS == 0
    qw = N_IDX_HEADS * IDX_DIM
    return pl.pallas_call(
        functools.partial(_dsa_select_kernel, tq=tq, tk=tk, n_sel=n_sel, n_kblocks=S // tk),
        grid=(B, nq),
        in_specs=[pl.BlockSpec((qw, tq), lambda b, i: (qix_blk, b * nq + i)),
                  pl.BlockSpec((S, 2 * IDX_DIM), lambda b, i: (b, 0)),
                  pl.BlockSpec((tq, 2 * IDX_DIM), lambda b, i: (b * nq + i, 1))],
        out_specs=pl.BlockSpec((1, S, tq), lambda b, i: (b * nq + i, 0, 0)),
        out_shape=jax.ShapeDtypeStruct((B * nq, S, tq), BF16),
        scratch_shapes=[pltpu.VMEM((S, tq), I16), pltpu.VMEM((S, tq), I16)],
        compiler_params=_params("arbitrary", "arbitrary"),
        name="dsa_select",
    )(zt, zs, zs)


def _rel_bucket(dist):
    n = jnp.maximum(dist, 0)
    max_exact = N_BUCKETS // 2
    nf = jnp.maximum(n, 1).astype(F32)
    large = max_exact + (jnp.log(nf / max_exact) / math.log(MAX_DISTANCE / max_exact)
                         * (N_BUCKETS - max_exact)).astype(jnp.int32)
    large = jnp.minimum(large, N_BUCKETS - 1)
    return jnp.where(n < max_exact, n, large)


def _bias_tiles(rel_bias, tq):
    assert tq >= MAX_DISTANCE
    rb = rel_bias.astype(F32)
    key = jnp.arange(tq)[:, None]
    query = jnp.arange(tq)[None, :]
    bucket = _rel_bucket(jnp.stack([query - key, tq + query - key]))
    onehot = (bucket[None] == jnp.arange(N_BUCKETS)[:, None, None, None]).astype(F32)
    return jnp.einsum("nh,nikq->hikq", (rb - rb[N_BUCKETS - 1]) * LOG2_E, onehot,
                      precision=lax.Precision.HIGHEST)


def _dsa_attn_kernel(q_ref, k_ref, vt_ref, mask_ref, bias_ref, o_ref, m_ref, acc_ref, logit_ref, *,
                     tq, n_heads):
    qi = pl.program_id(2)
    m_ref[...] = jnp.full_like(m_ref, NEG)
    acc_ref[...] = jnp.zeros_like(acc_ref)
    ones_rows = jnp.ones((BF16_ROWS, tq), BF16)

    def step(blocks):
        k0s = [pl.multiple_of(kb * tq, tq) for kb, _ in blocks]
        masks = [mask_ref[0, pl.ds(k0, tq), :].astype(F32) for k0 in k0s]

        def head_scores(g):
            s = jnp.dot(k_ref[pl.ds(k0s[0], len(blocks) * tq), _head_cols(g)], q_ref[_head_cols(g), :],
                        preferred_element_type=F32)
            return [s[j * tq:(j + 1) * tq] for j in range(len(blocks))]

        def head_logits(g, scores):
            m_new = m_ref[g]
            for j, (_, bias_idx) in enumerate(blocks):
                s = scores[j] + masks[j]
                if bias_idx is not None:
                    s = s + bias_ref[g, bias_idx]
                logit_ref[g, j] = s
                m_new = jnp.maximum(m_new, jnp.max(s, axis=0, keepdims=True))
            return m_new

        def head_accumulate(g, m_new):
            acc = jnp.exp2(m_ref[g] - m_new) * acc_ref[g]
            for j, k0 in enumerate(k0s):
                p = jnp.exp2(logit_ref[g, j] - m_new).astype(BF16)
                vt = jnp.concatenate([vt_ref[_head_cols(g), pl.ds(k0, tq)], ones_rows], axis=0)
                acc = acc + jnp.dot(vt, p, preferred_element_type=F32)
            acc_ref[g] = acc
            m_ref[g] = m_new

        scores = [head_scores(g) for g in range(n_heads)]
        maxima = [head_logits(g, scores[g]) for g in range(n_heads)]
        for g in range(n_heads):
            head_accumulate(g, maxima[g])

    n_far = jnp.maximum(qi - 1, 0)

    def far_pair(i, carry):
        step([(2 * i, None), (2 * i + 1, None)])
        return carry

    lax.fori_loop(0, n_far // 2, far_pair, 0)

    @pl.when(n_far % 2 == 1)
    def _():
        step([(n_far - 1, None)])

    @pl.when(qi >= 1)
    def _():
        step([(qi - 1, 1), (qi, 0)])

    @pl.when(qi == 0)
    def _():
        step([(qi, 0)])
    for g in range(n_heads):
        acc = acc_ref[g]
        out_t = acc[:HEAD_DIM] / acc[HEAD_DIM:HEAD_DIM + 1]
        o_ref[:, _head_cols(g)] = out_t.T.astype(o_ref.dtype)


def _dsa_attn(z, zt, mask, bias_tiles, B, S, H, q_blk, k_blk, v_blk):
    M = z.shape[0]
    tq = bias_tiles.shape[-1]
    nq = S // tq
    G = min(H, DSA_HEAD_GROUP)
    gw = G * HEAD_DIM
    return pl.pallas_call(
        functools.partial(_dsa_attn_kernel, tq=tq, n_heads=G),
        grid=(B, H // G, nq),
        in_specs=_attn_group_specs(S, tq, nq, G, q_blk, k_blk, v_blk, kv_buffers=1) + [
            pl.BlockSpec((1, S, tq), lambda b, hg, i: (b * nq + i, 0, 0)),
            pl.BlockSpec((G, 2, tq, tq), lambda b, hg, i: (hg, 0, 0, 0),
                         pipeline_mode=pl.Buffered(1))],
        out_specs=pl.BlockSpec((tq, gw), lambda b, hg, i: (b * nq + i, hg)),
        out_shape=jax.ShapeDtypeStruct((M, H * HEAD_DIM), BF16),
        scratch_shapes=[pltpu.VMEM((G, 1, tq), F32),
                        pltpu.VMEM((G, HEAD_DIM + BF16_ROWS, tq), F32),
                        pltpu.VMEM((G, 2, tq, tq), F32)],
        compiler_params=_params("arbitrary", "arbitrary", "arbitrary"),
        name="dsa_attn",
    )(zt, z, zt, mask, bias_tiles)


def _mix_out_kernel(osb_ref, ods_ref, gsb_ref, gds_ref, x_ref, wsb_ref, wds_ref, wout_ref,
                    gt_ref, g_ref, sc_ref, sh_ref, x1_ref, h2_ref):
    t_sb = jnp.dot(osb_ref[...], wsb_ref[...], preferred_element_type=F32)
    t_ds = jnp.dot(ods_ref[...], wds_ref[...], preferred_element_type=F32)
    merged = (_sigmoid(gsb_ref[...].astype(F32)) * t_sb
              + _sigmoid(gds_ref[...].astype(F32)) * t_ds)
    y = jnp.dot(merged.astype(BF16), wout_ref[...], preferred_element_type=F32)
    x1 = x_ref[...] + gt_ref[0] * y
    x1_ref[...] = x1
    h2_ref[...] = _rms_mod(x1, g_ref[...], sc_ref[0], sh_ref[0]).astype(h2_ref.dtype)


def _mix_out(o_sb, o_ds, z, gate_blk, x2, w_sb, w_ds, w_out, gt, g, sc, sh, S):
    M, D = x2.shape
    B = gt.shape[0]
    W = o_sb.shape[1]
    tm = min(S, 512)
    per_b = S // tm
    row = lambda i: (i, 0)
    const = lambda i: (0, 0)
    per_batch = lambda i: (i // per_b, 0, 0)
    return pl.pallas_call(
        _mix_out_kernel,
        grid=(M // tm,),
        in_specs=[pl.BlockSpec((tm, W), row),
                  pl.BlockSpec((tm, W), row),
                  pl.BlockSpec((tm, D), lambda i: (i, gate_blk)),
                  pl.BlockSpec((tm, D), lambda i: (i, gate_blk + 1)),
                  pl.BlockSpec((tm, D), row),
                  pl.BlockSpec((W, D), const),
                  pl.BlockSpec((W, D), const),
                  pl.BlockSpec((D, D), const),
                  pl.BlockSpec((1, 1, D), per_batch),
                  pl.BlockSpec((1, D), const),
                  pl.BlockSpec((1, 1, D), per_batch),
                  pl.BlockSpec((1, 1, D), per_batch)],
        out_specs=[pl.BlockSpec((tm, D), row), pl.BlockSpec((tm, D), row)],
        out_shape=[jax.ShapeDtypeStruct((M, D), F32), jax.ShapeDtypeStruct((M, D), BF16)],
        compiler_params=_params("arbitrary"),
        name="mix_out",
    )(o_sb, o_ds, z, z, x2, w_sb, w_ds, w_out, gt.reshape(B, 1, D), g.reshape(1, D),
      sc.reshape(B, 1, D), sh.reshape(B, 1, D))


HALO = BF16_ROWS


def _ffn_act_kernel(h_ref, halo_ref, wg_ref, wu_ref, cw_ref, cb_ref, o_ref, wgb_ref, wub_ref, *,
                    tiles_per_seq):
    i = pl.program_id(1)

    @pl.when(i == 0)
    def _():
        wgb_ref[...] = wg_ref[...].astype(BF16)
        wub_ref[...] = wu_ref[...].astype(BF16)

    h = h_ref[...]
    wg = wgb_ref[...]
    g0 = jnp.dot(h, wg, preferred_element_type=F32)
    g_prev = jnp.dot(halo_ref[...], wg, preferred_element_type=F32)
    g_prev = jnp.where(i % tiles_per_seq == 0, 0.0, g_prev)
    ridx = lax.broadcasted_iota(jnp.int32, g0.shape, 0)
    g1 = jnp.where(ridx == 0, g_prev[HALO - 1:HALO, :], pltpu.roll(g0, 1, 0))
    g2 = jnp.where(ridx == 0, g_prev[HALO - 2:HALO - 1, :],
                   jnp.where(ridx == 1, g_prev[HALO - 1:HALO, :], pltpu.roll(g0, 2, 0)))
    cw = cw_ref[...]
    a = cb_ref[...] + g2 * cw[0:1, :] + g1 * cw[1:2, :] + g0 * cw[2:3, :]
    u = jnp.dot(h, wub_ref[...], preferred_element_type=F32)
    o_ref[...] = (a * _sigmoid(a) * u).astype(o_ref.dtype)


def _ffn_out_kernel(a_ref, wd_ref, x1_ref, gt_ref, gf_ref, o_ref):
    k = pl.program_id(1)

    @pl.when(k == 0)
    def _():
        o_ref[...] = jnp.zeros_like(o_ref)

    o_ref[...] += jnp.dot(a_ref[...], wd_ref[...], preferred_element_type=F32)

    @pl.when(k == pl.num_programs(1) - 1)
    def _():
        x = x1_ref[...] + gt_ref[0] * o_ref[...]
        ms = jnp.mean(x * x, axis=-1, keepdims=True)
        o_ref[...] = x * lax.rsqrt(ms + EPS) * gf_ref[...]


def _conv_ffn(h2, x1, w_gate, w_up, layer, w_down, conv_w, conv_b, gt, g_final, S):
    M, D = x1.shape
    B = gt.shape[0]
    F = w_gate.shape[2]
    tm = min(S, 1024)
    tf = min(F, 512)
    per_b = S // tm
    halo_per_tile = tm // HALO
    act = pl.pallas_call(
        functools.partial(_ffn_act_kernel, tiles_per_seq=per_b),
        grid=(F // tf, M // tm),
        in_specs=[pl.BlockSpec((tm, D), lambda f, i: (i, 0)),
                  pl.BlockSpec((HALO, D), lambda f, i: (jnp.maximum(i * halo_per_tile - 1, 0), 0)),
                  pl.BlockSpec((None, D, tf), lambda f, i: (layer, 0, f)),
                  pl.BlockSpec((None, D, tf), lambda f, i: (layer, 0, f)),
                  pl.BlockSpec((CONV_WIDTH, tf), lambda f, i: (0, f)),
                  pl.BlockSpec((1, tf), lambda f, i: (0, f))],
        out_specs=pl.BlockSpec((tm, tf), lambda f, i: (i, f)),
        out_shape=jax.ShapeDtypeStruct((M, F), BF16),
        scratch_shapes=[pltpu.VMEM((D, tf), BF16), pltpu.VMEM((D, tf), BF16)],
        compiler_params=_params("arbitrary", "arbitrary"),
        name="ffn_act",
    )(h2, h2, w_gate, w_up, conv_w, conv_b.reshape(1, F))
    return pl.pallas_call(
        _ffn_out_kernel,
        grid=(M // tm, F // tf),
        in_specs=[pl.BlockSpec((tm, tf), lambda i, k: (i, k)),
                  pl.BlockSpec((tf, D), lambda i, k: (k, 0)),
                  pl.BlockSpec((tm, D), lambda i, k: (i, 0)),
                  pl.BlockSpec((1, 1, D), lambda i, k: (i // per_b, 0, 0)),
                  pl.BlockSpec((1, D), lambda i, k: (0, 0))],
        out_specs=pl.BlockSpec((tm, D), lambda i, k: (i, 0)),
        out_shape=jax.ShapeDtypeStruct((M, D), F32),
        compiler_params=_params("arbitrary", "arbitrary"),
        name="ffn_out",
    )(act, w_down, x1, gt.reshape(B, 1, D), g_final.reshape(1, D))


def kernel(x, c, w_ada, b_ada, g_mix, w_in, w_o_sb, w_o_dsa, w_out, rel_bias, g_ffn, w_gate,
           w_up, conv_w, conv_b, w_down, g_final):
    B, S, D = x.shape
    depth = w_ada.shape[0]
    W = w_o_sb.shape[1]
    H = W // HEAD_DIM
    qw = N_IDX_HEADS * IDX_DIM
    assert w_o_dsa.shape[1] == W and D % W == 0 and S % ATTN_BLOCK == 0 and W % qw == 0
    n_sel = min(TOPK_MAX, S // 4)
    scale = HEAD_DIM ** -0.5
    x2 = x.reshape(B * S, D)

    for l in range(depth):
        mod = _adaln(c, w_ada[l], b_ada[l])
        sh1, sc1, gt1, sh2, sc2, gt2 = jnp.split(mod, 6, axis=-1)

        wt = w_in[l].T
        o_kix = 6 * W + qw
        o_wix = o_kix + IDX_DIM
        o_gate = o_wix + N_IDX_HEADS
        k_rows = wt[o_kix:o_wix]
        wt_tail = jnp.concatenate(
            [k_rows, k_rows, wt[o_wix:o_gate],
             jnp.zeros((2 * IDX_DIM - N_IDX_HEADS, D), wt.dtype)], axis=0)

        h1 = _norm_mod(x2, g_mix[l], sc1, sh1, S)
        z = _in_proj_wt(h1, wt, (1, 4), W, (), "in_proj")
        zt = _in_proj_wt(h1, wt, (0, 2, 3, 5, 6), W, ((0, scale), (2, scale * LOG2_E)),
                         "in_proj_t", transposed=True)
        zg = _in_proj_wt(h1, wt[o_gate:], tuple(range(2 * D // W)), W, (), "in_proj_gates")
        zs = _in_proj_wt(h1, wt_tail, (0,), 4 * IDX_DIM, (), "in_proj_idx")

        o_sb = _sb_attn(z, zt, B, S, H, 0, 0, H)
        mask = _dsa_select(zt, zs, B, S, 4 * W // qw, n_sel)
        o_ds = _dsa_attn(z, zt, mask, _bias_tiles(rel_bias, min(S, ATTN_BLOCK)),
                         B, S, H, 2 * H, H, 3 * H)

        x2, h2 = _mix_out(o_sb, o_ds, zg, 0, x2, w_o_sb[l].astype(BF16),
                          w_o_dsa[l].astype(BF16), w_out[l].astype(BF16), gt1, g_ffn[l], sc2, sh2, S)
        last = l == depth - 1
        assert last, "the final rms_norm is fused into the last layer's FFN"
        x2 = _conv_ffn(h2, x2, w_gate, w_up, l, w_down[l].astype(BF16),
                       conv_w[l], conv_b[l], gt2, g_final, S)
    return x2.reshape(B, S, D)
```

```python
import functools
import math

import jax
import jax.numpy as jnp
from jax import lax
from jax.experimental import pallas as pl
from jax.experimental.pallas import tpu as pltpu

HEAD_DIM = 128
N_IDX_HEADS = 16
IDX_DIM = 64
TOPK_MAX = 256
N_BUCKETS = 32
MAX_DISTANCE = 128
CONV_WIDTH = 3
EPS = 1e-6

F32 = jnp.float32
BF16 = jnp.bfloat16
NEG = -1e30
I16 = jnp.int16
INT_MIN = -2 ** 31
I16_MIN = -2 ** 15
EXP_ZERO_BELOW = 104.0
LOG2_E = math.log2(math.e)
V7X_VMEM_LIMIT = 56 * 1024 * 1024
BF16_ROWS = 16
ATTN_BLOCK = 256
SB_HEAD_GROUP = 4
DSA_HEAD_GROUP = 8
SCORE_BLOCKS = 4
SCORE_SPAN = 2
COUNT_BLOCKS = 4
COUNT_CHUNK = 64
NT_DIMS = (((1,), (1,)), ((), ()))


def _params(*sem):
    return pltpu.CompilerParams(dimension_semantics=sem, vmem_limit_bytes=V7X_VMEM_LIMIT)


def _sigmoid(x):
    return 1.0 / (1.0 + jnp.exp(-x))


def _adaln_kernel(ct_ref, w_ref, b_ref, o_ref):
    ct = ct_ref[...]
    act = ct * _sigmoid(ct)
    w = w_ref[...]
    for b in range(ct.shape[1]):
        o_ref[b:b + 1, :] = jnp.sum(act[:, b:b + 1] * w, axis=0, keepdims=True) + b_ref[...]


def _adaln(c, w, bias):
    B, D = c.shape
    N = w.shape[1]
    tn = min(N, 1024)
    return pl.pallas_call(
        _adaln_kernel,
        grid=(N // tn,),
        in_specs=[pl.BlockSpec((D, B), lambda j: (0, 0)),
                  pl.BlockSpec((D, tn), lambda j: (0, j)),
                  pl.BlockSpec((1, tn), lambda j: (0, j))],
        out_specs=pl.BlockSpec((B, tn), lambda j: (0, j)),
        out_shape=jax.ShapeDtypeStruct((B, N), F32),
        compiler_params=_params("arbitrary"),
        name="adaln",
    )(c.T, w, bias.reshape(1, N))


def _rms_mod(x, g, sc, sh):
    ms = jnp.mean(x * x, axis=-1, keepdims=True)
    y = x * lax.rsqrt(ms + EPS) * g
    return y * (1.0 + sc) + sh


def _norm_mod_kernel(x_ref, g_ref, sc_ref, sh_ref, o_ref):
    o_ref[...] = _rms_mod(x_ref[...], g_ref[...], sc_ref[0], sh_ref[0]).astype(o_ref.dtype)


def _norm_mod(x2, g, sc, sh, S):
    M, D = x2.shape
    B = sc.shape[0]
    tm = min(S, 1024)
    per_b = S // tm
    return pl.pallas_call(
        _norm_mod_kernel,
        grid=(M // tm,),
        in_specs=[pl.BlockSpec((tm, D), lambda i: (i, 0)),
                  pl.BlockSpec((1, D), lambda i: (0, 0)),
                  pl.BlockSpec((1, 1, D), lambda i: (i // per_b, 0, 0)),
                  pl.BlockSpec((1, 1, D), lambda i: (i // per_b, 0, 0))],
        out_specs=pl.BlockSpec((tm, D), lambda i: (i, 0)),
        out_shape=jax.ShapeDtypeStruct((M, D), BF16),
        compiler_params=_params("arbitrary"),
        name="norm_mod",
    )(x2, g.reshape(1, D), sc.reshape(B, 1, D), sh.reshape(B, 1, D))


def _in_proj_kernel(a_ref, b_ref, o_ref, *, tile_scales):
    acc = jnp.dot(a_ref[...], b_ref[...], preferred_element_type=F32)
    o_ref[...] = (acc * _tile_scale(tile_scales)).astype(o_ref.dtype)


def _in_proj_wt_kernel(a_ref, wt_ref, o_ref, wb_ref, *, tile_scales):
    @pl.when(pl.program_id(1) == 0)
    def _():
        wb_ref[...] = wt_ref[...].T.astype(BF16)

    _in_proj_kernel(a_ref, wb_ref, o_ref, tile_scales=tile_scales)


def _tile_scale(tile_scales):
    j = pl.program_id(0)
    scale = jnp.float32(1.0)
    for tile, tile_scale in tile_scales:
        scale = jnp.where(j == tile, tile_scale, scale)
    return scale


def _in_proj_t_kernel(a_ref, wt_ref, o_ref, wb_ref, *, tile_scales):
    @pl.when(pl.program_id(1) == 0)
    def _():
        wb_ref[...] = wt_ref[...].astype(BF16)

    acc = lax.dot_general(wb_ref[...], a_ref[...], NT_DIMS, preferred_element_type=F32)
    o_ref[...] = (acc * _tile_scale(tile_scales)).astype(o_ref.dtype)


def _tile_index_fn(tiles):
    steps = []
    for j, t in enumerate(tiles):
        while len(steps) < t - j:
            steps.append(j)
    return lambda j: j + sum(jnp.where(j >= first, 1, 0) for first in steps)


def _in_proj_wt(h, wt, tiles, tn, tile_scales, name, transposed=False):
    M, K = h.shape
    tm = min(M, 1024)
    n = len(tiles)
    tile_of = _tile_index_fn(tiles)
    in_specs = [pl.BlockSpec((tm, K), lambda j, i: (i, 0)),
                pl.BlockSpec((tn, K), lambda j, i: (tile_of(j), 0))]
    if transposed:
        body, wb_shape = functools.partial(_in_proj_t_kernel, tile_scales=tile_scales), (tn, K)
        out_spec = pl.BlockSpec((tn, tm), lambda j, i: (j, i))
        out_shape = jax.ShapeDtypeStruct((n * tn, M), BF16)
    else:
        body, wb_shape = functools.partial(_in_proj_wt_kernel, tile_scales=tile_scales), (K, tn)
        out_spec = pl.BlockSpec((tm, tn), lambda j, i: (i, j))
        out_shape = jax.ShapeDtypeStruct((M, n * tn), BF16)
    return pl.pallas_call(
        body,
        grid=(n, M // tm),
        in_specs=in_specs,
        out_specs=out_spec,
        out_shape=out_shape,
        scratch_shapes=[pltpu.VMEM(wb_shape, BF16)],
        compiler_params=_params("arbitrary", "arbitrary"),
        name=name,
    )(h, wt)


def _key_query_iotas(k0, q0, tk, tq):
    key_pos = k0 + lax.broadcasted_iota(jnp.int32, (tk, tq), 0)
    query_pos = q0 + lax.broadcasted_iota(jnp.int32, (tk, tq), 1)
    return key_pos, query_pos


def _head_cols(g):
    return slice(g * HEAD_DIM, (g + 1) * HEAD_DIM)


def _sb_kernel(q_ref, k_ref, vt_ref, o_ref, run_ref, acc_ref, *, tq, tk, n_heads):
    qi = pl.program_id(2)
    q0 = qi * tq
    run_ref[...] = jnp.zeros_like(run_ref)
    acc_ref[...] = jnp.zeros_like(acc_ref)
    later = (lax.broadcasted_iota(jnp.int32, (tk, tk), 1)
             > lax.broadcasted_iota(jnp.int32, (tk, tk), 0)).astype(BF16)

    def step(kbs, first_is_diagonal):
        heads = range(n_heads)
        k0s = [pl.multiple_of(kb * tk, tk) for kb in kbs]
        causal = None
        if first_is_diagonal:
            key_pos, query_pos = _key_query_iotas(k0s[0], q0, tk, tq)
            causal = key_pos < query_pos
        masked = [first_is_diagonal and j == 0 for j in range(len(kbs))]
        zs = [[jnp.dot(k_ref[pl.ds(k0, tk), _head_cols(g)], q_ref[_head_cols(g), :],
                       preferred_element_type=F32) for g in heads] for k0 in k0s]
        sps = [[jnp.maximum(z, 0.0) + jnp.log(1.0 + jnp.exp(-jnp.abs(z))) for z in zj] for zj in zs]
        spms = [[jnp.where(causal, sp, 0.0) if masked[j] else sp for sp in spj]
                for j, spj in enumerate(sps)]
        his = [[spm.astype(BF16) for spm in spj] for spj in spms]
        los = [[(spm - hi.astype(F32)).astype(BF16) for spm, hi in zip(spj, hij)]
               for spj, hij in zip(spms, his)]
        suffixes = [[jnp.dot(later, hi, preferred_element_type=F32)
                     + jnp.dot(later, lo, preferred_element_type=F32) for hi, lo in zip(hij, loj)]
                    for hij, loj in zip(his, los)]
        weights, new_runs = [], []
        for g in heads:
            run = run_ref[g]
            head_weights = []
            for j in range(len(kbs)):
                a = jnp.exp(zs[j][g] - sps[j][g] - suffixes[j][g] - run)
                if masked[j]:
                    a = jnp.where(causal, a, 0.0)
                head_weights.append(a.astype(BF16))
                run = run + jnp.sum(spms[j][g], axis=0, keepdims=True)
            weights.append(head_weights)
            new_runs.append(run)
        min_run = None
        for g in heads:
            acc = acc_ref[g]
            for j, k0 in enumerate(k0s):
                acc = acc + jnp.dot(vt_ref[_head_cols(g), pl.ds(k0, tk)], weights[g][j],
                                    preferred_element_type=F32)
            acc_ref[g] = acc
            run_ref[g] = new_runs[g]
            head_min = jnp.min(new_runs[g])
            min_run = head_min if min_run is None else jnp.minimum(min_run, head_min)
        return min_run > EXP_ZERO_BELOW

    def body(carry):
        kb, _ = carry
        return kb - 1, step([kb], False)

    def cond(carry):
        kb, dead = carry
        return jnp.logical_and(kb >= 0, jnp.logical_not(dead))

    @pl.when(qi == 0)
    def _():
        step([qi], True)

    @pl.when(qi >= 1)
    def _():
        dead = step([qi, qi - 1], True)
        lax.while_loop(cond, body, (qi - 2, dead))

    for g in range(n_heads):
        o_ref[:, _head_cols(g)] = acc_ref[g].T.astype(o_ref.dtype)


def _attn_group_specs(S, tq, nq, G, q_blk, k_blk, v_blk, kv_buffers):
    assert q_blk % G == 0 and k_blk % G == 0 and v_blk % G == 0
    gw = G * HEAD_DIM
    mode = pl.Buffered(kv_buffers)
    return [pl.BlockSpec((gw, tq), lambda b, hg, i: (q_blk // G + hg, b * nq + i)),
            pl.BlockSpec((S, gw), lambda b, hg, i: (b, k_blk // G + hg), pipeline_mode=mode),
            pl.BlockSpec((gw, S), lambda b, hg, i: (v_blk // G + hg, b), pipeline_mode=mode)]


def _sb_attn(z, zt, B, S, H, q_blk, k_blk, v_blk):
    M = z.shape[0]
    tq = tk = min(S, ATTN_BLOCK)
    nq = S // tq
    G = min(H, SB_HEAD_GROUP)
    gw = G * HEAD_DIM
    return pl.pallas_call(
        functools.partial(_sb_kernel, tq=tq, tk=tk, n_heads=G),
        grid=(B, H // G, nq),
        in_specs=_attn_group_specs(S, tq, nq, G, q_blk, k_blk, v_blk, kv_buffers=2),
        out_specs=pl.BlockSpec((tq, gw), lambda b, hg, i: (b * nq + i, hg)),
        out_shape=jax.ShapeDtypeStruct((M, H * HEAD_DIM), BF16),
        scratch_shapes=[pltpu.VMEM((G, 1, tq), F32), pltpu.VMEM((G, HEAD_DIM, tq), F32)],
        compiler_params=_params("arbitrary", "arbitrary", "arbitrary"),
        name="sb_attn",
    )(zt, z, zt)


def _dsa_select_kernel(q_ref, k_ref, w_ref, mask_ref, hi_ref, lo_ref, *, tq, tk, n_sel, n_kblocks):
    qi = pl.program_id(1)
    q0 = qi * tq
    nkb = (q0 + tq + tk - 1) // tk
    w_scale = (N_IDX_HEADS ** -0.5) * (IDX_DIM ** -0.5)
    wt = (w_ref[...].astype(F32) * w_scale).T

    def score_rows(kb, n_blocks, diagonal):
        rows = n_blocks * tk
        k0 = pl.multiple_of(kb * tk, tk)
        lane = lax.broadcasted_iota(jnp.int32, (rows, 2 * IDX_DIM), 1)
        kk = k_ref[pl.ds(k0, rows), :]
        k_halves = (jnp.where(lane < IDX_DIM, kk, jnp.zeros_like(kk)),
                    jnp.where(lane >= IDX_DIM, kk, jnp.zeros_like(kk)))
        acc = jnp.zeros((rows, tq), F32)
        for p in range(N_IDX_HEADS // 2):
            q2t = q_ref[p * 2 * IDX_DIM:(p + 1) * 2 * IDX_DIM, :]
            for half in range(2):
                h = 2 * p + half
                ph = jnp.dot(k_halves[half], q2t, preferred_element_type=F32)
                acc = acc + jnp.maximum(ph, 0.0) * wt[h:h + 1, :]
        bits = lax.bitcast_convert_type(acc, jnp.int32)
        key = bits ^ ((bits >> 31) & 0x7FFFFFFF)
        if diagonal:
            key_pos, query_pos = _key_query_iotas(k0, q0, rows, tq)
            key = jnp.where(key_pos <= query_pos, key, INT_MIN)
        hi_ref[pl.ds(k0, rows), :] = (key >> 16).astype(I16)
        lo_ref[pl.ds(k0, rows), :] = (key ^ 0x8000).astype(I16)

    def score_group(i, carry):
        for j in range(0, SCORE_BLOCKS, SCORE_SPAN):
            score_rows(SCORE_BLOCKS * i + j, SCORE_SPAN, False)
        return carry

    n_before = nkb - 1
    lax.fori_loop(0, n_before // SCORE_BLOCKS, score_group, 0)
    first_left = n_before - n_before % SCORE_BLOCKS
    for left in range(1, SCORE_BLOCKS):
        @pl.when(n_before % SCORE_BLOCKS == left)
        def _(left=left):
            for j in range(left):
                score_rows(first_left + j, 1, False)

    score_rows(nkb - 1, 1, True)

    n_count = (nkb + COUNT_BLOCKS - 1) // COUNT_BLOCKS
    rows = COUNT_BLOCKS * tk

    def pad_block(kb, carry):
        k0 = pl.multiple_of(kb * tk, tk)
        hi_ref[pl.ds(k0, tk), :] = jnp.full((tk, tq), I16_MIN, I16)
        lo_ref[pl.ds(k0, tk), :] = jnp.full((tk, tq), I16_MIN, I16)
        return carry

    lax.fori_loop(nkb, n_count * COUNT_BLOCKS, pad_block, 0)

    def count(flags_fn):
        def blk(i, cnt):
            r0 = pl.multiple_of(i * rows, rows)
            groups = []
            for c in range(rows // COUNT_CHUNK):
                flags = flags_fn(r0 + c * COUNT_CHUNK, COUNT_CHUNK)
                parts = [flags[r * BF16_ROWS:(r + 1) * BF16_ROWS, :]
                         for r in range(COUNT_CHUNK // BF16_ROWS)]
                while len(parts) > 1:
                    parts = [a + b for a, b in zip(parts[0::2], parts[1::2])]
                groups.append(parts[0])
            while len(groups) > 1:
                groups = [a + b for a, b in zip(groups[0::2], groups[1::2])]
            return cnt + groups[0]
        cnt = lax.fori_loop(0, n_count, blk, jnp.zeros((BF16_ROWS, tq), I16))
        return jnp.sum(cnt.astype(jnp.int32), axis=0, keepdims=True)

    flag, no_flag = I16(1), I16(0)

    def count_ge(ref, thr16):
        return count(lambda r, n: jnp.where(ref[pl.ds(r, n), :] >= thr16, flag, no_flag))

    def kth_largest(ref, k):
        def bisect(it, carry):
            thr, n_ge, n_gt = carry
            cand = thr + jnp.left_shift(jnp.int32(1), 15 - it)
            cnt = count_ge(ref, cand.astype(I16))
            ok = cnt >= k
            return jnp.where(ok, cand, thr), jnp.where(ok, cnt, n_ge), jnp.where(ok, n_gt, cnt)
        zeros = jnp.zeros((1, tq), jnp.int32)
        thr, n_ge, n_gt = lax.fori_loop(0, 16, bisect, (zeros + I16_MIN, zeros, zeros))
        return thr.astype(I16), n_ge, n_gt

    thr_hi, _, above_hi = kth_largest(hi_ref, n_sel)
    need_lo = n_sel - above_hi

    def park(kb, carry):
        k0 = pl.multiple_of(kb * tk, tk)
        lo_ref[pl.ds(k0, tk), :] = jnp.where(hi_ref[pl.ds(k0, tk), :] == thr_hi,
                                             lo_ref[pl.ds(k0, tk), :], I16(I16_MIN))
        return carry

    lax.fori_loop(0, nkb, park, 0)
    thr_lo, at_least_lo, above_lo = kth_largest(lo_ref, need_lo)
    need_ties = need_lo - above_lo
    n_ties = jnp.where(thr_lo == I16(I16_MIN), tk * n_kblocks, at_least_lo - above_lo)
    all_ties_fit = jnp.max(n_ties - need_ties) <= 0

    one, zero, neg = BF16(1.0), BF16(0.0), BF16(NEG)

    def store_mask(kb, sel, diagonal):
        k0 = pl.multiple_of(kb * tk, tk)
        if diagonal:
            key_pos, query_pos = _key_query_iotas(k0, q0, tk, tq)
            sel = sel * (key_pos <= query_pos).astype(BF16)
        mask_ref[0, pl.ds(k0, tk), :] = jnp.where(sel > zero, zero, neg)

    def emit_all_ties(kb, diagonal):
        k0 = pl.multiple_of(kb * tk, tk)
        hi = hi_ref[pl.ds(k0, tk), :]
        lo = lo_ref[pl.ds(k0, tk), :]
        store_mask(kb, jnp.where(hi > thr_hi, one,
                                 jnp.where(hi == thr_hi, jnp.where(lo >= thr_lo, one, zero), zero)),
                   diagonal)

    earlier = (lax.broadcasted_iota(jnp.int32, (tk, tk), 1)
               < lax.broadcasted_iota(jnp.int32, (tk, tk), 0)).astype(BF16)
    ones_rows = jnp.ones((BF16_ROWS, tk), BF16)
    need_ties_f = need_ties.astype(F32)

    def emit_ranked_ties(kb, ties_seen, diagonal):
        k0 = pl.multiple_of(kb * tk, tk)
        hi = hi_ref[pl.ds(k0, tk), :]
        lo = lo_ref[pl.ds(k0, tk), :]
        above = jnp.where(hi > thr_hi, one, jnp.where(lo > thr_lo, one, zero))
        eq = jnp.where(hi == thr_hi, jnp.where(lo == thr_lo, one, zero), zero)
        rank = jnp.dot(earlier, eq, preferred_element_type=F32).astype(BF16)
        room = jnp.clip(need_ties_f - ties_seen, -1.0, float(tk)).astype(BF16)
        store_mask(kb, jnp.where(rank < room, jnp.maximum(above, eq), above), diagonal)
        return ties_seen + jnp.dot(ones_rows, eq, preferred_element_type=F32)[0:1]

    def emit_fast():
        def body(kb, carry):
            emit_all_ties(kb, False)
            return carry
        lax.fori_loop(0, nkb - 1, body, 0)
        emit_all_ties(nkb - 1, True)

    def emit_slow():
        ties_seen = lax.fori_loop(0, nkb - 1, lambda kb, seen: emit_ranked_ties(kb, seen, False),
                                  jnp.zeros((1, tq), F32))
        emit_ranked_ties(nkb - 1, ties_seen, True)

    lax.cond(all_ties_fit, emit_fast, emit_slow)

    def fill(kb, carry):
        k0 = pl.multiple_of(kb * tk, tk)
        mask_ref[0, pl.ds(k0, tk), :] = jnp.full((tk, tq), NEG, mask_ref.dtype)
        return carry

    lax.fori_loop(nkb, n_kblocks, fill, 0)


def _dsa_select(zt, zs, B, S, qix_blk, n_sel):
    tq = tk = min(S, ATTN_BLOCK)
    nq = S // tq
    assert nq % COUNT_BLOCKS == 0
    qw = N_IDX_HEADS * IDX_DIM
    return pl.pallas_call(
        functools.partial(_dsa_select_kernel, tq=tq, tk=tk, n_sel=n_sel, n_kblocks=S // tk),
        grid=(B, nq),
        in_specs=[pl.BlockSpec((qw, tq), lambda b, i: (qix_blk, b * nq + i)),
                  pl.BlockSpec((S, 2 * IDX_DIM), lambda b, i: (b, 0)),
                  pl.BlockSpec((tq, 2 * IDX_DIM), lambda b, i: (b * nq + i, 1))],
        out_specs=pl.BlockSpec((1, S, tq), lambda b, i: (b * nq + i, 0, 0)),
        out_shape=jax.ShapeDtypeStruct((B * nq, S, tq), BF16),
        scratch_shapes=[pltpu.VMEM((S, tq), I16), pltpu.VMEM((S, tq), I16)],
        compiler_params=_params("arbitrary", "arbitrary"),
        name="dsa_select",
    )(zt, zs, zs)


def _rel_bucket(dist):
    n = jnp.maximum(dist, 0)
    max_exact = N_BUCKETS // 2
    nf = jnp.maximum(n, 1).astype(F32)
    large = max_exact + (jnp.log(nf / max_exact) / math.log(MAX_DISTANCE / max_exact)
                         * (N_BUCKETS - max_exact)).astype(jnp.int32)
    large = jnp.minimum(large, N_BUCKETS - 1)
    return jnp.where(n < max_exact, n, large)


def _bias_tiles(rel_bias, tq):
    assert tq >= MAX_DISTANCE
    rb = rel_bias.astype(F32)
    key = jnp.arange(tq)[:, None]
    query = jnp.arange(tq)[None, :]
    bucket = _rel_bucket(jnp.stack([query - key, tq + query - key]))
    onehot = (bucket[None] == jnp.arange(N_BUCKETS)[:, None, None, None]).astype(F32)
    return jnp.einsum("nh,nikq->hikq", (rb - rb[N_BUCKETS - 1]) * LOG2_E, onehot,
                      precision=lax.Precision.HIGHEST)


def _dsa_attn_kernel(q_ref, k_ref, vt_ref, mask_ref, bias_ref, o_ref, m_ref, acc_ref, logit_ref, *,
                     tq, n_heads):
    qi = pl.program_id(2)
    m_ref[...] = jnp.full_like(m_ref, NEG)
    acc_ref[...] = jnp.zeros_like(acc_ref)
    ones_rows = jnp.ones((BF16_ROWS, tq), BF16)

    def step(blocks):
        k0s = [pl.multiple_of(kb * tq, tq) for kb, _ in blocks]
        masks = [mask_ref[0, pl.ds(k0, tq), :].astype(F32) for k0 in k0s]

        def head_scores(g):
            s = jnp.dot(k_ref[pl.ds(k0s[0], len(blocks) * tq), _head_cols(g)], q_ref[_head_cols(g), :],
                        preferred_element_type=F32)
            return [s[j * tq:(j + 1) * tq] for j in range(len(blocks))]

        def head_logits(g, scores):
            m_new = m_ref[g]
            for j, (_, bias_idx) in enumerate(blocks):
                s = scores[j] + masks[j]
                if bias_idx is not None:
                    s = s + bias_ref[g, bias_idx]
                logit_ref[g, j] = s
                m_new = jnp.maximum(m_new, jnp.max(s, axis=0, keepdims=True))
            return m_new

        def head_accumulate(g, m_new):
            acc = jnp.exp2(m_ref[g] - m_new) * acc_ref[g]
            for j, k0 in enumerate(k0s):
                p = jnp.exp2(logit_ref[g, j] - m_new).astype(BF16)
                vt = jnp.concatenate([vt_ref[_head_cols(g), pl.ds(k0, tq)], ones_rows], axis=0)
                acc = acc + jnp.dot(vt, p, preferred_element_type=F32)
            acc_ref[g] = acc
            m_ref[g] = m_new

        scores = [head_scores(g) for g in range(n_heads)]
        maxima = [head_logits(g, scores[g]) for g in range(n_heads)]
        for g in range(n_heads):
            head_accumulate(g, maxima[g])

    n_far = jnp.maximum(qi - 1, 0)

    def far_pair(i, carry):
        step([(2 * i, None), (2 * i + 1, None)])
        return carry

    lax.fori_loop(0, n_far // 2, far_pair, 0)

    @pl.when(n_far % 2 == 1)
    def _():
        step([(n_far - 1, None)])

    @pl.when(qi >= 1)
    def _():
        step([(qi - 1, 1), (qi, 0)])

    @pl.when(qi == 0)
    def _():
        step([(qi, 0)])
    for g in range(n_heads):
        acc = acc_ref[g]
        out_t = acc[:HEAD_DIM] / acc[HEAD_DIM:HEAD_DIM + 1]
        o_ref[:, _head_cols(g)] = out_t.T.astype(o_ref.dtype)


def _dsa_attn(z, zt, mask, bias_tiles, B, S, H, q_blk, k_blk, v_blk):
    M = z.shape[0]
    tq = bias_tiles.shape[-1]
    nq = S // tq
    G = min(H, DSA_HEAD_GROUP)
    gw = G * HEAD_DIM
    return pl.pallas_call(
        functools.partial(_dsa_attn_kernel, tq=tq, n_heads=G),
        grid=(B, H // G, nq),
        in_specs=_attn_group_specs(S, tq, nq, G, q_blk, k_blk, v_blk, kv_buffers=1) + [
            pl.BlockSpec((1, S, tq), lambda b, hg, i: (b * nq + i, 0, 0)),
            pl.BlockSpec((G, 2, tq, tq), lambda b, hg, i: (hg, 0, 0, 0),
                         pipeline_mode=pl.Buffered(1))],
        out_specs=pl.BlockSpec((tq, gw), lambda b, hg, i: (b * nq + i, hg)),
        out_shape=jax.ShapeDtypeStruct((M, H * HEAD_DIM), BF16),
        scratch_shapes=[pltpu.VMEM((G, 1, tq), F32),
                        pltpu.VMEM((G, HEAD_DIM + BF16_ROWS, tq), F32),
                        pltpu.VMEM((G, 2, tq, tq), F32)],
        compiler_params=_params("arbitrary", "arbitrary", "arbitrary"),
        name="dsa_attn",
    )(zt, z, zt, mask, bias_tiles)


def _mix_out_kernel(osb_ref, ods_ref, gsb_ref, gds_ref, x_ref, wsb_ref, wds_ref, wout_ref,
                    gt_ref, g_ref, sc_ref, sh_ref, x1_ref, h2_ref):
    t_sb = jnp.dot(osb_ref[...], wsb_ref[...], preferred_element_type=F32)
    t_ds = jnp.dot(ods_ref[...], wds_ref[...], preferred_element_type=F32)
    merged = (_sigmoid(gsb_ref[...].astype(F32)) * t_sb
              + _sigmoid(gds_ref[...].astype(F32)) * t_ds)
    y = jnp.dot(merged.astype(BF16), wout_ref[...], preferred_element_type=F32)
    x1 = x_ref[...] + gt_ref[0] * y
    x1_ref[...] = x1
    h2_ref[...] = _rms_mod(x1, g_ref[...], sc_ref[0], sh_ref[0]).astype(h2_ref.dtype)


def _mix_out(o_sb, o_ds, z, gate_blk, x2, w_sb, w_ds, w_out, gt, g, sc, sh, S):
    M, D = x2.shape
    B = gt.shape[0]
    W = o_sb.shape[1]
    tm = min(S, 512)
    per_b = S // tm
    row = lambda i: (i, 0)
    const = lambda i: (0, 0)
    per_batch = lambda i: (i // per_b, 0, 0)
    return pl.pallas_call(
        _mix_out_kernel,
        grid=(M // tm,),
        in_specs=[pl.BlockSpec((tm, W), row),
                  pl.BlockSpec((tm, W), row),
                  pl.BlockSpec((tm, D), lambda i: (i, gate_blk)),
                  pl.BlockSpec((tm, D), lambda i: (i, gate_blk + 1)),
                  pl.BlockSpec((tm, D), row),
                  pl.BlockSpec((W, D), const),
                  pl.BlockSpec((W, D), const),
                  pl.BlockSpec((D, D), const),
                  pl.BlockSpec((1, 1, D), per_batch),
                  pl.BlockSpec((1, D), const),
                  pl.BlockSpec((1, 1, D), per_batch),
                  pl.BlockSpec((1, 1, D), per_batch)],
        out_specs=[pl.BlockSpec((tm, D), row), pl.BlockSpec((tm, D), row)],
        out_shape=[jax.ShapeDtypeStruct((M, D), F32), jax.ShapeDtypeStruct((M, D), BF16)],
        compiler_params=_params("arbitrary"),
        name="mix_out",
    )(o_sb, o_ds, z, z, x2, w_sb, w_ds, w_out, gt.reshape(B, 1, D), g.reshape(1, D),
      sc.reshape(B, 1, D), sh.reshape(B, 1, D))


HALO = BF16_ROWS


def _ffn_act_kernel(h_ref, halo_ref, wg_ref, wu_ref, cw_ref, cb_ref, o_ref, wgb_ref, wub_ref, *,
                    tiles_per_seq):
    i = pl.program_id(1)

    @pl.when(i == 0)
    def _():
        wgb_ref[...] = wg_ref[...].astype(BF16)
        wub_ref[...] = wu_ref[...].astype(BF16)

    h = h_ref[...]
    wg = wgb_ref[...]
    g0 = jnp.dot(h, wg, preferred_element_type=F32)
    g_prev = jnp.dot(halo_ref[...], wg, preferred_element_type=F32)
    g_prev = jnp.where(i % tiles_per_seq == 0, 0.0, g_prev)
    ridx = lax.broadcasted_iota(jnp.int32, g0.shape, 0)
    g1 = jnp.where(ridx == 0, g_prev[HALO - 1:HALO, :], pltpu.roll(g0, 1, 0))
    g2 = jnp.where(ridx == 0, g_prev[HALO - 2:HALO - 1, :],
                   jnp.where(ridx == 1, g_prev[HALO - 1:HALO, :], pltpu.roll(g0, 2, 0)))
    cw = cw_ref[...]
    a = cb_ref[...] + g2 * cw[0:1, :] + g1 * cw[1:2, :] + g0 * cw[2:3, :]
    u = jnp.dot(h, wub_ref[...], preferred_element_type=F32)
    o_ref[...] = (a * _sigmoid(a) * u).astype(o_ref.dtype)


def _ffn_out_kernel(a_ref, wd_ref, x1_ref, gt_ref, gf_ref, o_ref):
    k = pl.program_id(1)

    @pl.when(k == 0)
    def _():
        o_ref[...] = jnp.zeros_like(o_ref)

    o_ref[...] += jnp.dot(a_ref[...], wd_ref[...], preferred_element_type=F32)

    @pl.when(k == pl.num_programs(1) - 1)
    def _():
        x = x1_ref[...] + gt_ref[0] * o_ref[...]
        ms = jnp.mean(x * x, axis=-1, keepdims=True)
        o_ref[...] = x * lax.rsqrt(ms + EPS) * gf_ref[...]


def _conv_ffn(h2, x1, w_gate, w_up, layer, w_down, conv_w, conv_b, gt, g_final, S):
    M, D = x1.shape
    B = gt.shape[0]
    F = w_gate.shape[2]
    tm = min(S, 1024)
    tf = min(F, 512)
    per_b = S // tm
    halo_per_tile = tm // HALO
    act = pl.pallas_call(
        functools.partial(_ffn_act_kernel, tiles_per_seq=per_b),
        grid=(F // tf, M // tm),
        in_specs=[pl.BlockSpec((tm, D), lambda f, i: (i, 0)),
                  pl.BlockSpec((HALO, D), lambda f, i: (jnp.maximum(i * halo_per_tile - 1, 0), 0)),
                  pl.BlockSpec((None, D, tf), lambda f, i: (layer, 0, f)),
                  pl.BlockSpec((None, D, tf), lambda f, i: (layer, 0, f)),
                  pl.BlockSpec((CONV_WIDTH, tf), lambda f, i: (0, f)),
                  pl.BlockSpec((1, tf), lambda f, i: (0, f))],
        out_specs=pl.BlockSpec((tm, tf), lambda f, i: (i, f)),
        out_shape=jax.ShapeDtypeStruct((M, F), BF16),
        scratch_shapes=[pltpu.VMEM((D, tf), BF16), pltpu.VMEM((D, tf), BF16)],
        compiler_params=_params("arbitrary", "arbitrary"),
        name="ffn_act",
    )(h2, h2, w_gate, w_up, conv_w, conv_b.reshape(1, F))
    return pl.pallas_call(
        _ffn_out_kernel,
        grid=(M // tm, F // tf),
        in_specs=[pl.BlockSpec((tm, tf), lambda i, k: (i, k)),
                  pl.BlockSpec((tf, D), lambda i, k: (k, 0)),
                  pl.BlockSpec((tm, D), lambda i, k: (i, 0)),
                  pl.BlockSpec((1, 1, D), lambda i, k: (i // per_b, 0, 0)),
                  pl.BlockSpec((1, D), lambda i, k: (0, 0))],
        out_specs=pl.BlockSpec((tm, D), lambda i, k: (i, 0)),
        out_shape=jax.ShapeDtypeStruct((M, D), F32),
        compiler_params=_params("arbitrary", "arbitrary"),
        name="ffn_out",
    )(act, w_down, x1, gt.reshape(B, 1, D), g_final.reshape(1, D))


def kernel(x, c, w_ada, b_ada, g_mix, w_in, w_o_sb, w_o_dsa, w_out, rel_bias, g_ffn, w_gate,
           w_up, conv_w, conv_b, w_down, g_final):
    B, S, D = x.shape
    depth = w_ada.shape[0]
    W = w_o_sb.shape[1]
    H = W // HEAD_DIM
    qw = N_IDX_HEADS * IDX_DIM
    assert w_o_dsa.shape[1] == W and D % W == 0 and S % ATTN_BLOCK == 0 and W % qw == 0
    n_sel = min(TOPK_MAX, S // 4)
    scale = HEAD_DIM ** -0.5
    x2 = x.reshape(B * S, D)

    for l in range(depth):
        mod = _adaln(c, w_ada[l], b_ada[l])
        sh1, sc1, gt1, sh2, sc2, gt2 = jnp.split(mod, 6, axis=-1)

        wt = w_in[l].T
        o_kix = 6 * W + qw
        o_wix = o_kix + IDX_DIM
        o_gate = o_wix + N_IDX_HEADS
        k_rows = wt[o_kix:o_wix]
        wt_tail = jnp.concatenate(
            [k_rows, k_rows, wt[o_wix:o_gate],
             jnp.zeros((2 * IDX_DIM - N_IDX_HEADS, D), wt.dtype)], axis=0)

        h1 = _norm_mod(x2, g_mix[l], sc1, sh1, S)
        z = _in_proj_wt(h1, wt, (1, 4), W, (), "in_proj")
        zt = _in_proj_wt(h1, wt, (0, 2, 3, 5, 6), W, ((0, scale), (2, scale * LOG2_E)),
                         "in_proj_t", transposed=True)
        zg = _in_proj_wt(h1, wt[o_gate:], tuple(range(2 * D // W)), W, (), "in_proj_gates")
        zs = _in_proj_wt(h1, wt_tail, (0,), 4 * IDX_DIM, (), "in_proj_idx")

        o_sb = _sb_attn(z, zt, B, S, H, 0, 0, H)
        mask = _dsa_select(zt, zs, B, S, 4 * W // qw, n_sel)
        o_ds = _dsa_attn(z, zt, mask, _bias_tiles(rel_bias, min(S, ATTN_BLOCK)),
                         B, S, H, 2 * H, H, 3 * H)

        x2, h2 = _mix_out(o_sb, o_ds, zg, 0, x2, w_o_sb[l].astype(BF16),
                          w_o_dsa[l].astype(BF16), w_out[l].astype(BF16), gt1, g_ffn[l], sc2, sh2, S)
        last = l == depth - 1
        assert last, "the final rms_norm is fused into the last layer's FFN"
        x2 = _conv_ffn(h2, x2, w_gate, w_up, l, w_down[l].astype(BF16),
                       conv_w[l], conv_b[l], gt2, g_final, S)
    return x2.reshape(B, S, D)
```

```python
import functools
import math

import jax
import jax.numpy as jnp
from jax import lax
from jax.experimental import pallas as pl
from jax.experimental.pallas import tpu as pltpu

HEAD_DIM = 128
N_IDX_HEADS = 16
IDX_DIM = 64
TOPK_MAX = 256
N_BUCKETS = 32
MAX_DISTANCE = 128
CONV_WIDTH = 3
EPS = 1e-6

F32 = jnp.float32
BF16 = jnp.bfloat16
NEG = -1e30
I16 = jnp.int16
INT_MIN = -2 ** 31
I16_MIN = -2 ** 15
EXP_ZERO_BELOW = 104.0
LOG2_E = math.log2(math.e)
V7X_VMEM_LIMIT = 56 * 1024 * 1024
BF16_ROWS = 16
ATTN_BLOCK = 256
SB_HEAD_GROUP = 4
DSA_HEAD_GROUP = 8
FAR_BLOCKS = 3
SCORE_BLOCKS = 4
COUNT_BLOCKS = 4
COUNT_CHUNK = 64
NT_DIMS = (((1,), (1,)), ((), ()))


def _params(*sem):
    return pltpu.CompilerParams(dimension_semantics=sem, vmem_limit_bytes=V7X_VMEM_LIMIT)


def _sigmoid(x):
    return 1.0 / (1.0 + jnp.exp(-x))


def _adaln_kernel(ct_ref, w_ref, b_ref, o_ref):
    ct = ct_ref[...]
    act = ct * _sigmoid(ct)
    w = w_ref[...]
    for b in range(ct.shape[1]):
        o_ref[b:b + 1, :] = jnp.sum(act[:, b:b + 1] * w, axis=0, keepdims=True) + b_ref[...]


def _adaln(c, w, bias):
    B, D = c.shape
    N = w.shape[1]
    tn = min(N, 1024)
    return pl.pallas_call(
        _adaln_kernel,
        grid=(N // tn,),
        in_specs=[pl.BlockSpec((D, B), lambda j: (0, 0)),
                  pl.BlockSpec((D, tn), lambda j: (0, j)),
                  pl.BlockSpec((1, tn), lambda j: (0, j))],
        out_specs=pl.BlockSpec((B, tn), lambda j: (0, j)),
        out_shape=jax.ShapeDtypeStruct((B, N), F32),
        compiler_params=_params("arbitrary"),
        name="adaln",
    )(c.T, w, bias.reshape(1, N))


def _rms_mod(x, g, sc, sh):
    ms = jnp.mean(x * x, axis=-1, keepdims=True)
    y = x * lax.rsqrt(ms + EPS) * g
    return y * (1.0 + sc) + sh


def _norm_mod_kernel(x_ref, g_ref, sc_ref, sh_ref, o_ref):
    o_ref[...] = _rms_mod(x_ref[...], g_ref[...], sc_ref[0], sh_ref[0]).astype(o_ref.dtype)


def _norm_mod(x2, g, sc, sh, S):
    M, D = x2.shape
    B = sc.shape[0]
    tm = min(S, 1024)
    per_b = S // tm
    return pl.pallas_call(
        _norm_mod_kernel,
        grid=(M // tm,),
        in_specs=[pl.BlockSpec((tm, D), lambda i: (i, 0)),
                  pl.BlockSpec((1, D), lambda i: (0, 0)),
                  pl.BlockSpec((1, 1, D), lambda i: (i // per_b, 0, 0)),
                  pl.BlockSpec((1, 1, D), lambda i: (i // per_b, 0, 0))],
        out_specs=pl.BlockSpec((tm, D), lambda i: (i, 0)),
        out_shape=jax.ShapeDtypeStruct((M, D), BF16),
        compiler_params=_params("arbitrary"),
        name="norm_mod",
    )(x2, g.reshape(1, D), sc.reshape(B, 1, D), sh.reshape(B, 1, D))


def _in_proj_kernel(a_ref, b_ref, o_ref, *, tile_scales):
    acc = jnp.dot(a_ref[...], b_ref[...], preferred_element_type=F32)
    o_ref[...] = (acc * _tile_scale(tile_scales)).astype(o_ref.dtype)


def _in_proj_wt_kernel(a_ref, wt_ref, o_ref, wb_ref, *, tile_scales):
    @pl.when(pl.program_id(1) == 0)
    def _():
        wb_ref[...] = wt_ref[...].T.astype(BF16)

    _in_proj_kernel(a_ref, wb_ref, o_ref, tile_scales=tile_scales)


def _tile_scale(tile_scales):
    j = pl.program_id(0)
    scale = jnp.float32(1.0)
    for tile, tile_scale in tile_scales:
        scale = jnp.where(j == tile, tile_scale, scale)
    return scale


def _in_proj_t_kernel(a_ref, wt_ref, o_ref, wb_ref, *, tile_scales):
    @pl.when(pl.program_id(1) == 0)
    def _():
        wb_ref[...] = wt_ref[...].astype(BF16)

    acc = lax.dot_general(wb_ref[...], a_ref[...], NT_DIMS, preferred_element_type=F32)
    o_ref[...] = (acc * _tile_scale(tile_scales)).astype(o_ref.dtype)


def _tile_index_fn(tiles):
    steps = []
    for j, t in enumerate(tiles):
        while len(steps) < t - j:
            steps.append(j)
    return lambda j: j + sum(jnp.where(j >= first, 1, 0) for first in steps)


def _in_proj_wt(h, wt, tiles, tn, tile_scales, name, transposed=False):
    M, K = h.shape
    tm = min(M, 1024)
    n = len(tiles)
    tile_of = _tile_index_fn(tiles)
    in_specs = [pl.BlockSpec((tm, K), lambda j, i: (i, 0)),
                pl.BlockSpec((tn, K), lambda j, i: (tile_of(j), 0))]
    if transposed:
        body, wb_shape = functools.partial(_in_proj_t_kernel, tile_scales=tile_scales), (tn, K)
        out_spec = pl.BlockSpec((tn, tm), lambda j, i: (j, i))
        out_shape = jax.ShapeDtypeStruct((n * tn, M), BF16)
    else:
        body, wb_shape = functools.partial(_in_proj_wt_kernel, tile_scales=tile_scales), (K, tn)
        out_spec = pl.BlockSpec((tm, tn), lambda j, i: (i, j))
        out_shape = jax.ShapeDtypeStruct((M, n * tn), BF16)
    return pl.pallas_call(
        body,
        grid=(n, M // tm),
        in_specs=in_specs,
        out_specs=out_spec,
        out_shape=out_shape,
        scratch_shapes=[pltpu.VMEM(wb_shape, BF16)],
        compiler_params=_params("arbitrary", "arbitrary"),
        name=name,
    )(h, wt)


def _key_query_iotas(k0, q0, tk, tq):
    key_pos = k0 + lax.broadcasted_iota(jnp.int32, (tk, tq), 0)
    query_pos = q0 + lax.broadcasted_iota(jnp.int32, (tk, tq), 1)
    return key_pos, query_pos


def _head_cols(g):
    return slice(g * HEAD_DIM, (g + 1) * HEAD_DIM)


def _sb_kernel(q_ref, k_ref, vt_ref, o_ref, run_ref, acc_ref, *, tq, tk, n_heads):
    qi = pl.program_id(2)
    q0 = qi * tq
    run_ref[...] = jnp.zeros_like(run_ref)
    acc_ref[...] = jnp.zeros_like(acc_ref)
    later = (lax.broadcasted_iota(jnp.int32, (tk, tk), 1)
             > lax.broadcasted_iota(jnp.int32, (tk, tk), 0)).astype(BF16)

    def step(kbs, first_is_diagonal):
        heads = range(n_heads)
        k0s = [pl.multiple_of(kb * tk, tk) for kb in kbs]
        causal = None
        if first_is_diagonal:
            key_pos, query_pos = _key_query_iotas(k0s[0], q0, tk, tq)
            causal = key_pos < query_pos
        masked = [first_is_diagonal and j == 0 for j in range(len(kbs))]
        zs = [[jnp.dot(k_ref[pl.ds(k0, tk), _head_cols(g)], q_ref[_head_cols(g), :],
                       preferred_element_type=F32) for g in heads] for k0 in k0s]
        sps = [[jnp.maximum(z, 0.0) + jnp.log(1.0 + jnp.exp(-jnp.abs(z))) for z in zj] for zj in zs]
        spms = [[jnp.where(causal, sp, 0.0) if masked[j] else sp for sp in spj]
                for j, spj in enumerate(sps)]
        his = [[spm.astype(BF16) for spm in spj] for spj in spms]
        los = [[(spm - hi.astype(F32)).astype(BF16) for spm, hi in zip(spj, hij)]
               for spj, hij in zip(spms, his)]
        suffixes = [[jnp.dot(later, hi, preferred_element_type=F32)
                     + jnp.dot(later, lo, preferred_element_type=F32) for hi, lo in zip(hij, loj)]
                    for hij, loj in zip(his, los)]
        weights, new_runs = [], []
        for g in heads:
            run = run_ref[g]
            head_weights = []
            for j in range(len(kbs)):
                a = jnp.exp(zs[j][g] - sps[j][g] - suffixes[j][g] - run)
                if masked[j]:
                    a = jnp.where(causal, a, 0.0)
                head_weights.append(a.astype(BF16))
                run = run + jnp.sum(spms[j][g], axis=0, keepdims=True)
            weights.append(head_weights)
            new_runs.append(run)
        min_run = None
        for g in heads:
            acc = acc_ref[g]
            for j, k0 in enumerate(k0s):
                acc = acc + jnp.dot(vt_ref[_head_cols(g), pl.ds(k0, tk)], weights[g][j],
                                    preferred_element_type=F32)
            acc_ref[g] = acc
            run_ref[g] = new_runs[g]
            head_min = jnp.min(new_runs[g])
            min_run = head_min if min_run is None else jnp.minimum(min_run, head_min)
        return min_run > EXP_ZERO_BELOW

    def body(carry):
        kb, _ = carry
        return kb - 1, step([kb], False)

    def cond(carry):
        kb, dead = carry
        return jnp.logical_and(kb >= 0, jnp.logical_not(dead))

    @pl.when(qi == 0)
    def _():
        step([qi], True)

    @pl.when(qi >= 1)
    def _():
        dead = step([qi, qi - 1], True)
        lax.while_loop(cond, body, (qi - 2, dead))

    for g in range(n_heads):
        o_ref[:, _head_cols(g)] = acc_ref[g].T.astype(o_ref.dtype)


def _attn_group_specs(S, tq, nq, G, q_blk, k_blk, v_blk, kv_buffers):
    assert q_blk % G == 0 and k_blk % G == 0 and v_blk % G == 0
    gw = G * HEAD_DIM
    mode = pl.Buffered(kv_buffers)
    return [pl.BlockSpec((gw, tq), lambda b, hg, i: (q_blk // G + hg, b * nq + i)),
            pl.BlockSpec((S, gw), lambda b, hg, i: (b, k_blk // G + hg), pipeline_mode=mode),
            pl.BlockSpec((gw, S), lambda b, hg, i: (v_blk // G + hg, b), pipeline_mode=mode)]


def _sb_attn(z, zt, B, S, H, q_blk, k_blk, v_blk):
    M = z.shape[0]
    tq = tk = min(S, ATTN_BLOCK)
    nq = S // tq
    G = min(H, SB_HEAD_GROUP)
    gw = G * HEAD_DIM
    return pl.pallas_call(
        functools.partial(_sb_kernel, tq=tq, tk=tk, n_heads=G),
        grid=(B, H // G, nq),
        in_specs=_attn_group_specs(S, tq, nq, G, q_blk, k_blk, v_blk, kv_buffers=2),
        out_specs=pl.BlockSpec((tq, gw), lambda b, hg, i: (b * nq + i, hg)),
        out_shape=jax.ShapeDtypeStruct((M, H * HEAD_DIM), BF16),
        scratch_shapes=[pltpu.VMEM((G, 1, tq), F32), pltpu.VMEM((G, HEAD_DIM, tq), F32)],
        compiler_params=_params("arbitrary", "arbitrary", "arbitrary"),
        name="sb_attn",
    )(zt, z, zt)


def _dsa_select_kernel(q_ref, k_ref, w_ref, mask_ref, hi_ref, lo_ref, *, tq, tk, n_sel, n_kblocks):
    qi = pl.program_id(1)
    q0 = qi * tq
    nkb = (q0 + tq + tk - 1) // tk
    lane = lax.broadcasted_iota(jnp.int32, (tk, 2 * IDX_DIM), 1)
    w_scale = (N_IDX_HEADS ** -0.5) * (IDX_DIM ** -0.5)
    wt = (w_ref[...].astype(F32) * w_scale).T

    def score_block(kb, diagonal):
        k0 = pl.multiple_of(kb * tk, tk)
        kk = k_ref[pl.ds(k0, tk), :]
        k_halves = (jnp.where(lane < IDX_DIM, kk, jnp.zeros_like(kk)),
                    jnp.where(lane >= IDX_DIM, kk, jnp.zeros_like(kk)))
        acc = jnp.zeros((tk, tq), F32)
        for p in range(N_IDX_HEADS // 2):
            q2t = q_ref[p * 2 * IDX_DIM:(p + 1) * 2 * IDX_DIM, :]
            for half in range(2):
                h = 2 * p + half
                ph = jnp.dot(k_halves[half], q2t, preferred_element_type=F32)
                acc = acc + jnp.maximum(ph, 0.0) * wt[h:h + 1, :]
        bits = lax.bitcast_convert_type(acc, jnp.int32)
        key = bits ^ ((bits >> 31) & 0x7FFFFFFF)
        if diagonal:
            key_pos, query_pos = _key_query_iotas(k0, q0, tk, tq)
            key = jnp.where(key_pos <= query_pos, key, INT_MIN)
        hi_ref[pl.ds(k0, tk), :] = (key >> 16).astype(I16)
        lo_ref[pl.ds(k0, tk), :] = (key ^ 0x8000).astype(I16)

    def score_group(i, carry):
        for j in range(SCORE_BLOCKS):
            score_block(SCORE_BLOCKS * i + j, False)
        return carry

    n_before = nkb - 1
    lax.fori_loop(0, n_before // SCORE_BLOCKS, score_group, 0)
    first_left = n_before - n_before % SCORE_BLOCKS
    for left in range(1, SCORE_BLOCKS):
        @pl.when(n_before % SCORE_BLOCKS == left)
        def _(left=left):
            for j in range(left):
                score_block(first_left + j, False)

    score_block(nkb - 1, True)

    n_count = (nkb + COUNT_BLOCKS - 1) // COUNT_BLOCKS
    rows = COUNT_BLOCKS * tk

    def pad_block(kb, carry):
        k0 = pl.multiple_of(kb * tk, tk)
        hi_ref[pl.ds(k0, tk), :] = jnp.full((tk, tq), I16_MIN, I16)
        lo_ref[pl.ds(k0, tk), :] = jnp.full((tk, tq), I16_MIN, I16)
        return carry

    lax.fori_loop(nkb, n_count * COUNT_BLOCKS, pad_block, 0)

    def count(flags_fn):
        def blk(i, cnt):
            r0 = pl.multiple_of(i * rows, rows)
            groups = []
            for c in range(rows // COUNT_CHUNK):
                flags = flags_fn(r0 + c * COUNT_CHUNK, COUNT_CHUNK)
                parts = [flags[r * BF16_ROWS:(r + 1) * BF16_ROWS, :]
                         for r in range(COUNT_CHUNK // BF16_ROWS)]
                while len(parts) > 1:
                    parts = [a + b for a, b in zip(parts[0::2], parts[1::2])]
                groups.append(parts[0])
            while len(groups) > 1:
                groups = [a + b for a, b in zip(groups[0::2], groups[1::2])]
            return cnt + groups[0]
        cnt = lax.fori_loop(0, n_count, blk, jnp.zeros((BF16_ROWS, tq), I16))
        return jnp.sum(cnt.astype(jnp.int32), axis=0, keepdims=True)

    flag, no_flag = I16(1), I16(0)

    def count_ge(ref, thr16):
        return count(lambda r, n: jnp.where(ref[pl.ds(r, n), :] >= thr16, flag, no_flag))

    def kth_largest(ref, k):
        def bisect(it, carry):
            thr, n_ge, n_gt = carry
            cand = thr + jnp.left_shift(jnp.int32(1), 15 - it)
            cnt = count_ge(ref, cand.astype(I16))
            ok = cnt >= k
            return jnp.where(ok, cand, thr), jnp.where(ok, cnt, n_ge), jnp.where(ok, n_gt, cnt)
        zeros = jnp.zeros((1, tq), jnp.int32)
        thr, n_ge, n_gt = lax.fori_loop(0, 16, bisect, (zeros + I16_MIN, zeros, zeros))
        return thr.astype(I16), n_ge, n_gt

    thr_hi, _, above_hi = kth_largest(hi_ref, n_sel)
    need_lo = n_sel - above_hi

    def park(kb, carry):
        k0 = pl.multiple_of(kb * tk, tk)
        lo_ref[pl.ds(k0, tk), :] = jnp.where(hi_ref[pl.ds(k0, tk), :] == thr_hi,
                                             lo_ref[pl.ds(k0, tk), :], I16(I16_MIN))
        return carry

    lax.fori_loop(0, nkb, park, 0)
    thr_lo, at_least_lo, above_lo = kth_largest(lo_ref, need_lo)
    need_ties = need_lo - above_lo
    n_ties = jnp.where(thr_lo == I16(I16_MIN), tk * n_kblocks, at_least_lo - above_lo)
    all_ties_fit = jnp.max(n_ties - need_ties) <= 0

    one, zero, neg = BF16(1.0), BF16(0.0), BF16(NEG)

    def store_mask(kb, sel, diagonal):
        k0 = pl.multiple_of(kb * tk, tk)
        if diagonal:
            key_pos, query_pos = _key_query_iotas(k0, q0, tk, tq)
            sel = sel * (key_pos <= query_pos).astype(BF16)
        mask_ref[0, pl.ds(k0, tk), :] = jnp.where(sel > zero, zero, neg)

    def emit_all_ties(kb, diagonal):
        k0 = pl.multiple_of(kb * tk, tk)
        hi = hi_ref[pl.ds(k0, tk), :]
        lo = lo_ref[pl.ds(k0, tk), :]
        store_mask(kb, jnp.where(hi > thr_hi, one,
                                 jnp.where(hi == thr_hi, jnp.where(lo >= thr_lo, one, zero), zero)),
                   diagonal)

    earlier = (lax.broadcasted_iota(jnp.int32, (tk, tk), 1)
               < lax.broadcasted_iota(jnp.int32, (tk, tk), 0)).astype(BF16)
    ones_rows = jnp.ones((BF16_ROWS, tk), BF16)
    need_ties_f = need_ties.astype(F32)

    def emit_ranked_ties(kb, ties_seen, diagonal):
        k0 = pl.multiple_of(kb * tk, tk)
        hi = hi_ref[pl.ds(k0, tk), :]
        lo = lo_ref[pl.ds(k0, tk), :]
        above = jnp.where(hi > thr_hi, one, jnp.where(lo > thr_lo, one, zero))
        eq = jnp.where(hi == thr_hi, jnp.where(lo == thr_lo, one, zero), zero)
        rank = jnp.dot(earlier, eq, preferred_element_type=F32).astype(BF16)
        room = jnp.clip(need_ties_f - ties_seen, -1.0, float(tk)).astype(BF16)
        store_mask(kb, jnp.where(rank < room, jnp.maximum(above, eq), above), diagonal)
        return ties_seen + jnp.dot(ones_rows, eq, preferred_element_type=F32)[0:1]

    def emit_fast():
        def body(kb, carry):
            emit_all_ties(kb, False)
            return carry
        lax.fori_loop(0, nkb - 1, body, 0)
        emit_all_ties(nkb - 1, True)

    def emit_slow():
        ties_seen = lax.fori_loop(0, nkb - 1, lambda kb, seen: emit_ranked_ties(kb, seen, False),
                                  jnp.zeros((1, tq), F32))
        emit_ranked_ties(nkb - 1, ties_seen, True)

    lax.cond(all_ties_fit, emit_fast, emit_slow)

    def fill(kb, carry):
        k0 = pl.multiple_of(kb * tk, tk)
        mask_ref[0, pl.ds(k0, tk), :] = jnp.full((tk, tq), NEG, mask_ref.dtype)
        return carry

    lax.fori_loop(nkb, n_kblocks, fill, 0)


def _dsa_select(zt, zs, B, S, qix_blk, n_sel):
    tq = tk = min(S, ATTN_BLOCK)
    nq = S // tq
    assert nq % COUNT_BLOCKS == 0
    qw = N_IDX_HEADS * IDX_DIM
    return pl.pallas_call(
        functools.partial(_dsa_select_kernel, tq=tq, tk=tk, n_sel=n_sel, n_kblocks=S // tk),
        grid=(B, nq),
        in_specs=[pl.BlockSpec((qw, tq), lambda b, i: (qix_blk, b * nq + i)),
                  pl.BlockSpec((S, 2 * IDX_DIM), lambda b, i: (b, 0)),
                  pl.BlockSpec((tq, 2 * IDX_DIM), lambda b, i: (b * nq + i, 1))],
        out_specs=pl.BlockSpec((1, S, tq), lambda b, i: (b * nq + i, 0, 0)),
        out_shape=jax.ShapeDtypeStruct((B * nq, S, tq), BF16),
        scratch_shapes=[pltpu.VMEM((S, tq), I16), pltpu.VMEM((S, tq), I16)],
        compiler_params=_params("arbitrary", "arbitrary"),
        name="dsa_select",
    )(zt, zs, zs)


def _rel_bucket(dist):
    n = jnp.maximum(dist, 0)
    max_exact = N_BUCKETS // 2
    nf = jnp.maximum(n, 1).astype(F32)
    large = max_exact + (jnp.log(nf / max_exact) / math.log(MAX_DISTANCE / max_exact)
                         * (N_BUCKETS - max_exact)).astype(jnp.int32)
    large = jnp.minimum(large, N_BUCKETS - 1)
    return jnp.where(n < max_exact, n, large)


def _bias_tiles(rel_bias, tq):
    assert tq >= MAX_DISTANCE
    rb = rel_bias.astype(F32)
    key = jnp.arange(tq)[:, None]
    query = jnp.arange(tq)[None, :]
    bucket = _rel_bucket(jnp.stack([query - key, tq + query - key]))
    onehot = (bucket[None] == jnp.arange(N_BUCKETS)[:, None, None, None]).astype(F32)
    return jnp.einsum("nh,nikq->hikq", (rb - rb[N_BUCKETS - 1]) * LOG2_E, onehot,
                      precision=lax.Precision.HIGHEST)


def _dsa_attn_kernel(q_ref, k_ref, vt_ref, mask_ref, bias_ref, o_ref, m_ref, acc_ref, logit_ref, *,
                     tq, n_heads):
    qi = pl.program_id(2)
    m_ref[...] = jnp.full_like(m_ref, NEG)
    acc_ref[...] = jnp.zeros_like(acc_ref)
    ones_rows = jnp.ones((BF16_ROWS, tq), BF16)

    def step(blocks):
        k0s = [pl.multiple_of(kb * tq, tq) for kb, _ in blocks]
        masks = [mask_ref[0, pl.ds(k0, tq), :].astype(F32) for k0 in k0s]

        def head_scores(g):
            s = jnp.dot(k_ref[pl.ds(k0s[0], len(blocks) * tq), _head_cols(g)], q_ref[_head_cols(g), :],
                        preferred_element_type=F32)
            return [s[j * tq:(j + 1) * tq] for j in range(len(blocks))]

        def head_logits(g, scores):
            m_new = m_ref[g]
            for j, (_, bias_idx) in enumerate(blocks):
                s = scores[j] + masks[j]
                if bias_idx is not None:
                    s = s + bias_ref[g, bias_idx]
                logit_ref[g, j] = s
                m_new = jnp.maximum(m_new, jnp.max(s, axis=0, keepdims=True))
            return m_new

        def head_accumulate(g, m_new):
            acc = jnp.exp2(m_ref[g] - m_new) * acc_ref[g]
            for j, k0 in enumerate(k0s):
                p = jnp.exp2(logit_ref[g, j] - m_new).astype(BF16)
                vt = jnp.concatenate([vt_ref[_head_cols(g), pl.ds(k0, tq)], ones_rows], axis=0)
                acc = acc + jnp.dot(vt, p, preferred_element_type=F32)
            acc_ref[g] = acc
            m_ref[g] = m_new

        scores = [head_scores(g) for g in range(n_heads)]
        maxima = [head_logits(g, scores[g]) for g in range(n_heads)]
        for g in range(n_heads):
            head_accumulate(g, maxima[g])

    n_far = jnp.maximum(qi - 1, 0)

    def far_group(i, carry):
        step([(FAR_BLOCKS * i + j, None) for j in range(FAR_BLOCKS)])
        return carry

    lax.fori_loop(0, n_far // FAR_BLOCKS, far_group, 0)
    first_left = n_far - n_far % FAR_BLOCKS
    for left in range(1, FAR_BLOCKS):
        @pl.when(n_far % FAR_BLOCKS == left)
        def _(left=left):
            step([(first_left + j, None) for j in range(left)])

    @pl.when(qi >= 1)
    def _():
        step([(qi - 1, 1), (qi, 0)])

    @pl.when(qi == 0)
    def _():
        step([(qi, 0)])
    for g in range(n_heads):
        acc = acc_ref[g]
        out_t = acc[:HEAD_DIM] / acc[HEAD_DIM:HEAD_DIM + 1]
        o_ref[:, _head_cols(g)] = out_t.T.astype(o_ref.dtype)


def _dsa_attn(z, zt, mask, bias_tiles, B, S, H, q_blk, k_blk, v_blk):
    M = z.shape[0]
    tq = bias_tiles.shape[-1]
    nq = S // tq
    G = min(H, DSA_HEAD_GROUP)
    gw = G * HEAD_DIM
    return pl.pallas_call(
        functools.partial(_dsa_attn_kernel, tq=tq, n_heads=G),
        grid=(B, H // G, nq),
        in_specs=_attn_group_specs(S, tq, nq, G, q_blk, k_blk, v_blk, kv_buffers=1) + [
            pl.BlockSpec((1, S, tq), lambda b, hg, i: (b * nq + i, 0, 0)),
            pl.BlockSpec((G, 2, tq, tq), lambda b, hg, i: (hg, 0, 0, 0),
                         pipeline_mode=pl.Buffered(1))],
        out_specs=pl.BlockSpec((tq, gw), lambda b, hg, i: (b * nq + i, hg)),
        out_shape=jax.ShapeDtypeStruct((M, H * HEAD_DIM), BF16),
        scratch_shapes=[pltpu.VMEM((G, 1, tq), F32),
                        pltpu.VMEM((G, HEAD_DIM + BF16_ROWS, tq), F32),
                        pltpu.VMEM((G, FAR_BLOCKS, tq, tq), F32)],
        compiler_params=_params("arbitrary", "arbitrary", "arbitrary"),
        name="dsa_attn",
    )(zt, z, zt, mask, bias_tiles)


def _mix_out_kernel(osb_ref, ods_ref, gsb_ref, gds_ref, x_ref, wsb_ref, wds_ref, wout_ref,
                    gt_ref, g_ref, sc_ref, sh_ref, x1_ref, h2_ref):
    t_sb = jnp.dot(osb_ref[...], wsb_ref[...], preferred_element_type=F32)
    t_ds = jnp.dot(ods_ref[...], wds_ref[...], preferred_element_type=F32)
    merged = (_sigmoid(gsb_ref[...].astype(F32)) * t_sb
              + _sigmoid(gds_ref[...].astype(F32)) * t_ds)
    y = jnp.dot(merged.astype(BF16), wout_ref[...], preferred_element_type=F32)
    x1 = x_ref[...] + gt_ref[0] * y
    x1_ref[...] = x1
    h2_ref[...] = _rms_mod(x1, g_ref[...], sc_ref[0], sh_ref[0]).astype(h2_ref.dtype)


def _mix_out(o_sb, o_ds, z, gate_blk, x2, w_sb, w_ds, w_out, gt, g, sc, sh, S):
    M, D = x2.shape
    B = gt.shape[0]
    W = o_sb.shape[1]
    tm = min(S, 512)
    per_b = S // tm
    row = lambda i: (i, 0)
    const = lambda i: (0, 0)
    per_batch = lambda i: (i // per_b, 0, 0)
    return pl.pallas_call(
        _mix_out_kernel,
        grid=(M // tm,),
        in_specs=[pl.BlockSpec((tm, W), row),
                  pl.BlockSpec((tm, W), row),
                  pl.BlockSpec((tm, D), lambda i: (i, gate_blk)),
                  pl.BlockSpec((tm, D), lambda i: (i, gate_blk + 1)),
                  pl.BlockSpec((tm, D), row),
                  pl.BlockSpec((W, D), const),
                  pl.BlockSpec((W, D), const),
                  pl.BlockSpec((D, D), const),
                  pl.BlockSpec((1, 1, D), per_batch),
                  pl.BlockSpec((1, D), const),
                  pl.BlockSpec((1, 1, D), per_batch),
                  pl.BlockSpec((1, 1, D), per_batch)],
        out_specs=[pl.BlockSpec((tm, D), row), pl.BlockSpec((tm, D), row)],
        out_shape=[jax.ShapeDtypeStruct((M, D), F32), jax.ShapeDtypeStruct((M, D), BF16)],
        compiler_params=_params("arbitrary"),
        name="mix_out",
    )(o_sb, o_ds, z, z, x2, w_sb, w_ds, w_out, gt.reshape(B, 1, D), g.reshape(1, D),
      sc.reshape(B, 1, D), sh.reshape(B, 1, D))


HALO = BF16_ROWS


def _ffn_act_kernel(h_ref, halo_ref, wg_ref, wu_ref, cw_ref, cb_ref, o_ref, wgb_ref, wub_ref, *,
                    tiles_per_seq):
    i = pl.program_id(1)

    @pl.when(i == 0)
    def _():
        wgb_ref[...] = wg_ref[...].astype(BF16)
        wub_ref[...] = wu_ref[...].astype(BF16)

    h = h_ref[...]
    wg = wgb_ref[...]
    g0 = jnp.dot(h, wg, preferred_element_type=F32)
    g_prev = jnp.dot(halo_ref[...], wg, preferred_element_type=F32)
    g_prev = jnp.where(i % tiles_per_seq == 0, 0.0, g_prev)
    ridx = lax.broadcasted_iota(jnp.int32, g0.shape, 0)
    g1 = jnp.where(ridx == 0, g_prev[HALO - 1:HALO, :], pltpu.roll(g0, 1, 0))
    g2 = jnp.where(ridx == 0, g_prev[HALO - 2:HALO - 1, :],
                   jnp.where(ridx == 1, g_prev[HALO - 1:HALO, :], pltpu.roll(g0, 2, 0)))
    cw = cw_ref[...]
    a = cb_ref[...] + g2 * cw[0:1, :] + g1 * cw[1:2, :] + g0 * cw[2:3, :]
    u = jnp.dot(h, wub_ref[...], preferred_element_type=F32)
    o_ref[...] = (a * _sigmoid(a) * u).astype(o_ref.dtype)


def _ffn_out_kernel(a_ref, wd_ref, x1_ref, gt_ref, gf_ref, o_ref):
    k = pl.program_id(1)

    @pl.when(k == 0)
    def _():
        o_ref[...] = jnp.zeros_like(o_ref)

    o_ref[...] += jnp.dot(a_ref[...], wd_ref[...], preferred_element_type=F32)

    @pl.when(k == pl.num_programs(1) - 1)
    def _():
        x = x1_ref[...] + gt_ref[0] * o_ref[...]
        ms = jnp.mean(x * x, axis=-1, keepdims=True)
        o_ref[...] = x * lax.rsqrt(ms + EPS) * gf_ref[...]


def _conv_ffn(h2, x1, w_gate, w_up, layer, w_down, conv_w, conv_b, gt, g_final, S):
    M, D = x1.shape
    B = gt.shape[0]
    F = w_gate.shape[2]
    tm = min(S, 1024)
    tf = min(F, 512)
    per_b = S // tm
    halo_per_tile = tm // HALO
    act = pl.pallas_call(
        functools.partial(_ffn_act_kernel, tiles_per_seq=per_b),
        grid=(F // tf, M // tm),
        in_specs=[pl.BlockSpec((tm, D), lambda f, i: (i, 0)),
                  pl.BlockSpec((HALO, D), lambda f, i: (jnp.maximum(i * halo_per_tile - 1, 0), 0)),
                  pl.BlockSpec((None, D, tf), lambda f, i: (layer, 0, f)),
                  pl.BlockSpec((None, D, tf), lambda f, i: (layer, 0, f)),
                  pl.BlockSpec((CONV_WIDTH, tf), lambda f, i: (0, f)),
                  pl.BlockSpec((1, tf), lambda f, i: (0, f))],
        out_specs=pl.BlockSpec((tm, tf), lambda f, i: (i, f)),
        out_shape=jax.ShapeDtypeStruct((M, F), BF16),
        scratch_shapes=[pltpu.VMEM((D, tf), BF16), pltpu.VMEM((D, tf), BF16)],
        compiler_params=_params("arbitrary", "arbitrary"),
        name="ffn_act",
    )(h2, h2, w_gate, w_up, conv_w, conv_b.reshape(1, F))
    return pl.pallas_call(
        _ffn_out_kernel,
        grid=(M // tm, F // tf),
        in_specs=[pl.BlockSpec((tm, tf), lambda i, k: (i, k)),
                  pl.BlockSpec((tf, D), lambda i, k: (k, 0)),
                  pl.BlockSpec((tm, D), lambda i, k: (i, 0)),
                  pl.BlockSpec((1, 1, D), lambda i, k: (i // per_b, 0, 0)),
                  pl.BlockSpec((1, D), lambda i, k: (0, 0))],
        out_specs=pl.BlockSpec((tm, D), lambda i, k: (i, 0)),
        out_shape=jax.ShapeDtypeStruct((M, D), F32),
        compiler_params=_params("arbitrary", "arbitrary"),
        name="ffn_out",
    )(act, w_down, x1, gt.reshape(B, 1, D), g_final.reshape(1, D))


def kernel(x, c, w_ada, b_ada, g_mix, w_in, w_o_sb, w_o_dsa, w_out, rel_bias, g_ffn, w_gate,
           w_up, conv_w, conv_b, w_down, g_final):
    B, S, D = x.shape
    depth = w_ada.shape[0]
    W = w_o_sb.shape[1]
    H = W // HEAD_DIM
    qw = N_IDX_HEADS * IDX_DIM
    assert w_o_dsa.shape[1] == W and D % W == 0 and S % ATTN_BLOCK == 0 and W % qw == 0
    n_sel = min(TOPK_MAX, S // 4)
    scale = HEAD_DIM ** -0.5
    x2 = x.reshape(B * S, D)

    for l in range(depth):
        mod = _adaln(c, w_ada[l], b_ada[l])
        sh1, sc1, gt1, sh2, sc2, gt2 = jnp.split(mod, 6, axis=-1)

        wt = w_in[l].T
        o_kix = 6 * W + qw
        o_wix = o_kix + IDX_DIM
        o_gate = o_wix + N_IDX_HEADS
        k_rows = wt[o_kix:o_wix]
        wt_tail = jnp.concatenate(
            [k_rows, k_rows, wt[o_wix:o_gate],
             jnp.zeros((2 * IDX_DIM - N_IDX_HEADS, D), wt.dtype)], axis=0)

        h1 = _norm_mod(x2, g_mix[l], sc1, sh1, S)
        z = _in_proj_wt(h1, wt, (1, 4), W, (), "in_proj")
        zt = _in_proj_wt(h1, wt, (0, 2, 3, 5, 6), W, ((0, scale), (2, scale * LOG2_E)),
                         "in_proj_t", transposed=True)
        zg = _in_proj_wt(h1, wt[o_gate:], tuple(range(2 * D // W)), W, (), "in_proj_gates")
        zs = _in_proj_wt(h1, wt_tail, (0,), 4 * IDX_DIM, (), "in_proj_idx")

        o_sb = _sb_attn(z, zt, B, S, H, 0, 0, H)
        mask = _dsa_select(zt, zs, B, S, 4 * W // qw, n_sel)
        o_ds = _dsa_attn(z, zt, mask, _bias_tiles(rel_bias, min(S, ATTN_BLOCK)),
                         B, S, H, 2 * H, H, 3 * H)

        x2, h2 = _mix_out(o_sb, o_ds, zg, 0, x2, w_o_sb[l].astype(BF16),
                          w_o_dsa[l].astype(BF16), w_out[l].astype(BF16), gt1, g_ffn[l], sc2, sh2, S)
        last = l == depth - 1
        assert last, "the final rms_norm is fused into the last layer's FFN"
        x2 = _conv_ffn(h2, x2, w_gate, w_up, l, w_down[l].astype(BF16),
                       conv_w[l], conv_b[l], gt2, g_final, S)
    return x2.reshape(B, S, D)
```

```python
import functools
import math

import jax
import jax.numpy as jnp
from jax import lax
from jax.experimental import pallas as pl
from jax.experimental.pallas import tpu as pltpu

HEAD_DIM = 128
N_IDX_HEADS = 16
IDX_DIM = 64
TOPK_MAX = 256
N_BUCKETS = 32
MAX_DISTANCE = 128
CONV_WIDTH = 3
EPS = 1e-6

F32 = jnp.float32
BF16 = jnp.bfloat16
NEG = -1e30
I16 = jnp.int16
INT_MIN = -2 ** 31
I16_MIN = -2 ** 15
EXP_ZERO_BELOW = 104.0
LOG2_E = math.log2(math.e)
V7X_VMEM_LIMIT = 58 * 1024 * 1024
BF16_ROWS = 16
ATTN_BLOCK = 256
SB_HEAD_GROUP = 4
DSA_HEAD_GROUP = 8
FAR_BLOCKS = 4
SCORE_BLOCKS = 4
COUNT_BLOCKS = 4
COUNT_CHUNK = 64
NT_DIMS = (((1,), (1,)), ((), ()))


def _params(*sem):
    return pltpu.CompilerParams(dimension_semantics=sem, vmem_limit_bytes=V7X_VMEM_LIMIT)


def _sigmoid(x):
    return 1.0 / (1.0 + jnp.exp(-x))


def _adaln_kernel(ct_ref, w_ref, b_ref, o_ref):
    ct = ct_ref[...]
    act = ct * _sigmoid(ct)
    w = w_ref[...]
    for b in range(ct.shape[1]):
        o_ref[b:b + 1, :] = jnp.sum(act[:, b:b + 1] * w, axis=0, keepdims=True) + b_ref[...]


def _adaln(c, w, bias):
    B, D = c.shape
    N = w.shape[1]
    tn = min(N, 1024)
    return pl.pallas_call(
        _adaln_kernel,
        grid=(N // tn,),
        in_specs=[pl.BlockSpec((D, B), lambda j: (0, 0)),
                  pl.BlockSpec((D, tn), lambda j: (0, j)),
                  pl.BlockSpec((1, tn), lambda j: (0, j))],
        out_specs=pl.BlockSpec((B, tn), lambda j: (0, j)),
        out_shape=jax.ShapeDtypeStruct((B, N), F32),
        compiler_params=_params("arbitrary"),
        name="adaln",
    )(c.T, w, bias.reshape(1, N))


def _rms_mod(x, g, sc, sh):
    ms = jnp.mean(x * x, axis=-1, keepdims=True)
    y = x * lax.rsqrt(ms + EPS) * g
    return y * (1.0 + sc) + sh


def _norm_mod_kernel(x_ref, g_ref, sc_ref, sh_ref, o_ref):
    o_ref[...] = _rms_mod(x_ref[...], g_ref[...], sc_ref[0], sh_ref[0]).astype(o_ref.dtype)


def _norm_mod(x2, g, sc, sh, S):
    M, D = x2.shape
    B = sc.shape[0]
    tm = min(S, 1024)
    per_b = S // tm
    return pl.pallas_call(
        _norm_mod_kernel,
        grid=(M // tm,),
        in_specs=[pl.BlockSpec((tm, D), lambda i: (i, 0)),
                  pl.BlockSpec((1, D), lambda i: (0, 0)),
                  pl.BlockSpec((1, 1, D), lambda i: (i // per_b, 0, 0)),
                  pl.BlockSpec((1, 1, D), lambda i: (i // per_b, 0, 0))],
        out_specs=pl.BlockSpec((tm, D), lambda i: (i, 0)),
        out_shape=jax.ShapeDtypeStruct((M, D), BF16),
        compiler_params=_params("arbitrary"),
        name="norm_mod",
    )(x2, g.reshape(1, D), sc.reshape(B, 1, D), sh.reshape(B, 1, D))


def _in_proj_kernel(a_ref, b_ref, o_ref, *, tile_scales):
    acc = jnp.dot(a_ref[...], b_ref[...], preferred_element_type=F32)
    o_ref[...] = (acc * _tile_scale(tile_scales)).astype(o_ref.dtype)


def _in_proj_wt_kernel(a_ref, wt_ref, o_ref, wb_ref, *, tile_scales):
    @pl.when(pl.program_id(1) == 0)
    def _():
        wb_ref[...] = wt_ref[...].T.astype(BF16)

    _in_proj_kernel(a_ref, wb_ref, o_ref, tile_scales=tile_scales)


def _tile_scale(tile_scales):
    j = pl.program_id(0)
    scale = jnp.float32(1.0)
    for tile, tile_scale in tile_scales:
        scale = jnp.where(j == tile, tile_scale, scale)
    return scale


def _in_proj_t_kernel(a_ref, wt_ref, o_ref, wb_ref, *, tile_scales):
    @pl.when(pl.program_id(1) == 0)
    def _():
        wb_ref[...] = wt_ref[...].astype(BF16)

    acc = lax.dot_general(wb_ref[...], a_ref[...], NT_DIMS, preferred_element_type=F32)
    o_ref[...] = (acc * _tile_scale(tile_scales)).astype(o_ref.dtype)


def _tile_index_fn(tiles):
    steps = []
    for j, t in enumerate(tiles):
        while len(steps) < t - j:
            steps.append(j)
    return lambda j: j + sum(jnp.where(j >= first, 1, 0) for first in steps)


def _in_proj_wt(h, wt, tiles, tn, tile_scales, name, transposed=False):
    M, K = h.shape
    tm = min(M, 1024)
    n = len(tiles)
    tile_of = _tile_index_fn(tiles)
    in_specs = [pl.BlockSpec((tm, K), lambda j, i: (i, 0)),
                pl.BlockSpec((tn, K), lambda j, i: (tile_of(j), 0))]
    if transposed:
        body, wb_shape = functools.partial(_in_proj_t_kernel, tile_scales=tile_scales), (tn, K)
        out_spec = pl.BlockSpec((tn, tm), lambda j, i: (j, i))
        out_shape = jax.ShapeDtypeStruct((n * tn, M), BF16)
    else:
        body, wb_shape = functools.partial(_in_proj_wt_kernel, tile_scales=tile_scales), (K, tn)
        out_spec = pl.BlockSpec((tm, tn), lambda j, i: (i, j))
        out_shape = jax.ShapeDtypeStruct((M, n * tn), BF16)
    return pl.pallas_call(
        body,
        grid=(n, M // tm),
        in_specs=in_specs,
        out_specs=out_spec,
        out_shape=out_shape,
        scratch_shapes=[pltpu.VMEM(wb_shape, BF16)],
        compiler_params=_params("arbitrary", "arbitrary"),
        name=name,
    )(h, wt)


def _key_query_iotas(k0, q0, tk, tq):
    key_pos = k0 + lax.broadcasted_iota(jnp.int32, (tk, tq), 0)
    query_pos = q0 + lax.broadcasted_iota(jnp.int32, (tk, tq), 1)
    return key_pos, query_pos


def _head_cols(g):
    return slice(g * HEAD_DIM, (g + 1) * HEAD_DIM)


def _sb_kernel(q_ref, k_ref, vt_ref, o_ref, run_ref, acc_ref, *, tq, tk, n_heads):
    qi = pl.program_id(2)
    q0 = qi * tq
    run_ref[...] = jnp.zeros_like(run_ref)
    acc_ref[...] = jnp.zeros_like(acc_ref)
    later = (lax.broadcasted_iota(jnp.int32, (tk, tk), 1)
             > lax.broadcasted_iota(jnp.int32, (tk, tk), 0)).astype(BF16)

    def step(kbs, first_is_diagonal):
        heads = range(n_heads)
        k0s = [pl.multiple_of(kb * tk, tk) for kb in kbs]
        causal = None
        if first_is_diagonal:
            key_pos, query_pos = _key_query_iotas(k0s[0], q0, tk, tq)
            causal = key_pos < query_pos
        masked = [first_is_diagonal and j == 0 for j in range(len(kbs))]
        zs = [[jnp.dot(k_ref[pl.ds(k0, tk), _head_cols(g)], q_ref[_head_cols(g), :],
                       preferred_element_type=F32) for g in heads] for k0 in k0s]
        sps = [[jnp.maximum(z, 0.0) + jnp.log(1.0 + jnp.exp(-jnp.abs(z))) for z in zj] for zj in zs]
        spms = [[jnp.where(causal, sp, 0.0) if masked[j] else sp for sp in spj]
                for j, spj in enumerate(sps)]
        his = [[spm.astype(BF16) for spm in spj] for spj in spms]
        los = [[(spm - hi.astype(F32)).astype(BF16) for spm, hi in zip(spj, hij)]
               for spj, hij in zip(spms, his)]
        suffixes = [[jnp.dot(later, hi, preferred_element_type=F32)
                     + jnp.dot(later, lo, preferred_element_type=F32) for hi, lo in zip(hij, loj)]
                    for hij, loj in zip(his, los)]
        weights, new_runs = [], []
        for g in heads:
            run = run_ref[g]
            head_weights = []
            for j in range(len(kbs)):
                a = jnp.exp(zs[j][g] - sps[j][g] - suffixes[j][g] - run)
                if masked[j]:
                    a = jnp.where(causal, a, 0.0)
                head_weights.append(a.astype(BF16))
                run = run + jnp.sum(spms[j][g], axis=0, keepdims=True)
            weights.append(head_weights)
            new_runs.append(run)
        min_run = None
        for g in heads:
            acc = acc_ref[g]
            for j, k0 in enumerate(k0s):
                acc = acc + jnp.dot(vt_ref[_head_cols(g), pl.ds(k0, tk)], weights[g][j],
                                    preferred_element_type=F32)
            acc_ref[g] = acc
            run_ref[g] = new_runs[g]
            head_min = jnp.min(new_runs[g])
            min_run = head_min if min_run is None else jnp.minimum(min_run, head_min)
        return min_run > EXP_ZERO_BELOW

    def body(carry):
        kb, _ = carry
        return kb - 1, step([kb], False)

    def cond(carry):
        kb, dead = carry
        return jnp.logical_and(kb >= 0, jnp.logical_not(dead))

    @pl.when(qi == 0)
    def _():
        step([qi], True)

    @pl.when(qi >= 1)
    def _():
        dead = step([qi, qi - 1], True)
        lax.while_loop(cond, body, (qi - 2, dead))

    for g in range(n_heads):
        o_ref[:, _head_cols(g)] = acc_ref[g].T.astype(o_ref.dtype)


def _attn_group_specs(S, tq, nq, G, q_blk, k_blk, v_blk, kv_buffers):
    assert q_blk % G == 0 and k_blk % G == 0 and v_blk % G == 0
    gw = G * HEAD_DIM
    mode = pl.Buffered(kv_buffers)
    return [pl.BlockSpec((gw, tq), lambda b, hg, i: (q_blk // G + hg, b * nq + i)),
            pl.BlockSpec((S, gw), lambda b, hg, i: (b, k_blk // G + hg), pipeline_mode=mode),
            pl.BlockSpec((gw, S), lambda b, hg, i: (v_blk // G + hg, b), pipeline_mode=mode)]


def _sb_attn(z, zt, B, S, H, q_blk, k_blk, v_blk):
    M = z.shape[0]
    tq = tk = min(S, ATTN_BLOCK)
    nq = S // tq
    G = min(H, SB_HEAD_GROUP)
    gw = G * HEAD_DIM
    return pl.pallas_call(
        functools.partial(_sb_kernel, tq=tq, tk=tk, n_heads=G),
        grid=(B, H // G, nq),
        in_specs=_attn_group_specs(S, tq, nq, G, q_blk, k_blk, v_blk, kv_buffers=2),
        out_specs=pl.BlockSpec((tq, gw), lambda b, hg, i: (b * nq + i, hg)),
        out_shape=jax.ShapeDtypeStruct((M, H * HEAD_DIM), BF16),
        scratch_shapes=[pltpu.VMEM((G, 1, tq), F32), pltpu.VMEM((G, HEAD_DIM, tq), F32)],
        compiler_params=_params("arbitrary", "arbitrary", "arbitrary"),
        name="sb_attn",
    )(zt, z, zt)


def _dsa_select_kernel(q_ref, k_ref, w_ref, mask_ref, hi_ref, lo_ref, *, tq, tk, n_sel, n_kblocks):
    qi = pl.program_id(1)
    q0 = qi * tq
    nkb = (q0 + tq + tk - 1) // tk
    lane = lax.broadcasted_iota(jnp.int32, (tk, 2 * IDX_DIM), 1)
    w_scale = (N_IDX_HEADS ** -0.5) * (IDX_DIM ** -0.5)
    wt = (w_ref[...].astype(F32) * w_scale).T

    def score_block(kb, diagonal):
        k0 = pl.multiple_of(kb * tk, tk)
        kk = k_ref[pl.ds(k0, tk), :]
        k_halves = (jnp.where(lane < IDX_DIM, kk, jnp.zeros_like(kk)),
                    jnp.where(lane >= IDX_DIM, kk, jnp.zeros_like(kk)))
        acc = jnp.zeros((tk, tq), F32)
        for p in range(N_IDX_HEADS // 2):
            q2t = q_ref[p * 2 * IDX_DIM:(p + 1) * 2 * IDX_DIM, :]
            for half in range(2):
                h = 2 * p + half
                ph = jnp.dot(k_halves[half], q2t, preferred_element_type=F32)
                acc = acc + jnp.maximum(ph, 0.0) * wt[h:h + 1, :]
        bits = lax.bitcast_convert_type(acc, jnp.int32)
        key = bits ^ ((bits >> 31) & 0x7FFFFFFF)
        if diagonal:
            key_pos, query_pos = _key_query_iotas(k0, q0, tk, tq)
            key = jnp.where(key_pos <= query_pos, key, INT_MIN)
        hi_ref[pl.ds(k0, tk), :] = (key >> 16).astype(I16)
        lo_ref[pl.ds(k0, tk), :] = (key ^ 0x8000).astype(I16)

    def score_group(i, carry):
        for j in range(SCORE_BLOCKS):
            score_block(SCORE_BLOCKS * i + j, False)
        return carry

    n_before = nkb - 1
    lax.fori_loop(0, n_before // SCORE_BLOCKS, score_group, 0)
    first_left = n_before - n_before % SCORE_BLOCKS
    for left in range(1, SCORE_BLOCKS):
        @pl.when(n_before % SCORE_BLOCKS == left)
        def _(left=left):
            for j in range(left):
                score_block(first_left + j, False)

    score_block(nkb - 1, True)

    n_count = (nkb + COUNT_BLOCKS - 1) // COUNT_BLOCKS
    rows = COUNT_BLOCKS * tk

    def pad_block(kb, carry):
        k0 = pl.multiple_of(kb * tk, tk)
        hi_ref[pl.ds(k0, tk), :] = jnp.full((tk, tq), I16_MIN, I16)
        lo_ref[pl.ds(k0, tk), :] = jnp.full((tk, tq), I16_MIN, I16)
        return carry

    lax.fori_loop(nkb, n_count * COUNT_BLOCKS, pad_block, 0)

    def count(flags_fn):
        def blk(i, cnt):
            r0 = pl.multiple_of(i * rows, rows)
            groups = []
            for c in range(rows // COUNT_CHUNK):
                flags = flags_fn(r0 + c * COUNT_CHUNK, COUNT_CHUNK)
                parts = [flags[r * BF16_ROWS:(r + 1) * BF16_ROWS, :]
                         for r in range(COUNT_CHUNK // BF16_ROWS)]
                while len(parts) > 1:
                    parts = [a + b for a, b in zip(parts[0::2], parts[1::2])]
                groups.append(parts[0])
            while len(groups) > 1:
                groups = [a + b for a, b in zip(groups[0::2], groups[1::2])]
            return cnt + groups[0]
        cnt = lax.fori_loop(0, n_count, blk, jnp.zeros((BF16_ROWS, tq), I16))
        return jnp.sum(cnt.astype(jnp.int32), axis=0, keepdims=True)

    flag, no_flag = I16(1), I16(0)

    def count_ge(ref, thr16):
        return count(lambda r, n: jnp.where(ref[pl.ds(r, n), :] >= thr16, flag, no_flag))

    def kth_largest(ref, k):
        def bisect(it, carry):
            thr, n_ge, n_gt = carry
            cand = thr + jnp.left_shift(jnp.int32(1), 15 - it)
            cnt = count_ge(ref, cand.astype(I16))
            ok = cnt >= k
            return jnp.where(ok, cand, thr), jnp.where(ok, cnt, n_ge), jnp.where(ok, n_gt, cnt)
        zeros = jnp.zeros((1, tq), jnp.int32)
        thr, n_ge, n_gt = lax.fori_loop(0, 16, bisect, (zeros + I16_MIN, zeros, zeros))
        return thr.astype(I16), n_ge, n_gt

    thr_hi, _, above_hi = kth_largest(hi_ref, n_sel)
    need_lo = n_sel - above_hi

    def park(kb, carry):
        k0 = pl.multiple_of(kb * tk, tk)
        lo_ref[pl.ds(k0, tk), :] = jnp.where(hi_ref[pl.ds(k0, tk), :] == thr_hi,
                                             lo_ref[pl.ds(k0, tk), :], I16(I16_MIN))
        return carry

    lax.fori_loop(0, nkb, park, 0)
    thr_lo, at_least_lo, above_lo = kth_largest(lo_ref, need_lo)
    need_ties = need_lo - above_lo
    n_ties = jnp.where(thr_lo == I16(I16_MIN), tk * n_kblocks, at_least_lo - above_lo)
    all_ties_fit = jnp.max(n_ties - need_ties) <= 0

    one, zero, neg = BF16(1.0), BF16(0.0), BF16(NEG)

    def store_mask(kb, sel, diagonal):
        k0 = pl.multiple_of(kb * tk, tk)
        if diagonal:
            key_pos, query_pos = _key_query_iotas(k0, q0, tk, tq)
            sel = sel * (key_pos <= query_pos).astype(BF16)
        mask_ref[0, pl.ds(k0, tk), :] = jnp.where(sel > zero, zero, neg)

    def emit_all_ties(kb, diagonal):
        k0 = pl.multiple_of(kb * tk, tk)
        hi = hi_ref[pl.ds(k0, tk), :]
        lo = lo_ref[pl.ds(k0, tk), :]
        store_mask(kb, jnp.where(hi > thr_hi, one,
                                 jnp.where(hi == thr_hi, jnp.where(lo >= thr_lo, one, zero), zero)),
                   diagonal)

    earlier = (lax.broadcasted_iota(jnp.int32, (tk, tk), 1)
               < lax.broadcasted_iota(jnp.int32, (tk, tk), 0)).astype(BF16)
    ones_rows = jnp.ones((BF16_ROWS, tk), BF16)
    need_ties_f = need_ties.astype(F32)

    def emit_ranked_ties(kb, ties_seen, diagonal):
        k0 = pl.multiple_of(kb * tk, tk)
        hi = hi_ref[pl.ds(k0, tk), :]
        lo = lo_ref[pl.ds(k0, tk), :]
        above = jnp.where(hi > thr_hi, one, jnp.where(lo > thr_lo, one, zero))
        eq = jnp.where(hi == thr_hi, jnp.where(lo == thr_lo, one, zero), zero)
        rank = jnp.dot(earlier, eq, preferred_element_type=F32).astype(BF16)
        room = jnp.clip(need_ties_f - ties_seen, -1.0, float(tk)).astype(BF16)
        store_mask(kb, jnp.where(rank < room, jnp.maximum(above, eq), above), diagonal)
        return ties_seen + jnp.dot(ones_rows, eq, preferred_element_type=F32)[0:1]

    def emit_fast():
        def body(kb, carry):
            emit_all_ties(kb, False)
            return carry
        lax.fori_loop(0, nkb - 1, body, 0)
        emit_all_ties(nkb - 1, True)

    def emit_slow():
        ties_seen = lax.fori_loop(0, nkb - 1, lambda kb, seen: emit_ranked_ties(kb, seen, False),
                                  jnp.zeros((1, tq), F32))
        emit_ranked_ties(nkb - 1, ties_seen, True)

    lax.cond(all_ties_fit, emit_fast, emit_slow)

    def fill(kb, carry):
        k0 = pl.multiple_of(kb * tk, tk)
        mask_ref[0, pl.ds(k0, tk), :] = jnp.full((tk, tq), NEG, mask_ref.dtype)
        return carry

    lax.fori_loop(nkb, n_kblocks, fill, 0)


def _dsa_select(zt, zs, B, S, qix_blk, n_sel):
    tq = tk = min(S, ATTN_BLOCK)
    nq = S // tq
    assert nq % COUNT_BLOCKS == 0
    qw = N_IDX_HEADS * IDX_DIM
    return pl.pallas_call(
        functools.partial(_dsa_select_kernel, tq=tq, tk=tk, n_sel=n_sel, n_kblocks=S // tk),
        grid=(B, nq),
        in_specs=[pl.BlockSpec((qw, tq), lambda b, i: (qix_blk, b * nq + i)),
                  pl.BlockSpec((S, 2 * IDX_DIM), lambda b, i: (b, 0)),
                  pl.BlockSpec((tq, 2 * IDX_DIM), lambda b, i: (b * nq + i, 1))],
        out_specs=pl.BlockSpec((1, S, tq), lambda b, i: (b * nq + i, 0, 0)),
        out_shape=jax.ShapeDtypeStruct((B * nq, S, tq), BF16),
        scratch_shapes=[pltpu.VMEM((S, tq), I16), pltpu.VMEM((S, tq), I16)],
        compiler_params=_params("arbitrary", "arbitrary"),
        name="dsa_select",
    )(zt, zs, zs)


def _rel_bucket(dist):
    n = jnp.maximum(dist, 0)
    max_exact = N_BUCKETS // 2
    nf = jnp.maximum(n, 1).astype(F32)
    large = max_exact + (jnp.log(nf / max_exact) / math.log(MAX_DISTANCE / max_exact)
                         * (N_BUCKETS - max_exact)).astype(jnp.int32)
    large = jnp.minimum(large, N_BUCKETS - 1)
    return jnp.where(n < max_exact, n, large)


def _bias_tiles(rel_bias, tq):
    assert tq >= MAX_DISTANCE
    rb = rel_bias.astype(F32)
    key = jnp.arange(tq)[:, None]
    query = jnp.arange(tq)[None, :]
    bucket = _rel_bucket(jnp.stack([query - key, tq + query - key]))
    onehot = (bucket[None] == jnp.arange(N_BUCKETS)[:, None, None, None]).astype(F32)
    return jnp.einsum("nh,nikq->hikq", (rb - rb[N_BUCKETS - 1]) * LOG2_E, onehot,
                      precision=lax.Precision.HIGHEST)


def _dsa_attn_kernel(q_ref, k_ref, vt_ref, mask_ref, bias_ref, o_ref, m_ref, acc_ref, logit_ref, *,
                     tq, n_heads):
    qi = pl.program_id(2)
    m_ref[...] = jnp.full_like(m_ref, NEG)
    acc_ref[...] = jnp.zeros_like(acc_ref)
    ones_rows = jnp.ones((BF16_ROWS, tq), BF16)

    def step(blocks):
        k0s = [pl.multiple_of(kb * tq, tq) for kb, _ in blocks]
        masks = [mask_ref[0, pl.ds(k0, tq), :].astype(F32) for k0 in k0s]

        def head_scores(g):
            s = jnp.dot(k_ref[pl.ds(k0s[0], len(blocks) * tq), _head_cols(g)], q_ref[_head_cols(g), :],
                        preferred_element_type=F32)
            return [s[j * tq:(j + 1) * tq] for j in range(len(blocks))]

        def head_logits(g, scores):
            m_new = m_ref[g]
            for j, (_, bias_idx) in enumerate(blocks):
                s = scores[j] + masks[j]
                if bias_idx is not None:
                    s = s + bias_ref[g, bias_idx]
                logit_ref[g, j] = s
                m_new = jnp.maximum(m_new, jnp.max(s, axis=0, keepdims=True))
            return m_new

        def head_accumulate(g, m_new):
            acc = jnp.exp2(m_ref[g] - m_new) * acc_ref[g]
            for j, k0 in enumerate(k0s):
                p = jnp.exp2(logit_ref[g, j] - m_new).astype(BF16)
                vt = jnp.concatenate([vt_ref[_head_cols(g), pl.ds(k0, tq)], ones_rows], axis=0)
                acc = acc + jnp.dot(vt, p, preferred_element_type=F32)
            acc_ref[g] = acc
            m_ref[g] = m_new

        scores = [head_scores(g) for g in range(n_heads)]
        maxima = [head_logits(g, scores[g]) for g in range(n_heads)]
        for g in range(n_heads):
            head_accumulate(g, maxima[g])

    n_far = jnp.maximum(qi - 1, 0)

    def far_group(i, carry):
        step([(FAR_BLOCKS * i + j, None) for j in range(FAR_BLOCKS)])
        return carry

    lax.fori_loop(0, n_far // FAR_BLOCKS, far_group, 0)
    first_left = n_far - n_far % FAR_BLOCKS
    for left in range(1, FAR_BLOCKS):
        @pl.when(n_far % FAR_BLOCKS == left)
        def _(left=left):
            step([(first_left + j, None) for j in range(left)])

    @pl.when(qi >= 1)
    def _():
        step([(qi - 1, 1), (qi, 0)])

    @pl.when(qi == 0)
    def _():
        step([(qi, 0)])
    for g in range(n_heads):
        acc = acc_ref[g]
        out_t = acc[:HEAD_DIM] / acc[HEAD_DIM:HEAD_DIM + 1]
        o_ref[:, _head_cols(g)] = out_t.T.astype(o_ref.dtype)


def _dsa_attn(z, zt, mask, bias_tiles, B, S, H, q_blk, k_blk, v_blk):
    M = z.shape[0]
    tq = bias_tiles.shape[-1]
    nq = S // tq
    G = min(H, DSA_HEAD_GROUP)
    gw = G * HEAD_DIM
    return pl.pallas_call(
        functools.partial(_dsa_attn_kernel, tq=tq, n_heads=G),
        grid=(B, H // G, nq),
        in_specs=_attn_group_specs(S, tq, nq, G, q_blk, k_blk, v_blk, kv_buffers=1) + [
            pl.BlockSpec((1, S, tq), lambda b, hg, i: (b * nq + i, 0, 0)),
            pl.BlockSpec((G, 2, tq, tq), lambda b, hg, i: (hg, 0, 0, 0),
                         pipeline_mode=pl.Buffered(1))],
        out_specs=pl.BlockSpec((tq, gw), lambda b, hg, i: (b * nq + i, hg)),
        out_shape=jax.ShapeDtypeStruct((M, H * HEAD_DIM), BF16),
        scratch_shapes=[pltpu.VMEM((G, 1, tq), F32),
                        pltpu.VMEM((G, HEAD_DIM + BF16_ROWS, tq), F32),
                        pltpu.VMEM((G, FAR_BLOCKS, tq, tq), F32)],
        compiler_params=_params("arbitrary", "arbitrary", "arbitrary"),
        name="dsa_attn",
    )(zt, z, zt, mask, bias_tiles)


def _mix_out_kernel(osb_ref, ods_ref, gsb_ref, gds_ref, x_ref, wsb_ref, wds_ref, wout_ref,
                    gt_ref, g_ref, sc_ref, sh_ref, x1_ref, h2_ref):
    t_sb = jnp.dot(osb_ref[...], wsb_ref[...], preferred_element_type=F32)
    t_ds = jnp.dot(ods_ref[...], wds_ref[...], preferred_element_type=F32)
    merged = (_sigmoid(gsb_ref[...].astype(F32)) * t_sb
              + _sigmoid(gds_ref[...].astype(F32)) * t_ds)
    y = jnp.dot(merged.astype(BF16), wout_ref[...], preferred_element_type=F32)
    x1 = x_ref[...] + gt_ref[0] * y
    x1_ref[...] = x1
    h2_ref[...] = _rms_mod(x1, g_ref[...], sc_ref[0], sh_ref[0]).astype(h2_ref.dtype)


def _mix_out(o_sb, o_ds, z, gate_blk, x2, w_sb, w_ds, w_out, gt, g, sc, sh, S):
    M, D = x2.shape
    B = gt.shape[0]
    W = o_sb.shape[1]
    tm = min(S, 512)
    per_b = S // tm
    row = lambda i: (i, 0)
    const = lambda i: (0, 0)
    per_batch = lambda i: (i // per_b, 0, 0)
    return pl.pallas_call(
        _mix_out_kernel,
        grid=(M // tm,),
        in_specs=[pl.BlockSpec((tm, W), row),
                  pl.BlockSpec((tm, W), row),
                  pl.BlockSpec((tm, D), lambda i: (i, gate_blk)),
                  pl.BlockSpec((tm, D), lambda i: (i, gate_blk + 1)),
                  pl.BlockSpec((tm, D), row),
                  pl.BlockSpec((W, D), const),
                  pl.BlockSpec((W, D), const),
                  pl.BlockSpec((D, D), const),
                  pl.BlockSpec((1, 1, D), per_batch),
                  pl.BlockSpec((1, D), const),
                  pl.BlockSpec((1, 1, D), per_batch),
                  pl.BlockSpec((1, 1, D), per_batch)],
        out_specs=[pl.BlockSpec((tm, D), row), pl.BlockSpec((tm, D), row)],
        out_shape=[jax.ShapeDtypeStruct((M, D), F32), jax.ShapeDtypeStruct((M, D), BF16)],
        compiler_params=_params("arbitrary"),
        name="mix_out",
    )(o_sb, o_ds, z, z, x2, w_sb, w_ds, w_out, gt.reshape(B, 1, D), g.reshape(1, D),
      sc.reshape(B, 1, D), sh.reshape(B, 1, D))


HALO = BF16_ROWS


def _ffn_act_kernel(h_ref, halo_ref, wg_ref, wu_ref, cw_ref, cb_ref, o_ref, wgb_ref, wub_ref, *,
                    tiles_per_seq):
    i = pl.program_id(1)

    @pl.when(i == 0)
    def _():
        wgb_ref[...] = wg_ref[...].astype(BF16)
        wub_ref[...] = wu_ref[...].astype(BF16)

    h = h_ref[...]
    wg = wgb_ref[...]
    g0 = jnp.dot(h, wg, preferred_element_type=F32)
    g_prev = jnp.dot(halo_ref[...], wg, preferred_element_type=F32)
    g_prev = jnp.where(i % tiles_per_seq == 0, 0.0, g_prev)
    ridx = lax.broadcasted_iota(jnp.int32, g0.shape, 0)
    g1 = jnp.where(ridx == 0, g_prev[HALO - 1:HALO, :], pltpu.roll(g0, 1, 0))
    g2 = jnp.where(ridx == 0, g_prev[HALO - 2:HALO - 1, :],
                   jnp.where(ridx == 1, g_prev[HALO - 1:HALO, :], pltpu.roll(g0, 2, 0)))
    cw = cw_ref[...]
    a = cb_ref[...] + g2 * cw[0:1, :] + g1 * cw[1:2, :] + g0 * cw[2:3, :]
    u = jnp.dot(h, wub_ref[...], preferred_element_type=F32)
    o_ref[...] = (a * _sigmoid(a) * u).astype(o_ref.dtype)


def _ffn_out_kernel(a_ref, wd_ref, x1_ref, gt_ref, gf_ref, o_ref):
    k = pl.program_id(1)

    @pl.when(k == 0)
    def _():
        o_ref[...] = jnp.zeros_like(o_ref)

    o_ref[...] += jnp.dot(a_ref[...], wd_ref[...], preferred_element_type=F32)

    @pl.when(k == pl.num_programs(1) - 1)
    def _():
        x = x1_ref[...] + gt_ref[0] * o_ref[...]
        ms = jnp.mean(x * x, axis=-1, keepdims=True)
        o_ref[...] = x * lax.rsqrt(ms + EPS) * gf_ref[...]


def _conv_ffn(h2, x1, w_gate, w_up, layer, w_down, conv_w, conv_b, gt, g_final, S):
    M, D = x1.shape
    B = gt.shape[0]
    F = w_gate.shape[2]
    tm = min(S, 1024)
    tf = min(F, 512)
    per_b = S // tm
    halo_per_tile = tm // HALO
    act = pl.pallas_call(
        functools.partial(_ffn_act_kernel, tiles_per_seq=per_b),
        grid=(F // tf, M // tm),
        in_specs=[pl.BlockSpec((tm, D), lambda f, i: (i, 0)),
                  pl.BlockSpec((HALO, D), lambda f, i: (jnp.maximum(i * halo_per_tile - 1, 0), 0)),
                  pl.BlockSpec((None, D, tf), lambda f, i: (layer, 0, f)),
                  pl.BlockSpec((None, D, tf), lambda f, i: (layer, 0, f)),
                  pl.BlockSpec((CONV_WIDTH, tf), lambda f, i: (0, f)),
                  pl.BlockSpec((1, tf), lambda f, i: (0, f))],
        out_specs=pl.BlockSpec((tm, tf), lambda f, i: (i, f)),
        out_shape=jax.ShapeDtypeStruct((M, F), BF16),
        scratch_shapes=[pltpu.VMEM((D, tf), BF16), pltpu.VMEM((D, tf), BF16)],
        compiler_params=_params("arbitrary", "arbitrary"),
        name="ffn_act",
    )(h2, h2, w_gate, w_up, conv_w, conv_b.reshape(1, F))
    return pl.pallas_call(
        _ffn_out_kernel,
        grid=(M // tm, F // tf),
        in_specs=[pl.BlockSpec((tm, tf), lambda i, k: (i, k)),
                  pl.BlockSpec((tf, D), lambda i, k: (k, 0)),
                  pl.BlockSpec((tm, D), lambda i, k: (i, 0)),
                  pl.BlockSpec((1, 1, D), lambda i, k: (i // per_b, 0, 0)),
                  pl.BlockSpec((1, D), lambda i, k: (0, 0))],
        out_specs=pl.BlockSpec((tm, D), lambda i, k: (i, 0)),
        out_shape=jax.ShapeDtypeStruct((M, D), F32),
        compiler_params=_params("arbitrary", "arbitrary"),
        name="ffn_out",
    )(act, w_down, x1, gt.reshape(B, 1, D), g_final.reshape(1, D))


def kernel(x, c, w_ada, b_ada, g_mix, w_in, w_o_sb, w_o_dsa, w_out, rel_bias, g_ffn, w_gate,
           w_up, conv_w, conv_b, w_down, g_final):
    B, S, D = x.shape
    depth = w_ada.shape[0]
    W = w_o_sb.shape[1]
    H = W // HEAD_DIM
    qw = N_IDX_HEADS * IDX_DIM
    assert w_o_dsa.shape[1] == W and D % W == 0 and S % ATTN_BLOCK == 0 and W % qw == 0
    n_sel = min(TOPK_MAX, S // 4)
    scale = HEAD_DIM ** -0.5
    x2 = x.reshape(B * S, D)

    for l in range(depth):
        mod = _adaln(c, w_ada[l], b_ada[l])
        sh1, sc1, gt1, sh2, sc2, gt2 = jnp.split(mod, 6, axis=-1)

        wt = w_in[l].T
        o_kix = 6 * W + qw
        o_wix = o_kix + IDX_DIM
        o_gate = o_wix + N_IDX_HEADS
        k_rows = wt[o_kix:o_wix]
        wt_tail = jnp.concatenate(
            [k_rows, k_rows, wt[o_wix:o_gate],
             jnp.zeros((2 * IDX_DIM - N_IDX_HEADS, D), wt.dtype)], axis=0)

        h1 = _norm_mod(x2, g_mix[l], sc1, sh1, S)
        z = _in_proj_wt(h1, wt, (1, 4), W, (), "in_proj")
        zt = _in_proj_wt(h1, wt, (0, 2, 3, 5, 6), W, ((0, scale), (2, scale * LOG2_E)),
                         "in_proj_t", transposed=True)
        zg = _in_proj_wt(h1, wt[o_gate:], tuple(range(2 * D // W)), W, (), "in_proj_gates")
        zs = _in_proj_wt(h1, wt_tail, (0,), 4 * IDX_DIM, (), "in_proj_idx")

        o_sb = _sb_attn(z, zt, B, S, H, 0, 0, H)
        mask = _dsa_select(zt, zs, B, S, 4 * W // qw, n_sel)
        o_ds = _dsa_attn(z, zt, mask, _bias_tiles(rel_bias, min(S, ATTN_BLOCK)),
                         B, S, H, 2 * H, H, 3 * H)

        x2, h2 = _mix_out(o_sb, o_ds, zg, 0, x2, w_o_sb[l].astype(BF16),
                          w_o_dsa[l].astype(BF16), w_out[l].astype(BF16), gt1, g_ffn[l], sc2, sh2, S)
        last = l == depth - 1
        assert last, "the final rms_norm is fused into the last layer's FFN"
        x2 = _conv_ffn(h2, x2, w_gate, w_up, l, w_down[l].astype(BF16),
                       conv_w[l], conv_b[l], gt2, g_final, S)
    return x2.reshape(B, S, D)
```

```python
import functools
import math

import jax
import jax.numpy as jnp
from jax import lax
from jax.experimental import pallas as pl
from jax.experimental.pallas import tpu as pltpu

HEAD_DIM = 128
N_IDX_HEADS = 16
IDX_DIM = 64
TOPK_MAX = 256
N_BUCKETS = 32
MAX_DISTANCE = 128
CONV_WIDTH = 3
EPS = 1e-6

F32 = jnp.float32
BF16 = jnp.bfloat16
NEG = -1e30
I16 = jnp.int16
INT_MIN = -2 ** 31
I16_MIN = -2 ** 15
EXP_ZERO_BELOW = 104.0
LOG2_E = math.log2(math.e)
V7X_VMEM_LIMIT = 58 * 1024 * 1024
BF16_ROWS = 16
ATTN_BLOCK = 256
SB_HEAD_GROUP = 4
DSA_HEAD_GROUP = 8
FAR_BLOCKS = 4
SCORE_BLOCKS = 4
COUNT_BLOCKS = 4
COUNT_CHUNK = 64
NT_DIMS = (((1,), (1,)), ((), ()))


def _params(*sem):
    return pltpu.CompilerParams(dimension_semantics=sem, vmem_limit_bytes=V7X_VMEM_LIMIT)


def _sigmoid(x):
    return 1.0 / (1.0 + jnp.exp(-x))


def _adaln_kernel(ct_ref, w_ref, b_ref, o_ref):
    ct = ct_ref[...]
    act = ct * _sigmoid(ct)
    w = w_ref[...]
    for b in range(ct.shape[1]):
        o_ref[b:b + 1, :] = jnp.sum(act[:, b:b + 1] * w, axis=0, keepdims=True) + b_ref[...]


def _adaln(c, w, bias):
    B, D = c.shape
    N = w.shape[1]
    tn = min(N, 1024)
    return pl.pallas_call(
        _adaln_kernel,
        grid=(N // tn,),
        in_specs=[pl.BlockSpec((D, B), lambda j: (0, 0)),
                  pl.BlockSpec((D, tn), lambda j: (0, j)),
                  pl.BlockSpec((1, tn), lambda j: (0, j))],
        out_specs=pl.BlockSpec((B, tn), lambda j: (0, j)),
        out_shape=jax.ShapeDtypeStruct((B, N), F32),
        compiler_params=_params("arbitrary"),
        name="adaln",
    )(c.T, w, bias.reshape(1, N))


def _rms_mod(x, g, sc, sh):
    ms = jnp.mean(x * x, axis=-1, keepdims=True)
    y = x * lax.rsqrt(ms + EPS) * g
    return y * (1.0 + sc) + sh


def _norm_mod_kernel(x_ref, g_ref, sc_ref, sh_ref, o_ref):
    o_ref[...] = _rms_mod(x_ref[...], g_ref[...], sc_ref[0], sh_ref[0]).astype(o_ref.dtype)


def _norm_mod(x2, g, sc, sh, S):
    M, D = x2.shape
    B = sc.shape[0]
    tm = min(S, 1024)
    per_b = S // tm
    return pl.pallas_call(
        _norm_mod_kernel,
        grid=(M // tm,),
        in_specs=[pl.BlockSpec((tm, D), lambda i: (i, 0)),
                  pl.BlockSpec((1, D), lambda i: (0, 0)),
                  pl.BlockSpec((1, 1, D), lambda i: (i // per_b, 0, 0)),
                  pl.BlockSpec((1, 1, D), lambda i: (i // per_b, 0, 0))],
        out_specs=pl.BlockSpec((tm, D), lambda i: (i, 0)),
        out_shape=jax.ShapeDtypeStruct((M, D), BF16),
        compiler_params=_params("arbitrary"),
        name="norm_mod",
    )(x2, g.reshape(1, D), sc.reshape(B, 1, D), sh.reshape(B, 1, D))


def _in_proj_kernel(a_ref, b_ref, o_ref, *, tile_scales):
    acc = jnp.dot(a_ref[...], b_ref[...], preferred_element_type=F32)
    o_ref[...] = (acc * _tile_scale(tile_scales)).astype(o_ref.dtype)


def _in_proj_wt_kernel(a_ref, wt_ref, o_ref, wb_ref, *, tile_scales):
    @pl.when(pl.program_id(1) == 0)
    def _():
        wb_ref[...] = wt_ref[...].T.astype(BF16)

    _in_proj_kernel(a_ref, wb_ref, o_ref, tile_scales=tile_scales)


def _tile_scale(tile_scales):
    j = pl.program_id(0)
    scale = jnp.float32(1.0)
    for tile, tile_scale in tile_scales:
        scale = jnp.where(j == tile, tile_scale, scale)
    return scale


def _in_proj_t_kernel(a_ref, wt_ref, o_ref, wb_ref, *, tile_scales):
    @pl.when(pl.program_id(1) == 0)
    def _():
        wb_ref[...] = wt_ref[...].astype(BF16)

    acc = lax.dot_general(wb_ref[...], a_ref[...], NT_DIMS, preferred_element_type=F32)
    o_ref[...] = (acc * _tile_scale(tile_scales)).astype(o_ref.dtype)


def _tile_index_fn(tiles):
    steps = []
    for j, t in enumerate(tiles):
        while len(steps) < t - j:
            steps.append(j)
    return lambda j: j + sum(jnp.where(j >= first, 1, 0) for first in steps)


def _in_proj_wt(h, wt, tiles, tn, tile_scales, name, transposed=False):
    M, K = h.shape
    tm = min(M, 1024)
    n = len(tiles)
    tile_of = _tile_index_fn(tiles)
    in_specs = [pl.BlockSpec((tm, K), lambda j, i: (i, 0)),
                pl.BlockSpec((tn, K), lambda j, i: (tile_of(j), 0))]
    if transposed:
        body, wb_shape = functools.partial(_in_proj_t_kernel, tile_scales=tile_scales), (tn, K)
        out_spec = pl.BlockSpec((tn, tm), lambda j, i: (j, i))
        out_shape = jax.ShapeDtypeStruct((n * tn, M), BF16)
    else:
        body, wb_shape = functools.partial(_in_proj_wt_kernel, tile_scales=tile_scales), (K, tn)
        out_spec = pl.BlockSpec((tm, tn), lambda j, i: (i, j))
        out_shape = jax.ShapeDtypeStruct((M, n * tn), BF16)
    return pl.pallas_call(
        body,
        grid=(n, M // tm),
        in_specs=in_specs,
        out_specs=out_spec,
        out_shape=out_shape,
        scratch_shapes=[pltpu.VMEM(wb_shape, BF16)],
        compiler_params=_params("arbitrary", "arbitrary"),
        name=name,
    )(h, wt)


def _key_query_iotas(k0, q0, tk, tq):
    key_pos = k0 + lax.broadcasted_iota(jnp.int32, (tk, tq), 0)
    query_pos = q0 + lax.broadcasted_iota(jnp.int32, (tk, tq), 1)
    return key_pos, query_pos


def _head_cols(g):
    return slice(g * HEAD_DIM, (g + 1) * HEAD_DIM)


def _sb_kernel(q_ref, k_ref, vt_ref, o_ref, run_ref, acc_ref, *, tq, tk, n_heads):
    qi = pl.program_id(2)
    q0 = qi * tq
    run_ref[...] = jnp.zeros_like(run_ref)
    acc_ref[...] = jnp.zeros_like(acc_ref)
    later = (lax.broadcasted_iota(jnp.int32, (tk, tk), 1)
             > lax.broadcasted_iota(jnp.int32, (tk, tk), 0)).astype(BF16)

    def step(kbs, first_is_diagonal):
        heads = range(n_heads)
        k0s = [pl.multiple_of(kb * tk, tk) for kb in kbs]
        causal = None
        if first_is_diagonal:
            key_pos, query_pos = _key_query_iotas(k0s[0], q0, tk, tq)
            causal = key_pos < query_pos
        masked = [first_is_diagonal and j == 0 for j in range(len(kbs))]
        spans = [jnp.dot(k_ref[pl.ds(k0s[-1], len(kbs) * tk), _head_cols(g)], q_ref[_head_cols(g), :],
                         preferred_element_type=F32) for g in heads]
        zs = [[span[(len(kbs) - 1 - j) * tk:(len(kbs) - j) * tk] for span in spans]
              for j in range(len(kbs))]
        sps = [[jnp.maximum(z, 0.0) + jnp.log(1.0 + jnp.exp(-jnp.abs(z))) for z in zj] for zj in zs]
        spms = [[jnp.where(causal, sp, 0.0) if masked[j] else sp for sp in spj]
                for j, spj in enumerate(sps)]
        his = [[spm.astype(BF16) for spm in spj] for spj in spms]
        los = [[(spm - hi.astype(F32)).astype(BF16) for spm, hi in zip(spj, hij)]
               for spj, hij in zip(spms, his)]
        suffixes = [[jnp.dot(later, hi, preferred_element_type=F32)
                     + jnp.dot(later, lo, preferred_element_type=F32) for hi, lo in zip(hij, loj)]
                    for hij, loj in zip(his, los)]
        weights, new_runs = [], []
        for g in heads:
            run = run_ref[g]
            head_weights = []
            for j in range(len(kbs)):
                a = jnp.exp(zs[j][g] - sps[j][g] - suffixes[j][g] - run)
                if masked[j]:
                    a = jnp.where(causal, a, 0.0)
                head_weights.append(a.astype(BF16))
                run = run + jnp.sum(spms[j][g], axis=0, keepdims=True)
            weights.append(head_weights)
            new_runs.append(run)
        min_run = None
        for g in heads:
            acc = acc_ref[g]
            for j, k0 in enumerate(k0s):
                acc = acc + jnp.dot(vt_ref[_head_cols(g), pl.ds(k0, tk)], weights[g][j],
                                    preferred_element_type=F32)
            acc_ref[g] = acc
            run_ref[g] = new_runs[g]
            head_min = jnp.min(new_runs[g])
            min_run = head_min if min_run is None else jnp.minimum(min_run, head_min)
        return min_run > EXP_ZERO_BELOW

    def body(carry):
        kb, _ = carry
        return kb - 1, step([kb], False)

    def cond(carry):
        kb, dead = carry
        return jnp.logical_and(kb >= 0, jnp.logical_not(dead))

    @pl.when(qi == 0)
    def _():
        step([qi], True)

    @pl.when(qi >= 1)
    def _():
        dead = step([qi, qi - 1], True)
        lax.while_loop(cond, body, (qi - 2, dead))

    for g in range(n_heads):
        o_ref[:, _head_cols(g)] = acc_ref[g].T.astype(o_ref.dtype)


def _attn_group_specs(S, tq, nq, G, q_blk, k_blk, v_blk, kv_buffers):
    assert q_blk % G == 0 and k_blk % G == 0 and v_blk % G == 0
    gw = G * HEAD_DIM
    mode = pl.Buffered(kv_buffers)
    return [pl.BlockSpec((gw, tq), lambda b, hg, i: (q_blk // G + hg, b * nq + i)),
            pl.BlockSpec((S, gw), lambda b, hg, i: (b, k_blk // G + hg), pipeline_mode=mode),
            pl.BlockSpec((gw, S), lambda b, hg, i: (v_blk // G + hg, b), pipeline_mode=mode)]


def _sb_attn(z, zt, B, S, H, q_blk, k_blk, v_blk):
    M = z.shape[0]
    tq = tk = min(S, ATTN_BLOCK)
    nq = S // tq
    G = min(H, SB_HEAD_GROUP)
    gw = G * HEAD_DIM
    return pl.pallas_call(
        functools.partial(_sb_kernel, tq=tq, tk=tk, n_heads=G),
        grid=(B, H // G, nq),
        in_specs=_attn_group_specs(S, tq, nq, G, q_blk, k_blk, v_blk, kv_buffers=2),
        out_specs=pl.BlockSpec((tq, gw), lambda b, hg, i: (b * nq + i, hg)),
        out_shape=jax.ShapeDtypeStruct((M, H * HEAD_DIM), BF16),
        scratch_shapes=[pltpu.VMEM((G, 1, tq), F32), pltpu.VMEM((G, HEAD_DIM, tq), F32)],
        compiler_params=_params("arbitrary", "arbitrary", "arbitrary"),
        name="sb_attn",
    )(zt, z, zt)


def _dsa_select_kernel(q_ref, k_ref, w_ref, mask_ref, hi_ref, lo_ref, *, tq, tk, n_sel, n_kblocks):
    qi = pl.program_id(1)
    q0 = qi * tq
    nkb = (q0 + tq + tk - 1) // tk
    lane = lax.broadcasted_iota(jnp.int32, (tk, 2 * IDX_DIM), 1)
    w_scale = (N_IDX_HEADS ** -0.5) * (IDX_DIM ** -0.5)
    wt = (w_ref[...].astype(F32) * w_scale).T

    def score_block(kb, diagonal):
        k0 = pl.multiple_of(kb * tk, tk)
        kk = k_ref[pl.ds(k0, tk), :]
        k_halves = (jnp.where(lane < IDX_DIM, kk, jnp.zeros_like(kk)),
                    jnp.where(lane >= IDX_DIM, kk, jnp.zeros_like(kk)))
        acc = jnp.zeros((tk, tq), F32)
        for p in range(N_IDX_HEADS // 2):
            q2t = q_ref[p * 2 * IDX_DIM:(p + 1) * 2 * IDX_DIM, :]
            for half in range(2):
                h = 2 * p + half
                ph = jnp.dot(k_halves[half], q2t, preferred_element_type=F32)
                acc = acc + jnp.maximum(ph, 0.0) * wt[h:h + 1, :]
        bits = lax.bitcast_convert_type(acc, jnp.int32)
        key = bits ^ ((bits >> 31) & 0x7FFFFFFF)
        if diagonal:
            key_pos, query_pos = _key_query_iotas(k0, q0, tk, tq)
            key = jnp.where(key_pos <= query_pos, key, INT_MIN)
        hi_ref[pl.ds(k0, tk), :] = (key >> 16).astype(I16)
        lo_ref[pl.ds(k0, tk), :] = (key ^ 0x8000).astype(I16)

    def score_group(i, carry):
        for j in range(SCORE_BLOCKS):
            score_block(SCORE_BLOCKS * i + j, False)
        return carry

    n_before = nkb - 1
    lax.fori_loop(0, n_before // SCORE_BLOCKS, score_group, 0)
    first_left = n_before - n_before % SCORE_BLOCKS
    for left in range(1, SCORE_BLOCKS):
        @pl.when(n_before % SCORE_BLOCKS == left)
        def _(left=left):
            for j in range(left):
                score_block(first_left + j, False)

    score_block(nkb - 1, True)

    n_count = (nkb + COUNT_BLOCKS - 1) // COUNT_BLOCKS
    rows = COUNT_BLOCKS * tk

    def pad_block(kb, carry):
        k0 = pl.multiple_of(kb * tk, tk)
        hi_ref[pl.ds(k0, tk), :] = jnp.full((tk, tq), I16_MIN, I16)
        lo_ref[pl.ds(k0, tk), :] = jnp.full((tk, tq), I16_MIN, I16)
        return carry

    lax.fori_loop(nkb, n_count * COUNT_BLOCKS, pad_block, 0)

    def count(flags_fn):
        def blk(i, cnt):
            r0 = pl.multiple_of(i * rows, rows)
            groups = []
            for c in range(rows // COUNT_CHUNK):
                flags = flags_fn(r0 + c * COUNT_CHUNK, COUNT_CHUNK)
                parts = [flags[r * BF16_ROWS:(r + 1) * BF16_ROWS, :]
                         for r in range(COUNT_CHUNK // BF16_ROWS)]
                while len(parts) > 1:
                    parts = [a + b for a, b in zip(parts[0::2], parts[1::2])]
                groups.append(parts[0])
            while len(groups) > 1:
                groups = [a + b for a, b in zip(groups[0::2], groups[1::2])]
            return cnt + groups[0]
        cnt = lax.fori_loop(0, n_count, blk, jnp.zeros((BF16_ROWS, tq), I16))
        return jnp.sum(cnt.astype(jnp.int32), axis=0, keepdims=True)

    flag, no_flag = I16(1), I16(0)

    def count_ge(ref, thr16):
        return count(lambda r, n: jnp.where(ref[pl.ds(r, n), :] >= thr16, flag, no_flag))

    def kth_largest(ref, k):
        def bisect(it, carry):
            thr, n_ge, n_gt = carry
            cand = thr + jnp.left_shift(jnp.int32(1), 15 - it)
            cnt = count_ge(ref, cand.astype(I16))
            ok = cnt >= k
            return jnp.where(ok, cand, thr), jnp.where(ok, cnt, n_ge), jnp.where(ok, n_gt, cnt)
        zeros = jnp.zeros((1, tq), jnp.int32)
        thr, n_ge, n_gt = lax.fori_loop(0, 16, bisect, (zeros + I16_MIN, zeros, zeros))
        return thr.astype(I16), n_ge, n_gt

    thr_hi, _, above_hi = kth_largest(hi_ref, n_sel)
    need_lo = n_sel - above_hi

    def park(kb, carry):
        k0 = pl.multiple_of(kb * tk, tk)
        lo_ref[pl.ds(k0, tk), :] = jnp.where(hi_ref[pl.ds(k0, tk), :] == thr_hi,
                                             lo_ref[pl.ds(k0, tk), :], I16(I16_MIN))
        return carry

    lax.fori_loop(0, nkb, park, 0)
    thr_lo, at_least_lo, above_lo = kth_largest(lo_ref, need_lo)
    need_ties = need_lo - above_lo
    n_ties = jnp.where(thr_lo == I16(I16_MIN), tk * n_kblocks, at_least_lo - above_lo)
    all_ties_fit = jnp.max(n_ties - need_ties) <= 0

    one, zero, neg = BF16(1.0), BF16(0.0), BF16(NEG)

    def store_mask(kb, sel, diagonal):
        k0 = pl.multiple_of(kb * tk, tk)
        if diagonal:
            key_pos, query_pos = _key_query_iotas(k0, q0, tk, tq)
            sel = sel * (key_pos <= query_pos).astype(BF16)
        mask_ref[0, pl.ds(k0, tk), :] = jnp.where(sel > zero, zero, neg)

    def emit_all_ties(kb, diagonal):
        k0 = pl.multiple_of(kb * tk, tk)
        hi = hi_ref[pl.ds(k0, tk), :]
        lo = lo_ref[pl.ds(k0, tk), :]
        store_mask(kb, jnp.where(hi > thr_hi, one,
                                 jnp.where(hi == thr_hi, jnp.where(lo >= thr_lo, one, zero), zero)),
                   diagonal)

    earlier = (lax.broadcasted_iota(jnp.int32, (tk, tk), 1)
               < lax.broadcasted_iota(jnp.int32, (tk, tk), 0)).astype(BF16)
    ones_rows = jnp.ones((BF16_ROWS, tk), BF16)
    need_ties_f = need_ties.astype(F32)

    def emit_ranked_ties(kb, ties_seen, diagonal):
        k0 = pl.multiple_of(kb * tk, tk)
        hi = hi_ref[pl.ds(k0, tk), :]
        lo = lo_ref[pl.ds(k0, tk), :]
        above = jnp.where(hi > thr_hi, one, jnp.where(lo > thr_lo, one, zero))
        eq = jnp.where(hi == thr_hi, jnp.where(lo == thr_lo, one, zero), zero)
        rank = jnp.dot(earlier, eq, preferred_element_type=F32).astype(BF16)
        room = jnp.clip(need_ties_f - ties_seen, -1.0, float(tk)).astype(BF16)
        store_mask(kb, jnp.where(rank < room, jnp.maximum(above, eq), above), diagonal)
        return ties_seen + jnp.dot(ones_rows, eq, preferred_element_type=F32)[0:1]

    def emit_fast():
        def body(kb, carry):
            emit_all_ties(kb, False)
            return carry
        lax.fori_loop(0, nkb - 1, body, 0)
        emit_all_ties(nkb - 1, True)

    def emit_slow():
        ties_seen = lax.fori_loop(0, nkb - 1, lambda kb, seen: emit_ranked_ties(kb, seen, False),
                                  jnp.zeros((1, tq), F32))
        emit_ranked_ties(nkb - 1, ties_seen, True)

    lax.cond(all_ties_fit, emit_fast, emit_slow)

    def fill(kb, carry):
        k0 = pl.multiple_of(kb * tk, tk)
        mask_ref[0, pl.ds(k0, tk), :] = jnp.full((tk, tq), NEG, mask_ref.dtype)
        return carry

    lax.fori_loop(nkb, n_kblocks, fill, 0)


def _dsa_select(zt, zs, B, S, qix_blk, n_sel):
    tq = tk = min(S, ATTN_BLOCK)
    nq = S // tq
    assert nq % COUNT_BLOCKS == 0
    qw = N_IDX_HEADS * IDX_DIM
    return pl.pallas_call(
        functools.partial(_dsa_select_kernel, tq=tq, tk=tk, n_sel=n_sel, n_kblocks=S // tk),
        grid=(B, nq),
        in_specs=[pl.BlockSpec((qw, tq), lambda b, i: (qix_blk, b * nq + i)),
                  pl.BlockSpec((S, 2 * IDX_DIM), lambda b, i: (b, 0)),
                  pl.BlockSpec((tq, 2 * IDX_DIM), lambda b, i: (b * nq + i, 1))],
        out_specs=pl.BlockSpec((1, S, tq), lambda b, i: (b * nq + i, 0, 0)),
        out_shape=jax.ShapeDtypeStruct((B * nq, S, tq), BF16),
        scratch_shapes=[pltpu.VMEM((S, tq), I16), pltpu.VMEM((S, tq), I16)],
        compiler_params=_params("arbitrary", "arbitrary"),
        name="dsa_select",
    )(zt, zs, zs)


def _rel_bucket(dist):
    n = jnp.maximum(dist, 0)
    max_exact = N_BUCKETS // 2
    nf = jnp.maximum(n, 1).astype(F32)
    large = max_exact + (jnp.log(nf / max_exact) / math.log(MAX_DISTANCE / max_exact)
                         * (N_BUCKETS - max_exact)).astype(jnp.int32)
    large = jnp.minimum(large, N_BUCKETS - 1)
    return jnp.where(n < max_exact, n, large)


def _bias_tiles(rel_bias, tq):
    assert tq >= MAX_DISTANCE
    rb = rel_bias.astype(F32)
    key = jnp.arange(tq)[:, None]
    query = jnp.arange(tq)[None, :]
    bucket = _rel_bucket(jnp.stack([query - key, tq + query - key]))
    onehot = (bucket[None] == jnp.arange(N_BUCKETS)[:, None, None, None]).astype(F32)
    return jnp.einsum("nh,nikq->hikq", (rb - rb[N_BUCKETS - 1]) * LOG2_E, onehot,
                      precision=lax.Precision.HIGHEST)


def _dsa_attn_kernel(q_ref, k_ref, vt_ref, mask_ref, bias_ref, o_ref, m_ref, acc_ref, logit_ref, *,
                     tq, n_heads):
    qi = pl.program_id(2)
    m_ref[...] = jnp.full_like(m_ref, NEG)
    acc_ref[...] = jnp.zeros_like(acc_ref)
    ones_rows = jnp.ones((BF16_ROWS, tq), BF16)

    def step(blocks):
        k0s = [pl.multiple_of(kb * tq, tq) for kb, _ in blocks]
        masks = [mask_ref[0, pl.ds(k0, tq), :].astype(F32) for k0 in k0s]

        def head_scores(g):
            s = jnp.dot(k_ref[pl.ds(k0s[0], len(blocks) * tq), _head_cols(g)], q_ref[_head_cols(g), :],
                        preferred_element_type=F32)
            return [s[j * tq:(j + 1) * tq] for j in range(len(blocks))]

        def head_logits(g, scores):
            m_new = m_ref[g]
            for j, (_, bias_idx) in enumerate(blocks):
                s = scores[j] + masks[j]
                if bias_idx is not None:
                    s = s + bias_ref[g, bias_idx]
                logit_ref[g, j] = s
                m_new = jnp.maximum(m_new, jnp.max(s, axis=0, keepdims=True))
            return m_new

        def head_accumulate(g, m_new):
            acc = jnp.exp2(m_ref[g] - m_new) * acc_ref[g]
            for j, k0 in enumerate(k0s):
                p = jnp.exp2(logit_ref[g, j] - m_new).astype(BF16)
                vt = jnp.concatenate([vt_ref[_head_cols(g), pl.ds(k0, tq)], ones_rows], axis=0)
                acc = acc + jnp.dot(vt, p, preferred_element_type=F32)
            acc_ref[g] = acc
            m_ref[g] = m_new

        scores = [head_scores(g) for g in range(n_heads)]
        maxima = [head_logits(g, scores[g]) for g in range(n_heads)]
        for g in range(n_heads):
            head_accumulate(g, maxima[g])

    n_far = jnp.maximum(qi - 1, 0)

    def far_group(i, carry):
        step([(FAR_BLOCKS * i + j, None) for j in range(FAR_BLOCKS)])
        return carry

    lax.fori_loop(0, n_far // FAR_BLOCKS, far_group, 0)
    first_left = n_far - n_far % FAR_BLOCKS
    for left in range(1, FAR_BLOCKS):
        @pl.when(n_far % FAR_BLOCKS == left)
        def _(left=left):
            step([(first_left + j, None) for j in range(left)])

    @pl.when(qi >= 1)
    def _():
        step([(qi - 1, 1), (qi, 0)])

    @pl.when(qi == 0)
    def _():
        step([(qi, 0)])
    for g in range(n_heads):
        acc = acc_ref[g]
        out_t = acc[:HEAD_DIM] / acc[HEAD_DIM:HEAD_DIM + 1]
        o_ref[:, _head_cols(g)] = out_t.T.astype(o_ref.dtype)


def _dsa_attn(z, zt, mask, bias_tiles, B, S, H, q_blk, k_blk, v_blk):
    M = z.shape[0]
    tq = bias_tiles.shape[-1]
    nq = S // tq
    G = min(H, DSA_HEAD_GROUP)
    gw = G * HEAD_DIM
    return pl.pallas_call(
        functools.partial(_dsa_attn_kernel, tq=tq, n_heads=G),
        grid=(B, H // G, nq),
        in_specs=_attn_group_specs(S, tq, nq, G, q_blk, k_blk, v_blk, kv_buffers=1) + [
            pl.BlockSpec((1, S, tq), lambda b, hg, i: (b * nq + i, 0, 0)),
            pl.BlockSpec((G, 2, tq, tq), lambda b, hg, i: (hg, 0, 0, 0),
                         pipeline_mode=pl.Buffered(1))],
        out_specs=pl.BlockSpec((tq, gw), lambda b, hg, i: (b * nq + i, hg)),
        out_shape=jax.ShapeDtypeStruct((M, H * HEAD_DIM), BF16),
        scratch_shapes=[pltpu.VMEM((G, 1, tq), F32),
                        pltpu.VMEM((G, HEAD_DIM + BF16_ROWS, tq), F32),
                        pltpu.VMEM((G, FAR_BLOCKS, tq, tq), F32)],
        compiler_params=_params("arbitrary", "arbitrary", "arbitrary"),
        name="dsa_attn",
    )(zt, z, zt, mask, bias_tiles)


def _mix_out_kernel(osb_ref, ods_ref, gsb_ref, gds_ref, x_ref, wsb_ref, wds_ref, wout_ref,
                    gt_ref, g_ref, sc_ref, sh_ref, x1_ref, h2_ref):
    t_sb = jnp.dot(osb_ref[...], wsb_ref[...], preferred_element_type=F32)
    t_ds = jnp.dot(ods_ref[...], wds_ref[...], preferred_element_type=F32)
    merged = (_sigmoid(gsb_ref[...].astype(F32)) * t_sb
              + _sigmoid(gds_ref[...].astype(F32)) * t_ds)
    y = jnp.dot(merged.astype(BF16), wout_ref[...], preferred_element_type=F32)
    x1 = x_ref[...] + gt_ref[0] * y
    x1_ref[...] = x1
    h2_ref[...] = _rms_mod(x1, g_ref[...], sc_ref[0], sh_ref[0]).astype(h2_ref.dtype)


def _mix_out(o_sb, o_ds, z, gate_blk, x2, w_sb, w_ds, w_out, gt, g, sc, sh, S):
    M, D = x2.shape
    B = gt.shape[0]
    W = o_sb.shape[1]
    tm = min(S, 512)
    per_b = S // tm
    row = lambda i: (i, 0)
    const = lambda i: (0, 0)
    per_batch = lambda i: (i // per_b, 0, 0)
    return pl.pallas_call(
        _mix_out_kernel,
        grid=(M // tm,),
        in_specs=[pl.BlockSpec((tm, W), row),
                  pl.BlockSpec((tm, W), row),
                  pl.BlockSpec((tm, D), lambda i: (i, gate_blk)),
                  pl.BlockSpec((tm, D), lambda i: (i, gate_blk + 1)),
                  pl.BlockSpec((tm, D), row),
                  pl.BlockSpec((W, D), const),
                  pl.BlockSpec((W, D), const),
                  pl.BlockSpec((D, D), const),
                  pl.BlockSpec((1, 1, D), per_batch),
                  pl.BlockSpec((1, D), const),
                  pl.BlockSpec((1, 1, D), per_batch),
                  pl.BlockSpec((1, 1, D), per_batch)],
        out_specs=[pl.BlockSpec((tm, D), row), pl.BlockSpec((tm, D), row)],
        out_shape=[jax.ShapeDtypeStruct((M, D), F32), jax.ShapeDtypeStruct((M, D), BF16)],
        compiler_params=_params("arbitrary"),
        name="mix_out",
    )(o_sb, o_ds, z, z, x2, w_sb, w_ds, w_out, gt.reshape(B, 1, D), g.reshape(1, D),
      sc.reshape(B, 1, D), sh.reshape(B, 1, D))


HALO = BF16_ROWS


def _ffn_act_kernel(h_ref, halo_ref, wg_ref, wu_ref, cw_ref, cb_ref, o_ref, wgb_ref, wub_ref, *,
                    tiles_per_seq):
    i = pl.program_id(1)

    @pl.when(i == 0)
    def _():
        wgb_ref[...] = wg_ref[...].astype(BF16)
        wub_ref[...] = wu_ref[...].astype(BF16)

    h = h_ref[...]
    wg = wgb_ref[...]
    g0 = jnp.dot(h, wg, preferred_element_type=F32)
    g_prev = jnp.dot(halo_ref[...], wg, preferred_element_type=F32)
    g_prev = jnp.where(i % tiles_per_seq == 0, 0.0, g_prev)
    ridx = lax.broadcasted_iota(jnp.int32, g0.shape, 0)
    g1 = jnp.where(ridx == 0, g_prev[HALO - 1:HALO, :], pltpu.roll(g0, 1, 0))
    g2 = jnp.where(ridx == 0, g_prev[HALO - 2:HALO - 1, :],
                   jnp.where(ridx == 1, g_prev[HALO - 1:HALO, :], pltpu.roll(g0, 2, 0)))
    cw = cw_ref[...]
    a = cb_ref[...] + g2 * cw[0:1, :] + g1 * cw[1:2, :] + g0 * cw[2:3, :]
    u = jnp.dot(h, wub_ref[...], preferred_element_type=F32)
    o_ref[...] = (a * _sigmoid(a) * u).astype(o_ref.dtype)


def _ffn_out_kernel(a_ref, wd_ref, x1_ref, gt_ref, gf_ref, o_ref):
    k = pl.program_id(1)

    @pl.when(k == 0)
    def _():
        o_ref[...] = jnp.zeros_like(o_ref)

    o_ref[...] += jnp.dot(a_ref[...], wd_ref[...], preferred_element_type=F32)

    @pl.when(k == pl.num_programs(1) - 1)
    def _():
        x = x1_ref[...] + gt_ref[0] * o_ref[...]
        ms = jnp.mean(x * x, axis=-1, keepdims=True)
        o_ref[...] = x * lax.rsqrt(ms + EPS) * gf_ref[...]


def _conv_ffn(h2, x1, w_gate, w_up, layer, w_down, conv_w, conv_b, gt, g_final, S):
    M, D = x1.shape
    B = gt.shape[0]
    F = w_gate.shape[2]
    tm = min(S, 1024)
    tf = min(F, 512)
    per_b = S // tm
    halo_per_tile = tm // HALO
    act = pl.pallas_call(
        functools.partial(_ffn_act_kernel, tiles_per_seq=per_b),
        grid=(F // tf, M // tm),
        in_specs=[pl.BlockSpec((tm, D), lambda f, i: (i, 0)),
                  pl.BlockSpec((HALO, D), lambda f, i: (jnp.maximum(i * halo_per_tile - 1, 0), 0)),
                  pl.BlockSpec((None, D, tf), lambda f, i: (layer, 0, f)),
                  pl.BlockSpec((None, D, tf), lambda f, i: (layer, 0, f)),
                  pl.BlockSpec((CONV_WIDTH, tf), lambda f, i: (0, f)),
                  pl.BlockSpec((1, tf), lambda f, i: (0, f))],
        out_specs=pl.BlockSpec((tm, tf), lambda f, i: (i, f)),
        out_shape=jax.ShapeDtypeStruct((M, F), BF16),
        scratch_shapes=[pltpu.VMEM((D, tf), BF16), pltpu.VMEM((D, tf), BF16)],
        compiler_params=_params("arbitrary", "arbitrary"),
        name="ffn_act",
    )(h2, h2, w_gate, w_up, conv_w, conv_b.reshape(1, F))
    return pl.pallas_call(
        _ffn_out_kernel,
        grid=(M // tm, F // tf),
        in_specs=[pl.BlockSpec((tm, tf), lambda i, k: (i, k)),
                  pl.BlockSpec((tf, D), lambda i, k: (k, 0)),
                  pl.BlockSpec((tm, D), lambda i, k: (i, 0)),
                  pl.BlockSpec((1, 1, D), lambda i, k: (i // per_b, 0, 0)),
                  pl.BlockSpec((1, D), lambda i, k: (0, 0))],
        out_specs=pl.BlockSpec((tm, D), lambda i, k: (i, 0)),
        out_shape=jax.ShapeDtypeStruct((M, D), F32),
        compiler_params=_params("arbitrary", "arbitrary"),
        name="ffn_out",
    )(act, w_down, x1, gt.reshape(B, 1, D), g_final.reshape(1, D))


def kernel(x, c, w_ada, b_ada, g_mix, w_in, w_o_sb, w_o_dsa, w_out, rel_bias, g_ffn, w_gate,
           w_up, conv_w, conv_b, w_down, g_final):
    B, S, D = x.shape
    depth = w_ada.shape[0]
    W = w_o_sb.shape[1]
    H = W // HEAD_DIM
    qw = N_IDX_HEADS * IDX_DIM
    assert w_o_dsa.shape[1] == W and D % W == 0 and S % ATTN_BLOCK == 0 and W % qw == 0
    n_sel = min(TOPK_MAX, S // 4)
    scale = HEAD_DIM ** -0.5
    x2 = x.reshape(B * S, D)

    for l in range(depth):
        mod = _adaln(c, w_ada[l], b_ada[l])
        sh1, sc1, gt1, sh2, sc2, gt2 = jnp.split(mod, 6, axis=-1)

        wt = w_in[l].T
        o_kix = 6 * W + qw
        o_wix = o_kix + IDX_DIM
        o_gate = o_wix + N_IDX_HEADS
        k_rows = wt[o_kix:o_wix]
        wt_tail = jnp.concatenate(
            [k_rows, k_rows, wt[o_wix:o_gate],
             jnp.zeros((2 * IDX_DIM - N_IDX_HEADS, D), wt.dtype)], axis=0)

        h1 = _norm_mod(x2, g_mix[l], sc1, sh1, S)
        z = _in_proj_wt(h1, wt, (1, 4), W, (), "in_proj")
        zt = _in_proj_wt(h1, wt, (0, 2, 3, 5, 6), W, ((0, scale), (2, scale * LOG2_E)),
                         "in_proj_t", transposed=True)
        zg = _in_proj_wt(h1, wt[o_gate:], tuple(range(2 * D // W)), W, (), "in_proj_gates")
        zs = _in_proj_wt(h1, wt_tail, (0,), 4 * IDX_DIM, (), "in_proj_idx")

        o_sb = _sb_attn(z, zt, B, S, H, 0, 0, H)
        mask = _dsa_select(zt, zs, B, S, 4 * W // qw, n_sel)
        o_ds = _dsa_attn(z, zt, mask, _bias_tiles(rel_bias, min(S, ATTN_BLOCK)),
                         B, S, H, 2 * H, H, 3 * H)

        x2, h2 = _mix_out(o_sb, o_ds, zg, 0, x2, w_o_sb[l].astype(BF16),
                          w_o_dsa[l].astype(BF16), w_out[l].astype(BF16), gt1, g_ffn[l], sc2, sh2, S)
        last = l == depth - 1
        assert last, "the final rms_norm is fused into the last layer's FFN"
        x2 = _conv_ffn(h2, x2, w_gate, w_up, l, w_down[l].astype(BF16),
                       conv_w[l], conv_b[l], gt2, g_final, S)
    return x2.reshape(B, S, D)
```

```python
import functools
import math

import jax
import jax.numpy as jnp
from jax import lax
from jax.experimental import pallas as pl
from jax.experimental.pallas import tpu as pltpu

HEAD_DIM = 128
N_IDX_HEADS = 16
IDX_DIM = 64
TOPK_MAX = 256
N_BUCKETS = 32
MAX_DISTANCE = 128
CONV_WIDTH = 3
EPS = 1e-6

F32 = jnp.float32
BF16 = jnp.bfloat16
NEG = -1e30
I16 = jnp.int16
INT_MIN = -2 ** 31
I16_MIN = -2 ** 15
EXP_ZERO_BELOW = 104.0
LOG2_E = math.log2(math.e)
V7X_VMEM_LIMIT = 58 * 1024 * 1024
BF16_ROWS = 16
ATTN_BLOCK = 256
SB_HEAD_GROUP = 4
DSA_HEAD_GROUP = 8
FAR_BLOCKS = 4
SCORE_BLOCKS = 4
COUNT_BLOCKS = 4
COUNT_CHUNK = 64
NT_DIMS = (((1,), (1,)), ((), ()))


def _params(*sem):
    return pltpu.CompilerParams(dimension_semantics=sem, vmem_limit_bytes=V7X_VMEM_LIMIT)


def _sigmoid(x):
    return 1.0 / (1.0 + jnp.exp(-x))


def _adaln_kernel(ct_ref, w_ref, b_ref, o_ref):
    ct = ct_ref[...]
    act = ct * _sigmoid(ct)
    w = w_ref[...]
    for b in range(ct.shape[1]):
        o_ref[b:b + 1, :] = jnp.sum(act[:, b:b + 1] * w, axis=0, keepdims=True) + b_ref[...]


def _adaln(c, w, bias):
    B, D = c.shape
    N = w.shape[1]
    tn = min(N, 1024)
    return pl.pallas_call(
        _adaln_kernel,
        grid=(N // tn,),
        in_specs=[pl.BlockSpec((D, B), lambda j: (0, 0)),
                  pl.BlockSpec((D, tn), lambda j: (0, j)),
                  pl.BlockSpec((1, tn), lambda j: (0, j))],
        out_specs=pl.BlockSpec((B, tn), lambda j: (0, j)),
        out_shape=jax.ShapeDtypeStruct((B, N), F32),
        compiler_params=_params("arbitrary"),
        name="adaln",
    )(c.T, w, bias.reshape(1, N))


def _rms_mod(x, g, sc, sh):
    ms = jnp.mean(x * x, axis=-1, keepdims=True)
    y = x * lax.rsqrt(ms + EPS) * g
    return y * (1.0 + sc) + sh


def _norm_mod_kernel(x_ref, g_ref, sc_ref, sh_ref, o_ref):
    o_ref[...] = _rms_mod(x_ref[...], g_ref[...], sc_ref[0], sh_ref[0]).astype(o_ref.dtype)


def _norm_mod(x2, g, sc, sh, S):
    M, D = x2.shape
    B = sc.shape[0]
    tm = min(S, 1024)
    per_b = S // tm
    return pl.pallas_call(
        _norm_mod_kernel,
        grid=(M // tm,),
        in_specs=[pl.BlockSpec((tm, D), lambda i: (i, 0)),
                  pl.BlockSpec((1, D), lambda i: (0, 0)),
                  pl.BlockSpec((1, 1, D), lambda i: (i // per_b, 0, 0)),
                  pl.BlockSpec((1, 1, D), lambda i: (i // per_b, 0, 0))],
        out_specs=pl.BlockSpec((tm, D), lambda i: (i, 0)),
        out_shape=jax.ShapeDtypeStruct((M, D), BF16),
        compiler_params=_params("arbitrary"),
        name="norm_mod",
    )(x2, g.reshape(1, D), sc.reshape(B, 1, D), sh.reshape(B, 1, D))


def _in_proj_kernel(a_ref, b_ref, o_ref, *, tile_scales):
    acc = jnp.dot(a_ref[...], b_ref[...], preferred_element_type=F32)
    o_ref[...] = (acc * _tile_scale(tile_scales)).astype(o_ref.dtype)


def _in_proj_wt_kernel(a_ref, wt_ref, o_ref, wb_ref, *, tile_scales):
    @pl.when(pl.program_id(1) == 0)
    def _():
        wb_ref[...] = wt_ref[...].T.astype(BF16)

    _in_proj_kernel(a_ref, wb_ref, o_ref, tile_scales=tile_scales)


def _tile_scale(tile_scales):
    j = pl.program_id(0)
    scale = jnp.float32(1.0)
    for tile, tile_scale in tile_scales:
        scale = jnp.where(j == tile, tile_scale, scale)
    return scale


def _in_proj_t_kernel(a_ref, wt_ref, o_ref, wb_ref, *, tile_scales):
    @pl.when(pl.program_id(1) == 0)
    def _():
        wb_ref[...] = wt_ref[...].astype(BF16)

    acc = lax.dot_general(wb_ref[...], a_ref[...], NT_DIMS, preferred_element_type=F32)
    o_ref[...] = (acc * _tile_scale(tile_scales)).astype(o_ref.dtype)


def _tile_index_fn(tiles):
    steps = []
    for j, t in enumerate(tiles):
        while len(steps) < t - j:
            steps.append(j)
    return lambda j: j + sum(jnp.where(j >= first, 1, 0) for first in steps)


def _in_proj_wt(h, wt, tiles, tn, tile_scales, name, transposed=False):
    M, K = h.shape
    tm = min(M, 1024)
    n = len(tiles)
    tile_of = _tile_index_fn(tiles)
    in_specs = [pl.BlockSpec((tm, K), lambda j, i: (i, 0)),
                pl.BlockSpec((tn, K), lambda j, i: (tile_of(j), 0))]
    if transposed:
        body, wb_shape = functools.partial(_in_proj_t_kernel, tile_scales=tile_scales), (tn, K)
        out_spec = pl.BlockSpec((tn, tm), lambda j, i: (j, i))
        out_shape = jax.ShapeDtypeStruct((n * tn, M), BF16)
    else:
        body, wb_shape = functools.partial(_in_proj_wt_kernel, tile_scales=tile_scales), (K, tn)
        out_spec = pl.BlockSpec((tm, tn), lambda j, i: (i, j))
        out_shape = jax.ShapeDtypeStruct((M, n * tn), BF16)
    return pl.pallas_call(
        body,
        grid=(n, M // tm),
        in_specs=in_specs,
        out_specs=out_spec,
        out_shape=out_shape,
        scratch_shapes=[pltpu.VMEM(wb_shape, BF16)],
        compiler_params=_params("arbitrary", "arbitrary"),
        name=name,
    )(h, wt)


def _key_query_iotas(k0, q0, tk, tq):
    key_pos = k0 + lax.broadcasted_iota(jnp.int32, (tk, tq), 0)
    query_pos = q0 + lax.broadcasted_iota(jnp.int32, (tk, tq), 1)
    return key_pos, query_pos


def _head_cols(g):
    return slice(g * HEAD_DIM, (g + 1) * HEAD_DIM)


def _sb_kernel(q_ref, k_ref, vt_ref, o_ref, run_ref, acc_ref, *, tq, tk, n_heads):
    qi = pl.program_id(2)
    q0 = qi * tq
    run_ref[...] = jnp.zeros_like(run_ref)
    acc_ref[...] = jnp.zeros_like(acc_ref)
    later = (lax.broadcasted_iota(jnp.int32, (tk, tk), 1)
             > lax.broadcasted_iota(jnp.int32, (tk, tk), 0)).astype(BF16)

    def step(kbs, first_is_diagonal):
        heads = range(n_heads)
        k0s = [pl.multiple_of(kb * tk, tk) for kb in kbs]
        causal = None
        if first_is_diagonal:
            key_pos, query_pos = _key_query_iotas(k0s[0], q0, tk, tq)
            causal = key_pos < query_pos
        masked = [first_is_diagonal and j == 0 for j in range(len(kbs))]
        spans = [jnp.dot(k_ref[pl.ds(k0s[-1], len(kbs) * tk), _head_cols(g)], q_ref[_head_cols(g), :],
                         preferred_element_type=F32) for g in heads]
        zs = [[span[(len(kbs) - 1 - j) * tk:(len(kbs) - j) * tk] for span in spans]
              for j in range(len(kbs))]
        sps = [[jnp.maximum(z, 0.0) + jnp.log(1.0 + jnp.exp(-jnp.abs(z))) for z in zj] for zj in zs]
        spms = [[jnp.where(causal, sp, 0.0) if masked[j] else sp for sp in spj]
                for j, spj in enumerate(sps)]
        his = [[spm.astype(BF16) for spm in spj] for spj in spms]
        los = [[(spm - hi.astype(F32)).astype(BF16) for spm, hi in zip(spj, hij)]
               for spj, hij in zip(spms, his)]
        suffixes = [[jnp.dot(later, hi, preferred_element_type=F32)
                     + jnp.dot(later, lo, preferred_element_type=F32) for hi, lo in zip(hij, loj)]
                    for hij, loj in zip(his, los)]
        weights, new_runs = [], []
        for g in heads:
            run = run_ref[g]
            head_weights = []
            for j in range(len(kbs)):
                a = jnp.exp(zs[j][g] - sps[j][g] - suffixes[j][g] - run)
                if masked[j]:
                    a = jnp.where(causal, a, 0.0)
                head_weights.append(a.astype(BF16))
                run = run + jnp.sum(spms[j][g], axis=0, keepdims=True)
            weights.append(head_weights)
            new_runs.append(run)
        min_run = None
        for g in heads:
            acc = acc_ref[g]
            for j, k0 in enumerate(k0s):
                acc = acc + jnp.dot(vt_ref[_head_cols(g), pl.ds(k0, tk)], weights[g][j],
                                    preferred_element_type=F32)
            acc_ref[g] = acc
            run_ref[g] = new_runs[g]
            head_min = jnp.min(new_runs[g])
            min_run = head_min if min_run is None else jnp.minimum(min_run, head_min)
        return min_run > EXP_ZERO_BELOW

    def body(carry):
        kb, _ = carry
        return kb - 1, step([kb], False)

    def cond(carry):
        kb, dead = carry
        return jnp.logical_and(kb >= 0, jnp.logical_not(dead))

    @pl.when(qi == 0)
    def _():
        step([qi], True)

    @pl.when(qi >= 1)
    def _():
        dead = step([qi, qi - 1], True)
        lax.while_loop(cond, body, (qi - 2, dead))

    for g in range(n_heads):
        o_ref[:, _head_cols(g)] = acc_ref[g].T.astype(o_ref.dtype)


def _attn_group_specs(S, tq, nq, G, q_blk, k_blk, v_blk, kv_buffers):
    assert q_blk % G == 0 and k_blk % G == 0 and v_blk % G == 0
    gw = G * HEAD_DIM
    mode = pl.Buffered(kv_buffers)
    return [pl.BlockSpec((gw, tq), lambda b, hg, i: (q_blk // G + hg, b * nq + i)),
            pl.BlockSpec((S, gw), lambda b, hg, i: (b, k_blk // G + hg), pipeline_mode=mode),
            pl.BlockSpec((gw, S), lambda b, hg, i: (v_blk // G + hg, b), pipeline_mode=mode)]


def _sb_attn(z, zt, B, S, H, q_blk, k_blk, v_blk):
    M = z.shape[0]
    tq = tk = min(S, ATTN_BLOCK)
    nq = S // tq
    G = min(H, SB_HEAD_GROUP)
    gw = G * HEAD_DIM
    return pl.pallas_call(
        functools.partial(_sb_kernel, tq=tq, tk=tk, n_heads=G),
        grid=(B, H // G, nq),
        in_specs=_attn_group_specs(S, tq, nq, G, q_blk, k_blk, v_blk, kv_buffers=2),
        out_specs=pl.BlockSpec((tq, gw), lambda b, hg, i: (b * nq + i, hg)),
        out_shape=jax.ShapeDtypeStruct((M, H * HEAD_DIM), BF16),
        scratch_shapes=[pltpu.VMEM((G, 1, tq), F32), pltpu.VMEM((G, HEAD_DIM, tq), F32)],
        compiler_params=_params("arbitrary", "arbitrary", "arbitrary"),
        name="sb_attn",
    )(zt, z, zt)


def _dsa_select_kernel(q_ref, k_ref, w_ref, mask_ref, hi_ref, lo_ref, *, tq, tk, n_sel, n_kblocks):
    qi = pl.program_id(1)
    q0 = qi * tq
    nkb = (q0 + tq + tk - 1) // tk
    lane = lax.broadcasted_iota(jnp.int32, (tk, 2 * IDX_DIM), 1)
    w_scale = (N_IDX_HEADS ** -0.5) * (IDX_DIM ** -0.5)
    wt = (w_ref[...].astype(F32) * w_scale).T

    def score_block(kb, diagonal):
        k0 = pl.multiple_of(kb * tk, tk)
        kk = k_ref[pl.ds(k0, tk), :]
        k_halves = (jnp.where(lane < IDX_DIM, kk, jnp.zeros_like(kk)),
                    jnp.where(lane >= IDX_DIM, kk, jnp.zeros_like(kk)))
        acc = jnp.zeros((tk, tq), F32)
        for p in range(N_IDX_HEADS // 2):
            q2t = q_ref[p * 2 * IDX_DIM:(p + 1) * 2 * IDX_DIM, :]
            for half in range(2):
                h = 2 * p + half
                ph = jnp.dot(k_halves[half], q2t, preferred_element_type=F32)
                acc = acc + jnp.maximum(ph, 0.0) * wt[h:h + 1, :]
        bits = lax.bitcast_convert_type(acc, jnp.int32)
        key = bits ^ ((bits >> 31) & 0x7FFFFFFF)
        if diagonal:
            key_pos, query_pos = _key_query_iotas(k0, q0, tk, tq)
            key = jnp.where(key_pos <= query_pos, key, INT_MIN)
        hi_ref[pl.ds(k0, tk), :] = (key >> 16).astype(I16)
        lo_ref[pl.ds(k0, tk), :] = (key ^ 0x8000).astype(I16)

    def score_group(i, carry):
        for j in range(SCORE_BLOCKS):
            score_block(SCORE_BLOCKS * i + j, False)
        return carry

    n_before = nkb - 1
    lax.fori_loop(0, n_before // SCORE_BLOCKS, score_group, 0)
    first_left = n_before - n_before % SCORE_BLOCKS
    for left in range(1, SCORE_BLOCKS):
        @pl.when(n_before % SCORE_BLOCKS == left)
        def _(left=left):
            for j in range(left):
                score_block(first_left + j, False)

    score_block(nkb - 1, True)

    n_count = (nkb + COUNT_BLOCKS - 1) // COUNT_BLOCKS
    rows = COUNT_BLOCKS * tk

    def pad_block(kb, carry):
        k0 = pl.multiple_of(kb * tk, tk)
        hi_ref[pl.ds(k0, tk), :] = jnp.full((tk, tq), I16_MIN, I16)
        lo_ref[pl.ds(k0, tk), :] = jnp.full((tk, tq), I16_MIN, I16)
        return carry

    lax.fori_loop(nkb, n_count * COUNT_BLOCKS, pad_block, 0)

    def count(flags_fn):
        def blk(i, cnt):
            r0 = pl.multiple_of(i * rows, rows)
            groups = []
            for c in range(rows // COUNT_CHUNK):
                flags = flags_fn(r0 + c * COUNT_CHUNK, COUNT_CHUNK)
                parts = [flags[r * BF16_ROWS:(r + 1) * BF16_ROWS, :]
                         for r in range(COUNT_CHUNK // BF16_ROWS)]
                while len(parts) > 1:
                    parts = [a + b for a, b in zip(parts[0::2], parts[1::2])]
                groups.append(parts[0])
            while len(groups) > 1:
                groups = [a + b for a, b in zip(groups[0::2], groups[1::2])]
            return cnt + groups[0]
        cnt = lax.fori_loop(0, n_count, blk, jnp.zeros((BF16_ROWS, tq), I16))
        return jnp.sum(cnt.astype(jnp.int32), axis=0, keepdims=True)

    flag, no_flag = I16(1), I16(0)

    def count_ge(ref, thr16):
        return count(lambda r, n: jnp.where(ref[pl.ds(r, n), :] >= thr16, flag, no_flag))

    def kth_largest(ref, k):
        def bisect(it, carry):
            thr, n_ge, n_gt = carry
            cand = thr + jnp.left_shift(jnp.int32(1), 15 - it)
            cnt = count_ge(ref, cand.astype(I16))
            ok = cnt >= k
            return jnp.where(ok, cand, thr), jnp.where(ok, cnt, n_ge), jnp.where(ok, n_gt, cnt)
        zeros = jnp.zeros((1, tq), jnp.int32)
        thr, n_ge, n_gt = lax.fori_loop(0, 16, bisect, (zeros + I16_MIN, zeros, zeros))
        return thr.astype(I16), n_ge, n_gt

    thr_hi, _, above_hi = kth_largest(hi_ref, n_sel)
    need_lo = n_sel - above_hi

    def park(kb, carry):
        k0 = pl.multiple_of(kb * tk, tk)
        lo_ref[pl.ds(k0, tk), :] = jnp.where(hi_ref[pl.ds(k0, tk), :] == thr_hi,
                                             lo_ref[pl.ds(k0, tk), :], I16(I16_MIN))
        return carry

    lax.fori_loop(0, nkb, park, 0)
    thr_lo, at_least_lo, above_lo = kth_largest(lo_ref, need_lo)
    need_ties = need_lo - above_lo
    n_ties = jnp.where(thr_lo == I16(I16_MIN), tk * n_kblocks, at_least_lo - above_lo)
    all_ties_fit = jnp.max(n_ties - need_ties) <= 0

    one, zero, neg = BF16(1.0), BF16(0.0), BF16(NEG)

    def store_mask(kb, sel, diagonal):
        k0 = pl.multiple_of(kb * tk, tk)
        if diagonal:
            key_pos, query_pos = _key_query_iotas(k0, q0, tk, tq)
            sel = sel * (key_pos <= query_pos).astype(BF16)
        mask_ref[0, pl.ds(k0, tk), :] = jnp.where(sel > zero, zero, neg)

    def emit_all_ties(kb, diagonal):
        k0 = pl.multiple_of(kb * tk, tk)
        hi = hi_ref[pl.ds(k0, tk), :]
        lo = lo_ref[pl.ds(k0, tk), :]
        store_mask(kb, jnp.where(hi > thr_hi, one,
                                 jnp.where(hi == thr_hi, jnp.where(lo >= thr_lo, one, zero), zero)),
                   diagonal)

    earlier = (lax.broadcasted_iota(jnp.int32, (tk, tk), 1)
               < lax.broadcasted_iota(jnp.int32, (tk, tk), 0)).astype(BF16)
    ones_rows = jnp.ones((BF16_ROWS, tk), BF16)
    need_ties_f = need_ties.astype(F32)

    def emit_ranked_ties(kb, ties_seen, diagonal):
        k0 = pl.multiple_of(kb * tk, tk)
        hi = hi_ref[pl.ds(k0, tk), :]
        lo = lo_ref[pl.ds(k0, tk), :]
        above = jnp.where(hi > thr_hi, one, jnp.where(lo > thr_lo, one, zero))
        eq = jnp.where(hi == thr_hi, jnp.where(lo == thr_lo, one, zero), zero)
        rank = jnp.dot(earlier, eq, preferred_element_type=F32).astype(BF16)
        room = jnp.clip(need_ties_f - ties_seen, -1.0, float(tk)).astype(BF16)
        store_mask(kb, jnp.where(rank < room, jnp.maximum(above, eq), above), diagonal)
        return ties_seen + jnp.dot(ones_rows, eq, preferred_element_type=F32)[0:1]

    def emit_fast():
        def body(kb, carry):
            emit_all_ties(kb, False)
            return carry
        lax.fori_loop(0, nkb - 1, body, 0)
        emit_all_ties(nkb - 1, True)

    def emit_slow():
        ties_seen = lax.fori_loop(0, nkb - 1, lambda kb, seen: emit_ranked_ties(kb, seen, False),
                                  jnp.zeros((1, tq), F32))
        emit_ranked_ties(nkb - 1, ties_seen, True)

    lax.cond(all_ties_fit, emit_fast, emit_slow)

    def fill(kb, carry):
        k0 = pl.multiple_of(kb * tk, tk)
        mask_ref[0, pl.ds(k0, tk), :] = jnp.full((tk, tq), NEG, mask_ref.dtype)
        return carry

    lax.fori_loop(nkb, n_kblocks, fill, 0)


def _dsa_select(zt, zs, B, S, qix_blk, n_sel):
    tq = tk = min(S, ATTN_BLOCK)
    nq = S // tq
    assert nq % COUNT_BLOCKS == 0
    qw = N_IDX_HEADS * IDX_DIM
    return pl.pallas_call(
        functools.partial(_dsa_select_kernel, tq=tq, tk=tk, n_sel=n_sel, n_kblocks=S // tk),
        grid=(B, nq),
        in_specs=[pl.BlockSpec((qw, tq), lambda b, i: (qix_blk, b * nq + i)),
                  pl.BlockSpec((S, 2 * IDX_DIM), lambda b, i: (b, 0)),
                  pl.BlockSpec((tq, 2 * IDX_DIM), lambda b, i: (b * nq + i, 1))],
        out_specs=pl.BlockSpec((1, S, tq), lambda b, i: (b * nq + i, 0, 0)),
        out_shape=jax.ShapeDtypeStruct((B * nq, S, tq), BF16),
        scratch_shapes=[pltpu.VMEM((S, tq), I16), pltpu.VMEM((S, tq), I16)],
        compiler_params=_params("arbitrary", "arbitrary"),
        name="dsa_select",
    )(zt, zs, zs)


def _rel_bucket(dist):
    n = jnp.maximum(dist, 0)
    max_exact = N_BUCKETS // 2
    nf = jnp.maximum(n, 1).astype(F32)
    large = max_exact + (jnp.log(nf / max_exact) / math.log(MAX_DISTANCE / max_exact)
                         * (N_BUCKETS - max_exact)).astype(jnp.int32)
    large = jnp.minimum(large, N_BUCKETS - 1)
    return jnp.where(n < max_exact, n, large)


def _bias_tiles(rel_bias, tq):
    assert tq >= MAX_DISTANCE
    rb = rel_bias.astype(F32)
    key = jnp.arange(tq)[:, None]
    query = jnp.arange(tq)[None, :]
    bucket = _rel_bucket(jnp.stack([query - key, tq + query - key]))
    onehot = (bucket[None] == jnp.arange(N_BUCKETS)[:, None, None, None]).astype(F32)
    return jnp.einsum("nh,nikq->hikq", (rb - rb[N_BUCKETS - 1]) * LOG2_E, onehot,
                      precision=lax.Precision.HIGHEST)


def _dsa_attn_kernel(q_ref, k_ref, vt_ref, mask_ref, bias_ref, o_ref, m_ref, acc_ref, logit_ref, *,
                     tq, n_heads):
    qi = pl.program_id(2)
    m_ref[...] = jnp.full_like(m_ref, NEG)
    acc_ref[...] = jnp.zeros_like(acc_ref)
    ones_rows = jnp.ones((BF16_ROWS, tq), BF16)

    def step(blocks):
        k0s = [pl.multiple_of(kb * tq, tq) for kb, _ in blocks]
        masks = [mask_ref[0, pl.ds(k0, tq), :].astype(F32) for k0 in k0s]

        def head_scores(g):
            s = jnp.dot(k_ref[pl.ds(k0s[0], len(blocks) * tq), _head_cols(g)], q_ref[_head_cols(g), :],
                        preferred_element_type=F32)
            return [s[j * tq:(j + 1) * tq] for j in range(len(blocks))]

        def head_logits(g, scores):
            m_new = m_ref[g]
            for j, (_, bias_idx) in enumerate(blocks):
                s = scores[j] + masks[j]
                if bias_idx is not None:
                    s = s + bias_ref[g, bias_idx]
                logit_ref[g, j] = s
                m_new = jnp.maximum(m_new, jnp.max(s, axis=0, keepdims=True))
            return m_new

        def head_accumulate(g, m_new):
            acc = jnp.exp2(m_ref[g] - m_new) * acc_ref[g]
            for j, k0 in enumerate(k0s):
                p = jnp.exp2(logit_ref[g, j] - m_new).astype(BF16)
                vt = jnp.concatenate([vt_ref[_head_cols(g), pl.ds(k0, tq)], ones_rows], axis=0)
                acc = acc + jnp.dot(vt, p, preferred_element_type=F32)
            acc_ref[g] = acc
            m_ref[g] = m_new

        scores = [head_scores(g) for g in range(n_heads)]
        maxima = [head_logits(g, scores[g]) for g in range(n_heads)]
        for g in range(n_heads):
            head_accumulate(g, maxima[g])

    n_far = jnp.maximum(qi - 1, 0)

    def far_group(i, carry):
        step([(FAR_BLOCKS * i + j, None) for j in range(FAR_BLOCKS)])
        return carry

    lax.fori_loop(0, n_far // FAR_BLOCKS, far_group, 0)
    first_left = n_far - n_far % FAR_BLOCKS
    for left in range(1, FAR_BLOCKS):
        @pl.when(n_far % FAR_BLOCKS == left)
        def _(left=left):
            step([(first_left + j, None) for j in range(left)])

    @pl.when(qi >= 1)
    def _():
        step([(qi - 1, 1), (qi, 0)])

    @pl.when(qi == 0)
    def _():
        step([(qi, 0)])
    for g in range(n_heads):
        acc = acc_ref[g]
        out_t = acc[:HEAD_DIM] / acc[HEAD_DIM:HEAD_DIM + 1]
        o_ref[:, _head_cols(g)] = out_t.T.astype(o_ref.dtype)


def _dsa_attn(z, zt, mask, bias_tiles, B, S, H, q_blk, k_blk, v_blk):
    M = z.shape[0]
    tq = bias_tiles.shape[-1]
    nq = S // tq
    G = min(H, DSA_HEAD_GROUP)
    gw = G * HEAD_DIM
    return pl.pallas_call(
        functools.partial(_dsa_attn_kernel, tq=tq, n_heads=G),
        grid=(B, H // G, nq),
        in_specs=_attn_group_specs(S, tq, nq, G, q_blk, k_blk, v_blk, kv_buffers=1) + [
            pl.BlockSpec((1, S, tq), lambda b, hg, i: (b * nq + i, 0, 0)),
            pl.BlockSpec((G, 2, tq, tq), lambda b, hg, i: (hg, 0, 0, 0),
                         pipeline_mode=pl.Buffered(1))],
        out_specs=pl.BlockSpec((tq, gw), lambda b, hg, i: (b * nq + i, hg)),
        out_shape=jax.ShapeDtypeStruct((M, H * HEAD_DIM), BF16),
        scratch_shapes=[pltpu.VMEM((G, 1, tq), F32),
                        pltpu.VMEM((G, HEAD_DIM + BF16_ROWS, tq), F32),
                        pltpu.VMEM((G, FAR_BLOCKS, tq, tq), F32)],
        compiler_params=_params("arbitrary", "arbitrary", "arbitrary"),
        name="dsa_attn",
    )(zt, z, zt, mask, bias_tiles)


def _mix_out_kernel(osb_ref, ods_ref, gsb_ref, gds_ref, x_ref, wsb_ref, wds_ref, wout_ref,
                    gt_ref, g_ref, sc_ref, sh_ref, x1_ref, h2_ref):
    t_sb = jnp.dot(osb_ref[...], wsb_ref[...], preferred_element_type=F32)
    t_ds = jnp.dot(ods_ref[...], wds_ref[...], preferred_element_type=F32)
    merged = (_sigmoid(gsb_ref[...].astype(F32)) * t_sb
              + _sigmoid(gds_ref[...].astype(F32)) * t_ds)
    y = jnp.dot(merged.astype(BF16), wout_ref[...], preferred_element_type=F32)
    x1 = x_ref[...] + gt_ref[0] * y
    x1_ref[...] = x1
    h2_ref[...] = _rms_mod(x1, g_ref[...], sc_ref[0], sh_ref[0]).astype(h2_ref.dtype)


def _mix_out(o_sb, o_ds, z, gate_blk, x2, w_sb, w_ds, w_out, gt, g, sc, sh, S):
    M, D = x2.shape
    B = gt.shape[0]
    W = o_sb.shape[1]
    tm = min(S, 512)
    per_b = S // tm
    row = lambda i: (i, 0)
    const = lambda i: (0, 0)
    per_batch = lambda i: (i // per_b, 0, 0)
    return pl.pallas_call(
        _mix_out_kernel,
        grid=(M // tm,),
        in_specs=[pl.BlockSpec((tm, W), row),
                  pl.BlockSpec((tm, W), row),
                  pl.BlockSpec((tm, D), lambda i: (i, gate_blk)),
                  pl.BlockSpec((tm, D), lambda i: (i, gate_blk + 1)),
                  pl.BlockSpec((tm, D), row),
                  pl.BlockSpec((W, D), const),
                  pl.BlockSpec((W, D), const),
                  pl.BlockSpec((D, D), const),
                  pl.BlockSpec((1, 1, D), per_batch),
                  pl.BlockSpec((1, D), const),
                  pl.BlockSpec((1, 1, D), per_batch),
                  pl.BlockSpec((1, 1, D), per_batch)],
        out_specs=[pl.BlockSpec((tm, D), row), pl.BlockSpec((tm, D), row)],
        out_shape=[jax.ShapeDtypeStruct((M, D), F32), jax.ShapeDtypeStruct((M, D), BF16)],
        compiler_params=_params("arbitrary"),
        name="mix_out",
    )(o_sb, o_ds, z, z, x2, w_sb, w_ds, w_out, gt.reshape(B, 1, D), g.reshape(1, D),
      sc.reshape(B, 1, D), sh.reshape(B, 1, D))


HALO = BF16_ROWS


def _ffn_act_kernel(h_ref, halo_ref, wg_ref, wu_ref, cw_ref, cb_ref, o_ref, wgb_ref, wub_ref, *,
                    tiles_per_seq):
    i = pl.program_id(1)

    @pl.when(i == 0)
    def _():
        wgb_ref[...] = wg_ref[...].astype(BF16)
        wub_ref[...] = wu_ref[...].astype(BF16)

    h = h_ref[...]
    wg = wgb_ref[...]
    g0 = jnp.dot(h, wg, preferred_element_type=F32)
    g_prev = jnp.dot(halo_ref[...], wg, preferred_element_type=F32)
    g_prev = jnp.where(i % tiles_per_seq == 0, 0.0, g_prev)
    ridx = lax.broadcasted_iota(jnp.int32, g0.shape, 0)
    g1 = jnp.where(ridx == 0, g_prev[HALO - 1:HALO, :], pltpu.roll(g0, 1, 0))
    g2 = jnp.where(ridx == 0, g_prev[HALO - 2:HALO - 1, :],
                   jnp.where(ridx == 1, g_prev[HALO - 1:HALO, :], pltpu.roll(g0, 2, 0)))
    cw = cw_ref[...]
    a = cb_ref[...] + g2 * cw[0:1, :] + g1 * cw[1:2, :] + g0 * cw[2:3, :]
    u = jnp.dot(h, wub_ref[...], preferred_element_type=F32)
    o_ref[...] = (a * _sigmoid(a) * u).astype(o_ref.dtype)


def _ffn_out_kernel(a_ref, wd_ref, x1_ref, gt_ref, gf_ref, o_ref):
    k = pl.program_id(1)

    @pl.when(k == 0)
    def _():
        o_ref[...] = jnp.zeros_like(o_ref)

    o_ref[...] += jnp.dot(a_ref[...], wd_ref[...].astype(BF16), preferred_element_type=F32)

    @pl.when(k == pl.num_programs(1) - 1)
    def _():
        x = x1_ref[...] + gt_ref[0] * o_ref[...]
        ms = jnp.mean(x * x, axis=-1, keepdims=True)
        o_ref[...] = x * lax.rsqrt(ms + EPS) * gf_ref[...]


def _conv_ffn(h2, x1, w_gate, w_up, layer, w_down, conv_w, conv_b, gt, g_final, S):
    M, D = x1.shape
    B = gt.shape[0]
    F = w_gate.shape[2]
    tm = min(S, 1024)
    tf = min(F, 512)
    per_b = S // tm
    halo_per_tile = tm // HALO
    act = pl.pallas_call(
        functools.partial(_ffn_act_kernel, tiles_per_seq=per_b),
        grid=(F // tf, M // tm),
        in_specs=[pl.BlockSpec((tm, D), lambda f, i: (i, 0)),
                  pl.BlockSpec((HALO, D), lambda f, i: (jnp.maximum(i * halo_per_tile - 1, 0), 0)),
                  pl.BlockSpec((None, D, tf), lambda f, i: (layer, 0, f)),
                  pl.BlockSpec((None, D, tf), lambda f, i: (layer, 0, f)),
                  pl.BlockSpec((CONV_WIDTH, tf), lambda f, i: (0, f)),
                  pl.BlockSpec((1, tf), lambda f, i: (0, f))],
        out_specs=pl.BlockSpec((tm, tf), lambda f, i: (i, f)),
        out_shape=jax.ShapeDtypeStruct((M, F), BF16),
        scratch_shapes=[pltpu.VMEM((D, tf), BF16), pltpu.VMEM((D, tf), BF16)],
        compiler_params=_params("arbitrary", "arbitrary"),
        name="ffn_act",
    )(h2, h2, w_gate, w_up, conv_w, conv_b.reshape(1, F))
    return pl.pallas_call(
        _ffn_out_kernel,
        grid=(M // tm, F // tf),
        in_specs=[pl.BlockSpec((tm, tf), lambda i, k: (i, k)),
                  pl.BlockSpec((None, tf, D), lambda i, k: (layer, k, 0)),
                  pl.BlockSpec((tm, D), lambda i, k: (i, 0)),
                  pl.BlockSpec((1, 1, D), lambda i, k: (i // per_b, 0, 0)),
                  pl.BlockSpec((1, D), lambda i, k: (0, 0))],
        out_specs=pl.BlockSpec((tm, D), lambda i, k: (i, 0)),
        out_shape=jax.ShapeDtypeStruct((M, D), F32),
        compiler_params=_params("arbitrary", "arbitrary"),
        name="ffn_out",
    )(act, w_down, x1, gt.reshape(B, 1, D), g_final.reshape(1, D))


def kernel(x, c, w_ada, b_ada, g_mix, w_in, w_o_sb, w_o_dsa, w_out, rel_bias, g_ffn, w_gate,
           w_up, conv_w, conv_b, w_down, g_final):
    B, S, D = x.shape
    depth = w_ada.shape[0]
    W = w_o_sb.shape[1]
    H = W // HEAD_DIM
    qw = N_IDX_HEADS * IDX_DIM
    assert w_o_dsa.shape[1] == W and D % W == 0 and S % ATTN_BLOCK == 0 and W % qw == 0
    n_sel = min(TOPK_MAX, S // 4)
    scale = HEAD_DIM ** -0.5
    x2 = x.reshape(B * S, D)

    for l in range(depth):
        mod = _adaln(c, w_ada[l], b_ada[l])
        sh1, sc1, gt1, sh2, sc2, gt2 = jnp.split(mod, 6, axis=-1)

        wt = w_in[l].T
        o_kix = 6 * W + qw
        o_wix = o_kix + IDX_DIM
        o_gate = o_wix + N_IDX_HEADS
        k_rows = wt[o_kix:o_wix]
        wt_tail = jnp.concatenate(
            [k_rows, k_rows, wt[o_wix:o_gate],
             jnp.zeros((2 * IDX_DIM - N_IDX_HEADS, D), wt.dtype)], axis=0)

        h1 = _norm_mod(x2, g_mix[l], sc1, sh1, S)
        z = _in_proj_wt(h1, wt, (1, 4), W, (), "in_proj")
        zt = _in_proj_wt(h1, wt, (0, 2, 3, 5, 6), W, ((0, scale), (2, scale * LOG2_E)),
                         "in_proj_t", transposed=True)
        zg = _in_proj_wt(h1, wt[o_gate:], tuple(range(2 * D // W)), W, (), "in_proj_gates")
        zs = _in_proj_wt(h1, wt_tail, (0,), 4 * IDX_DIM, (), "in_proj_idx")

        o_sb = _sb_attn(z, zt, B, S, H, 0, 0, H)
        mask = _dsa_select(zt, zs, B, S, 4 * W // qw, n_sel)
        o_ds = _dsa_attn(z, zt, mask, _bias_tiles(rel_bias, min(S, ATTN_BLOCK)),
                         B, S, H, 2 * H, H, 3 * H)

        x2, h2 = _mix_out(o_sb, o_ds, zg, 0, x2, w_o_sb[l].astype(BF16),
                          w_o_dsa[l].astype(BF16), w_out[l].astype(BF16), gt1, g_ffn[l], sc2, sh2, S)
        last = l == depth - 1
        assert last, "the final rms_norm is fused into the last layer's FFN"
        x2 = _conv_ffn(h2, x2, w_gate, w_up, l, w_down,
                       conv_w[l], conv_b[l], gt2, g_final, S)
    return x2.reshape(B, S, D)
```
